```python
import jax, jax.numpy as jnp
from jax import lax
import numpy as np

D_MODEL = 2048
BATCH = 4
SEQ = 2048
DEPTH = 1
DEC_BATCH = 128
DEC_SEQ = 8
PAST_LEN = 16384
PAGE_SIZE = 128

HG_HEADS = 16
HG_DK = 128
HG_DV = D_MODEL // HG_HEADS
HG_KDIM = HG_HEADS * HG_DK
HG_VDIM = HG_HEADS * HG_DV
M_INNER = D_MODEL
M_HEADDIM = 64
M_HEADS = M_INNER // M_HEADDIM
M_GROUPS = 4
M_HPG = M_HEADS // M_GROUPS
M_DSTATE = 128
M_CONV = 4
M_CONV_DIM = M_INNER + 2 * M_GROUPS * M_DSTATE
N_BRANCH = 2
FFN_HIDDEN = -(-8 * D_MODEL // (3 * 256)) * 256
CHUNK = 64
EPS = 1e-6
IN_WIDTHS = (HG_KDIM, HG_KDIM, HG_VDIM, HG_VDIM, M_INNER, M_CONV_DIM, M_HEADS, N_BRANCH * D_MODEL)
W_IN_WIDTH = sum(IN_WIDTHS)

kernel_name = "hgrn2_mamba2_gated_hybrid_step"


def rmsnorm(x, w):
    xf = x.astype(jnp.float32)
    y = xf * lax.rsqrt(jnp.mean(xf * xf, axis=-1, keepdims=True) + EPS)
    return (y * w.astype(jnp.float32)).astype(x.dtype)


def _pad_time(a, n_pad):
    return jnp.pad(a, [(0, 0), (0, n_pad)] + [(0, 0)] * (a.ndim - 2))


def _to_chunks(a, nc, c):
    return jnp.moveaxis(a.reshape(a.shape[0], nc, c, *a.shape[2:]), 1, 0)


def _from_chunks(o, T):
    nc, B, c = o.shape[:3]
    return jnp.moveaxis(o, 0, 1).reshape(B, nc * c, *o.shape[3:])[:, :T]


def hgrn2_chunked(q, k, v, log_f, s0):
    T = q.shape[1]
    c = min(CHUNK, T)
    n_pad = (-T) % c
    q, k, v, log_f = (_pad_time(a, n_pad) for a in (q, k, v, log_f))
    nc = (T + n_pad) // c
    mask = jnp.tril(jnp.ones((c, c), dtype=bool))[None, :, :, None, None]

    def step(S, blk):
        qc, kc, vc, gc = blk
        b = jnp.cumsum(gc, axis=1)
        decay = jnp.exp(jnp.where(mask, b[:, :, None] - b[:, None, :], -jnp.inf))
        scores = jnp.einsum('bthd,bshd,btshd->bhts', qc, kc, decay)
        o = (jnp.einsum('bhts,bshv->bthv', scores, vc)
             + jnp.einsum('bthd,bhdv->bthv', qc * jnp.exp(b), S))
        b_last = b[:, -1]
        S = (jnp.exp(b_last)[..., None] * S
             + jnp.einsum('bshd,bshv->bhdv', kc * jnp.exp(b_last[:, None] - b), vc))
        return S, o

    S, o = lax.scan(step, s0, (_to_chunks(q, nc, c), _to_chunks(k, nc, c),
                              _to_chunks(v, nc, c), _to_chunks(log_f, nc, c)))
    return _from_chunks(o, T), S


def ssd_chunked(x, dt, A, Bm, Cm, s0):
    T = x.shape[1]
    c = min(CHUNK, T)
    n_pad = (-T) % c
    x, dt, Bm, Cm = (_pad_time(a, n_pad) for a in (x, dt, Bm, Cm))
    nc = (T + n_pad) // c
    mask = jnp.tril(jnp.ones((c, c), dtype=bool))[None, :, :, None, None]

    def step(S, blk):
        xc, dtc, bc, cc = blk
        acum = jnp.cumsum(dtc * A, axis=1)
        L = jnp.exp(jnp.where(mask, acum[:, :, None] - acum[:, None, :], -jnp.inf))
        cb = jnp.einsum('btgn,bsgn->bgts', cc, bc)
        y = (jnp.einsum('bgts,btsgr,bsgr,bsgrp->btgrp', cb, L, dtc, xc)
             + jnp.einsum('btgn,bgrpn,btgr->btgrp', cc, S, jnp.exp(acum)))
        a_last = acum[:, -1]
        S = (jnp.exp(a_last)[..., None, None] * S
             + jnp.einsum('bsgn,bsgr,bsgrp->bgrpn', bc, jnp.exp(a_last[:, None] - acum) * dtc, xc))
        return S, y

    S, y = lax.scan(step, s0, (_to_chunks(x, nc, c), _to_chunks(dt, nc, c),
                              _to_chunks(Bm, nc, c), _to_chunks(Cm, nc, c)))
    return _from_chunks(y, T), S


def causal_conv(xbc, buf, conv_w, conv_b):
    T = xbc.shape[1]
    xp = jnp.concatenate([buf.astype(xbc.dtype), xbc], axis=1)
    out = conv_b + sum(xp[:, j:j + T] * conv_w[j] for j in range(M_CONV))
    return jax.nn.silu(out), xp[:, -(M_CONV - 1):]


def _split_proj(proj):
    idx = np.cumsum(IN_WIDTHS)[:-1].tolist()
    return jnp.split(proj, idx, axis=-1)


def hgrn2_mixer(q, f_raw, v, g, lb, hg_norm, s0):
    B, T, _ = q.shape
    f32 = jnp.float32
    sig = jax.nn.sigmoid(f_raw.astype(f32))
    f = lb + (1.0 - lb) * sig
    log_f = jnp.log(f)
    k = (1.0 - lb) * jax.nn.sigmoid(-f_raw.astype(f32))
    qh = (q.astype(f32) * HG_DK ** -0.5).reshape(B, T, HG_HEADS, HG_DK)
    kh = k.reshape(B, T, HG_HEADS, HG_DK)
    gh = log_f.reshape(B, T, HG_HEADS, HG_DK)
    vh = v.astype(f32).reshape(B, T, HG_HEADS, HG_DV)
    o, s_new = hgrn2_chunked(qh, kh, vh, gh, s0.astype(f32))
    o = rmsnorm(o, hg_norm.reshape(HG_HEADS, HG_DV)).reshape(B, T, HG_VDIM)
    o = o * jax.nn.silu(g.astype(f32))
    return o.astype(q.dtype), s_new.astype(s0.dtype)


def mamba2_mixer(z, xbc, dt_raw, conv_buf, conv_w, conv_b, dt_bias, a_log, d_skip, ssm_norm, s0):
    B, T, _ = z.shape
    f32 = jnp.float32
    xbc_c, new_buf = causal_conv(xbc, conv_buf, conv_w, conv_b)
    xs, Bm, Cm = jnp.split(xbc_c, [M_INNER, M_INNER + M_GROUPS * M_DSTATE], axis=-1)
    xh = xs.astype(f32).reshape(B, T, M_GROUPS, M_HPG, M_HEADDIM)
    Bm = Bm.astype(f32).reshape(B, T, M_GROUPS, M_DSTATE)
    Cm = Cm.astype(f32).reshape(B, T, M_GROUPS, M_DSTATE)
    dt = jax.nn.softplus(dt_raw.astype(f32) + dt_bias.astype(f32)).reshape(B, T, M_GROUPS, M_HPG)
    A = -jnp.exp(a_log.astype(f32)).reshape(M_GROUPS, M_HPG)
    s0g = s0.astype(f32).reshape(B, M_GROUPS, M_HPG, M_HEADDIM, M_DSTATE)
    y, s_new = ssd_chunked(xh, dt, A, Bm, Cm, s0g)
    y = y + d_skip.astype(f32).reshape(M_GROUPS, M_HPG)[:, :, None] * xh
    y = y.reshape(B, T, M_INNER) * jax.nn.silu(z.astype(f32))
    y = rmsnorm(y.reshape(B, T, M_GROUPS, M_INNER // M_GROUPS),
                ssm_norm.reshape(M_GROUPS, M_INNER // M_GROUPS)).reshape(B, T, M_INNER)
    s_new = s_new.reshape(B, M_HEADS, M_HEADDIM, M_DSTATE)
    return y.astype(z.dtype), s_new.astype(s0.dtype), new_buf.astype(conv_buf.dtype)


def _layer(x, s_hg, s_ssm, conv_buf, lb, norm_mix, w_in, hg_norm, conv_w, conv_b, dt_bias,
           a_log, d_skip, ssm_norm, w_branch_hg, w_branch_ssm, w_out, norm_ffn,
           w_ffn_gate, w_ffn_up, w_ffn_down):
    B, T, _ = x.shape
    h = rmsnorm(x, norm_mix)
    proj = jnp.einsum('btd,de->bte', h, w_in)
    q, f_raw, v, g, z, xbc, dt_raw, gate_raw = _split_proj(proj)
    o_hg, s_hg_new = hgrn2_mixer(q, f_raw, v, g, lb, hg_norm, s_hg)
    o_m, s_ssm_new, buf_new = mamba2_mixer(z, xbc, dt_raw, conv_buf, conv_w, conv_b, dt_bias,
                                           a_log, d_skip, ssm_norm, s_ssm)
    gates = jax.nn.sigmoid(gate_raw.astype(jnp.float32)).reshape(B, T, N_BRANCH, D_MODEL)
    y_hg = jnp.einsum('bte,ed->btd', o_hg, w_branch_hg).astype(jnp.float32)
    y_m = jnp.einsum('bte,ed->btd', o_m, w_branch_ssm).astype(jnp.float32)
    merged = (gates[:, :, 0] * y_hg + gates[:, :, 1] * y_m).astype(x.dtype)
    x = x + jnp.einsum('btd,de->bte', merged, w_out)
    h2 = rmsnorm(x, norm_ffn)
    act = jax.nn.silu(jnp.einsum('btd,df->btf', h2, w_ffn_gate)) * jnp.einsum('btd,df->btf', h2, w_ffn_up)
    x = x + jnp.einsum('btf,fd->btd', act, w_ffn_down)
    return x, s_hg_new, s_ssm_new, buf_new


def setup_inputs(seed: int = 0) -> dict:
    key = jax.random.key(seed)
    ks = jax.random.split(key, 24)
    f32 = jnp.float32
    nrm = lambda k, shape, s: jax.random.normal(k, shape, f32) * s
    dt0 = jnp.exp(jax.random.uniform(ks[10], (DEPTH, M_HEADS), f32,
                                     float(np.log(1e-3)), float(np.log(1e-1))))
    return {
        'x_prompt': nrm(ks[0], (BATCH, SEQ, D_MODEL), 1.0),
        'x_sample': nrm(ks[1], (DEC_BATCH, DEC_SEQ, D_MODEL), 1.0),
        'state_hgrn': nrm(ks[2], (DEPTH, DEC_BATCH, HG_HEADS, HG_DK, HG_DV), 0.5),
        'state_ssm': nrm(ks[3], (DEPTH, DEC_BATCH, M_HEADS, M_HEADDIM, M_DSTATE), 0.1),
        'state_conv': nrm(ks[4], (DEPTH, DEC_BATCH, M_CONV - 1, M_CONV_DIM), 1.0),
        'norm_mix': 1.0 + nrm(ks[5], (DEPTH, D_MODEL), 0.02),
        'w_in': nrm(ks[6], (DEPTH, D_MODEL, W_IN_WIDTH), D_MODEL ** -0.5),
        'hg_lb_logits': nrm(ks[7], (DEPTH + 1, HG_KDIM), 0.5),
        'hg_norm': 1.0 + nrm(ks[8], (DEPTH, HG_VDIM), 0.02),
        'conv_w': nrm(ks[9], (DEPTH, M_CONV, M_CONV_DIM), M_CONV ** -0.5),
        'conv_b': nrm(ks[11], (DEPTH, M_CONV_DIM), 0.02),
        'dt_bias': dt0 + jnp.log(-jnp.expm1(-dt0)),
        'a_log': jnp.log(jax.random.uniform(ks[12], (DEPTH, M_HEADS), f32, 1.0, 16.0)),
        'd_skip': 1.0 + nrm(ks[13], (DEPTH, M_HEADS), 0.02),
        'ssm_norm': 1.0 + nrm(ks[14], (DEPTH, M_INNER), 0.02),
        'w_branch_hg': nrm(ks[15], (DEPTH, HG_VDIM, D_MODEL), HG_VDIM ** -0.5),
        'w_branch_ssm': nrm(ks[16], (DEPTH, M_INNER, D_MODEL), M_INNER ** -0.5),
        'w_out': nrm(ks[17], (DEPTH, D_MODEL, D_MODEL), D_MODEL ** -0.5),
        'norm_ffn': 1.0 + nrm(ks[18], (DEPTH, D_MODEL), 0.02),
        'w_ffn_gate': nrm(ks[19], (DEPTH, D_MODEL, FFN_HIDDEN), D_MODEL ** -0.5),
        'w_ffn_up': nrm(ks[20], (DEPTH, D_MODEL, FFN_HIDDEN), D_MODEL ** -0.5),
        'w_ffn_down': nrm(ks[21], (DEPTH, FFN_HIDDEN, D_MODEL), FFN_HIDDEN ** -0.5),
        'norm_final': 1.0 + nrm(ks[22], (D_MODEL,), 0.02),
    }


def reference(x_prompt, x_sample, state_hgrn, state_ssm, state_conv, norm_mix, w_in, hg_lb_logits,
              hg_norm, conv_w, conv_b, dt_bias, a_log, d_skip, ssm_norm, w_branch_hg, w_branch_ssm,
              w_out, norm_ffn, w_ffn_gate, w_ffn_up, w_ffn_down, norm_final):
    lb_all = jnp.cumsum(jax.nn.softmax(hg_lb_logits.astype(jnp.float32), axis=0), axis=0)
    xp, xs = x_prompt, x_sample
    hp_l, sp_l, cp_l, hs_l, ss_l, cs_l = [], [], [], [], [], []
    for l in range(DEPTH):
        wl = (norm_mix[l], w_in[l], hg_norm[l], conv_w[l], conv_b[l], dt_bias[l], a_log[l],
              d_skip[l], ssm_norm[l], w_branch_hg[l], w_branch_ssm[l], w_out[l], norm_ffn[l],
              w_ffn_gate[l], w_ffn_up[l], w_ffn_down[l])
        s_hg0 = jnp.zeros((BATCH, HG_HEADS, HG_DK, HG_DV), x_prompt.dtype)
        s_ssm0 = jnp.zeros((BATCH, M_HEADS, M_HEADDIM, M_DSTATE), x_prompt.dtype)
        buf0 = jnp.zeros((BATCH, M_CONV - 1, M_CONV_DIM), x_prompt.dtype)
        xp, hp, sp, cp = _layer(xp, s_hg0, s_ssm0, buf0, lb_all[l], *wl)
        xs, hs, ss, cs = _layer(xs, state_hgrn[l], state_ssm[l], state_conv[l], lb_all[l], *wl)
        hp_l.append(hp); sp_l.append(sp); cp_l.append(cp)
        hs_l.append(hs); ss_l.append(ss); cs_l.append(cs)
    y_prompt = rmsnorm(xp, norm_final)
    y_sample = rmsnorm(xs, norm_final)
    return (y_prompt, y_sample, jnp.stack(hp_l), jnp.stack(sp_l), jnp.stack(cp_l),
            jnp.stack(hs_l), jnp.stack(ss_l), jnp.stack(cs_l))
```

```python
import functools

import numpy as np
import jax
import jax.numpy as jnp
from jax import lax
from jax.experimental import pallas as pl
from jax.experimental.pallas import tpu as pltpu

f32 = jnp.float32
bf16 = jnp.bfloat16

EPS = 1e-6
LANES = 128
CHUNK = 64
HG_DK = 128
M_HEADDIM = 64
M_DSTATE = 128
M_GROUPS = 4
M_CONV = 4
VMEM_LIMIT = 56 * 1024 * 1024


def _cparams(sem):
    return pltpu.CompilerParams(dimension_semantics=sem, vmem_limit_bytes=VMEM_LIMIT)


def _dot(a, b):
    return jnp.dot(a, b, preferred_element_type=f32)


def _dot_nt(a, b):
    return lax.dot_general(a, b, (((1,), (1,)), ((), ())), preferred_element_type=f32)


def _dot_tn(a, b):
    return lax.dot_general(a, b, (((0,), (0,)), ((), ())), preferred_element_type=f32)


def _split3(x):
    hi = x.astype(bf16)
    r1 = x - hi.astype(f32)
    mid = r1.astype(bf16)
    lo = (r1 - mid.astype(f32)).astype(bf16)
    return hi, mid, lo


def _dot3(m, x):
    hi, mid, lo = _split3(x)
    return _dot(m, hi) + _dot(m, mid) + _dot(m, lo)


def _mxu_operand(x, small):
    xb = x.astype(bf16)
    return xb.astype(f32) if small else xb


def _rmsnorm_kernel(x_ref, w_ref, o_ref):
    x = x_ref[...]
    ms = jnp.mean(x * x, axis=-1, keepdims=True)
    o_ref[...] = (x * lax.rsqrt(ms + EPS) * w_ref[...]).astype(o_ref.dtype)


def _rmsnorm(x, w, out_dtype, tm=512):
    m, d = x.shape
    return pl.pallas_call(
        _rmsnorm_kernel,
        grid=(m // tm,),
        in_specs=[pl.BlockSpec((tm, d), lambda i: (i, 0)), pl.BlockSpec((1, d), lambda i: (0, 0))],
        out_specs=pl.BlockSpec((tm, d), lambda i: (i, 0)),
        out_shape=jax.ShapeDtypeStruct((m, d), out_dtype),
        compiler_params=_cparams(("arbitrary",)),
        name="rmsnorm",
    )(x, w.reshape(1, d))


def _mm_kernel(*refs, na, nd, nc, nt, no, epi):
    a_refs = refs[:na]
    refs = refs[na:]
    w_refs = refs[:nd]
    c_refs = refs[nd:nd + nc]
    t_refs = refs[nd + nc:nd + nc + nt]
    o_refs = refs[nd + nc + nt:nd + nc + nt + no]
    wb_refs = refs[nd + nc + nt + no:]

    @pl.when(pl.program_id(1) == 0)
    def _():
        for w_ref, wb_ref in zip(w_refs, wb_refs):
            wb_ref[...] = w_ref[...].astype(bf16)

    accs = [_dot(a_refs[min(i, na - 1)][...], wb_ref[...]) for i, wb_ref in enumerate(wb_refs)]
    outs = epi(accs, [c[...] for c in c_refs], [t[...] for t in t_refs])
    for o_ref, o in zip(o_refs, outs):
        o_ref[...] = o.astype(o_ref.dtype)


def _col_map(j, i, off):
    return (0, j + off)


def _tile_map(j, i, off):
    return (i, j + off)


def _mm(a_list, w_list, w_offs, n_cols, epi, out_dtypes, cols=(), tiles=(), tile_offs=(), tm=1024, tn=512, name="mm"):
    m = a_list[0].shape[0]
    na, nd, nc, nt, no = len(a_list), len(w_list), len(cols), len(tiles), len(out_dtypes)
    assert na in (1, nd)
    in_specs = [pl.BlockSpec((tm, a.shape[1]), lambda j, i: (i, 0)) for a in a_list]
    in_specs += [pl.BlockSpec((w.shape[0], tn), functools.partial(_col_map, off=off // tn)) for w, off in zip(w_list, w_offs)]
    in_specs += [pl.BlockSpec((c.shape[0], tn), functools.partial(_col_map, off=0)) for c in cols]
    in_specs += [pl.BlockSpec((tm, tn), functools.partial(_tile_map, off=off // tn)) for off in tile_offs]
    outs = pl.pallas_call(
        functools.partial(_mm_kernel, na=na, nd=nd, nc=nc, nt=nt, no=no, epi=epi),
        grid=(n_cols // tn, m // tm),
        in_specs=in_specs,
        out_specs=[pl.BlockSpec((tm, tn), functools.partial(_tile_map, off=0)) for _ in out_dtypes],
        out_shape=[jax.ShapeDtypeStruct((m, n_cols), dt) for dt in out_dtypes],
        scratch_shapes=[pltpu.VMEM((w.shape[0], tn), bf16) for w in w_list],
        compiler_params=_cparams(("arbitrary", "arbitrary")),
        name=name,
    )(*a_list, *w_list, *cols, *tiles)
    return outs


def _softplus(x):
    return jnp.maximum(x, 0.0) + jnp.log1p(jnp.exp(-jnp.abs(x)))


def _epi_forget(accs, cols, tiles):
    logits = cols[0]
    mx = jnp.max(logits, axis=0, keepdims=True)
    e = jnp.exp(logits - mx)
    lb = e[0:1, :] / jnp.sum(e, axis=0, keepdims=True)
    sig = jax.nn.sigmoid(accs[0])
    log_f = jnp.log(lb + (1.0 - lb) * sig)
    k = (1.0 - lb) * (1.0 - sig)
    return log_f, k


def _hgrn_consts(c, seg):
    nlev = int(np.log2(seg))
    mat = np.zeros(((nlev + 2) * c, c), np.float32)
    masks = np.zeros((nlev + 1, c, c), np.float32)
    for t in range(c):
        tl = t % seg
        base = t - tl
        for l in range(nlev):
            h = 1 << l
            pos = tl % (2 * h)
            ref = base + tl - pos + h - 1
            if pos >= h:
                mat[l * c + t, ref + 1:t + 1] = 1
            else:
                mat[l * c + t, t + 1:ref + 1] = 1
            for s in range(base, base + seg):
                sl = s % seg
                if sl // (2 * h) == tl // (2 * h) and pos >= h and sl % (2 * h) < h:
                    masks[l, t, s] = 1
        mat[nlev * c + t, base:t + 1] = 1
        mat[(nlev + 1) * c + t, t + 1:base + seg] = 1
        masks[nlev, t, t] = 1
    return mat, masks, nlev


def _hgrn_chunk(q, k, g, v, mat, masks, nlev, c):
    e_all = jnp.exp(_dot3(mat, g))
    s = jnp.where(masks[nlev], _dot_nt(q.astype(bf16), k.astype(bf16)), 0.0)
    for l in range(nlev):
        e = e_all[l * c:(l + 1) * c]
        s = jnp.where(masks[l], _dot_nt((q * e).astype(bf16), (k * e).astype(bf16)), s)
    o = _dot(s.astype(bf16), v.astype(bf16))
    eb = e_all[nlev * c:(nlev + 1) * c]
    er = e_all[(nlev + 1) * c:]
    return o, q * eb, k * er, eb


def _hgrn_out(o, sg, nw):
    ms = jnp.mean(o * o, axis=-1, keepdims=True)
    return o * lax.rsqrt(ms + EPS) * nw * sg


def _hgrn_prompt_kernel(q_ref, g_ref, k_ref, v_ref, sg_ref, nw_ref, mat_ref, mask_ref, o_ref, s_ref, st_ref,
                        *, c, nlev, n_chunks):
    t_blk = pl.program_id(2)

    @pl.when(t_blk == 0)
    def _():
        st_ref[...] = jnp.zeros_like(st_ref)

    masks = [mask_ref[l] > 0.5 for l in range(nlev + 1)]
    mat = mat_ref[...]
    nw = nw_ref[...]

    def body(ci, carry):
        r0 = pl.multiple_of(ci * c, c)
        rows = pl.ds(r0, c)
        v = v_ref[rows, :]
        o, qe, kd, eb = _hgrn_chunk(q_ref[rows, :], k_ref[rows, :], g_ref[rows, :], v, mat, masks, nlev, c)
        st = st_ref[...]
        o = o + _dot_nt(qe.astype(bf16), st.astype(bf16))
        st_ref[...] = st * eb[c - 1:c, :] + _dot_tn(v.astype(bf16), kd.astype(bf16))
        o_ref[rows, :] = _hgrn_out(o, sg_ref[rows, :], nw).astype(o_ref.dtype)
        return carry

    lax.fori_loop(0, n_chunks, body, 0)

    @pl.when(t_blk == pl.num_programs(2) - 1)
    def _():
        s_ref[0, 0] = st_ref[...].T


def _hgrn_sample_kernel(q_ref, g_ref, k_ref, v_ref, sg_ref, nw_ref, mat_ref, mask_ref, s_in_ref, o_ref, s_out_ref,
                        *, c, nlev, seg):
    masks = [mask_ref[l] > 0.5 for l in range(nlev + 1)]
    v = v_ref[...]
    o, qe, kd, eb = _hgrn_chunk(q_ref[...], k_ref[...], g_ref[...], v, mat_ref[...], masks, nlev, c)
    eb_t = jnp.concatenate([eb, eb], axis=0).T
    small = seg < 16
    parts = []
    for j in range(c // seg):
        sl = slice(j * seg, (j + 1) * seg)
        s0 = s_in_ref[j, 0]
        parts.append(_dot(_mxu_operand(qe[sl], small), _mxu_operand(s0, small)))
        last = j * seg + seg - 1
        upd = _dot_tn(_mxu_operand(kd[sl], small), _mxu_operand(v[sl], small))
        s_out_ref[j, 0] = s0 * eb_t[:, last:last + 1] + upd
    o = o + jnp.concatenate(parts, axis=0)
    o_ref[...] = _hgrn_out(o, sg_ref[...], nw_ref[...]).astype(o_ref.dtype)


def _hgrn(q, g, k, v, sg, nw, state, n_prompt, batch, seq, dec_seq):
    m, width = q.shape
    heads = width // LANES
    nw = nw.reshape(1, width)
    c = CHUNK
    tc = 512
    nt = seq // tc
    mat, masks, nlev = _hgrn_consts(c, c)
    row_spec = pl.BlockSpec((tc, LANES), lambda b, h, t: (b * nt + t, h))
    const2 = lambda shape: pl.BlockSpec(shape, lambda b, h, t: (0,) * len(shape))
    o_p, s_p = pl.pallas_call(
        functools.partial(_hgrn_prompt_kernel, c=c, nlev=nlev, n_chunks=tc // c),
        grid=(batch, heads, nt),
        in_specs=[row_spec] * 5 + [pl.BlockSpec((1, LANES), lambda b, h, t: (0, h)), const2(mat.shape), const2(masks.shape)],
        out_specs=[row_spec, pl.BlockSpec((1, 1, HG_DK, LANES), lambda b, h, t: (b, h, 0, 0))],
        out_shape=[jax.ShapeDtypeStruct((n_prompt, width), bf16),
                   jax.ShapeDtypeStruct((batch, heads, HG_DK, LANES), f32)],
        scratch_shapes=[pltpu.VMEM((LANES, HG_DK), f32)],
        compiler_params=_cparams(("arbitrary", "arbitrary", "arbitrary")),
        name="hgrn_prompt",
    )(q, g, k, v, sg, nw, jnp.asarray(mat, bf16), jnp.asarray(masks))
    n_sample = m - n_prompt
    per = c // dec_seq
    mat, masks, nlev = _hgrn_consts(c, dec_seq)
    base = n_prompt // c
    row_spec = pl.BlockSpec((c, LANES), lambda jb, h: (base + jb, h))
    out_row_spec = pl.BlockSpec((c, LANES), lambda jb, h: (jb, h))
    st_spec = pl.BlockSpec((per, 1, HG_DK, LANES), lambda jb, h: (jb, h, 0, 0))
    const2 = lambda shape: pl.BlockSpec(shape, lambda jb, h: (0,) * len(shape))
    o_s, s_s = pl.pallas_call(
        functools.partial(_hgrn_sample_kernel, c=c, nlev=nlev, seg=dec_seq),
        grid=(n_sample // c, heads),
        in_specs=[row_spec] * 5 + [pl.BlockSpec((1, LANES), lambda jb, h: (0, h)), const2(mat.shape), const2(masks.shape), st_spec],
        out_specs=[out_row_spec, st_spec],
        out_shape=[jax.ShapeDtypeStruct((n_sample, width), bf16), jax.ShapeDtypeStruct(state.shape, f32)],
        compiler_params=_cparams(("arbitrary", "arbitrary")),
        name="hgrn_sample",
    )(q, g, k, v, sg, nw, jnp.asarray(mat, bf16), jnp.asarray(masks), state)
    return o_p, s_p, o_s, s_s


GROUP_W = 512
PAIRS = GROUP_W // LANES
XBC_W = GROUP_W + 2 * M_DSTATE


def _ssd_consts(c, seg):
    t = np.arange(c)
    same = (t[:, None] // seg) == (t[None, :] // seg)
    tril = (same & (t[None, :] <= t[:, None])).astype(np.float32)
    return tril


def _ssd_chunk(xs, bm, cm, dt, zs, a_row, dsk, nw, lc, tril, get_state, set_state, c, seg):
    nseg = c // seg
    small = seg < 16
    lo_half = lax.broadcasted_iota(jnp.int32, (c, LANES), 1) < M_HEADDIM
    acum = _dot3(lc, dt * a_row)
    acum_t = acum.T
    dt_t = dt.T
    ea_t = jnp.exp(acum_t[0:8, :])
    cb = _dot_nt(cm.astype(bf16), bm.astype(bf16))
    ssq = jnp.zeros((c, 1), f32)
    ys = []
    for p in range(PAIRS):
        h0, h1 = 2 * p, 2 * p + 1
        xp = xs[:, p * LANES:(p + 1) * LANES]
        y = None
        for hh, sel in ((h0, lo_half), (h1, jnp.logical_not(lo_half))):
            dm = acum[:, hh:hh + 1] - acum_t[hh:hh + 1, :]
            lm = jnp.exp(jnp.where(tril, dm, -1e30))
            sc = (cb * lm * dt_t[hh:hh + 1, :]).astype(bf16)
            part = _dot(sc, jnp.where(sel, xp, 0.0).astype(bf16))
            y = part if y is None else y + part
        acp = jnp.where(lo_half, acum[:, h0:h0 + 1], acum[:, h1:h1 + 1])
        dtp = jnp.where(lo_half, dt[:, h0:h0 + 1], dt[:, h1:h1 + 1])
        if nseg == 1:
            alast = acp[c - 1:c, :]
        else:
            alast = jnp.concatenate(
                [jnp.broadcast_to(acp[j * seg + seg - 1:j * seg + seg, :], (seg, LANES)) for j in range(nseg)], axis=0)
        xw = xp * (jnp.exp(alast - acp) * dtp)
        cs_parts = []
        for j in range(nseg):
            sl = slice(j * seg, (j + 1) * seg)
            s0 = get_state(j, p)
            cs_parts.append(_dot_nt(_mxu_operand(cm[sl], small), _mxu_operand(s0, small)))
            upd = _dot_tn(_mxu_operand(xw[sl], small), _mxu_operand(bm[sl], small))
            last = j * seg + seg - 1
            decay = jnp.concatenate(
                [jnp.broadcast_to(ea_t[h0:h0 + 1, last:last + 1], (M_HEADDIM, LANES)),
                 jnp.broadcast_to(ea_t[h1:h1 + 1, last:last + 1], (M_HEADDIM, LANES))], axis=0)
            set_state(j, p, s0 * decay + upd)
        cs = cs_parts[0] if nseg == 1 else jnp.concatenate(cs_parts, axis=0)
        y = y + cs * jnp.exp(acp) + dsk[:, p * LANES:(p + 1) * LANES] * xp
        y = y * zs[:, p * LANES:(p + 1) * LANES]
        ssq = ssq + jnp.sum(y * y, axis=-1, keepdims=True)
        ys.append(y)
    scale = lax.rsqrt(ssq * (1.0 / GROUP_W) + EPS)
    return [ys[p] * scale * nw[:, p * LANES:(p + 1) * LANES] for p in range(PAIRS)]


def _conv_taps(xpad_ref, lead, rows, w, b):
    acc = None
    for j in range(M_CONV):
        term = xpad_ref[lead + (slice(5 + j + rows[0], 5 + j + rows[1]), slice(None))] * w[j:j + 1, :]
        acc = term if acc is None else acc + term
    return jax.nn.silu(b + acc)


def _ssd_prompt_kernel(xr_ref, br_ref, cr_ref, dt_ref, zs_ref, wx_ref, wb_ref, wc_ref, bx_ref, bb_ref, bc_ref,
                       alog_ref, dsk_ref, nw_ref, lc_ref, tril_ref, o_ref, s_ref, xpad_ref, xc_ref, st_ref,
                       *, c, rows):
    t_blk = pl.program_id(2)

    @pl.when(t_blk == 0)
    def _():
        st_ref[...] = jnp.zeros_like(st_ref)
        xpad_ref[0:8, :] = jnp.zeros((8, XBC_W), f32)

    @pl.when(t_blk > 0)
    def _():
        xpad_ref[0:8, :] = xpad_ref[rows:rows + 8, :]

    xpad_ref[8:8 + rows, 0:GROUP_W] = xr_ref[...]
    xpad_ref[8:8 + rows, GROUP_W:GROUP_W + M_DSTATE] = br_ref[...]
    xpad_ref[8:8 + rows, GROUP_W + M_DSTATE:XBC_W] = cr_ref[...]
    w = jnp.concatenate([wx_ref[...], wb_ref[...], wc_ref[...]], axis=1)
    b = jnp.concatenate([bx_ref[...], bb_ref[...], bc_ref[...]], axis=1)
    for i in range(rows // c):
        xc_ref[i * c:(i + 1) * c, :] = _conv_taps(xpad_ref, (), (i * c, (i + 1) * c), w, b)

    a_row = -jnp.exp(alog_ref[...])
    dsk = dsk_ref[...]
    nw = nw_ref[...]
    lc = lc_ref[...]
    tril = tril_ref[...] > 0.5

    def get_state(j, p):
        return st_ref[p]

    def set_state(j, p, val):
        st_ref[p] = val

    def body(ci, carry):
        r0 = pl.multiple_of(ci * c, c)
        rs = pl.ds(r0, c)
        outs = _ssd_chunk(xc_ref[rs, 0:GROUP_W], xc_ref[rs, GROUP_W:GROUP_W + M_DSTATE], xc_ref[rs, GROUP_W + M_DSTATE:XBC_W],
                          dt_ref[rs, :], zs_ref[rs, :], a_row, dsk, nw, lc, tril, get_state, set_state, c, c)
        for p in range(PAIRS):
            o_ref[rs, p * LANES:(p + 1) * LANES] = outs[p].astype(o_ref.dtype)
        return carry

    lax.fori_loop(0, rows // c, body, 0)

    @pl.when(t_blk == pl.num_programs(2) - 1)
    def _():
        for p in range(PAIRS):
            s_ref[0, p * LANES:(p + 1) * LANES, :] = st_ref[p]


def _ssd_sample_kernel(xr_ref, br_ref, cr_ref, dt_ref, zs_ref, wx_ref, wb_ref, wc_ref, bx_ref, bb_ref, bc_ref,
                       alog_ref, dsk_ref, nw_ref, lc_ref, tril_ref, hx_ref, hb_ref, hc_ref, s_in_ref,
                       o_ref, s_out_ref, xpad_ref, *, c, seg):
    nseg = c // seg
    w = jnp.concatenate([wx_ref[...], wb_ref[...], wc_ref[...]], axis=1)
    b = jnp.concatenate([bx_ref[...], bb_ref[...], bc_ref[...]], axis=1)
    conv = []
    for j in range(nseg):
        sl = slice(j * seg, (j + 1) * seg)
        xpad_ref[j, 5:8, 0:GROUP_W] = hx_ref[j]
        xpad_ref[j, 5:8, GROUP_W:GROUP_W + M_DSTATE] = hb_ref[j]
        xpad_ref[j, 5:8, GROUP_W + M_DSTATE:XBC_W] = hc_ref[j]
        xpad_ref[j, 8:8 + seg, 0:GROUP_W] = xr_ref[sl, :]
        xpad_ref[j, 8:8 + seg, GROUP_W:GROUP_W + M_DSTATE] = br_ref[sl, :]
        xpad_ref[j, 8:8 + seg, GROUP_W + M_DSTATE:XBC_W] = cr_ref[sl, :]
        conv.append(_conv_taps(xpad_ref, (j,), (0, seg), w, b))
    xc = jnp.concatenate(conv, axis=0)

    def get_state(j, p):
        return s_in_ref[j, p * LANES:(p + 1) * LANES, :]

    def set_state(j, p, val):
        s_out_ref[j, p * LANES:(p + 1) * LANES, :] = val

    outs = _ssd_chunk(xc[:, 0:GROUP_W], xc[:, GROUP_W:GROUP_W + M_DSTATE], xc[:, GROUP_W + M_DSTATE:XBC_W],
                      dt_ref[...], zs_ref[...], -jnp.exp(alog_ref[...]), dsk_ref[...], nw_ref[...], lc_ref[...],
                      tril_ref[...] > 0.5, get_state, set_state, c, seg)
    for p in range(PAIRS):
        o_ref[:, p * LANES:(p + 1) * LANES] = outs[p].astype(o_ref.dtype)


def _ssd(xbc, dt, zs, conv_w, conv_b, alog_p, dsk, nw, state, hist, n_prompt, batch, seq, dec_seq):
    m = xbc.shape[0]
    inner = zs.shape[1]
    c = CHUNK
    xb_blk = inner // M_DSTATE
    cb_blk = xb_blk + M_GROUPS

    def specs(row_map, nrow):
        def rm(fn):
            return lambda *ix: fn(row_map(*ix), ix[1])
        zero = lambda fn: (lambda *ix: fn(0, ix[1]))
        return [
            pl.BlockSpec((nrow, GROUP_W), rm(lambda r, g: (r, g))),
            pl.BlockSpec((nrow, M_DSTATE), rm(lambda r, g: (r, xb_blk + g))),
            pl.BlockSpec((nrow, M_DSTATE), rm(lambda r, g: (r, cb_blk + g))),
            pl.BlockSpec((nrow, LANES), rm(lambda r, g: (r, g))),
            pl.BlockSpec((nrow, GROUP_W), rm(lambda r, g: (r, g))),
            pl.BlockSpec((M_CONV, GROUP_W), zero(lambda r, g: (0, g))),
            pl.BlockSpec((M_CONV, M_DSTATE), zero(lambda r, g: (0, xb_blk + g))),
            pl.BlockSpec((M_CONV, M_DSTATE), zero(lambda r, g: (0, cb_blk + g))),
            pl.BlockSpec((1, GROUP_W), zero(lambda r, g: (0, g))),
            pl.BlockSpec((1, M_DSTATE), zero(lambda r, g: (0, xb_blk + g))),
            pl.BlockSpec((1, M_DSTATE), zero(lambda r, g: (0, cb_blk + g))),
            pl.BlockSpec((1, LANES), zero(lambda r, g: (0, g))),
            pl.BlockSpec((1, GROUP_W), zero(lambda r, g: (0, g))),
            pl.BlockSpec((1, GROUP_W), zero(lambda r, g: (0, g))),
            pl.BlockSpec((c, c), zero(lambda r, g: (0, 0))),
            pl.BlockSpec((c, c), zero(lambda r, g: (0, 0))),
        ]

    common = (xbc, xbc, xbc, dt, zs, conv_w, conv_w, conv_w, conv_b, conv_b, conv_b, alog_p, dsk, nw)
    rows = 512
    nt = seq // rows
    tril = _ssd_consts(c, c)
    in_specs = specs(lambda b, g, t: b * nt + t, rows)
    o_p, s_p = pl.pallas_call(
        functools.partial(_ssd_prompt_kernel, c=c, rows=rows),
        grid=(batch, M_GROUPS, nt),
        in_specs=in_specs,
        out_specs=[pl.BlockSpec((rows, GROUP_W), lambda b, g, t: (b * nt + t, g)),
                   pl.BlockSpec((1, GROUP_W, M_DSTATE), lambda b, g, t: (b, g, 0))],
        out_shape=[jax.ShapeDtypeStruct((n_prompt, inner), bf16),
                   jax.ShapeDtypeStruct((batch, inner, M_DSTATE), f32)],
        scratch_shapes=[pltpu.VMEM((rows + 8, XBC_W), f32), pltpu.VMEM((rows, XBC_W), f32),
                        pltpu.VMEM((PAIRS, LANES, M_DSTATE), f32)],
        compiler_params=_cparams(("arbitrary", "arbitrary", "arbitrary")),
        name="ssd_prompt",
    )(*common, jnp.asarray(tril, bf16), jnp.asarray(tril))
    n_sample = m - n_prompt
    per = c // dec_seq
    base = n_prompt // c
    tril = _ssd_consts(c, dec_seq)
    in_specs = specs(lambda jb, g: base + jb, c)
    in_specs += [
        pl.BlockSpec((per, M_CONV - 1, GROUP_W), lambda jb, g: (jb, 0, g)),
        pl.BlockSpec((per, M_CONV - 1, M_DSTATE), lambda jb, g: (jb, 0, xb_blk + g)),
        pl.BlockSpec((per, M_CONV - 1, M_DSTATE), lambda jb, g: (jb, 0, cb_blk + g)),
        pl.BlockSpec((per, GROUP_W, M_DSTATE), lambda jb, g: (jb, g, 0)),
    ]
    o_s, s_s = pl.pallas_call(
        functools.partial(_ssd_sample_kernel, c=c, seg=dec_seq),
        grid=(n_sample // c, M_GROUPS),
        in_specs=in_specs,
        out_specs=[pl.BlockSpec((c, GROUP_W), lambda jb, g: (jb, g)),
                   pl.BlockSpec((per, GROUP_W, M_DSTATE), lambda jb, g: (jb, g, 0))],
        out_shape=[jax.ShapeDtypeStruct((n_sample, inner), bf16), jax.ShapeDtypeStruct(state.shape, f32)],
        scratch_shapes=[pltpu.VMEM((per, 16, XBC_W), f32)],
        compiler_params=_cparams(("arbitrary", "arbitrary")),
        name="ssd_sample",
    )(*common, jnp.asarray(tril, bf16), jnp.asarray(tril), hist, hist, hist, state)
    return o_p, s_p, o_s, s_s


def kernel(x_prompt, x_sample, state_hgrn, state_ssm, state_conv, norm_mix, w_in, hg_lb_logits, hg_norm, conv_w, conv_b,
           dt_bias, a_log, d_skip, ssm_norm, w_branch_hg, w_branch_ssm, w_out, norm_ffn, w_ffn_gate, w_ffn_up,
           w_ffn_down, norm_final):
    batch, seq, d = x_prompt.shape
    dec_batch, dec_seq, _ = x_sample.shape
    n_prompt, n_sample = batch * seq, dec_batch * dec_seq
    hg_heads = state_hgrn.shape[2]
    kdim = hg_heads * HG_DK
    vdim = d
    inner = d
    m_heads = state_ssm.shape[2]
    conv_dim = conv_w.shape[2]
    hpg = m_heads // M_GROUPS

    x_all = jnp.concatenate([x_prompt.reshape(n_prompt, d), x_sample.reshape(n_sample, d)], axis=0)
    h = _rmsnorm(x_all, norm_mix[0], bf16)

    w = w_in[0]
    o_q, o_f, o_v, o_g, o_z, o_xbc = 0, kdim, 2 * kdim, 2 * kdim + vdim, 2 * kdim + 2 * vdim, 2 * kdim + 2 * vdim + inner
    o_dt = o_xbc + conv_dim
    o_gate = o_dt + m_heads
    scale = HG_DK ** -0.5
    (q,) = _mm([h], [w], [o_q], kdim, lambda a, c, t: (a[0] * scale,), [f32], name="proj_q")
    log_f, k = _mm([h], [w], [o_f], kdim, _epi_forget, [f32, f32], cols=[hg_lb_logits], name="proj_f")
    (v,) = _mm([h], [w], [o_v], vdim, lambda a, c, t: (a[0],), [f32], name="proj_v")
    (sg,) = _mm([h], [w], [o_g], vdim, lambda a, c, t: (jax.nn.silu(a[0]),), [f32], name="proj_g")
    (zs,) = _mm([h], [w], [o_z], inner, lambda a, c, t: (jax.nn.silu(a[0]),), [f32], name="proj_z")
    (xbc,) = _mm([h], [w], [o_xbc], conv_dim, lambda a, c, t: (a[0],), [f32], name="proj_xbc")
    w_dt = jnp.pad(w[:, o_dt:o_gate].reshape(d, M_GROUPS, hpg), ((0, 0), (0, 0), (0, LANES - hpg))).reshape(d, M_GROUPS * LANES)
    pad_heads = lambda p: jnp.pad(p.reshape(1, M_GROUPS, hpg), ((0, 0), (0, 0), (0, LANES - hpg))).reshape(1, M_GROUPS * LANES)
    (dt,) = _mm([h], [w_dt], [0], M_GROUPS * LANES, lambda a, c, t: (_softplus(a[0] + c[0]),), [f32],
                cols=[pad_heads(dt_bias[0])], name="proj_dt")
    (gates,) = _mm([h], [w[:, o_gate:]], [0], 2 * d, lambda a, c, t: (jax.nn.sigmoid(a[0]),), [f32], name="proj_gate")

    ohp, shp, ohs, shs = _hgrn(q, log_f, k, v, sg, hg_norm[0], state_hgrn[0], n_prompt, batch, seq, dec_seq)
    dsk = jnp.repeat(d_skip[0], M_HEADDIM).reshape(1, inner)
    omp, smp, oms, sms = _ssd(xbc, dt, zs, conv_w[0], conv_b[0].reshape(1, conv_dim), pad_heads(a_log[0]), dsk,
                              ssm_norm[0].reshape(1, inner), state_ssm[0].reshape(dec_batch, inner, M_DSTATE),
                              state_conv[0], n_prompt, batch, seq, dec_seq)
    o_hg = jnp.concatenate([ohp, ohs], axis=0)
    o_m = jnp.concatenate([omp, oms], axis=0)

    (merged,) = _mm([o_hg, o_m], [w_branch_hg[0], w_branch_ssm[0]], [0, 0], d,
                    lambda a, c, t: (t[0] * a[0] + t[1] * a[1],), [bf16], tiles=[gates, gates], tile_offs=[0, d],
                    name="merge")
    (x1,) = _mm([merged], [w_out[0]], [0], d, lambda a, c, t: (t[0] + a[0],), [f32], tiles=[x_all], tile_offs=[0],
                name="out_proj")
    h2 = _rmsnorm(x1, norm_ffn[0], bf16)
    ffn_hidden = w_ffn_gate.shape[2]
    (act,) = _mm([h2], [w_ffn_gate[0], w_ffn_up[0]], [0, 0], ffn_hidden,
                 lambda a, c, t: (jax.nn.silu(a[0]) * a[1],), [bf16], name="ffn_up")
    (x2,) = _mm([act], [w_ffn_down[0]], [0], d, lambda a, c, t: (t[0] + a[0],), [f32], tiles=[x1], tile_offs=[0],
                tm=512, tn=256, name="ffn_down")
    y = _rmsnorm(x2, norm_final, f32)

    y_prompt = y[:n_prompt].reshape(batch, seq, d)
    y_sample = y[n_prompt:].reshape(dec_batch, dec_seq, d)
    new_conv_p = xbc[:n_prompt].reshape(batch, seq, conv_dim)[:, seq - (M_CONV - 1):][None]
    new_conv_s = xbc[n_prompt:].reshape(dec_batch, dec_seq, conv_dim)[:, dec_seq - (M_CONV - 1):][None]
    return (y_prompt, y_sample,
            shp[None], smp.reshape(1, batch, m_heads, M_HEADDIM, M_DSTATE), new_conv_p,
            shs[None], sms.reshape(1, dec_batch, m_heads, M_HEADDIM, M_DSTATE), new_conv_s)
```

```python
import functools

import numpy as np
import jax
import jax.numpy as jnp
from jax import lax
from jax.experimental import pallas as pl
from jax.experimental.pallas import tpu as pltpu

f32 = jnp.float32
bf16 = jnp.bfloat16

EPS = 1e-6
LANES = 128
CHUNK = 64
HG_DK = 128
M_HEADDIM = 64
M_DSTATE = 128
M_GROUPS = 4
M_CONV = 4
VMEM_LIMIT = 56 * 1024 * 1024


def _cparams(sem):
    return pltpu.CompilerParams(dimension_semantics=sem, vmem_limit_bytes=VMEM_LIMIT)


def _dot(a, b):
    return jnp.dot(a, b, preferred_element_type=f32)


def _dot_nt(a, b):
    return lax.dot_general(a, b, (((1,), (1,)), ((), ())), preferred_element_type=f32)


def _dot_tn(a, b):
    return lax.dot_general(a, b, (((0,), (0,)), ((), ())), preferred_element_type=f32)


def _split3(x):
    hi = x.astype(bf16)
    r1 = x - hi.astype(f32)
    mid = r1.astype(bf16)
    lo = (r1 - mid.astype(f32)).astype(bf16)
    return hi, mid, lo


def _tile3(m):
    return jnp.asarray(np.tile(m, (1, 3)), bf16)


def _dot3(m3, x):
    return _dot(m3, jnp.concatenate(_split3(x), axis=0))


def _mxu_operand(x, small):
    xb = x.astype(bf16)
    return xb.astype(f32) if small else xb


def _rmsnorm_kernel(*refs, n_in, n_out, split):
    x_refs, w_ref, o_refs = refs[:n_in], refs[n_in], refs[n_in + 1:]

    def run(x_ref, o_ref):
        x = x_ref[...]
        ms = jnp.mean(x * x, axis=-1, keepdims=True)
        o_ref[...] = (x * lax.rsqrt(ms + EPS) * w_ref[...]).astype(o_ref.dtype)

    if n_in == 1 and n_out == 1:
        run(x_refs[0], o_refs[0])
    else:
        i = pl.program_id(0)
        pl.when(i < split)(lambda: run(x_refs[0], o_refs[0]))
        pl.when(i >= split)(lambda: run(x_refs[-1], o_refs[-1]))


def _rmsnorm(xs, w, out_dtype, n_first, split_out=False, tm=512):
    d = xs[0].shape[1]
    m = sum(x.shape[0] for x in xs)
    split = n_first // tm
    first = lambda i: (jnp.minimum(i, split - 1), 0)
    second = lambda i: (jnp.maximum(i - split, 0), 0)
    whole = lambda i: (i, 0)
    blk = lambda fn: pl.BlockSpec((tm, d), fn)
    in_specs = [blk(whole)] if len(xs) == 1 else [blk(first), blk(second)]
    if split_out:
        out_specs = [blk(first), blk(second)]
        out_shape = [jax.ShapeDtypeStruct((n_first, d), out_dtype), jax.ShapeDtypeStruct((m - n_first, d), out_dtype)]
    else:
        out_specs = [blk(whole)]
        out_shape = [jax.ShapeDtypeStruct((m, d), out_dtype)]
    outs = pl.pallas_call(
        functools.partial(_rmsnorm_kernel, n_in=len(xs), n_out=len(out_shape), split=split),
        grid=(m // tm,),
        in_specs=in_specs + [pl.BlockSpec((1, d), lambda i: (0, 0))],
        out_specs=out_specs,
        out_shape=out_shape,
        compiler_params=_cparams(("arbitrary",)),
        name="rmsnorm",
    )(*xs, w.reshape(1, d))
    return outs if split_out else outs[0]


def _mm_kernel(*refs, na, nd, nc, nt, no, epi, w_t):
    a_refs = refs[:na]
    refs = refs[na:]
    w_refs = refs[:nd]
    c_refs = refs[nd:nd + nc]
    t_refs = refs[nd + nc:nd + nc + nt]
    o_refs = refs[nd + nc + nt:nd + nc + nt + no]
    wb_refs = refs[nd + nc + nt + no:]

    @pl.when(pl.program_id(1) == 0)
    def _():
        for w_ref, wb_ref in zip(w_refs, wb_refs):
            wb_ref[...] = w_ref[...].astype(bf16)

    accs = [(_dot_nt if t else _dot)(a_refs[min(i, na - 1)][...], wb_ref[...])
            for i, (wb_ref, t) in enumerate(zip(wb_refs, w_t))]
    outs = epi(accs, [c[...] for c in c_refs], [t[...] for t in t_refs])
    for o_ref, o in zip(o_refs, outs):
        o_ref[...] = o.astype(o_ref.dtype)


def _col_map(j, i, off):
    return (0, j + off)


def _tile_map(j, i, off):
    return (i, j + off)


def _row_elem_map(j, i, off, tn):
    return (pl.multiple_of(off + j * tn, 8), 0)


def _mm(a_list, w_list, n_cols, epi, out_dtypes, cols=(), tiles=(), tm=1024, tn=512, name="mm"):
    m = a_list[0].shape[0]
    na, nd, nc, nt, no = len(a_list), len(w_list), len(cols), len(tiles), len(out_dtypes)
    assert na in (1, nd)
    in_specs = [pl.BlockSpec((tm, a.shape[1]), lambda j, i: (i, 0)) for a in a_list]
    scratch = []
    for w, off, transposed in w_list:
        if transposed:
            k = w.shape[1]
            assert off % 8 == 0
            in_specs.append(pl.BlockSpec((pl.Element(tn), pl.Element(k)), functools.partial(_row_elem_map, off=off, tn=tn)))
            scratch.append(pltpu.VMEM((tn, k), bf16))
        else:
            k = w.shape[0]
            in_specs.append(pl.BlockSpec((k, tn), functools.partial(_col_map, off=off // tn)))
            scratch.append(pltpu.VMEM((k, tn), bf16))
    in_specs += [pl.BlockSpec((c.shape[0], tn), functools.partial(_col_map, off=0)) for c in cols]
    in_specs += [pl.BlockSpec((tm, tn), fn) for _, fn in tiles]
    outs = pl.pallas_call(
        functools.partial(_mm_kernel, na=na, nd=nd, nc=nc, nt=nt, no=no, epi=epi, w_t=tuple(t for _, _, t in w_list)),
        grid=(n_cols // tn, m // tm),
        in_specs=in_specs,
        out_specs=[pl.BlockSpec((tm, tn), functools.partial(_tile_map, off=0)) for _ in out_dtypes],
        out_shape=[jax.ShapeDtypeStruct((m, n_cols), dt) for dt in out_dtypes],
        scratch_shapes=scratch,
        compiler_params=_cparams(("arbitrary", "arbitrary")),
        name=name,
    )(*a_list, *[w for w, _, _ in w_list], *cols, *[t for t, _ in tiles])
    return outs


def _softplus(x):
    return jnp.maximum(x, 0.0) + jnp.log1p(jnp.exp(-jnp.abs(x)))


def _epi_forget(accs, cols, tiles):
    logits = cols[0]
    mx = jnp.max(logits, axis=0, keepdims=True)
    e = jnp.exp(logits - mx)
    lb = e[0:1, :] / jnp.sum(e, axis=0, keepdims=True)
    sig = jax.nn.sigmoid(accs[0])
    log_f = jnp.log(lb + (1.0 - lb) * sig)
    k = (1.0 - lb) * (1.0 - sig)
    return log_f, k


def _hgrn_consts(c, seg):
    nlev = int(np.log2(seg))
    mat = np.zeros(((nlev + 2) * c, c), np.float32)
    masks = np.zeros((nlev + 1, c, c), np.float32)
    for t in range(c):
        tl = t % seg
        base = t - tl
        for l in range(nlev):
            h = 1 << l
            pos = tl % (2 * h)
            ref = base + tl - pos + h - 1
            if pos >= h:
                mat[l * c + t, ref + 1:t + 1] = 1
            else:
                mat[l * c + t, t + 1:ref + 1] = 1
            for s in range(base, base + seg):
                sl = s % seg
                if sl // (2 * h) == tl // (2 * h) and pos >= h and sl % (2 * h) < h:
                    masks[l, t, s] = 1
        mat[nlev * c + t, base:t + 1] = 1
        mat[(nlev + 1) * c + t, t + 1:base + seg] = 1
        masks[nlev, t, t] = 1
    return mat, masks, nlev


def _hgrn_chunk(q, k, g, v, mat, masks, nlev, c):
    e_all = jnp.exp(_dot3(mat, g))
    s = jnp.where(masks[nlev], _dot_nt(q.astype(bf16), k.astype(bf16)), 0.0)
    for l in range(nlev):
        e = e_all[l * c:(l + 1) * c]
        s = jnp.where(masks[l], _dot_nt((q * e).astype(bf16), (k * e).astype(bf16)), s)
    o = _dot(s.astype(bf16), v.astype(bf16))
    eb = e_all[nlev * c:(nlev + 1) * c]
    er = e_all[(nlev + 1) * c:]
    return o, q * eb, k * er, eb


def _hgrn_out(o, sg, nw):
    ms = jnp.mean(o * o, axis=-1, keepdims=True)
    return o * lax.rsqrt(ms + EPS) * nw * sg


def _hgrn_prompt_kernel(q_ref, g_ref, k_ref, v_ref, sg_ref, nw_ref, mat_ref, mask_ref, o_ref, s_ref, st_ref,
                        *, c, nlev, n_chunks):
    t_blk = pl.program_id(2)

    @pl.when(t_blk == 0)
    def _():
        st_ref[...] = jnp.zeros_like(st_ref)

    masks = [mask_ref[l] > 0.5 for l in range(nlev + 1)]
    mat = mat_ref[...]
    nw = nw_ref[...]

    def body(ci, carry):
        r0 = pl.multiple_of(ci * c, c)
        rows = pl.ds(r0, c)
        v = v_ref[rows, :]
        o, qe, kd, eb = _hgrn_chunk(q_ref[rows, :], k_ref[rows, :], g_ref[rows, :], v, mat, masks, nlev, c)
        st = st_ref[...]
        o = o + _dot_nt(qe.astype(bf16), st.astype(bf16))
        st_ref[...] = st * eb[c - 1:c, :] + _dot_tn(v.astype(bf16), kd.astype(bf16))
        o_ref[rows, :] = _hgrn_out(o, sg_ref[rows, :], nw).astype(o_ref.dtype)
        return carry

    lax.fori_loop(0, n_chunks, body, 0, unroll=True)

    @pl.when(t_blk == pl.num_programs(2) - 1)
    def _():
        s_ref[0, 0] = st_ref[...].T


def _hgrn_sample_kernel(q_ref, g_ref, k_ref, v_ref, sg_ref, nw_ref, mat_ref, mask_ref, s_in_ref, o_prev_ref,
                        o_ref, s_out_ref, *, c, nlev, seg):
    del o_prev_ref
    masks = [mask_ref[l] > 0.5 for l in range(nlev + 1)]
    v = v_ref[...]
    o, qe, kd, eb = _hgrn_chunk(q_ref[...], k_ref[...], g_ref[...], v, mat_ref[...], masks, nlev, c)
    eb_t = jnp.concatenate([eb, eb], axis=0).T
    small = seg < 16
    parts = []
    for j in range(c // seg):
        sl = slice(j * seg, (j + 1) * seg)
        s0 = s_in_ref[j, 0]
        parts.append(_dot(_mxu_operand(qe[sl], small), _mxu_operand(s0, small)))
        last = j * seg + seg - 1
        upd = _dot_tn(_mxu_operand(kd[sl], small), _mxu_operand(v[sl], small))
        s_out_ref[j, 0] = s0 * eb_t[:, last:last + 1] + upd
    o = o + jnp.concatenate(parts, axis=0)
    o_ref[...] = _hgrn_out(o, sg_ref[...], nw_ref[...]).astype(o_ref.dtype)


def _hgrn(q, g, k, v, sg, nw, state, n_prompt, batch, seq, dec_seq):
    m, width = q.shape
    heads = width // LANES
    nw = nw.reshape(1, width)
    c = CHUNK
    tc = 512
    nt = seq // tc
    mat, masks, nlev = _hgrn_consts(c, c)
    row_spec = pl.BlockSpec((tc, LANES), lambda b, h, t: (b * nt + t, h))
    const2 = lambda shape: pl.BlockSpec(shape, lambda b, h, t: (0,) * len(shape))
    o_p, s_p = pl.pallas_call(
        functools.partial(_hgrn_prompt_kernel, c=c, nlev=nlev, n_chunks=tc // c),
        grid=(batch, heads, nt),
        in_specs=[row_spec] * 5 + [pl.BlockSpec((1, LANES), lambda b, h, t: (0, h)), const2((mat.shape[0], 3 * c)), const2(masks.shape)],
        out_specs=[row_spec, pl.BlockSpec((1, 1, HG_DK, LANES), lambda b, h, t: (b, h, 0, 0))],
        out_shape=[jax.ShapeDtypeStruct((m, width), bf16),
                   jax.ShapeDtypeStruct((batch, heads, HG_DK, LANES), f32)],
        scratch_shapes=[pltpu.VMEM((LANES, HG_DK), f32)],
        compiler_params=_cparams(("arbitrary", "arbitrary", "arbitrary")),
        name="hgrn_prompt",
    )(q, g, k, v, sg, nw, _tile3(mat), jnp.asarray(masks))
    n_sample = m - n_prompt
    per = c // dec_seq
    mat, masks, nlev = _hgrn_consts(c, dec_seq)
    base = n_prompt // c
    row_spec = pl.BlockSpec((c, LANES), lambda jb, h: (base + jb, h))
    st_spec = pl.BlockSpec((per, 1, HG_DK, LANES), lambda jb, h: (jb, h, 0, 0))
    const2 = lambda shape: pl.BlockSpec(shape, lambda jb, h: (0,) * len(shape))
    o, s_s = pl.pallas_call(
        functools.partial(_hgrn_sample_kernel, c=c, nlev=nlev, seg=dec_seq),
        grid=(n_sample // c, heads),
        in_specs=[row_spec] * 5 + [pl.BlockSpec((1, LANES), lambda jb, h: (0, h)), const2((mat.shape[0], 3 * c)),
                                   const2(masks.shape), st_spec, pl.BlockSpec(memory_space=pl.ANY)],
        out_specs=[row_spec, st_spec],
        out_shape=[jax.ShapeDtypeStruct((m, width), bf16), jax.ShapeDtypeStruct(state.shape, f32)],
        input_output_aliases={9: 0},
        compiler_params=_cparams(("arbitrary", "arbitrary")),
        name="hgrn_sample",
    )(q, g, k, v, sg, nw, _tile3(mat), jnp.asarray(masks), state, o_p)
    return o, s_p, s_s


GROUP_W = 512
PAIRS = GROUP_W // LANES
XBC_W = GROUP_W + 2 * M_DSTATE


def _ssd_consts(c, seg):
    t = np.arange(c)
    same = (t[:, None] // seg) == (t[None, :] // seg)
    tril = (same & (t[None, :] <= t[:, None])).astype(np.float32)
    return tril


def _ssd_chunk(xs, bm, cm, dt, zs, a_row, dsk, nw, lc, tril, get_state, set_state, c, seg):
    nseg = c // seg
    small = seg < 16
    lo_half = lax.broadcasted_iota(jnp.int32, (c, LANES), 1) < M_HEADDIM
    acum = _dot3(lc, dt * a_row)
    acum_t = acum.T
    dt_t = dt.T
    ea_t = jnp.exp(acum_t[0:8, :])
    cb = _dot_nt(cm.astype(bf16), bm.astype(bf16))
    ssq = jnp.zeros((c, 1), f32)
    ys = []
    for p in range(PAIRS):
        h0, h1 = 2 * p, 2 * p + 1
        xp = xs[:, p * LANES:(p + 1) * LANES]
        y = None
        for hh, sel in ((h0, lo_half), (h1, jnp.logical_not(lo_half))):
            dm = acum[:, hh:hh + 1] - acum_t[hh:hh + 1, :]
            lm = jnp.exp(jnp.where(tril, dm, -1e30))
            sc = (cb * lm * dt_t[hh:hh + 1, :]).astype(bf16)
            part = _dot(sc, jnp.where(sel, xp, 0.0).astype(bf16))
            y = part if y is None else y + part
        acp = jnp.where(lo_half, acum[:, h0:h0 + 1], acum[:, h1:h1 + 1])
        dtp = jnp.where(lo_half, dt[:, h0:h0 + 1], dt[:, h1:h1 + 1])
        if nseg == 1:
            alast = acp[c - 1:c, :]
        else:
            alast = jnp.concatenate(
                [jnp.broadcast_to(acp[j * seg + seg - 1:j * seg + seg, :], (seg, LANES)) for j in range(nseg)], axis=0)
        xw = xp * (jnp.exp(alast - acp) * dtp)
        cs_parts = []
        for j in range(nseg):
            sl = slice(j * seg, (j + 1) * seg)
            s0 = get_state(j, p)
            cs_parts.append(_dot_nt(_mxu_operand(cm[sl], small), _mxu_operand(s0, small)))
            upd = _dot_tn(_mxu_operand(xw[sl], small), _mxu_operand(bm[sl], small))
            last = j * seg + seg - 1
            decay = jnp.concatenate(
                [jnp.broadcast_to(ea_t[h0:h0 + 1, last:last + 1], (M_HEADDIM, LANES)),
                 jnp.broadcast_to(ea_t[h1:h1 + 1, last:last + 1], (M_HEADDIM, LANES))], axis=0)
            set_state(j, p, s0 * decay + upd)
        cs = cs_parts[0] if nseg == 1 else jnp.concatenate(cs_parts, axis=0)
        y = y + cs * jnp.exp(acp) + dsk[:, p * LANES:(p + 1) * LANES] * xp
        y = y * zs[:, p * LANES:(p + 1) * LANES]
        ssq = ssq + jnp.sum(y * y, axis=-1, keepdims=True)
        ys.append(y)
    scale = lax.rsqrt(ssq * (1.0 / GROUP_W) + EPS)
    return [ys[p] * scale * nw[:, p * LANES:(p + 1) * LANES] for p in range(PAIRS)]


def _conv_taps(xpad_ref, lead, rows, w, b):
    acc = None
    for j in range(M_CONV):
        term = xpad_ref[lead + (slice(5 + j + rows[0], 5 + j + rows[1]), slice(None))] * w[j:j + 1, :]
        acc = term if acc is None else acc + term
    return jax.nn.silu(b + acc)


def _ssd_prompt_kernel(xr_ref, br_ref, cr_ref, dt_ref, zs_ref, wx_ref, wb_ref, wc_ref, bx_ref, bb_ref, bc_ref,
                       alog_ref, dsk_ref, nw_ref, lc_ref, tril_ref, o_ref, s_ref, xpad_ref, xc_ref, st_ref,
                       *, c, rows):
    t_blk = pl.program_id(2)

    @pl.when(t_blk == 0)
    def _():
        st_ref[...] = jnp.zeros_like(st_ref)
        xpad_ref[0:8, :] = jnp.zeros((8, XBC_W), f32)

    @pl.when(t_blk > 0)
    def _():
        xpad_ref[0:8, :] = xpad_ref[rows:rows + 8, :]

    xpad_ref[8:8 + rows, 0:GROUP_W] = xr_ref[...]
    xpad_ref[8:8 + rows, GROUP_W:GROUP_W + M_DSTATE] = br_ref[...]
    xpad_ref[8:8 + rows, GROUP_W + M_DSTATE:XBC_W] = cr_ref[...]
    w = jnp.concatenate([wx_ref[...], wb_ref[...], wc_ref[...]], axis=1)
    b = jnp.concatenate([bx_ref[...], bb_ref[...], bc_ref[...]], axis=1)
    for i in range(rows // c):
        xc_ref[i * c:(i + 1) * c, :] = _conv_taps(xpad_ref, (), (i * c, (i + 1) * c), w, b)

    a_row = -jnp.exp(alog_ref[...])
    dsk = dsk_ref[...]
    nw = nw_ref[...]
    lc = lc_ref[...]
    tril = tril_ref[...] > 0.5

    def get_state(j, p):
        return st_ref[p]

    def set_state(j, p, val):
        st_ref[p] = val

    def body(ci, carry):
        r0 = pl.multiple_of(ci * c, c)
        rs = pl.ds(r0, c)
        outs = _ssd_chunk(xc_ref[rs, 0:GROUP_W], xc_ref[rs, GROUP_W:GROUP_W + M_DSTATE], xc_ref[rs, GROUP_W + M_DSTATE:XBC_W],
                          dt_ref[rs, :], zs_ref[rs, :], a_row, dsk, nw, lc, tril, get_state, set_state, c, c)
        for p in range(PAIRS):
            o_ref[rs, p * LANES:(p + 1) * LANES] = outs[p].astype(o_ref.dtype)
        return carry

    lax.fori_loop(0, rows // c, body, 0, unroll=True)

    @pl.when(t_blk == pl.num_programs(2) - 1)
    def _():
        for p in range(PAIRS):
            s_ref[0, p * LANES:(p + 1) * LANES, :] = st_ref[p]


def _ssd_sample_kernel(xr_ref, br_ref, cr_ref, dt_ref, zs_ref, wx_ref, wb_ref, wc_ref, bx_ref, bb_ref, bc_ref,
                       alog_ref, dsk_ref, nw_ref, lc_ref, tril_ref, hx_ref, hb_ref, hc_ref, s_in_ref, o_prev_ref,
                       o_ref, s_out_ref, xpad_ref, *, c, seg):
    del o_prev_ref
    nseg = c // seg
    w = jnp.concatenate([wx_ref[...], wb_ref[...], wc_ref[...]], axis=1)
    b = jnp.concatenate([bx_ref[...], bb_ref[...], bc_ref[...]], axis=1)
    conv = []
    for j in range(nseg):
        sl = slice(j * seg, (j + 1) * seg)
        xpad_ref[j, 5:8, 0:GROUP_W] = hx_ref[j]
        xpad_ref[j, 5:8, GROUP_W:GROUP_W + M_DSTATE] = hb_ref[j]
        xpad_ref[j, 5:8, GROUP_W + M_DSTATE:XBC_W] = hc_ref[j]
        xpad_ref[j, 8:8 + seg, 0:GROUP_W] = xr_ref[sl, :]
        xpad_ref[j, 8:8 + seg, GROUP_W:GROUP_W + M_DSTATE] = br_ref[sl, :]
        xpad_ref[j, 8:8 + seg, GROUP_W + M_DSTATE:XBC_W] = cr_ref[sl, :]
        conv.append(_conv_taps(xpad_ref, (j,), (0, seg), w, b))
    xc = jnp.concatenate(conv, axis=0)

    def get_state(j, p):
        return s_in_ref[j, p * LANES:(p + 1) * LANES, :]

    def set_state(j, p, val):
        s_out_ref[j, p * LANES:(p + 1) * LANES, :] = val

    outs = _ssd_chunk(xc[:, 0:GROUP_W], xc[:, GROUP_W:GROUP_W + M_DSTATE], xc[:, GROUP_W + M_DSTATE:XBC_W],
                      dt_ref[...], zs_ref[...], -jnp.exp(alog_ref[...]), dsk_ref[...], nw_ref[...], lc_ref[...],
                      tril_ref[...] > 0.5, get_state, set_state, c, seg)
    for p in range(PAIRS):
        o_ref[:, p * LANES:(p + 1) * LANES] = outs[p].astype(o_ref.dtype)


def _ssd(xbc, dt, zs, conv_w, conv_b, alog_p, dsk, nw, state, hist, n_prompt, batch, seq, dec_seq):
    m = xbc.shape[0]
    inner = zs.shape[1]
    c = CHUNK
    xb_blk = inner // M_DSTATE
    cb_blk = xb_blk + M_GROUPS

    def specs(row_map, nrow):
        def rm(fn):
            return lambda *ix: fn(row_map(*ix), ix[1])
        zero = lambda fn: (lambda *ix: fn(0, ix[1]))
        return [
            pl.BlockSpec((nrow, GROUP_W), rm(lambda r, g: (r, g))),
            pl.BlockSpec((nrow, M_DSTATE), rm(lambda r, g: (r, xb_blk + g))),
            pl.BlockSpec((nrow, M_DSTATE), rm(lambda r, g: (r, cb_blk + g))),
            pl.BlockSpec((nrow, LANES), rm(lambda r, g: (r, g))),
            pl.BlockSpec((nrow, GROUP_W), rm(lambda r, g: (r, g))),
            pl.BlockSpec((M_CONV, GROUP_W), zero(lambda r, g: (0, g))),
            pl.BlockSpec((M_CONV, M_DSTATE), zero(lambda r, g: (0, xb_blk + g))),
            pl.BlockSpec((M_CONV, M_DSTATE), zero(lambda r, g: (0, cb_blk + g))),
            pl.BlockSpec((1, GROUP_W), zero(lambda r, g: (0, g))),
            pl.BlockSpec((1, M_DSTATE), zero(lambda r, g: (0, xb_blk + g))),
            pl.BlockSpec((1, M_DSTATE), zero(lambda r, g: (0, cb_blk + g))),
            pl.BlockSpec((1, LANES), zero(lambda r, g: (0, g))),
            pl.BlockSpec((1, GROUP_W), zero(lambda r, g: (0, g))),
            pl.BlockSpec((1, GROUP_W), zero(lambda r, g: (0, g))),
            pl.BlockSpec((c, 3 * c), zero(lambda r, g: (0, 0))),
            pl.BlockSpec((c, c), zero(lambda r, g: (0, 0))),
        ]

    common = (xbc, xbc, xbc, dt, zs, conv_w, conv_w, conv_w, conv_b, conv_b, conv_b, alog_p, dsk, nw)
    rows = 512
    nt = seq // rows
    tril = _ssd_consts(c, c)
    in_specs = specs(lambda b, g, t: b * nt + t, rows)
    o_p, s_p = pl.pallas_call(
        functools.partial(_ssd_prompt_kernel, c=c, rows=rows),
        grid=(batch, M_GROUPS, nt),
        in_specs=in_specs,
        out_specs=[pl.BlockSpec((rows, GROUP_W), lambda b, g, t: (b * nt + t, g)),
                   pl.BlockSpec((1, GROUP_W, M_DSTATE), lambda b, g, t: (b, g, 0))],
        out_shape=[jax.ShapeDtypeStruct((m, inner), bf16),
                   jax.ShapeDtypeStruct((batch, inner, M_DSTATE), f32)],
        scratch_shapes=[pltpu.VMEM((rows + 8, XBC_W), f32), pltpu.VMEM((rows, XBC_W), f32),
                        pltpu.VMEM((PAIRS, LANES, M_DSTATE), f32)],
        compiler_params=_cparams(("arbitrary", "arbitrary", "arbitrary")),
        name="ssd_prompt",
    )(*common, _tile3(tril), jnp.asarray(tril))
    n_sample = m - n_prompt
    per = c // dec_seq
    base = n_prompt // c
    tril = _ssd_consts(c, dec_seq)
    in_specs = specs(lambda jb, g: base + jb, c)
    in_specs += [
        pl.BlockSpec((per, M_CONV - 1, GROUP_W), lambda jb, g: (jb, 0, g)),
        pl.BlockSpec((per, M_CONV - 1, M_DSTATE), lambda jb, g: (jb, 0, xb_blk + g)),
        pl.BlockSpec((per, M_CONV - 1, M_DSTATE), lambda jb, g: (jb, 0, cb_blk + g)),
        pl.BlockSpec((per, GROUP_W, M_DSTATE), lambda jb, g: (jb, g, 0)),
        pl.BlockSpec(memory_space=pl.ANY),
    ]
    o, s_s = pl.pallas_call(
        functools.partial(_ssd_sample_kernel, c=c, seg=dec_seq),
        grid=(n_sample // c, M_GROUPS),
        in_specs=in_specs,
        out_specs=[pl.BlockSpec((c, GROUP_W), lambda jb, g: (base + jb, g)),
                   pl.BlockSpec((per, GROUP_W, M_DSTATE), lambda jb, g: (jb, g, 0))],
        out_shape=[jax.ShapeDtypeStruct((m, inner), bf16), jax.ShapeDtypeStruct(state.shape, f32)],
        scratch_shapes=[pltpu.VMEM((per, 16, XBC_W), f32)],
        input_output_aliases={len(in_specs) - 1: 0},
        compiler_params=_cparams(("arbitrary", "arbitrary")),
        name="ssd_sample",
    )(*common, _tile3(tril), jnp.asarray(tril), hist, hist, hist, state, o_p)
    return o, s_p, s_s


def kernel(x_prompt, x_sample, state_hgrn, state_ssm, state_conv, norm_mix, w_in, hg_lb_logits, hg_norm, conv_w, conv_b,
           dt_bias, a_log, d_skip, ssm_norm, w_branch_hg, w_branch_ssm, w_out, norm_ffn, w_ffn_gate, w_ffn_up,
           w_ffn_down, norm_final):
    batch, seq, d = x_prompt.shape
    dec_batch, dec_seq, _ = x_sample.shape
    n_prompt, n_sample = batch * seq, dec_batch * dec_seq
    hg_heads = state_hgrn.shape[2]
    kdim = hg_heads * HG_DK
    vdim = d
    inner = d
    m_heads = state_ssm.shape[2]
    conv_dim = conv_w.shape[2]
    hpg = m_heads // M_GROUPS

    xp2, xs2 = x_prompt.reshape(n_prompt, d), x_sample.reshape(n_sample, d)
    h = _rmsnorm([xp2, xs2], norm_mix[0], bf16, n_prompt)

    wt = jnp.swapaxes(w_in, 1, 2)[0]
    o_q, o_f, o_v, o_g, o_z, o_xbc = 0, kdim, 2 * kdim, 2 * kdim + vdim, 2 * kdim + 2 * vdim, 2 * kdim + 2 * vdim + inner
    o_dt = o_xbc + conv_dim
    o_gate = o_dt + m_heads
    scale = HG_DK ** -0.5
    ident = lambda a, c, t: (a[0],)
    silu = lambda a, c, t: (jax.nn.silu(a[0]),)
    (q,) = _mm([h], [(wt, o_q, True)], kdim, lambda a, c, t: (a[0] * scale,), [f32], name="proj_q")
    log_f, k = _mm([h], [(wt, o_f, True)], kdim, _epi_forget, [f32, f32], cols=[hg_lb_logits], name="proj_f")
    (v,) = _mm([h], [(wt, o_v, True)], vdim, ident, [f32], name="proj_v")
    (sg,) = _mm([h], [(wt, o_g, True)], vdim, silu, [f32], name="proj_g")
    (zs,) = _mm([h], [(wt, o_z, True)], inner, silu, [f32], name="proj_z")
    (xbc,) = _mm([h], [(wt, o_xbc, True)], conv_dim, ident, [f32], name="proj_xbc")
    pad_heads = lambda p: jnp.pad(p.reshape(M_GROUPS, hpg, -1), ((0, 0), (0, LANES - hpg), (0, 0))).reshape(M_GROUPS * LANES, -1)
    (dt,) = _mm([h], [(pad_heads(wt[o_dt:o_gate]), 0, True)], M_GROUPS * LANES,
                lambda a, c, t: (_softplus(a[0] + c[0]),), [f32], cols=[pad_heads(dt_bias[0]).T], name="proj_dt")
    (gates,) = _mm([h], [(wt, o_gate, True)], 2 * d, lambda a, c, t: (jax.nn.sigmoid(a[0]),), [f32], name="proj_gate")

    o_hg, shp, shs = _hgrn(q, log_f, k, v, sg, hg_norm[0], state_hgrn[0], n_prompt, batch, seq, dec_seq)
    dsk = jnp.repeat(d_skip[0], M_HEADDIM).reshape(1, inner)
    o_m, smp, sms = _ssd(xbc, dt, zs, conv_w[0], conv_b[0].reshape(1, conv_dim), pad_heads(a_log[0]).T, dsk,
                         ssm_norm[0].reshape(1, inner), state_ssm[0].reshape(dec_batch, inner, M_DSTATE),
                         state_conv[0], n_prompt, batch, seq, dec_seq)

    tn = 512
    (merged,) = _mm([o_hg, o_m], [(w_branch_hg[0], 0, False), (w_branch_ssm[0], 0, False)], d,
                    lambda a, c, t: (t[0] * a[0] + t[1] * a[1],), [bf16],
                    tiles=[(gates, lambda j, i: (i, j)), (gates, lambda j, i: (i, j + d // tn))], tn=tn, name="merge")
    tm = 1024
    n_p_tiles = n_prompt // tm

    def add_x(a, c, t):
        return (jnp.where(pl.program_id(1) < n_p_tiles, t[0], t[1]) + a[0],)

    (x1,) = _mm([merged], [(w_out[0], 0, False)], d, add_x, [f32],
                tiles=[(xp2, lambda j, i: (jnp.minimum(i, n_p_tiles - 1), j)),
                       (xs2, lambda j, i: (jnp.maximum(i - n_p_tiles, 0), j))], tm=tm, tn=tn, name="out_proj")
    h2 = _rmsnorm([x1], norm_ffn[0], bf16, n_prompt)
    ffn_hidden = w_ffn_gate.shape[2]
    (act,) = _mm([h2], [(w_ffn_gate[0], 0, False), (w_ffn_up[0], 0, False)], ffn_hidden,
                 lambda a, c, t: (jax.nn.silu(a[0]) * a[1],), [bf16], name="ffn_up")
    (x2,) = _mm([act], [(w_ffn_down[0], 0, False)], d, lambda a, c, t: (t[0] + a[0],), [f32],
                tiles=[(x1, lambda j, i: (i, j))], tm=512, tn=256, name="ffn_down")
    y_p, y_s = _rmsnorm([x2], norm_final, f32, n_prompt, split_out=True)

    hist = M_CONV - 1
    new_conv_p = jnp.stack([xbc[(b + 1) * seq - hist:(b + 1) * seq] for b in range(batch)])[None]
    new_conv_s = xbc[n_prompt:].reshape(dec_batch, dec_seq, conv_dim)[:, dec_seq - hist:][None]
    return (y_p.reshape(batch, seq, d), y_s.reshape(dec_batch, dec_seq, d),
            shp[None], smp.reshape(1, batch, m_heads, M_HEADDIM, M_DSTATE), new_conv_p,
            shs[None], sms.reshape(1, dec_batch, m_heads, M_HEADDIM, M_DSTATE), new_conv_s)
```

```python
import functools

import numpy as np
import jax
import jax.numpy as jnp
from jax import lax
from jax.experimental import pallas as pl
from jax.experimental.pallas import tpu as pltpu

f32 = jnp.float32
bf16 = jnp.bfloat16

EPS = 1e-6
LANES = 128
CHUNK = 64
HG_DK = 128
M_HEADDIM = 64
M_DSTATE = 128
M_GROUPS = 4
M_CONV = 4
VMEM_LIMIT = 56 * 1024 * 1024


def _cparams(sem):
    return pltpu.CompilerParams(dimension_semantics=sem, vmem_limit_bytes=VMEM_LIMIT)


def _dot(a, b):
    return jnp.dot(a, b, preferred_element_type=f32)


def _dot_nt(a, b):
    return lax.dot_general(a, b, (((1,), (1,)), ((), ())), preferred_element_type=f32)


def _dot_tn(a, b):
    return lax.dot_general(a, b, (((0,), (0,)), ((), ())), preferred_element_type=f32)


def _split3(x):
    hi = x.astype(bf16)
    r1 = x - hi.astype(f32)
    mid = r1.astype(bf16)
    lo = (r1 - mid.astype(f32)).astype(bf16)
    return hi, mid, lo


def _tile3(m):
    return jnp.asarray(np.tile(m, (1, 3)), bf16)


def _dot3(m3, x):
    return _dot(m3, jnp.concatenate(_split3(x), axis=0))


def _mxu_operand(x, small):
    xb = x.astype(bf16)
    return xb.astype(f32) if small else xb


def _rmsnorm_kernel(*refs, n_in, n_out, split, has_res):
    x_refs, w_ref, o_refs = refs[:n_in], refs[n_in], refs[n_in + 1 + has_res:]

    def run(x_ref, o_ref):
        x = x_ref[...]
        if has_res:
            x = x + refs[n_in + 1][...]
        ms = jnp.mean(x * x, axis=-1, keepdims=True)
        o_ref[...] = (x * lax.rsqrt(ms + EPS) * w_ref[...]).astype(o_ref.dtype)

    if n_in == 1 and n_out == 1:
        run(x_refs[0], o_refs[0])
    else:
        i = pl.program_id(0)
        pl.when(i < split)(lambda: run(x_refs[0], o_refs[0]))
        pl.when(i >= split)(lambda: run(x_refs[-1], o_refs[-1]))


def _rmsnorm(xs, w, out_dtype, n_first, split_out=False, res=None, tm=512):
    d = xs[0].shape[1]
    m = sum(x.shape[0] for x in xs)
    split = n_first // tm
    first = lambda i: (jnp.minimum(i, split - 1), 0)
    second = lambda i: (jnp.maximum(i - split, 0), 0)
    whole = lambda i: (i, 0)
    blk = lambda fn: pl.BlockSpec((tm, d), fn)
    in_specs = [blk(whole)] if len(xs) == 1 else [blk(first), blk(second)]
    if split_out:
        out_specs = [blk(first), blk(second)]
        out_shape = [jax.ShapeDtypeStruct((n_first, d), out_dtype), jax.ShapeDtypeStruct((m - n_first, d), out_dtype)]
    else:
        out_specs = [blk(whole)]
        out_shape = [jax.ShapeDtypeStruct((m, d), out_dtype)]
    extra = [] if res is None else [res]
    outs = pl.pallas_call(
        functools.partial(_rmsnorm_kernel, n_in=len(xs), n_out=len(out_shape), split=split, has_res=len(extra)),
        grid=(m // tm,),
        in_specs=in_specs + [pl.BlockSpec((1, d), lambda i: (0, 0))] + [blk(whole) for _ in extra],
        out_specs=out_specs,
        out_shape=out_shape,
        compiler_params=_cparams(("arbitrary",)),
        name="rmsnorm",
    )(*xs, w.reshape(1, d), *extra)
    return outs if split_out else outs[0]


def _mm_kernel(*refs, na, nd, nc, nt, no, epi, w_t):
    a_refs = refs[:na]
    refs = refs[na:]
    w_refs = refs[:nd]
    c_refs = refs[nd:nd + nc]
    t_refs = refs[nd + nc:nd + nc + nt]
    o_refs = refs[nd + nc + nt:nd + nc + nt + no]
    wb_refs = refs[nd + nc + nt + no:]

    @pl.when(pl.program_id(1) == 0)
    def _():
        for w_ref, wb_ref in zip(w_refs, wb_refs):
            wb_ref[...] = w_ref[...].astype(bf16)

    accs = [(_dot_nt if t else _dot)(a_refs[min(i, na - 1)][...], wb_ref[...])
            for i, (wb_ref, t) in enumerate(zip(wb_refs, w_t))]
    outs = epi(accs, [c[...] for c in c_refs], [t[...] for t in t_refs])
    for o_ref, o in zip(o_refs, outs):
        o_ref[...] = o.astype(o_ref.dtype)


def _col_map(j, i, off):
    return (0, j + off)


def _tile_map(j, i, off):
    return (i, j + off)


def _row_elem_map(j, i, off, tn):
    return (pl.multiple_of(off + j * tn, 8), 0)


def _mm(a_list, w_list, n_cols, epi, out_dtypes, cols=(), tiles=(), tm=1024, tn=512, w_single=False, name="mm"):
    m = a_list[0].shape[0]
    na, nd, nc, nt, no = len(a_list), len(w_list), len(cols), len(tiles), len(out_dtypes)
    assert na in (1, nd)
    in_specs = [pl.BlockSpec((tm, a.shape[1]), lambda j, i: (i, 0)) for a in a_list]
    scratch = []
    mode = dict(pipeline_mode=pl.Buffered(1)) if w_single else {}
    for w, off, transposed in w_list:
        if transposed:
            k = w.shape[1]
            assert off % 8 == 0
            in_specs.append(pl.BlockSpec((pl.Element(tn), pl.Element(k)), functools.partial(_row_elem_map, off=off, tn=tn), **mode))
            scratch.append(pltpu.VMEM((tn, k), bf16))
        else:
            k = w.shape[0]
            in_specs.append(pl.BlockSpec((k, tn), functools.partial(_col_map, off=off // tn), **mode))
            scratch.append(pltpu.VMEM((k, tn), bf16))
    in_specs += [pl.BlockSpec((c.shape[0], tn), functools.partial(_col_map, off=0)) for c in cols]
    in_specs += [pl.BlockSpec((tm, tn), fn) for _, fn in tiles]
    outs = pl.pallas_call(
        functools.partial(_mm_kernel, na=na, nd=nd, nc=nc, nt=nt, no=no, epi=epi, w_t=tuple(t for _, _, t in w_list)),
        grid=(n_cols // tn, m // tm),
        in_specs=in_specs,
        out_specs=[pl.BlockSpec((tm, tn), functools.partial(_tile_map, off=0)) for _ in out_dtypes],
        out_shape=[jax.ShapeDtypeStruct((m, n_cols), dt) for dt in out_dtypes],
        scratch_shapes=scratch,
        compiler_params=_cparams(("arbitrary", "arbitrary")),
        name=name,
    )(*a_list, *[w for w, _, _ in w_list], *cols, *[t for t, _ in tiles])
    return outs


def _ffn_kernel(h_ref, wg_ref, wu_ref, wd_ref, o_ref):
    @pl.when(pl.program_id(1) == 0)
    def _():
        o_ref[...] = jnp.zeros_like(o_ref)

    h = h_ref[...]
    gate = _dot(h, wg_ref[...].astype(bf16))
    up = _dot(h, wu_ref[...].astype(bf16))
    act = (jax.nn.silu(gate) * up).astype(bf16)
    o_ref[...] += _dot(act, wd_ref[...].astype(bf16))


def _ffn(h, w_gate, w_up, w_down, tm=1536, tf=256):
    m, d = h.shape
    hidden = w_gate.shape[1]
    return pl.pallas_call(
        _ffn_kernel,
        grid=(m // tm, hidden // tf),
        in_specs=[pl.BlockSpec((tm, d), lambda i, f: (i, 0), pipeline_mode=pl.Buffered(1)),
                  pl.BlockSpec((d, tf), lambda i, f: (0, f)),
                  pl.BlockSpec((d, tf), lambda i, f: (0, f)),
                  pl.BlockSpec((tf, d), lambda i, f: (f, 0))],
        out_specs=pl.BlockSpec((tm, d), lambda i, f: (i, 0)),
        out_shape=jax.ShapeDtypeStruct((m, d), f32),
        compiler_params=_cparams(("arbitrary", "arbitrary")),
        name="ffn",
    )(h, w_gate, w_up, w_down)


def _softplus(x):
    return jnp.maximum(x, 0.0) + jnp.log1p(jnp.exp(-jnp.abs(x)))


def _epi_forget(accs, cols, tiles):
    logits = cols[0]
    mx = jnp.max(logits, axis=0, keepdims=True)
    e = jnp.exp(logits - mx)
    lb = e[0:1, :] / jnp.sum(e, axis=0, keepdims=True)
    sig = jax.nn.sigmoid(accs[0])
    log_f = jnp.log(lb + (1.0 - lb) * sig)
    k = (1.0 - lb) * (1.0 - sig)
    return log_f, k


def _hgrn_consts(c, seg):
    nlev = int(np.log2(seg))
    mat = np.zeros(((nlev + 2) * c, c), np.float32)
    masks = np.zeros((nlev + 1, c, c), np.float32)
    for t in range(c):
        tl = t % seg
        base = t - tl
        for l in range(nlev):
            h = 1 << l
            pos = tl % (2 * h)
            ref = base + tl - pos + h - 1
            if pos >= h:
                mat[l * c + t, ref + 1:t + 1] = 1
            else:
                mat[l * c + t, t + 1:ref + 1] = 1
            for s in range(base, base + seg):
                sl = s % seg
                if sl // (2 * h) == tl // (2 * h) and pos >= h and sl % (2 * h) < h:
                    masks[l, t, s] = 1
        mat[nlev * c + t, base:t + 1] = 1
        mat[(nlev + 1) * c + t, t + 1:base + seg] = 1
        masks[nlev, t, t] = 1
    return mat, masks, nlev


def _hgrn_chunk(q, k, g, v, mat, masks, nlev, c):
    e_all = jnp.exp(_dot3(mat, g))
    s = jnp.where(masks[nlev], _dot_nt(q.astype(bf16), k.astype(bf16)), 0.0)
    for l in range(nlev):
        e = e_all[l * c:(l + 1) * c]
        s = jnp.where(masks[l], _dot_nt((q * e).astype(bf16), (k * e).astype(bf16)), s)
    o = _dot(s.astype(bf16), v.astype(bf16))
    eb = e_all[nlev * c:(nlev + 1) * c]
    er = e_all[(nlev + 1) * c:]
    return o, q * eb, k * er, eb


def _hgrn_out(o, sg, nw):
    ms = jnp.mean(o * o, axis=-1, keepdims=True)
    return o * lax.rsqrt(ms + EPS) * nw * sg


def _hgrn_prompt_kernel(q_ref, g_ref, k_ref, v_ref, sg_ref, nw_ref, mat_ref, mask_ref, o_ref, s_ref, st_ref,
                        *, c, nlev, n_chunks):
    t_blk = pl.program_id(2)

    @pl.when(t_blk == 0)
    def _():
        st_ref[...] = jnp.zeros_like(st_ref)

    masks = [mask_ref[l] > 0.5 for l in range(nlev + 1)]
    mat = mat_ref[...]
    nw = nw_ref[...]

    def body(ci, carry):
        r0 = pl.multiple_of(ci * c, c)
        rows = pl.ds(r0, c)
        v = v_ref[rows, :]
        o, qe, kd, eb = _hgrn_chunk(q_ref[rows, :], k_ref[rows, :], g_ref[rows, :], v, mat, masks, nlev, c)
        st = st_ref[...]
        o = o + _dot_nt(qe.astype(bf16), st.astype(bf16))
        st_ref[...] = st * eb[c - 1:c, :] + _dot_tn(v.astype(bf16), kd.astype(bf16))
        o_ref[rows, :] = _hgrn_out(o, sg_ref[rows, :], nw).astype(o_ref.dtype)
        return carry

    lax.fori_loop(0, n_chunks, body, 0, unroll=True)

    @pl.when(t_blk == pl.num_programs(2) - 1)
    def _():
        s_ref[0, 0] = st_ref[...].T


def _hgrn_sample_kernel(q_ref, g_ref, k_ref, v_ref, sg_ref, nw_ref, mat_ref, mask_ref, s_in_ref, o_prev_ref,
                        o_ref, s_out_ref, *, c, nlev, seg):
    del o_prev_ref
    masks = [mask_ref[l] > 0.5 for l in range(nlev + 1)]
    v = v_ref[...].astype(f32)
    o, qe, kd, eb = _hgrn_chunk(q_ref[...], k_ref[...], g_ref[...], v, mat_ref[...], masks, nlev, c)
    eb_t = jnp.concatenate([eb, eb], axis=0).T
    small = seg < 16
    parts = []
    for j in range(c // seg):
        sl = slice(j * seg, (j + 1) * seg)
        s0 = s_in_ref[j, 0]
        parts.append(_dot(_mxu_operand(qe[sl], small), _mxu_operand(s0, small)))
        last = j * seg + seg - 1
        upd = _dot_tn(_mxu_operand(kd[sl], small), _mxu_operand(v[sl], small))
        s_out_ref[j, 0] = s0 * eb_t[:, last:last + 1] + upd
    o = o + jnp.concatenate(parts, axis=0)
    o_ref[...] = _hgrn_out(o, sg_ref[...], nw_ref[...]).astype(o_ref.dtype)


def _hgrn(q, g, k, v, sg, nw, state, n_prompt, batch, seq, dec_seq):
    m, width = q.shape
    heads = width // LANES
    nw = nw.reshape(1, width)
    c = CHUNK
    tc = 512
    nt = seq // tc
    mat, masks, nlev = _hgrn_consts(c, c)
    row_spec = pl.BlockSpec((tc, LANES), lambda b, h, t: (b * nt + t, h))
    const2 = lambda shape: pl.BlockSpec(shape, lambda b, h, t: (0,) * len(shape))
    o_p, s_p = pl.pallas_call(
        functools.partial(_hgrn_prompt_kernel, c=c, nlev=nlev, n_chunks=tc // c),
        grid=(batch, heads, nt),
        in_specs=[row_spec] * 5 + [pl.BlockSpec((1, LANES), lambda b, h, t: (0, h)), const2((mat.shape[0], 3 * c)), const2(masks.shape)],
        out_specs=[row_spec, pl.BlockSpec((1, 1, HG_DK, LANES), lambda b, h, t: (b, h, 0, 0))],
        out_shape=[jax.ShapeDtypeStruct((m, width), bf16),
                   jax.ShapeDtypeStruct((batch, heads, HG_DK, LANES), f32)],
        scratch_shapes=[pltpu.VMEM((LANES, HG_DK), f32)],
        compiler_params=_cparams(("arbitrary", "arbitrary", "arbitrary")),
        name="hgrn_prompt",
    )(q, g, k, v, sg, nw, _tile3(mat), jnp.asarray(masks))
    n_sample = m - n_prompt
    per = c // dec_seq
    mat, masks, nlev = _hgrn_consts(c, dec_seq)
    base = n_prompt // c
    row_spec = pl.BlockSpec((c, LANES), lambda jb, h: (base + jb, h))
    st_spec = pl.BlockSpec((per, 1, HG_DK, LANES), lambda jb, h: (jb, h, 0, 0))
    const2 = lambda shape: pl.BlockSpec(shape, lambda jb, h: (0,) * len(shape))
    o, s_s = pl.pallas_call(
        functools.partial(_hgrn_sample_kernel, c=c, nlev=nlev, seg=dec_seq),
        grid=(n_sample // c, heads),
        in_specs=[row_spec] * 5 + [pl.BlockSpec((1, LANES), lambda jb, h: (0, h)), const2((mat.shape[0], 3 * c)),
                                   const2(masks.shape), st_spec, pl.BlockSpec(memory_space=pl.ANY)],
        out_specs=[row_spec, st_spec],
        out_shape=[jax.ShapeDtypeStruct((m, width), bf16), jax.ShapeDtypeStruct(state.shape, f32)],
        input_output_aliases={9: 0},
        compiler_params=_cparams(("arbitrary", "arbitrary")),
        name="hgrn_sample",
    )(q, g, k, v, sg, nw, _tile3(mat), jnp.asarray(masks), state, o_p)
    return o, s_p, s_s


GROUP_W = 512
PAIRS = GROUP_W // LANES
XBC_W = GROUP_W + 2 * M_DSTATE


def _ssd_consts(c, seg):
    t = np.arange(c)
    same = (t[:, None] // seg) == (t[None, :] // seg)
    tril = (same & (t[None, :] <= t[:, None])).astype(np.float32)
    return tril


def _ssd_chunk(xs, bm, cm, dt, zs, a_row, dsk, nw, lc, tril, get_state, set_state, c, seg):
    nseg = c // seg
    small = seg < 16
    lo_half = lax.broadcasted_iota(jnp.int32, (c, LANES), 1) < M_HEADDIM
    acum = _dot3(lc, dt * a_row)
    acum_t = acum.T
    dt_t = dt.T
    ea_t = jnp.exp(acum_t[0:8, :])
    cb = _dot_nt(cm.astype(bf16), bm.astype(bf16))
    ssq = jnp.zeros((c, 1), f32)
    ys = []
    for p in range(PAIRS):
        h0, h1 = 2 * p, 2 * p + 1
        xp = xs[:, p * LANES:(p + 1) * LANES]
        y = None
        for hh, sel in ((h0, lo_half), (h1, jnp.logical_not(lo_half))):
            dm = acum[:, hh:hh + 1] - acum_t[hh:hh + 1, :]
            lm = jnp.exp(jnp.where(tril, dm, -1e30))
            sc = (cb * lm * dt_t[hh:hh + 1, :]).astype(bf16)
            part = _dot(sc, jnp.where(sel, xp, 0.0).astype(bf16))
            y = part if y is None else y + part
        acp = jnp.where(lo_half, acum[:, h0:h0 + 1], acum[:, h1:h1 + 1])
        dtp = jnp.where(lo_half, dt[:, h0:h0 + 1], dt[:, h1:h1 + 1])
        if nseg == 1:
            alast = acp[c - 1:c, :]
        else:
            alast = jnp.concatenate(
                [jnp.broadcast_to(acp[j * seg + seg - 1:j * seg + seg, :], (seg, LANES)) for j in range(nseg)], axis=0)
        xw = xp * (jnp.exp(alast - acp) * dtp)
        cs_parts = []
        for j in range(nseg):
            sl = slice(j * seg, (j + 1) * seg)
            s0 = get_state(j, p)
            cs_parts.append(_dot_nt(_mxu_operand(cm[sl], small), _mxu_operand(s0, small)))
            upd = _dot_tn(_mxu_operand(xw[sl], small), _mxu_operand(bm[sl], small))
            last = j * seg + seg - 1
            decay = jnp.concatenate(
                [jnp.broadcast_to(ea_t[h0:h0 + 1, last:last + 1], (M_HEADDIM, LANES)),
                 jnp.broadcast_to(ea_t[h1:h1 + 1, last:last + 1], (M_HEADDIM, LANES))], axis=0)
            set_state(j, p, s0 * decay + upd)
        cs = cs_parts[0] if nseg == 1 else jnp.concatenate(cs_parts, axis=0)
        y = y + cs * jnp.exp(acp) + dsk[:, p * LANES:(p + 1) * LANES] * xp
        y = y * zs[:, p * LANES:(p + 1) * LANES]
        ssq = ssq + jnp.sum(y * y, axis=-1, keepdims=True)
        ys.append(y)
    scale = lax.rsqrt(ssq * (1.0 / GROUP_W) + EPS)
    return [ys[p] * scale * nw[:, p * LANES:(p + 1) * LANES] for p in range(PAIRS)]


def _conv_taps(xpad_ref, lead, rows, w, b):
    acc = None
    for j in range(M_CONV):
        term = xpad_ref[lead + (slice(5 + j + rows[0], 5 + j + rows[1]), slice(None))] * w[j:j + 1, :]
        acc = term if acc is None else acc + term
    return jax.nn.silu(b + acc)


def _ssd_prompt_kernel(xr_ref, br_ref, cr_ref, dt_ref, zs_ref, wx_ref, wb_ref, wc_ref, bx_ref, bb_ref, bc_ref,
                       alog_ref, dsk_ref, nw_ref, lc_ref, tril_ref, o_ref, s_ref, xpad_ref, xc_ref, st_ref,
                       *, c, rows):
    t_blk = pl.program_id(2)

    @pl.when(t_blk == 0)
    def _():
        st_ref[...] = jnp.zeros_like(st_ref)
        xpad_ref[0:8, :] = jnp.zeros((8, XBC_W), f32)

    @pl.when(t_blk > 0)
    def _():
        xpad_ref[0:8, :] = xpad_ref[rows:rows + 8, :]

    xpad_ref[8:8 + rows, 0:GROUP_W] = xr_ref[...]
    xpad_ref[8:8 + rows, GROUP_W:GROUP_W + M_DSTATE] = br_ref[...]
    xpad_ref[8:8 + rows, GROUP_W + M_DSTATE:XBC_W] = cr_ref[...]
    w = jnp.concatenate([wx_ref[...], wb_ref[...], wc_ref[...]], axis=1)
    b = jnp.concatenate([bx_ref[...], bb_ref[...], bc_ref[...]], axis=1)
    for i in range(rows // c):
        xc_ref[i * c:(i + 1) * c, :] = _conv_taps(xpad_ref, (), (i * c, (i + 1) * c), w, b)

    a_row = -jnp.exp(alog_ref[...])
    dsk = dsk_ref[...]
    nw = nw_ref[...]
    lc = lc_ref[...]
    tril = tril_ref[...] > 0.5

    def get_state(j, p):
        return st_ref[p]

    def set_state(j, p, val):
        st_ref[p] = val

    def body(ci, carry):
        r0 = pl.multiple_of(ci * c, c)
        rs = pl.ds(r0, c)
        outs = _ssd_chunk(xc_ref[rs, 0:GROUP_W], xc_ref[rs, GROUP_W:GROUP_W + M_DSTATE], xc_ref[rs, GROUP_W + M_DSTATE:XBC_W],
                          dt_ref[rs, :], zs_ref[rs, :], a_row, dsk, nw, lc, tril, get_state, set_state, c, c)
        for p in range(PAIRS):
            o_ref[rs, p * LANES:(p + 1) * LANES] = outs[p].astype(o_ref.dtype)
        return carry

    lax.fori_loop(0, rows // c, body, 0, unroll=True)

    @pl.when(t_blk == pl.num_programs(2) - 1)
    def _():
        for p in range(PAIRS):
            s_ref[0, p * LANES:(p + 1) * LANES, :] = st_ref[p]


def _ssd_sample_kernel(xr_ref, br_ref, cr_ref, dt_ref, zs_ref, wx_ref, wb_ref, wc_ref, bx_ref, bb_ref, bc_ref,
                       alog_ref, dsk_ref, nw_ref, lc_ref, tril_ref, hx_ref, hb_ref, hc_ref, s_in_ref, o_prev_ref,
                       o_ref, s_out_ref, xpad_ref, *, c, seg):
    del o_prev_ref
    nseg = c // seg
    w = jnp.concatenate([wx_ref[...], wb_ref[...], wc_ref[...]], axis=1)
    b = jnp.concatenate([bx_ref[...], bb_ref[...], bc_ref[...]], axis=1)
    conv = []
    for j in range(nseg):
        sl = slice(j * seg, (j + 1) * seg)
        xpad_ref[j, 5:8, 0:GROUP_W] = hx_ref[j]
        xpad_ref[j, 5:8, GROUP_W:GROUP_W + M_DSTATE] = hb_ref[j]
        xpad_ref[j, 5:8, GROUP_W + M_DSTATE:XBC_W] = hc_ref[j]
        xpad_ref[j, 8:8 + seg, 0:GROUP_W] = xr_ref[sl, :]
        xpad_ref[j, 8:8 + seg, GROUP_W:GROUP_W + M_DSTATE] = br_ref[sl, :]
        xpad_ref[j, 8:8 + seg, GROUP_W + M_DSTATE:XBC_W] = cr_ref[sl, :]
        conv.append(_conv_taps(xpad_ref, (j,), (0, seg), w, b))
    xc = jnp.concatenate(conv, axis=0)

    def get_state(j, p):
        return s_in_ref[j, p * LANES:(p + 1) * LANES, :]

    def set_state(j, p, val):
        s_out_ref[j, p * LANES:(p + 1) * LANES, :] = val

    outs = _ssd_chunk(xc[:, 0:GROUP_W], xc[:, GROUP_W:GROUP_W + M_DSTATE], xc[:, GROUP_W + M_DSTATE:XBC_W],
                      dt_ref[...], zs_ref[...], -jnp.exp(alog_ref[...]), dsk_ref[...], nw_ref[...], lc_ref[...],
                      tril_ref[...] > 0.5, get_state, set_state, c, seg)
    for p in range(PAIRS):
        o_ref[:, p * LANES:(p + 1) * LANES] = outs[p].astype(o_ref.dtype)


def _ssd(xbc, dt, zs, conv_w, conv_b, alog_p, dsk, nw, state, hist, n_prompt, batch, seq, dec_seq):
    m = xbc.shape[0]
    inner = zs.shape[1]
    c = CHUNK
    xb_blk = inner // M_DSTATE
    cb_blk = xb_blk + M_GROUPS

    def specs(row_map, nrow):
        def rm(fn):
            return lambda *ix: fn(row_map(*ix), ix[1])
        zero = lambda fn: (lambda *ix: fn(0, ix[1]))
        return [
            pl.BlockSpec((nrow, GROUP_W), rm(lambda r, g: (r, g))),
            pl.BlockSpec((nrow, M_DSTATE), rm(lambda r, g: (r, xb_blk + g))),
            pl.BlockSpec((nrow, M_DSTATE), rm(lambda r, g: (r, cb_blk + g))),
            pl.BlockSpec((nrow, LANES), rm(lambda r, g: (r, g))),
            pl.BlockSpec((nrow, GROUP_W), rm(lambda r, g: (r, g))),
            pl.BlockSpec((M_CONV, GROUP_W), zero(lambda r, g: (0, g))),
            pl.BlockSpec((M_CONV, M_DSTATE), zero(lambda r, g: (0, xb_blk + g))),
            pl.BlockSpec((M_CONV, M_DSTATE), zero(lambda r, g: (0, cb_blk + g))),
            pl.BlockSpec((1, GROUP_W), zero(lambda r, g: (0, g))),
            pl.BlockSpec((1, M_DSTATE), zero(lambda r, g: (0, xb_blk + g))),
            pl.BlockSpec((1, M_DSTATE), zero(lambda r, g: (0, cb_blk + g))),
            pl.BlockSpec((1, LANES), zero(lambda r, g: (0, g))),
            pl.BlockSpec((1, GROUP_W), zero(lambda r, g: (0, g))),
            pl.BlockSpec((1, GROUP_W), zero(lambda r, g: (0, g))),
            pl.BlockSpec((c, 3 * c), zero(lambda r, g: (0, 0))),
            pl.BlockSpec((c, c), zero(lambda r, g: (0, 0))),
        ]

    common = (xbc, xbc, xbc, dt, zs, conv_w, conv_w, conv_w, conv_b, conv_b, conv_b, alog_p, dsk, nw)
    rows = 512
    nt = seq // rows
    tril = _ssd_consts(c, c)
    in_specs = specs(lambda b, g, t: b * nt + t, rows)
    o_p, s_p = pl.pallas_call(
        functools.partial(_ssd_prompt_kernel, c=c, rows=rows),
        grid=(batch, M_GROUPS, nt),
        in_specs=in_specs,
        out_specs=[pl.BlockSpec((rows, GROUP_W), lambda b, g, t: (b * nt + t, g)),
                   pl.BlockSpec((1, GROUP_W, M_DSTATE), lambda b, g, t: (b, g, 0))],
        out_shape=[jax.ShapeDtypeStruct((m, inner), bf16),
                   jax.ShapeDtypeStruct((batch, inner, M_DSTATE), f32)],
        scratch_shapes=[pltpu.VMEM((rows + 8, XBC_W), f32), pltpu.VMEM((rows, XBC_W), f32),
                        pltpu.VMEM((PAIRS, LANES, M_DSTATE), f32)],
        compiler_params=_cparams(("arbitrary", "arbitrary", "arbitrary")),
        name="ssd_prompt",
    )(*common, _tile3(tril), jnp.asarray(tril))
    n_sample = m - n_prompt
    per = c // dec_seq
    base = n_prompt // c
    tril = _ssd_consts(c, dec_seq)
    in_specs = specs(lambda jb, g: base + jb, c)
    in_specs += [
        pl.BlockSpec((per, M_CONV - 1, GROUP_W), lambda jb, g: (jb, 0, g)),
        pl.BlockSpec((per, M_CONV - 1, M_DSTATE), lambda jb, g: (jb, 0, xb_blk + g)),
        pl.BlockSpec((per, M_CONV - 1, M_DSTATE), lambda jb, g: (jb, 0, cb_blk + g)),
        pl.BlockSpec((per, GROUP_W, M_DSTATE), lambda jb, g: (jb, g, 0)),
        pl.BlockSpec(memory_space=pl.ANY),
    ]
    o, s_s = pl.pallas_call(
        functools.partial(_ssd_sample_kernel, c=c, seg=dec_seq),
        grid=(n_sample // c, M_GROUPS),
        in_specs=in_specs,
        out_specs=[pl.BlockSpec((c, GROUP_W), lambda jb, g: (base + jb, g)),
                   pl.BlockSpec((per, GROUP_W, M_DSTATE), lambda jb, g: (jb, g, 0))],
        out_shape=[jax.ShapeDtypeStruct((m, inner), bf16), jax.ShapeDtypeStruct(state.shape, f32)],
        scratch_shapes=[pltpu.VMEM((per, 16, XBC_W), f32)],
        input_output_aliases={len(in_specs) - 1: 0},
        compiler_params=_cparams(("arbitrary", "arbitrary")),
        name="ssd_sample",
    )(*common, _tile3(tril), jnp.asarray(tril), hist, hist, hist, state, o_p)
    return o, s_p, s_s


def kernel(x_prompt, x_sample, state_hgrn, state_ssm, state_conv, norm_mix, w_in, hg_lb_logits, hg_norm, conv_w, conv_b,
           dt_bias, a_log, d_skip, ssm_norm, w_branch_hg, w_branch_ssm, w_out, norm_ffn, w_ffn_gate, w_ffn_up,
           w_ffn_down, norm_final):
    batch, seq, d = x_prompt.shape
    dec_batch, dec_seq, _ = x_sample.shape
    n_prompt, n_sample = batch * seq, dec_batch * dec_seq
    hg_heads = state_hgrn.shape[2]
    kdim = hg_heads * HG_DK
    vdim = d
    inner = d
    m_heads = state_ssm.shape[2]
    conv_dim = conv_w.shape[2]
    hpg = m_heads // M_GROUPS

    xp2, xs2 = x_prompt.reshape(n_prompt, d), x_sample.reshape(n_sample, d)
    h = _rmsnorm([xp2, xs2], norm_mix[0], bf16, n_prompt)

    wt = jnp.swapaxes(w_in, 1, 2)[0]
    o_q, o_f, o_v, o_g, o_z, o_xbc = 0, kdim, 2 * kdim, 2 * kdim + vdim, 2 * kdim + 2 * vdim, 2 * kdim + 2 * vdim + inner
    o_dt = o_xbc + conv_dim
    o_gate = o_dt + m_heads
    scale = HG_DK ** -0.5
    ident = lambda a, c, t: (a[0],)
    silu = lambda a, c, t: (jax.nn.silu(a[0]),)
    wide = dict(tm=512, tn=2048, w_single=True)
    (q,) = _mm([h], [(wt, o_q, True)], kdim, lambda a, c, t: (a[0] * scale,), [f32], name="proj_q", **wide)
    log_f, k = _mm([h], [(wt, o_f, True)], kdim, _epi_forget, [f32, f32], cols=[hg_lb_logits], name="proj_f", **wide)
    (v,) = _mm([h], [(wt, o_v, True)], vdim, ident, [bf16], name="proj_v", **wide)
    (sg,) = _mm([h], [(wt, o_g, True)], vdim, silu, [f32], name="proj_g", **wide)
    (zs,) = _mm([h], [(wt, o_z, True)], inner, silu, [f32], name="proj_z", **wide)
    (xbc,) = _mm([h], [(wt, o_xbc, True)], conv_dim, ident, [f32], tm=512, tn=conv_dim // 2, w_single=True, name="proj_xbc")
    pad_heads = lambda p: jnp.pad(p.reshape(M_GROUPS, hpg, -1), ((0, 0), (0, LANES - hpg), (0, 0))).reshape(M_GROUPS * LANES, -1)
    (dt,) = _mm([h], [(pad_heads(wt[o_dt:o_gate]), 0, True)], M_GROUPS * LANES,
                lambda a, c, t: (_softplus(a[0] + c[0]),), [f32], cols=[pad_heads(dt_bias[0]).T], name="proj_dt")
    (gates,) = _mm([h], [(wt, o_gate, True)], 2 * d, lambda a, c, t: (jax.nn.sigmoid(a[0]),), [f32], name="proj_gate", **wide)

    o_hg, shp, shs = _hgrn(q, log_f, k, v, sg, hg_norm[0], state_hgrn[0], n_prompt, batch, seq, dec_seq)
    dsk = jnp.repeat(d_skip[0], M_HEADDIM).reshape(1, inner)
    o_m, smp, sms = _ssd(xbc, dt, zs, conv_w[0], conv_b[0].reshape(1, conv_dim), pad_heads(a_log[0]).T, dsk,
                         ssm_norm[0].reshape(1, inner), state_ssm[0].reshape(dec_batch, inner, M_DSTATE),
                         state_conv[0], n_prompt, batch, seq, dec_seq)

    tn = 512
    (merged,) = _mm([o_hg, o_m], [(w_branch_hg[0], 0, False), (w_branch_ssm[0], 0, False)], d,
                    lambda a, c, t: (t[0] * a[0] + t[1] * a[1],), [bf16],
                    tiles=[(gates, lambda j, i: (i, j)), (gates, lambda j, i: (i, j + d // tn))], tn=tn, name="merge")
    tm = 1024
    n_p_tiles = n_prompt // tm

    def add_x(a, c, t):
        return (jnp.where(pl.program_id(1) < n_p_tiles, t[0], t[1]) + a[0],)

    (x1,) = _mm([merged], [(w_out[0], 0, False)], d, add_x, [f32],
                tiles=[(xp2, lambda j, i: (jnp.minimum(i, n_p_tiles - 1), j)),
                       (xs2, lambda j, i: (jnp.maximum(i - n_p_tiles, 0), j))], tm=tm, tn=tn, name="out_proj")
    h2 = _rmsnorm([x1], norm_ffn[0], bf16, n_prompt)
    ffn = _ffn(h2, w_ffn_gate[0], w_ffn_up[0], w_ffn_down[0])
    y_p, y_s = _rmsnorm([x1], norm_final, f32, n_prompt, split_out=True, res=ffn)

    hist = M_CONV - 1
    new_conv_p = jnp.stack([xbc[(b + 1) * seq - hist:(b + 1) * seq] for b in range(batch)])[None]
    new_conv_s = xbc[n_prompt:].reshape(dec_batch, dec_seq, conv_dim)[:, dec_seq - hist:][None]
    return (y_p.reshape(batch, seq, d), y_s.reshape(dec_batch, dec_seq, d),
            shp[None], smp.reshape(1, batch, m_heads, M_HEADDIM, M_DSTATE), new_conv_p,
            shs[None], sms.reshape(1, dec_batch, m_heads, M_HEADDIM, M_DSTATE), new_conv_s)
```

```python
import functools

import numpy as np
import jax
import jax.numpy as jnp
from jax import lax
from jax.experimental import pallas as pl
from jax.experimental.pallas import tpu as pltpu

f32 = jnp.float32
bf16 = jnp.bfloat16

EPS = 1e-6
LANES = 128
CHUNK = 64
HG_DK = 128
M_HEADDIM = 64
M_DSTATE = 128
M_GROUPS = 4
M_CONV = 4
VMEM_LIMIT = 56 * 1024 * 1024


def _cparams(sem):
    return pltpu.CompilerParams(dimension_semantics=sem, vmem_limit_bytes=VMEM_LIMIT)


def _dot(a, b):
    return jnp.dot(a, b, preferred_element_type=f32)


def _dot_nt(a, b):
    return lax.dot_general(a, b, (((1,), (1,)), ((), ())), preferred_element_type=f32)


def _dot_tn(a, b):
    return lax.dot_general(a, b, (((0,), (0,)), ((), ())), preferred_element_type=f32)


def _split3(x):
    hi = x.astype(bf16)
    r1 = x - hi.astype(f32)
    mid = r1.astype(bf16)
    lo = (r1 - mid.astype(f32)).astype(bf16)
    return hi, mid, lo


def _tile3(m):
    return jnp.asarray(np.tile(m, (1, 3)), bf16)


def _dot3(m3, x):
    return _dot(m3, jnp.concatenate(_split3(x), axis=0))


def _mxu_operand(x, small):
    xb = x.astype(bf16)
    return xb.astype(f32) if small else xb


def _rmsnorm_kernel(*refs, n_in, n_out, split, has_res):
    x_refs, w_ref, o_refs = refs[:n_in], refs[n_in], refs[n_in + 1 + has_res:]

    def run(x_ref, o_ref):
        x = x_ref[...]
        if has_res:
            x = x + refs[n_in + 1][...]
        ms = jnp.mean(x * x, axis=-1, keepdims=True)
        o_ref[...] = (x * lax.rsqrt(ms + EPS) * w_ref[...]).astype(o_ref.dtype)

    if n_in == 1 and n_out == 1:
        run(x_refs[0], o_refs[0])
    else:
        i = pl.program_id(0)
        pl.when(i < split)(lambda: run(x_refs[0], o_refs[0]))
        pl.when(i >= split)(lambda: run(x_refs[-1], o_refs[-1]))


def _rmsnorm(xs, w, out_dtype, n_first, split_out=False, res=None, tm=512):
    d = xs[0].shape[1]
    m = sum(x.shape[0] for x in xs)
    split = n_first // tm
    first = lambda i: (jnp.minimum(i, split - 1), 0)
    second = lambda i: (jnp.maximum(i - split, 0), 0)
    whole = lambda i: (i, 0)
    blk = lambda fn: pl.BlockSpec((tm, d), fn)
    in_specs = [blk(whole)] if len(xs) == 1 else [blk(first), blk(second)]
    if split_out:
        out_specs = [blk(first), blk(second)]
        out_shape = [jax.ShapeDtypeStruct((n_first, d), out_dtype), jax.ShapeDtypeStruct((m - n_first, d), out_dtype)]
    else:
        out_specs = [blk(whole)]
        out_shape = [jax.ShapeDtypeStruct((m, d), out_dtype)]
    extra = [] if res is None else [res]
    outs = pl.pallas_call(
        functools.partial(_rmsnorm_kernel, n_in=len(xs), n_out=len(out_shape), split=split, has_res=len(extra)),
        grid=(m // tm,),
        in_specs=in_specs + [pl.BlockSpec((1, d), lambda i: (0, 0))] + [blk(whole) for _ in extra],
        out_specs=out_specs,
        out_shape=out_shape,
        compiler_params=_cparams(("arbitrary",)),
        name="rmsnorm",
    )(*xs, w.reshape(1, d), *extra)
    return outs if split_out else outs[0]


def _mm_kernel(*refs, a_parts, split, nd, nc, nt, no, epi, w_t):
    a_vals = []
    for parts in a_parts:
        if parts == 2:
            a_vals.append(jnp.where(pl.program_id(1) < split, refs[0][...], refs[1][...]))
        else:
            a_vals.append(refs[0][...])
        refs = refs[parts:]
    na = len(a_vals)
    w_refs = refs[:nd]
    c_refs = refs[nd:nd + nc]
    t_refs = refs[nd + nc:nd + nc + nt]
    o_refs = refs[nd + nc + nt:nd + nc + nt + no]
    wb_refs = refs[nd + nc + nt + no:]

    @pl.when(pl.program_id(1) == 0)
    def _():
        for w_ref, wb_ref in zip(w_refs, wb_refs):
            wb_ref[...] = w_ref[...].astype(bf16)

    accs = [(_dot_nt if t else _dot)(a_vals[min(i, na - 1)], wb_ref[...])
            for i, (wb_ref, t) in enumerate(zip(wb_refs, w_t))]
    outs = epi(accs, [c[...] for c in c_refs], [t[...] for t in t_refs])
    for o_ref, o in zip(o_refs, outs):
        o_ref[...] = o.astype(o_ref.dtype)


def _col_map(j, i, off):
    return (0, j + off)


def _tile_map(j, i, off):
    return (i, j + off)


def _row_elem_map(j, i, off, tn):
    return (pl.multiple_of(off + j * tn, 8), 0)


def _mm(a_list, w_list, n_cols, epi, out_dtypes, cols=(), tiles=(), tm=1024, tn=512, w_single=False, name="mm"):
    na, nd, nc, nt, no = len(a_list), len(w_list), len(cols), len(tiles), len(out_dtypes)
    assert na in (1, nd)
    a_list = [a if isinstance(a, tuple) else (a,) for a in a_list]
    m = sum(p.shape[0] for p in a_list[0])
    split = a_list[0][0].shape[0] // tm
    in_specs, a_flat = [], []
    for parts in a_list:
        k = parts[0].shape[1]
        if len(parts) == 2:
            in_specs.append(pl.BlockSpec((tm, k), lambda j, i: (jnp.minimum(i, split - 1), 0)))
            in_specs.append(pl.BlockSpec((tm, k), lambda j, i: (jnp.maximum(i - split, 0), 0)))
        else:
            in_specs.append(pl.BlockSpec((tm, k), lambda j, i: (i, 0)))
        a_flat += list(parts)
    scratch = []
    mode = dict(pipeline_mode=pl.Buffered(1)) if w_single else {}
    for w, off, transposed in w_list:
        if transposed:
            k = w.shape[1]
            assert off % 8 == 0
            in_specs.append(pl.BlockSpec((pl.Element(tn), pl.Element(k)), functools.partial(_row_elem_map, off=off, tn=tn), **mode))
            scratch.append(pltpu.VMEM((tn, k), bf16))
        else:
            k = w.shape[0]
            in_specs.append(pl.BlockSpec((k, tn), functools.partial(_col_map, off=off // tn), **mode))
            scratch.append(pltpu.VMEM((k, tn), bf16))
    in_specs += [pl.BlockSpec((c.shape[0], tn), functools.partial(_col_map, off=0)) for c in cols]
    in_specs += [pl.BlockSpec((tm, tn), fn) for _, fn in tiles]
    outs = pl.pallas_call(
        functools.partial(_mm_kernel, a_parts=tuple(len(p) for p in a_list), split=split, nd=nd, nc=nc, nt=nt, no=no,
                          epi=epi, w_t=tuple(t for _, _, t in w_list)),
        grid=(n_cols // tn, m // tm),
        in_specs=in_specs,
        out_specs=[pl.BlockSpec((tm, tn), functools.partial(_tile_map, off=0)) for _ in out_dtypes],
        out_shape=[jax.ShapeDtypeStruct((m, n_cols), dt) for dt in out_dtypes],
        scratch_shapes=scratch,
        compiler_params=_cparams(("arbitrary", "arbitrary")),
        name=name,
    )(*a_flat, *[w for w, _, _ in w_list], *cols, *[t for t, _ in tiles])
    return outs


def _ffn_kernel(h_ref, wg_ref, wu_ref, wd_ref, o_ref):
    @pl.when(pl.program_id(1) == 0)
    def _():
        o_ref[...] = jnp.zeros_like(o_ref)

    h = h_ref[...]
    gate = _dot(h, wg_ref[...].astype(bf16))
    up = _dot(h, wu_ref[...].astype(bf16))
    act = (jax.nn.silu(gate) * up).astype(bf16)
    o_ref[...] += _dot(act, wd_ref[...].astype(bf16))


def _ffn(h, w_gate, w_up, w_down, tm=1536, tf=256):
    m, d = h.shape
    hidden = w_gate.shape[1]
    return pl.pallas_call(
        _ffn_kernel,
        grid=(m // tm, hidden // tf),
        in_specs=[pl.BlockSpec((tm, d), lambda i, f: (i, 0), pipeline_mode=pl.Buffered(1)),
                  pl.BlockSpec((d, tf), lambda i, f: (0, f)),
                  pl.BlockSpec((d, tf), lambda i, f: (0, f)),
                  pl.BlockSpec((tf, d), lambda i, f: (f, 0))],
        out_specs=pl.BlockSpec((tm, d), lambda i, f: (i, 0)),
        out_shape=jax.ShapeDtypeStruct((m, d), f32),
        compiler_params=_cparams(("arbitrary", "arbitrary")),
        name="ffn",
    )(h, w_gate, w_up, w_down)


def _softplus(x):
    return jnp.maximum(x, 0.0) + jnp.log1p(jnp.exp(-jnp.abs(x)))


def _epi_forget(accs, cols, tiles):
    logits = cols[0]
    mx = jnp.max(logits, axis=0, keepdims=True)
    e = jnp.exp(logits - mx)
    lb = e[0:1, :] / jnp.sum(e, axis=0, keepdims=True)
    sig = jax.nn.sigmoid(accs[0])
    log_f = jnp.log(lb + (1.0 - lb) * sig)
    k = (1.0 - lb) * (1.0 - sig)
    return log_f, k


def _hgrn_consts(c, seg):
    nlev = int(np.log2(seg))
    mat = np.zeros(((nlev + 2) * c, c), np.float32)
    masks = np.zeros((nlev + 1, c, c), np.float32)
    for t in range(c):
        tl = t % seg
        base = t - tl
        for l in range(nlev):
            h = 1 << l
            pos = tl % (2 * h)
            ref = base + tl - pos + h - 1
            if pos >= h:
                mat[l * c + t, ref + 1:t + 1] = 1
            else:
                mat[l * c + t, t + 1:ref + 1] = 1
            for s in range(base, base + seg):
                sl = s % seg
                if sl // (2 * h) == tl // (2 * h) and pos >= h and sl % (2 * h) < h:
                    masks[l, t, s] = 1
        mat[nlev * c + t, base:t + 1] = 1
        mat[(nlev + 1) * c + t, t + 1:base + seg] = 1
        masks[nlev, t, t] = 1
    return mat, masks, nlev


def _hgrn_chunks(q_ref, k_ref, g_ref, v_ref, mat, masks, nlev, c):
    n_chunks = q_ref.shape[0] // c
    heads = q_ref.shape[1] // LANES
    rows = lambda ci: slice(ci * c, (ci + 1) * c)
    cols = lambda h: slice(h * LANES, (h + 1) * LANES)
    items = [(ci, h) for ci in range(n_chunks) for h in range(heads)]
    e_all = [jnp.exp(_dot3(mat, g_ref[rows(ci), :])) for ci in range(n_chunks)]
    q = {(ci, h): q_ref[rows(ci), cols(h)] for ci, h in items}
    k = {(ci, h): k_ref[rows(ci), cols(h)] for ci, h in items}
    v = {(ci, h): v_ref[rows(ci), cols(h)].astype(bf16) for ci, h in items}
    s = {it: jnp.where(masks[nlev], _dot_nt(q[it].astype(bf16), k[it].astype(bf16)), 0.0) for it in items}
    for l in range(nlev):
        for ci, h in items:
            e = e_all[ci][l * c:(l + 1) * c, cols(h)]
            s[ci, h] = jnp.where(masks[l], _dot_nt((q[ci, h] * e).astype(bf16), (k[ci, h] * e).astype(bf16)), s[ci, h])
    res = {}
    for ci, h in items:
        o = _dot(s[ci, h].astype(bf16), v[ci, h])
        eb = e_all[ci][nlev * c:(nlev + 1) * c, cols(h)]
        er = e_all[ci][(nlev + 1) * c:, cols(h)]
        res[ci, h] = (o, q[ci, h] * eb, k[ci, h] * er, eb, v[ci, h])
    return res


def _hgrn_out(o, sg, nw):
    ms = jnp.mean(o * o, axis=-1, keepdims=True)
    return o * lax.rsqrt(ms + EPS) * nw * sg


def _hgrn_prompt_kernel(q_ref, g_ref, k_ref, v_ref, sg_ref, nw_ref, mat_ref, mask_ref, o_ref, s_ref, st_ref,
                        *, c, nlev, n_chunks):
    t_blk = pl.program_id(2)

    @pl.when(t_blk == 0)
    def _():
        st_ref[...] = jnp.zeros_like(st_ref)

    masks = [mask_ref[l] > 0.5 for l in range(nlev + 1)]
    mat = mat_ref[...]
    nw = nw_ref[...]

    res = _hgrn_chunks(q_ref, k_ref, g_ref, v_ref, mat, masks, nlev, c)
    upd = {it: _dot_tn(r[4], r[2].astype(bf16)) for it, r in res.items()}
    for h in range(st_ref.shape[0]):
        cs = slice(h * LANES, (h + 1) * LANES)
        st = st_ref[h]
        for ci in range(n_chunks):
            rows = slice(ci * c, (ci + 1) * c)
            o, qe, _, eb, _ = res[ci, h]
            o = o + _dot_nt(qe.astype(bf16), st.astype(bf16))
            st = st * eb[c - 1:c, :] + upd[ci, h]
            o_ref[rows, cs] = _hgrn_out(o, sg_ref[rows, cs], nw[:, cs]).astype(o_ref.dtype)
        st_ref[h] = st

    @pl.when(t_blk == pl.num_programs(2) - 1)
    def _():
        for h in range(st_ref.shape[0]):
            s_ref[0, h] = st_ref[h].T


def _hgrn_sample_kernel(q_ref, g_ref, k_ref, v_ref, sg_ref, nw_ref, mat_ref, mask_ref, s_in_ref,
                        o_ref, s_out_ref, *, c, nlev, seg):
    masks = [mask_ref[l] > 0.5 for l in range(nlev + 1)]
    mat = mat_ref[...]
    nw = nw_ref[...]
    small = seg < 16
    per = c // seg
    res = _hgrn_chunks(q_ref, k_ref, g_ref, v_ref, mat, masks, nlev, c)
    for (ci, h), (o, qe, kd, eb, v) in res.items():
        rows = slice(ci * c, (ci + 1) * c)
        cs = slice(h * LANES, (h + 1) * LANES)
        v = v.astype(f32)
        eb_t = jnp.concatenate([eb, eb], axis=0).T
        parts = []
        for j in range(per):
            sl = slice(j * seg, (j + 1) * seg)
            s0 = s_in_ref[ci * per + j, h]
            parts.append(_dot(_mxu_operand(qe[sl], small), _mxu_operand(s0, small)))
            last = j * seg + seg - 1
            upd = _dot_tn(_mxu_operand(kd[sl], small), _mxu_operand(v[sl], small))
            s_out_ref[ci * per + j, h] = s0 * eb_t[:, last:last + 1] + upd
        o = o + jnp.concatenate(parts, axis=0)
        o_ref[rows, cs] = _hgrn_out(o, sg_ref[rows, cs], nw[:, cs]).astype(o_ref.dtype)


def _hgrn(q, g, k, v, sg, nw, state, n_prompt, batch, seq, dec_seq):
    m, width = q.shape
    heads = width // LANES
    nw = nw.reshape(1, width)
    c = CHUNK
    hp = 2
    hw = hp * LANES
    tc = 512
    nt = seq // tc
    mat, masks, nlev = _hgrn_consts(c, c)
    row_spec = pl.BlockSpec((tc, hw), lambda b, h, t: (b * nt + t, h))
    const2 = lambda shape: pl.BlockSpec(shape, lambda b, h, t: (0,) * len(shape))
    o_p, s_p = pl.pallas_call(
        functools.partial(_hgrn_prompt_kernel, c=c, nlev=nlev, n_chunks=tc // c),
        grid=(batch, heads // hp, nt),
        in_specs=[row_spec] * 5 + [pl.BlockSpec((1, hw), lambda b, h, t: (0, h)), const2((mat.shape[0], 3 * c)), const2(masks.shape)],
        out_specs=[row_spec, pl.BlockSpec((1, hp, HG_DK, LANES), lambda b, h, t: (b, h, 0, 0))],
        out_shape=[jax.ShapeDtypeStruct((n_prompt, width), bf16),
                   jax.ShapeDtypeStruct((batch, heads, HG_DK, LANES), f32)],
        scratch_shapes=[pltpu.VMEM((hp, LANES, HG_DK), f32)],
        compiler_params=_cparams(("arbitrary", "arbitrary", "arbitrary")),
        name="hgrn_prompt",
    )(q, g, k, v, sg, nw, _tile3(mat), jnp.asarray(masks))
    n_sample = m - n_prompt
    rows = 2 * c
    per = rows // dec_seq
    mat, masks, nlev = _hgrn_consts(c, dec_seq)
    base = n_prompt // rows
    row_spec = pl.BlockSpec((rows, hw), lambda jb, h: (base + jb, h))
    st_spec = pl.BlockSpec((per, hp, HG_DK, LANES), lambda jb, h: (jb, h, 0, 0))
    const2 = lambda shape: pl.BlockSpec(shape, lambda jb, h: (0,) * len(shape))
    o_s, s_s = pl.pallas_call(
        functools.partial(_hgrn_sample_kernel, c=c, nlev=nlev, seg=dec_seq),
        grid=(n_sample // rows, heads // hp),
        in_specs=[row_spec] * 5 + [pl.BlockSpec((1, hw), lambda jb, h: (0, h)), const2((mat.shape[0], 3 * c)),
                                   const2(masks.shape), st_spec],
        out_specs=[pl.BlockSpec((rows, hw), lambda jb, h: (jb, h)), st_spec],
        out_shape=[jax.ShapeDtypeStruct((n_sample, width), bf16), jax.ShapeDtypeStruct(state.shape, f32)],
        compiler_params=_cparams(("arbitrary", "arbitrary")),
        name="hgrn_sample",
    )(q, g, k, v, sg, nw, _tile3(mat), jnp.asarray(masks), state)
    return (o_p, o_s), s_p, s_s


GROUP_W = 512
PAIRS = GROUP_W // LANES
XBC_W = GROUP_W + 2 * M_DSTATE


def _ssd_consts(c, seg):
    t = np.arange(c)
    same = (t[:, None] // seg) == (t[None, :] // seg)
    tril = (same & (t[None, :] <= t[:, None])).astype(np.float32)
    return tril


def _ssd_chunks(chunks, a_row, dsk, nw, lc, tril, get_state, set_state, c, seg, state_t):
    nseg = c // seg
    small = seg < 16
    assert 2 * c == LANES and 2 * M_HEADDIM == LANES and not (state_t and nseg > 1)
    lo_half = lax.broadcasted_iota(jnp.int32, (c, LANES), 1) < M_HEADDIM
    lo_row = lo_half[0:1, :]
    n = len(chunks)
    items = [(ci, p) for ci in range(n) for p in range(PAIRS)]
    pc = lambda p: slice(p * LANES, (p + 1) * LANES)
    acum = [_dot3(lc, dt * a_row) for _, _, _, dt, _ in chunks]
    bmb = [bm.astype(bf16) for _, bm, _, _, _ in chunks]
    cmb = [cm.astype(bf16) for _, _, cm, _, _ in chunks]
    cb2 = [_dot_nt(cmb[ci], jnp.concatenate([bmb[ci], bmb[ci]], axis=0)) for ci in range(n)]
    acum_t = [jnp.concatenate([a, a], axis=0).T for a in acum]
    dt_t = [jnp.concatenate([ch[3], ch[3]], axis=0).T for ch in chunks]
    if state_t:
        bm_t = [ch[1].T.astype(bf16) for ch in chunks]
    else:
        ea_t = [jnp.exp(a[0:8, 0:c]) for a in acum_t]
    y, xw, ea = {}, {}, {}
    for ci, p in items:
        h0, h1 = 2 * p, 2 * p + 1
        xp, dt = chunks[ci][0][:, pc(p)], chunks[ci][3]
        acp = jnp.where(lo_half, acum[ci][:, h0:h0 + 1], acum[ci][:, h1:h1 + 1])
        dtp = jnp.where(lo_half, dt[:, h0:h0 + 1], dt[:, h1:h1 + 1])
        a_src = jnp.where(lo_row, acum_t[ci][h0:h0 + 1, :], acum_t[ci][h1:h1 + 1, :])
        dt_src = jnp.where(lo_row, dt_t[ci][h0:h0 + 1, :], dt_t[ci][h1:h1 + 1, :])
        lm = jnp.exp(jnp.where(tril, acp - a_src, -1e30))
        sc = (cb2[ci] * lm * dt_src).astype(bf16)
        x_blk = jnp.concatenate([jnp.where(lo_half, xp, 0.0), jnp.where(lo_half, 0.0, xp)], axis=0).astype(bf16)
        y[ci, p] = _dot(sc, x_blk)
        if nseg == 1:
            alast = acp[c - 1:c, :]
        else:
            alast = jnp.concatenate(
                [jnp.broadcast_to(acp[j * seg + seg - 1:j * seg + seg, :], (seg, LANES)) for j in range(nseg)], axis=0)
        xw[ci, p] = xp * (jnp.exp(alast - acp) * dtp)
        ea[ci, p] = jnp.exp(acp)
    cs = {}
    if state_t:
        upd = {(ci, p): _dot(bm_t[ci], xw[ci, p].astype(bf16)) for ci, p in items}
        for p in range(PAIRS):
            st = get_state(0, 0, p)
            for ci in range(n):
                cs[ci, p] = _dot(cmb[ci], st.astype(bf16))
                st = st * ea[ci, p][c - 1:c, :] + upd[ci, p]
            set_state(0, 0, p, st)
    else:
        for ci, p in items:
            h0, h1 = 2 * p, 2 * p + 1
            bm, cm = chunks[ci][1], chunks[ci][2]
            parts = []
            for j in range(nseg):
                sl = slice(j * seg, (j + 1) * seg)
                s0 = get_state(ci, j, p)
                parts.append(_dot_nt(_mxu_operand(cm[sl], small), _mxu_operand(s0, small)))
                upd = _dot_tn(_mxu_operand(xw[ci, p][sl], small), _mxu_operand(bm[sl], small))
                last = j * seg + seg - 1
                decay = jnp.concatenate(
                    [jnp.broadcast_to(ea_t[ci][h0:h0 + 1, last:last + 1], (M_HEADDIM, LANES)),
                     jnp.broadcast_to(ea_t[ci][h1:h1 + 1, last:last + 1], (M_HEADDIM, LANES))], axis=0)
                set_state(ci, j, p, s0 * decay + upd)
            cs[ci, p] = parts[0] if nseg == 1 else jnp.concatenate(parts, axis=0)
    outs = []
    for ci in range(n):
        xs, zs = chunks[ci][0], chunks[ci][4]
        ssq = jnp.zeros((c, 1), f32)
        ys = []
        for p in range(PAIRS):
            yp = (y[ci, p] + cs[ci, p] * ea[ci, p] + dsk[:, pc(p)] * xs[:, pc(p)]) * zs[:, pc(p)]
            ssq = ssq + jnp.sum(yp * yp, axis=-1, keepdims=True)
            ys.append(yp)
        scale = lax.rsqrt(ssq * (1.0 / GROUP_W) + EPS)
        outs.append([ys[p] * scale * nw[:, pc(p)] for p in range(PAIRS)])
    return outs


def _conv_taps(xpad_ref, lead, rows, w, b):
    acc = None
    for j in range(M_CONV):
        term = xpad_ref[lead + (slice(5 + j + rows[0], 5 + j + rows[1]), slice(None))] * w[j:j + 1, :]
        acc = term if acc is None else acc + term
    return jax.nn.silu(b + acc)


def _ssd_prompt_kernel(xr_ref, br_ref, cr_ref, dt_ref, zs_ref, wx_ref, wb_ref, wc_ref, bx_ref, bb_ref, bc_ref,
                       alog_ref, dsk_ref, nw_ref, lc_ref, tril_ref, o_ref, s_ref, xpad_ref, xc_ref, st_ref,
                       *, c, rows):
    t_blk = pl.program_id(2)

    @pl.when(t_blk == 0)
    def _():
        st_ref[...] = jnp.zeros_like(st_ref)
        xpad_ref[0:8, :] = jnp.zeros((8, XBC_W), f32)

    @pl.when(t_blk > 0)
    def _():
        xpad_ref[0:8, :] = xpad_ref[rows:rows + 8, :]

    xpad_ref[8:8 + rows, 0:GROUP_W] = xr_ref[...]
    xpad_ref[8:8 + rows, GROUP_W:GROUP_W + M_DSTATE] = br_ref[...]
    xpad_ref[8:8 + rows, GROUP_W + M_DSTATE:XBC_W] = cr_ref[...]
    w = jnp.concatenate([wx_ref[...], wb_ref[...], wc_ref[...]], axis=1)
    b = jnp.concatenate([bx_ref[...], bb_ref[...], bc_ref[...]], axis=1)
    for i in range(rows // c):
        xc_ref[i * c:(i + 1) * c, :] = _conv_taps(xpad_ref, (), (i * c, (i + 1) * c), w, b)

    a_row = -jnp.exp(alog_ref[...])
    dsk = dsk_ref[...]
    nw = nw_ref[...]
    lc = lc_ref[...]
    tril = tril_ref[...] > 0.5

    def get_state(ci, j, p):
        return st_ref[p]

    def set_state(ci, j, p, val):
        st_ref[p] = val

    rs = lambda ci: slice(ci * c, (ci + 1) * c)
    chunks = [(xc_ref[rs(ci), 0:GROUP_W], xc_ref[rs(ci), GROUP_W:GROUP_W + M_DSTATE], xc_ref[rs(ci), GROUP_W + M_DSTATE:XBC_W],
               dt_ref[rs(ci), :], zs_ref[rs(ci), :]) for ci in range(rows // c)]
    outs = _ssd_chunks(chunks, a_row, dsk, nw, lc, tril, get_state, set_state, c, c, state_t=True)
    for ci, out in enumerate(outs):
        for p in range(PAIRS):
            o_ref[rs(ci), p * LANES:(p + 1) * LANES] = out[p].astype(o_ref.dtype)

    @pl.when(t_blk == pl.num_programs(2) - 1)
    def _():
        for p in range(PAIRS):
            s_ref[0, p * LANES:(p + 1) * LANES, :] = st_ref[p].T


def _ssd_sample_kernel(xr_ref, br_ref, cr_ref, dt_ref, zs_ref, wx_ref, wb_ref, wc_ref, bx_ref, bb_ref, bc_ref,
                       alog_ref, dsk_ref, nw_ref, lc_ref, tril_ref, hx_ref, hb_ref, hc_ref, s_in_ref,
                       o_ref, s_out_ref, xpad_ref, *, c, seg):
    nseg = c // seg
    w = jnp.concatenate([wx_ref[...], wb_ref[...], wc_ref[...]], axis=1)
    b = jnp.concatenate([bx_ref[...], bb_ref[...], bc_ref[...]], axis=1)
    conv = []
    for j in range(nseg):
        sl = slice(j * seg, (j + 1) * seg)
        xpad_ref[j, 5:8, 0:GROUP_W] = hx_ref[j]
        xpad_ref[j, 5:8, GROUP_W:GROUP_W + M_DSTATE] = hb_ref[j]
        xpad_ref[j, 5:8, GROUP_W + M_DSTATE:XBC_W] = hc_ref[j]
        xpad_ref[j, 8:8 + seg, 0:GROUP_W] = xr_ref[sl, :]
        xpad_ref[j, 8:8 + seg, GROUP_W:GROUP_W + M_DSTATE] = br_ref[sl, :]
        xpad_ref[j, 8:8 + seg, GROUP_W + M_DSTATE:XBC_W] = cr_ref[sl, :]
        conv.append(_conv_taps(xpad_ref, (j,), (0, seg), w, b))
    xc = jnp.concatenate(conv, axis=0)

    def get_state(ci, j, p):
        return s_in_ref[j, p * LANES:(p + 1) * LANES, :]

    def set_state(ci, j, p, val):
        s_out_ref[j, p * LANES:(p + 1) * LANES, :] = val

    chunks = [(xc[:, 0:GROUP_W], xc[:, GROUP_W:GROUP_W + M_DSTATE], xc[:, GROUP_W + M_DSTATE:XBC_W], dt_ref[...], zs_ref[...])]
    (out,) = _ssd_chunks(chunks, -jnp.exp(alog_ref[...]), dsk_ref[...], nw_ref[...], lc_ref[...],
                         tril_ref[...] > 0.5, get_state, set_state, c, seg, state_t=False)
    for p in range(PAIRS):
        o_ref[:, p * LANES:(p + 1) * LANES] = out[p].astype(o_ref.dtype)


def _ssd(xbc, dt, zs, conv_w, conv_b, alog_p, dsk, nw, state, hist, n_prompt, batch, seq, dec_seq):
    m = xbc.shape[0]
    inner = zs.shape[1]
    c = CHUNK
    xb_blk = inner // M_DSTATE
    cb_blk = xb_blk + M_GROUPS

    def specs(row_map, nrow):
        def rm(fn):
            return lambda *ix: fn(row_map(*ix), ix[1])
        zero = lambda fn: (lambda *ix: fn(0, ix[1]))
        return [
            pl.BlockSpec((nrow, GROUP_W), rm(lambda r, g: (r, g))),
            pl.BlockSpec((nrow, M_DSTATE), rm(lambda r, g: (r, xb_blk + g))),
            pl.BlockSpec((nrow, M_DSTATE), rm(lambda r, g: (r, cb_blk + g))),
            pl.BlockSpec((nrow, LANES), rm(lambda r, g: (r, g))),
            pl.BlockSpec((nrow, GROUP_W), rm(lambda r, g: (r, g))),
            pl.BlockSpec((M_CONV, GROUP_W), zero(lambda r, g: (0, g))),
            pl.BlockSpec((M_CONV, M_DSTATE), zero(lambda r, g: (0, xb_blk + g))),
            pl.BlockSpec((M_CONV, M_DSTATE), zero(lambda r, g: (0, cb_blk + g))),
            pl.BlockSpec((1, GROUP_W), zero(lambda r, g: (0, g))),
            pl.BlockSpec((1, M_DSTATE), zero(lambda r, g: (0, xb_blk + g))),
            pl.BlockSpec((1, M_DSTATE), zero(lambda r, g: (0, cb_blk + g))),
            pl.BlockSpec((1, LANES), zero(lambda r, g: (0, g))),
            pl.BlockSpec((1, GROUP_W), zero(lambda r, g: (0, g))),
            pl.BlockSpec((1, GROUP_W), zero(lambda r, g: (0, g))),
            pl.BlockSpec((c, 3 * c), zero(lambda r, g: (0, 0))),
            pl.BlockSpec((c, 2 * c), zero(lambda r, g: (0, 0))),
        ]

    common = (xbc, xbc, xbc, dt, zs, conv_w, conv_w, conv_w, conv_b, conv_b, conv_b, alog_p, dsk, nw)
    rows = 512
    nt = seq // rows
    tril = _ssd_consts(c, c)
    in_specs = specs(lambda b, g, t: b * nt + t, rows)
    o_p, s_p = pl.pallas_call(
        functools.partial(_ssd_prompt_kernel, c=c, rows=rows),
        grid=(batch, M_GROUPS, nt),
        in_specs=in_specs,
        out_specs=[pl.BlockSpec((rows, GROUP_W), lambda b, g, t: (b * nt + t, g)),
                   pl.BlockSpec((1, GROUP_W, M_DSTATE), lambda b, g, t: (b, g, 0))],
        out_shape=[jax.ShapeDtypeStruct((n_prompt, inner), bf16),
                   jax.ShapeDtypeStruct((batch, inner, M_DSTATE), f32)],
        scratch_shapes=[pltpu.VMEM((rows + 8, XBC_W), f32), pltpu.VMEM((rows, XBC_W), f32),
                        pltpu.VMEM((PAIRS, LANES, M_DSTATE), f32)],
        compiler_params=_cparams(("arbitrary", "arbitrary", "arbitrary")),
        name="ssd_prompt",
    )(*common, _tile3(tril), jnp.asarray(np.tile(tril, (1, 2))))
    n_sample = m - n_prompt
    per = c // dec_seq
    base = n_prompt // c
    tril = _ssd_consts(c, dec_seq)
    in_specs = specs(lambda jb, g: base + jb, c)
    in_specs += [
        pl.BlockSpec((per, M_CONV - 1, GROUP_W), lambda jb, g: (jb, 0, g)),
        pl.BlockSpec((per, M_CONV - 1, M_DSTATE), lambda jb, g: (jb, 0, xb_blk + g)),
        pl.BlockSpec((per, M_CONV - 1, M_DSTATE), lambda jb, g: (jb, 0, cb_blk + g)),
        pl.BlockSpec((per, GROUP_W, M_DSTATE), lambda jb, g: (jb, g, 0)),
    ]
    o_s, s_s = pl.pallas_call(
        functools.partial(_ssd_sample_kernel, c=c, seg=dec_seq),
        grid=(n_sample // c, M_GROUPS),
        in_specs=in_specs,
        out_specs=[pl.BlockSpec((c, GROUP_W), lambda jb, g: (jb, g)),
                   pl.BlockSpec((per, GROUP_W, M_DSTATE), lambda jb, g: (jb, g, 0))],
        out_shape=[jax.ShapeDtypeStruct((n_sample, inner), bf16), jax.ShapeDtypeStruct(state.shape, f32)],
        scratch_shapes=[pltpu.VMEM((per, 16, XBC_W), f32)],
        compiler_params=_cparams(("arbitrary", "arbitrary")),
        name="ssd_sample",
    )(*common, _tile3(tril), jnp.asarray(np.tile(tril, (1, 2))), hist, hist, hist, state)
    return (o_p, o_s), s_p, s_s


def kernel(x_prompt, x_sample, state_hgrn, state_ssm, state_conv, norm_mix, w_in, hg_lb_logits, hg_norm, conv_w, conv_b,
           dt_bias, a_log, d_skip, ssm_norm, w_branch_hg, w_branch_ssm, w_out, norm_ffn, w_ffn_gate, w_ffn_up,
           w_ffn_down, norm_final):
    batch, seq, d = x_prompt.shape
    dec_batch, dec_seq, _ = x_sample.shape
    n_prompt, n_sample = batch * seq, dec_batch * dec_seq
    hg_heads = state_hgrn.shape[2]
    kdim = hg_heads * HG_DK
    vdim = d
    inner = d
    m_heads = state_ssm.shape[2]
    conv_dim = conv_w.shape[2]
    hpg = m_heads // M_GROUPS

    xp2, xs2 = x_prompt.reshape(n_prompt, d), x_sample.reshape(n_sample, d)
    h = _rmsnorm([xp2, xs2], norm_mix[0], bf16, n_prompt)

    wt = jnp.swapaxes(w_in, 1, 2)[0]
    o_q, o_f, o_v, o_g, o_z, o_xbc = 0, kdim, 2 * kdim, 2 * kdim + vdim, 2 * kdim + 2 * vdim, 2 * kdim + 2 * vdim + inner
    o_dt = o_xbc + conv_dim
    o_gate = o_dt + m_heads
    scale = HG_DK ** -0.5
    ident = lambda a, c, t: (a[0],)
    silu = lambda a, c, t: (jax.nn.silu(a[0]),)
    wide = dict(tm=512, tn=2048, w_single=True)
    (q,) = _mm([h], [(wt, o_q, True)], kdim, lambda a, c, t: (a[0] * scale,), [f32], name="proj_q", **wide)
    log_f, k = _mm([h], [(wt, o_f, True)], kdim, _epi_forget, [f32, f32], cols=[hg_lb_logits], name="proj_f", **wide)
    (v,) = _mm([h], [(wt, o_v, True)], vdim, ident, [bf16], name="proj_v", **wide)
    (sg,) = _mm([h], [(wt, o_g, True)], vdim, silu, [f32], name="proj_g", **wide)
    (zs,) = _mm([h], [(wt, o_z, True)], inner, silu, [f32], name="proj_z", **wide)
    (xbc,) = _mm([h], [(wt, o_xbc, True)], conv_dim, ident, [f32], tm=512, tn=conv_dim // 2, w_single=True, name="proj_xbc")
    pad_heads = lambda p: jnp.pad(p.reshape(M_GROUPS, hpg, -1), ((0, 0), (0, LANES - hpg), (0, 0))).reshape(M_GROUPS * LANES, -1)
    (dt,) = _mm([h], [(pad_heads(wt[o_dt:o_gate]), 0, True)], M_GROUPS * LANES,
                lambda a, c, t: (_softplus(a[0] + c[0]),), [f32], cols=[pad_heads(dt_bias[0]).T], name="proj_dt")
    (gates,) = _mm([h], [(wt, o_gate, True)], 2 * d, lambda a, c, t: (jax.nn.sigmoid(a[0]),), [f32], name="proj_gate", **wide)

    o_hg, shp, shs = _hgrn(q, log_f, k, v, sg, hg_norm[0], state_hgrn[0], n_prompt, batch, seq, dec_seq)
    dsk = jnp.repeat(d_skip[0], M_HEADDIM).reshape(1, inner)
    o_m, smp, sms = _ssd(xbc, dt, zs, conv_w[0], conv_b[0].reshape(1, conv_dim), pad_heads(a_log[0]).T, dsk,
                         ssm_norm[0].reshape(1, inner), state_ssm[0].reshape(dec_batch, inner, M_DSTATE),
                         state_conv[0], n_prompt, batch, seq, dec_seq)

    tn = 1024
    (merged,) = _mm([o_hg, o_m], [(w_branch_hg[0], 0, False), (w_branch_ssm[0], 0, False)], d,
                    lambda a, c, t: (t[0] * a[0] + t[1] * a[1],), [bf16],
                    tiles=[(gates, lambda j, i: (i, j)), (gates, lambda j, i: (i, j + d // tn))],
                    tm=512, tn=tn, w_single=True, name="merge")
    tm, tn = 1024, 512
    n_p_tiles = n_prompt // tm

    def add_x(a, c, t):
        return (jnp.where(pl.program_id(1) < n_p_tiles, t[0], t[1]) + a[0],)

    (x1,) = _mm([merged], [(w_out[0], 0, False)], d, add_x, [f32],
                tiles=[(xp2, lambda j, i: (jnp.minimum(i, n_p_tiles - 1), j)),
                       (xs2, lambda j, i: (jnp.maximum(i - n_p_tiles, 0), j))], tm=tm, tn=tn, name="out_proj")
    h2 = _rmsnorm([x1], norm_ffn[0], bf16, n_prompt)
    ffn = _ffn(h2, w_ffn_gate[0], w_ffn_up[0], w_ffn_down[0])
    y_p, y_s = _rmsnorm([x1], norm_final, f32, n_prompt, split_out=True, res=ffn)

    hist = M_CONV - 1
    new_conv_p = jnp.stack([xbc[(b + 1) * seq - hist:(b + 1) * seq] for b in range(batch)])[None]
    new_conv_s = xbc[n_prompt:].reshape(dec_batch, dec_seq, conv_dim)[:, dec_seq - hist:][None]
    return (y_p.reshape(batch, seq, d), y_s.reshape(dec_batch, dec_seq, d),
            shp[None], smp.reshape(1, batch, m_heads, M_HEADDIM, M_DSTATE), new_conv_p,
            shs[None], sms.reshape(1, dec_batch, m_heads, M_HEADDIM, M_DSTATE), new_conv_s)
```

```python
import functools

import numpy as np
import jax
import jax.numpy as jnp
from jax import lax
from jax.experimental import pallas as pl
from jax.experimental.pallas import tpu as pltpu

f32 = jnp.float32
bf16 = jnp.bfloat16

EPS = 1e-6
LANES = 128
CHUNK = 64
HG_DK = 128
M_HEADDIM = 64
M_DSTATE = 128
M_GROUPS = 4
M_CONV = 4
VMEM_LIMIT = 56 * 1024 * 1024


def _cparams(sem):
    return pltpu.CompilerParams(dimension_semantics=sem, vmem_limit_bytes=VMEM_LIMIT)


def _dot(a, b):
    return jnp.dot(a, b, preferred_element_type=f32)


def _dot_nt(a, b):
    return lax.dot_general(a, b, (((1,), (1,)), ((), ())), preferred_element_type=f32)


def _dot_tn(a, b):
    return lax.dot_general(a, b, (((0,), (0,)), ((), ())), preferred_element_type=f32)


def _split(x, n):
    pieces = []
    for _ in range(n - 1):
        p = x.astype(bf16)
        pieces.append(p)
        x = x - p.astype(f32)
    return pieces + [x.astype(bf16)]


def _tile3(m, n=3):
    return jnp.asarray(np.tile(m, (1, n)), bf16)


def _dot3(mn, x):
    return _dot(mn, jnp.concatenate(_split(x, mn.shape[1] // x.shape[0]), axis=0))


def _mxu_operand(x, small):
    xb = x.astype(bf16)
    return xb.astype(f32) if small else xb


def _rmsnorm_kernel(*refs, n_in, n_out, split, has_res):
    x_refs, w_ref, o_refs = refs[:n_in], refs[n_in], refs[n_in + 1 + has_res:]

    def run(x_ref, o_ref):
        x = x_ref[...]
        if has_res:
            x = x + refs[n_in + 1][...]
        ms = jnp.mean(x * x, axis=-1, keepdims=True)
        o_ref[...] = (x * lax.rsqrt(ms + EPS) * w_ref[...]).astype(o_ref.dtype)

    if n_in == 1 and n_out == 1:
        run(x_refs[0], o_refs[0])
    else:
        i = pl.program_id(0)
        pl.when(i < split)(lambda: run(x_refs[0], o_refs[0]))
        pl.when(i >= split)(lambda: run(x_refs[-1], o_refs[-1]))


def _rmsnorm(xs, w, out_dtype, n_first, split_out=False, res=None, tm=512):
    d = xs[0].shape[1]
    m = sum(x.shape[0] for x in xs)
    split = n_first // tm
    first = lambda i: (jnp.minimum(i, split - 1), 0)
    second = lambda i: (jnp.maximum(i - split, 0), 0)
    whole = lambda i: (i, 0)
    blk = lambda fn: pl.BlockSpec((tm, d), fn)
    in_specs = [blk(whole)] if len(xs) == 1 else [blk(first), blk(second)]
    if split_out:
        out_specs = [blk(first), blk(second)]
        out_shape = [jax.ShapeDtypeStruct((n_first, d), out_dtype), jax.ShapeDtypeStruct((m - n_first, d), out_dtype)]
    else:
        out_specs = [blk(whole)]
        out_shape = [jax.ShapeDtypeStruct((m, d), out_dtype)]
    extra = [] if res is None else [res]
    outs = pl.pallas_call(
        functools.partial(_rmsnorm_kernel, n_in=len(xs), n_out=len(out_shape), split=split, has_res=len(extra)),
        grid=(m // tm,),
        in_specs=in_specs + [pl.BlockSpec((1, d), lambda i: (0, 0))] + [blk(whole) for _ in extra],
        out_specs=out_specs,
        out_shape=out_shape,
        compiler_params=_cparams(("arbitrary",)),
        name="rmsnorm",
    )(*xs, w.reshape(1, d), *extra)
    return outs if split_out else outs[0]


def _mm_kernel(*refs, a_parts, split, nd, nc, nt, no, epi, w_t):
    a_vals = []
    for parts in a_parts:
        if parts == 2:
            a_vals.append(jnp.where(pl.program_id(1) < split, refs[0][...], refs[1][...]))
        else:
            a_vals.append(refs[0][...])
        refs = refs[parts:]
    na = len(a_vals)
    w_refs = refs[:nd]
    c_refs = refs[nd:nd + nc]
    t_refs = refs[nd + nc:nd + nc + nt]
    o_refs = refs[nd + nc + nt:nd + nc + nt + no]
    wb_refs = refs[nd + nc + nt + no:]

    @pl.when(pl.program_id(1) == 0)
    def _():
        for w_ref, wb_ref in zip(w_refs, wb_refs):
            wb_ref[...] = w_ref[...].astype(bf16)

    accs = [(_dot_nt if t else _dot)(a_vals[min(i, na - 1)], wb_ref[...])
            for i, (wb_ref, t) in enumerate(zip(wb_refs, w_t))]
    outs = epi(accs, [c[...] for c in c_refs], [t[...] for t in t_refs])
    for o_ref, o in zip(o_refs, outs):
        o_ref[...] = o.astype(o_ref.dtype)


def _col_map(j, i, off):
    return (0, j + off)


def _tile_map(j, i, off):
    return (i, j + off)


def _row_elem_map(j, i, off, tn):
    return (pl.multiple_of(off + j * tn, 8), 0)


def _mm(a_list, w_list, n_cols, epi, out_dtypes, cols=(), tiles=(), tm=1024, tn=512, w_single=False, name="mm"):
    na, nd, nc, nt, no = len(a_list), len(w_list), len(cols), len(tiles), len(out_dtypes)
    assert na in (1, nd)
    a_list = [a if isinstance(a, tuple) else (a,) for a in a_list]
    m = sum(p.shape[0] for p in a_list[0])
    split = a_list[0][0].shape[0] // tm
    in_specs, a_flat = [], []
    for parts in a_list:
        k = parts[0].shape[1]
        if len(parts) == 2:
            in_specs.append(pl.BlockSpec((tm, k), lambda j, i: (jnp.minimum(i, split - 1), 0)))
            in_specs.append(pl.BlockSpec((tm, k), lambda j, i: (jnp.maximum(i - split, 0), 0)))
        else:
            in_specs.append(pl.BlockSpec((tm, k), lambda j, i: (i, 0)))
        a_flat += list(parts)
    scratch = []
    mode = dict(pipeline_mode=pl.Buffered(1)) if w_single else {}
    for w, off, transposed in w_list:
        if transposed:
            k = w.shape[1]
            assert off % 8 == 0
            in_specs.append(pl.BlockSpec((pl.Element(tn), pl.Element(k)), functools.partial(_row_elem_map, off=off, tn=tn), **mode))
            scratch.append(pltpu.VMEM((tn, k), bf16))
        else:
            k = w.shape[0]
            in_specs.append(pl.BlockSpec((k, tn), functools.partial(_col_map, off=off // tn), **mode))
            scratch.append(pltpu.VMEM((k, tn), bf16))
    in_specs += [pl.BlockSpec((c.shape[0], tn), functools.partial(_col_map, off=0)) for c in cols]
    in_specs += [pl.BlockSpec((tm, tn), t[1], **(dict(pipeline_mode=t[2]) if len(t) > 2 else {})) for t in tiles]
    outs = pl.pallas_call(
        functools.partial(_mm_kernel, a_parts=tuple(len(p) for p in a_list), split=split, nd=nd, nc=nc, nt=nt, no=no,
                          epi=epi, w_t=tuple(t for _, _, t in w_list)),
        grid=(n_cols // tn, m // tm),
        in_specs=in_specs,
        out_specs=[pl.BlockSpec((tm, tn), functools.partial(_tile_map, off=0)) for _ in out_dtypes],
        out_shape=[jax.ShapeDtypeStruct((m, n_cols), dt) for dt in out_dtypes],
        scratch_shapes=scratch,
        compiler_params=_cparams(("arbitrary", "arbitrary")),
        name=name,
    )(*a_flat, *[w for w, _, _ in w_list], *cols, *[t[0] for t in tiles])
    return outs


def _ffn_kernel(h_ref, wg_ref, wu_ref, wd_ref, o_ref):
    @pl.when(pl.program_id(1) == 0)
    def _():
        o_ref[...] = jnp.zeros_like(o_ref)

    h = h_ref[...]
    gate = _dot(h, wg_ref[...].astype(bf16))
    up = _dot(h, wu_ref[...].astype(bf16))
    act = (jax.nn.silu(gate) * up).astype(bf16)
    o_ref[...] += _dot(act, wd_ref[...].astype(bf16))


def _ffn(h, w_gate, w_up, w_down, tm=1536, tf=256):
    m, d = h.shape
    hidden = w_gate.shape[1]
    return pl.pallas_call(
        _ffn_kernel,
        grid=(m // tm, hidden // tf),
        in_specs=[pl.BlockSpec((tm, d), lambda i, f: (i, 0), pipeline_mode=pl.Buffered(1)),
                  pl.BlockSpec((d, tf), lambda i, f: (0, f)),
                  pl.BlockSpec((d, tf), lambda i, f: (0, f)),
                  pl.BlockSpec((tf, d), lambda i, f: (f, 0))],
        out_specs=pl.BlockSpec((tm, d), lambda i, f: (i, 0)),
        out_shape=jax.ShapeDtypeStruct((m, d), f32),
        compiler_params=_cparams(("arbitrary", "arbitrary")),
        name="ffn",
    )(h, w_gate, w_up, w_down)


def _softplus(x):
    return jnp.maximum(x, 0.0) + jnp.log1p(jnp.exp(-jnp.abs(x)))


def _epi_forget(accs, cols, tiles):
    logits = cols[0]
    mx = jnp.max(logits, axis=0, keepdims=True)
    e = jnp.exp(logits - mx)
    lb = e[0:1, :] / jnp.sum(e, axis=0, keepdims=True)
    sig = jax.nn.sigmoid(accs[0])
    log_f = jnp.log(lb + (1.0 - lb) * sig)
    k = (1.0 - lb) * (1.0 - sig)
    return log_f, k


HG_SPLIT = 2


def _hgrn_consts(c, seg):
    nlev = int(np.log2(seg))
    mat = np.zeros(((nlev + 2) * c, c), np.float32)
    masks = np.zeros((nlev + 1, c, c), np.float32)
    for t in range(c):
        tl = t % seg
        base = t - tl
        for l in range(nlev):
            h = 1 << l
            pos = tl % (2 * h)
            ref = base + tl - pos + h - 1
            if pos >= h:
                mat[l * c + t, ref + 1:t + 1] = 1
            else:
                mat[l * c + t, t + 1:ref + 1] = 1
            for s in range(base, base + seg):
                sl = s % seg
                if sl // (2 * h) == tl // (2 * h) and pos >= h and sl % (2 * h) < h:
                    masks[l, t, s] = 1
        mat[nlev * c + t, base:t + 1] = 1
        mat[(nlev + 1) * c + t, t + 1:base + seg] = 1
        masks[nlev, t, t] = 1
    return mat, masks, nlev


def _hgrn_chunks(q_ref, k_ref, g_ref, v_ref, mat, masks, nlev, c):
    n_chunks = q_ref.shape[0] // c
    heads = q_ref.shape[1] // LANES
    rows = lambda ci: slice(ci * c, (ci + 1) * c)
    cols = lambda h: slice(h * LANES, (h + 1) * LANES)
    items = [(ci, h) for ci in range(n_chunks) for h in range(heads)]
    e_all = [jnp.exp(_dot3(mat, g_ref[rows(ci), :])) for ci in range(n_chunks)]
    q = {(ci, h): q_ref[rows(ci), cols(h)] for ci, h in items}
    k = {(ci, h): k_ref[rows(ci), cols(h)] for ci, h in items}
    v = {(ci, h): v_ref[rows(ci), cols(h)].astype(bf16) for ci, h in items}
    s = {it: jnp.where(masks[nlev], _dot_nt(q[it].astype(bf16), k[it].astype(bf16)), 0.0) for it in items}
    for l in range(nlev):
        for ci, h in items:
            e = e_all[ci][l * c:(l + 1) * c, cols(h)]
            s[ci, h] = jnp.where(masks[l], _dot_nt((q[ci, h] * e).astype(bf16), (k[ci, h] * e).astype(bf16)), s[ci, h])
    res = {}
    for ci, h in items:
        o = _dot(s[ci, h].astype(bf16), v[ci, h])
        eb = e_all[ci][nlev * c:(nlev + 1) * c, cols(h)]
        er = e_all[ci][(nlev + 1) * c:, cols(h)]
        res[ci, h] = (o, q[ci, h] * eb, k[ci, h] * er, eb, v[ci, h])
    return res


def _hgrn_out(o, sg, nw):
    ms = jnp.mean(o * o, axis=-1, keepdims=True)
    return o * lax.rsqrt(ms + EPS) * nw * sg


def _hgrn_prompt_kernel(q_ref, g_ref, k_ref, v_ref, sg_ref, nw_ref, mat_ref, mask_ref, o_ref, s_ref, st_ref,
                        *, c, nlev, n_chunks):
    t_blk = pl.program_id(2)

    @pl.when(t_blk == 0)
    def _():
        st_ref[...] = jnp.zeros_like(st_ref)

    masks = [mask_ref[l] > 0.5 for l in range(nlev + 1)]
    mat = mat_ref[...]
    nw = nw_ref[...]

    res = _hgrn_chunks(q_ref, k_ref, g_ref, v_ref, mat, masks, nlev, c)
    upd = {it: _dot_tn(r[4], r[2].astype(bf16)) for it, r in res.items()}
    for h in range(st_ref.shape[0]):
        cs = slice(h * LANES, (h + 1) * LANES)
        st = st_ref[h]
        for ci in range(n_chunks):
            rows = slice(ci * c, (ci + 1) * c)
            o, qe, _, eb, _ = res[ci, h]
            o = o + _dot_nt(qe.astype(bf16), st.astype(bf16))
            st = st * eb[c - 1:c, :] + upd[ci, h]
            o_ref[rows, cs] = _hgrn_out(o, sg_ref[rows, cs], nw[:, cs]).astype(o_ref.dtype)
        st_ref[h] = st

    @pl.when(t_blk == pl.num_programs(2) - 1)
    def _():
        for h in range(st_ref.shape[0]):
            s_ref[0, h] = st_ref[h].T


def _hgrn_sample_kernel(q_ref, g_ref, k_ref, v_ref, sg_ref, nw_ref, mat_ref, mask_ref, s_in_ref,
                        o_ref, s_out_ref, *, c, nlev, seg):
    masks = [mask_ref[l] > 0.5 for l in range(nlev + 1)]
    mat = mat_ref[...]
    nw = nw_ref[...]
    small = seg < 16
    per = c // seg
    res = _hgrn_chunks(q_ref, k_ref, g_ref, v_ref, mat, masks, nlev, c)
    for (ci, h), (o, qe, kd, eb, v) in res.items():
        rows = slice(ci * c, (ci + 1) * c)
        cs = slice(h * LANES, (h + 1) * LANES)
        v = v.astype(f32)
        eb_t = jnp.concatenate([eb, eb], axis=0).T
        parts = []
        for j in range(per):
            sl = slice(j * seg, (j + 1) * seg)
            s0 = s_in_ref[ci * per + j, h]
            parts.append(_dot(_mxu_operand(qe[sl], small), _mxu_operand(s0, small)))
            last = j * seg + seg - 1
            upd = _dot_tn(_mxu_operand(kd[sl], small), _mxu_operand(v[sl], small))
            s_out_ref[ci * per + j, h] = s0 * eb_t[:, last:last + 1] + upd
        o = o + jnp.concatenate(parts, axis=0)
        o_ref[rows, cs] = _hgrn_out(o, sg_ref[rows, cs], nw[:, cs]).astype(o_ref.dtype)


def _hgrn(q, g, k, v, sg, nw, state, n_prompt, batch, seq, dec_seq):
    m, width = q.shape
    heads = width // LANES
    nw = nw.reshape(1, width)
    c = CHUNK
    hp = 2
    hw = hp * LANES
    tc = 512
    nt = seq // tc
    mat, masks, nlev = _hgrn_consts(c, c)
    row_spec = pl.BlockSpec((tc, hw), lambda b, h, t: (b * nt + t, h))
    const2 = lambda shape: pl.BlockSpec(shape, lambda b, h, t: (0,) * len(shape))
    o_p, s_p = pl.pallas_call(
        functools.partial(_hgrn_prompt_kernel, c=c, nlev=nlev, n_chunks=tc // c),
        grid=(batch, heads // hp, nt),
        in_specs=[row_spec] * 5 + [pl.BlockSpec((1, hw), lambda b, h, t: (0, h)), const2((mat.shape[0], HG_SPLIT * c)), const2(masks.shape)],
        out_specs=[row_spec, pl.BlockSpec((1, hp, HG_DK, LANES), lambda b, h, t: (b, h, 0, 0))],
        out_shape=[jax.ShapeDtypeStruct((n_prompt, width), bf16),
                   jax.ShapeDtypeStruct((batch, heads, HG_DK, LANES), f32)],
        scratch_shapes=[pltpu.VMEM((hp, LANES, HG_DK), f32)],
        compiler_params=_cparams(("arbitrary", "arbitrary", "arbitrary")),
        name="hgrn_prompt",
    )(q, g, k, v, sg, nw, _tile3(mat, HG_SPLIT), jnp.asarray(masks))
    n_sample = m - n_prompt
    rows = 2 * c
    per = rows // dec_seq
    mat, masks, nlev = _hgrn_consts(c, dec_seq)
    base = n_prompt // rows
    row_spec = pl.BlockSpec((rows, hw), lambda jb, h: (base + jb, h))
    st_spec = pl.BlockSpec((per, hp, HG_DK, LANES), lambda jb, h: (jb, h, 0, 0))
    const2 = lambda shape: pl.BlockSpec(shape, lambda jb, h: (0,) * len(shape))
    o_s, s_s = pl.pallas_call(
        functools.partial(_hgrn_sample_kernel, c=c, nlev=nlev, seg=dec_seq),
        grid=(n_sample // rows, heads // hp),
        in_specs=[row_spec] * 5 + [pl.BlockSpec((1, hw), lambda jb, h: (0, h)), const2((mat.shape[0], HG_SPLIT * c)),
                                   const2(masks.shape), st_spec],
        out_specs=[pl.BlockSpec((rows, hw), lambda jb, h: (jb, h)), st_spec],
        out_shape=[jax.ShapeDtypeStruct((n_sample, width), bf16), jax.ShapeDtypeStruct(state.shape, f32)],
        compiler_params=_cparams(("arbitrary", "arbitrary")),
        name="hgrn_sample",
    )(q, g, k, v, sg, nw, _tile3(mat, HG_SPLIT), jnp.asarray(masks), state)
    return (o_p, o_s), s_p, s_s


GROUP_W = 512
PAIRS = GROUP_W // LANES
XBC_W = GROUP_W + 2 * M_DSTATE


def _ssd_consts(c, seg):
    t = np.arange(c)
    same = (t[:, None] // seg) == (t[None, :] // seg)
    tril = (same & (t[None, :] <= t[:, None])).astype(np.float32)
    return tril


def _ssd_chunks(chunks, a_row, dsk, nw, lc, tril, get_state, set_state, c, seg, state_t):
    nseg = c // seg
    small = seg < 16
    assert 2 * c == LANES and 2 * M_HEADDIM == LANES and not (state_t and nseg > 1)
    lo_half = lax.broadcasted_iota(jnp.int32, (c, LANES), 1) < M_HEADDIM
    lo_row = lo_half[0:1, :]
    n = len(chunks)
    items = [(ci, p) for ci in range(n) for p in range(PAIRS)]
    pc = lambda p: slice(p * LANES, (p + 1) * LANES)
    acum = [_dot3(lc, dt * a_row) for _, _, _, dt, _ in chunks]
    bmb = [bm.astype(bf16) for _, bm, _, _, _ in chunks]
    cmb = [cm.astype(bf16) for _, _, cm, _, _ in chunks]
    cb2 = [_dot_nt(cmb[ci], jnp.concatenate([bmb[ci], bmb[ci]], axis=0)) for ci in range(n)]
    acum_t = [jnp.concatenate([a, a], axis=0).T for a in acum]
    dt_t = [jnp.concatenate([ch[3], ch[3]], axis=0).T for ch in chunks]
    if state_t:
        bm_t = [ch[1].T.astype(bf16) for ch in chunks]
    else:
        ea_t = [jnp.exp(a[0:8, 0:c]) for a in acum_t]
    y, xw, ea = {}, {}, {}
    for ci, p in items:
        h0, h1 = 2 * p, 2 * p + 1
        xp, dt = chunks[ci][0][:, pc(p)], chunks[ci][3]
        acp = jnp.where(lo_half, acum[ci][:, h0:h0 + 1], acum[ci][:, h1:h1 + 1])
        dtp = jnp.where(lo_half, dt[:, h0:h0 + 1], dt[:, h1:h1 + 1])
        a_src = jnp.where(lo_row, acum_t[ci][h0:h0 + 1, :], acum_t[ci][h1:h1 + 1, :])
        dt_src = jnp.where(lo_row, dt_t[ci][h0:h0 + 1, :], dt_t[ci][h1:h1 + 1, :])
        lm = jnp.exp(jnp.where(tril, acp - a_src, -1e30))
        sc = (cb2[ci] * lm * dt_src).astype(bf16)
        x_blk = jnp.concatenate([jnp.where(lo_half, xp, 0.0), jnp.where(lo_half, 0.0, xp)], axis=0).astype(bf16)
        y[ci, p] = _dot(sc, x_blk)
        if nseg == 1:
            alast = acp[c - 1:c, :]
        else:
            alast = jnp.concatenate(
                [jnp.broadcast_to(acp[j * seg + seg - 1:j * seg + seg, :], (seg, LANES)) for j in range(nseg)], axis=0)
        xw[ci, p] = xp * (jnp.exp(alast - acp) * dtp)
        ea[ci, p] = jnp.exp(acp)
    cs = {}
    if state_t:
        upd = {(ci, p): _dot(bm_t[ci], xw[ci, p].astype(bf16)) for ci, p in items}
        for p in range(PAIRS):
            st = get_state(0, 0, p)
            for ci in range(n):
                cs[ci, p] = _dot(cmb[ci], st.astype(bf16))
                st = st * ea[ci, p][c - 1:c, :] + upd[ci, p]
            set_state(0, 0, p, st)
    else:
        for ci, p in items:
            h0, h1 = 2 * p, 2 * p + 1
            bm, cm = chunks[ci][1], chunks[ci][2]
            parts = []
            for j in range(nseg):
                sl = slice(j * seg, (j + 1) * seg)
                s0 = get_state(ci, j, p)
                parts.append(_dot_nt(_mxu_operand(cm[sl], small), _mxu_operand(s0, small)))
                upd = _dot_tn(_mxu_operand(xw[ci, p][sl], small), _mxu_operand(bm[sl], small))
                last = j * seg + seg - 1
                decay = jnp.concatenate(
                    [jnp.broadcast_to(ea_t[ci][h0:h0 + 1, last:last + 1], (M_HEADDIM, LANES)),
                     jnp.broadcast_to(ea_t[ci][h1:h1 + 1, last:last + 1], (M_HEADDIM, LANES))], axis=0)
                set_state(ci, j, p, s0 * decay + upd)
            cs[ci, p] = parts[0] if nseg == 1 else jnp.concatenate(parts, axis=0)
    outs = []
    for ci in range(n):
        xs, zs = chunks[ci][0], chunks[ci][4]
        ssq = jnp.zeros((c, 1), f32)
        ys = []
        for p in range(PAIRS):
            yp = (y[ci, p] + cs[ci, p] * ea[ci, p] + dsk[:, pc(p)] * xs[:, pc(p)]) * zs[:, pc(p)]
            ssq = ssq + jnp.sum(yp * yp, axis=-1, keepdims=True)
            ys.append(yp)
        scale = lax.rsqrt(ssq * (1.0 / GROUP_W) + EPS)
        outs.append([ys[p] * scale * nw[:, pc(p)] for p in range(PAIRS)])
    return outs


def _conv_taps(xpad_ref, lead, rows, w, b):
    acc = None
    for j in range(M_CONV):
        term = xpad_ref[lead + (slice(5 + j + rows[0], 5 + j + rows[1]), slice(None))] * w[j:j + 1, :]
        acc = term if acc is None else acc + term
    return jax.nn.silu(b + acc)


def _ssd_prompt_kernel(xr_ref, br_ref, cr_ref, dt_ref, zs_ref, wx_ref, wb_ref, wc_ref, bx_ref, bb_ref, bc_ref,
                       alog_ref, dsk_ref, nw_ref, lc_ref, tril_ref, o_ref, s_ref, xpad_ref, xc_ref, st_ref,
                       *, c, rows):
    t_blk = pl.program_id(2)

    @pl.when(t_blk == 0)
    def _():
        st_ref[...] = jnp.zeros_like(st_ref)
        xpad_ref[0:8, :] = jnp.zeros((8, XBC_W), f32)

    @pl.when(t_blk > 0)
    def _():
        xpad_ref[0:8, :] = xpad_ref[rows:rows + 8, :]

    xpad_ref[8:8 + rows, 0:GROUP_W] = xr_ref[...]
    xpad_ref[8:8 + rows, GROUP_W:GROUP_W + M_DSTATE] = br_ref[...]
    xpad_ref[8:8 + rows, GROUP_W + M_DSTATE:XBC_W] = cr_ref[...]
    w = jnp.concatenate([wx_ref[...], wb_ref[...], wc_ref[...]], axis=1)
    b = jnp.concatenate([bx_ref[...], bb_ref[...], bc_ref[...]], axis=1)
    for i in range(rows // c):
        xc_ref[i * c:(i + 1) * c, :] = _conv_taps(xpad_ref, (), (i * c, (i + 1) * c), w, b)

    a_row = -jnp.exp(alog_ref[...])
    dsk = dsk_ref[...]
    nw = nw_ref[...]
    lc = lc_ref[...]
    tril = tril_ref[...] > 0.5

    def get_state(ci, j, p):
        return st_ref[p]

    def set_state(ci, j, p, val):
        st_ref[p] = val

    rs = lambda ci: slice(ci * c, (ci + 1) * c)
    chunks = [(xc_ref[rs(ci), 0:GROUP_W], xc_ref[rs(ci), GROUP_W:GROUP_W + M_DSTATE], xc_ref[rs(ci), GROUP_W + M_DSTATE:XBC_W],
               dt_ref[rs(ci), :], zs_ref[rs(ci), :]) for ci in range(rows // c)]
    outs = _ssd_chunks(chunks, a_row, dsk, nw, lc, tril, get_state, set_state, c, c, state_t=True)
    for ci, out in enumerate(outs):
        for p in range(PAIRS):
            o_ref[rs(ci), p * LANES:(p + 1) * LANES] = out[p].astype(o_ref.dtype)

    @pl.when(t_blk == pl.num_programs(2) - 1)
    def _():
        for p in range(PAIRS):
            s_ref[0, p * LANES:(p + 1) * LANES, :] = st_ref[p].T


def _ssd_sample_kernel(xr_ref, br_ref, cr_ref, dt_ref, zs_ref, wx_ref, wb_ref, wc_ref, bx_ref, bb_ref, bc_ref,
                       alog_ref, dsk_ref, nw_ref, lc_ref, tril_ref, hx_ref, hb_ref, hc_ref, s_in_ref,
                       o_ref, s_out_ref, xpad_ref, *, c, seg):
    nseg = c // seg
    w = jnp.concatenate([wx_ref[...], wb_ref[...], wc_ref[...]], axis=1)
    b = jnp.concatenate([bx_ref[...], bb_ref[...], bc_ref[...]], axis=1)
    conv = []
    for j in range(nseg):
        sl = slice(j * seg, (j + 1) * seg)
        xpad_ref[j, 5:8, 0:GROUP_W] = hx_ref[j]
        xpad_ref[j, 5:8, GROUP_W:GROUP_W + M_DSTATE] = hb_ref[j]
        xpad_ref[j, 5:8, GROUP_W + M_DSTATE:XBC_W] = hc_ref[j]
        xpad_ref[j, 8:8 + seg, 0:GROUP_W] = xr_ref[sl, :]
        xpad_ref[j, 8:8 + seg, GROUP_W:GROUP_W + M_DSTATE] = br_ref[sl, :]
        xpad_ref[j, 8:8 + seg, GROUP_W + M_DSTATE:XBC_W] = cr_ref[sl, :]
        conv.append(_conv_taps(xpad_ref, (j,), (0, seg), w, b))
    xc = jnp.concatenate(conv, axis=0)

    def get_state(ci, j, p):
        return s_in_ref[j, p * LANES:(p + 1) * LANES, :]

    def set_state(ci, j, p, val):
        s_out_ref[j, p * LANES:(p + 1) * LANES, :] = val

    chunks = [(xc[:, 0:GROUP_W], xc[:, GROUP_W:GROUP_W + M_DSTATE], xc[:, GROUP_W + M_DSTATE:XBC_W], dt_ref[...], zs_ref[...])]
    (out,) = _ssd_chunks(chunks, -jnp.exp(alog_ref[...]), dsk_ref[...], nw_ref[...], lc_ref[...],
                         tril_ref[...] > 0.5, get_state, set_state, c, seg, state_t=False)
    for p in range(PAIRS):
        o_ref[:, p * LANES:(p + 1) * LANES] = out[p].astype(o_ref.dtype)


def _ssd(xbc, dt, zs, conv_w, conv_b, alog_p, dsk, nw, state, hist, n_prompt, batch, seq, dec_seq):
    m = xbc.shape[0]
    inner = zs.shape[1]
    c = CHUNK
    xb_blk = inner // M_DSTATE
    cb_blk = xb_blk + M_GROUPS

    def specs(row_map, nrow):
        def rm(fn):
            return lambda *ix: fn(row_map(*ix), ix[1])
        zero = lambda fn: (lambda *ix: fn(0, ix[1]))
        return [
            pl.BlockSpec((nrow, GROUP_W), rm(lambda r, g: (r, g))),
            pl.BlockSpec((nrow, M_DSTATE), rm(lambda r, g: (r, xb_blk + g))),
            pl.BlockSpec((nrow, M_DSTATE), rm(lambda r, g: (r, cb_blk + g))),
            pl.BlockSpec((nrow, LANES), rm(lambda r, g: (r, g))),
            pl.BlockSpec((nrow, GROUP_W), rm(lambda r, g: (r, g))),
            pl.BlockSpec((M_CONV, GROUP_W), zero(lambda r, g: (0, g))),
            pl.BlockSpec((M_CONV, M_DSTATE), zero(lambda r, g: (0, xb_blk + g))),
            pl.BlockSpec((M_CONV, M_DSTATE), zero(lambda r, g: (0, cb_blk + g))),
            pl.BlockSpec((1, GROUP_W), zero(lambda r, g: (0, g))),
            pl.BlockSpec((1, M_DSTATE), zero(lambda r, g: (0, xb_blk + g))),
            pl.BlockSpec((1, M_DSTATE), zero(lambda r, g: (0, cb_blk + g))),
            pl.BlockSpec((1, LANES), zero(lambda r, g: (0, g))),
            pl.BlockSpec((1, GROUP_W), zero(lambda r, g: (0, g))),
            pl.BlockSpec((1, GROUP_W), zero(lambda r, g: (0, g))),
            pl.BlockSpec((c, 3 * c), zero(lambda r, g: (0, 0))),
            pl.BlockSpec((c, 2 * c), zero(lambda r, g: (0, 0))),
        ]

    common = (xbc, xbc, xbc, dt, zs, conv_w, conv_w, conv_w, conv_b, conv_b, conv_b, alog_p, dsk, nw)
    rows = 512
    nt = seq // rows
    tril = _ssd_consts(c, c)
    in_specs = specs(lambda b, g, t: b * nt + t, rows)
    o_p, s_p = pl.pallas_call(
        functools.partial(_ssd_prompt_kernel, c=c, rows=rows),
        grid=(batch, M_GROUPS, nt),
        in_specs=in_specs,
        out_specs=[pl.BlockSpec((rows, GROUP_W), lambda b, g, t: (b * nt + t, g)),
                   pl.BlockSpec((1, GROUP_W, M_DSTATE), lambda b, g, t: (b, g, 0))],
        out_shape=[jax.ShapeDtypeStruct((n_prompt, inner), bf16),
                   jax.ShapeDtypeStruct((batch, inner, M_DSTATE), f32)],
        scratch_shapes=[pltpu.VMEM((rows + 8, XBC_W), f32), pltpu.VMEM((rows, XBC_W), f32),
                        pltpu.VMEM((PAIRS, LANES, M_DSTATE), f32)],
        compiler_params=_cparams(("arbitrary", "arbitrary", "arbitrary")),
        name="ssd_prompt",
    )(*common, _tile3(tril), jnp.asarray(np.tile(tril, (1, 2))))
    n_sample = m - n_prompt
    per = c // dec_seq
    base = n_prompt // c
    tril = _ssd_consts(c, dec_seq)
    in_specs = specs(lambda jb, g: base + jb, c)
    in_specs += [
        pl.BlockSpec((per, M_CONV - 1, GROUP_W), lambda jb, g: (jb, 0, g)),
        pl.BlockSpec((per, M_CONV - 1, M_DSTATE), lambda jb, g: (jb, 0, xb_blk + g)),
        pl.BlockSpec((per, M_CONV - 1, M_DSTATE), lambda jb, g: (jb, 0, cb_blk + g)),
        pl.BlockSpec((per, GROUP_W, M_DSTATE), lambda jb, g: (jb, g, 0)),
    ]
    o_s, s_s = pl.pallas_call(
        functools.partial(_ssd_sample_kernel, c=c, seg=dec_seq),
        grid=(n_sample // c, M_GROUPS),
        in_specs=in_specs,
        out_specs=[pl.BlockSpec((c, GROUP_W), lambda jb, g: (jb, g)),
                   pl.BlockSpec((per, GROUP_W, M_DSTATE), lambda jb, g: (jb, g, 0))],
        out_shape=[jax.ShapeDtypeStruct((n_sample, inner), bf16), jax.ShapeDtypeStruct(state.shape, f32)],
        scratch_shapes=[pltpu.VMEM((per, 16, XBC_W), f32)],
        compiler_params=_cparams(("arbitrary", "arbitrary")),
        name="ssd_sample",
    )(*common, _tile3(tril), jnp.asarray(np.tile(tril, (1, 2))), hist, hist, hist, state)
    return (o_p, o_s), s_p, s_s


def kernel(x_prompt, x_sample, state_hgrn, state_ssm, state_conv, norm_mix, w_in, hg_lb_logits, hg_norm, conv_w, conv_b,
           dt_bias, a_log, d_skip, ssm_norm, w_branch_hg, w_branch_ssm, w_out, norm_ffn, w_ffn_gate, w_ffn_up,
           w_ffn_down, norm_final):
    batch, seq, d = x_prompt.shape
    dec_batch, dec_seq, _ = x_sample.shape
    n_prompt, n_sample = batch * seq, dec_batch * dec_seq
    hg_heads = state_hgrn.shape[2]
    kdim = hg_heads * HG_DK
    vdim = d
    inner = d
    m_heads = state_ssm.shape[2]
    conv_dim = conv_w.shape[2]
    hpg = m_heads // M_GROUPS

    xp2, xs2 = x_prompt.reshape(n_prompt, d), x_sample.reshape(n_sample, d)
    h = _rmsnorm([xp2, xs2], norm_mix[0], bf16, n_prompt)

    wt = jnp.swapaxes(w_in, 1, 2)[0]
    o_q, o_f, o_v, o_g, o_z, o_xbc = 0, kdim, 2 * kdim, 2 * kdim + vdim, 2 * kdim + 2 * vdim, 2 * kdim + 2 * vdim + inner
    o_dt = o_xbc + conv_dim
    o_gate = o_dt + m_heads
    scale = HG_DK ** -0.5
    ident = lambda a, c, t: (a[0],)
    silu = lambda a, c, t: (jax.nn.silu(a[0]),)
    wide = dict(tm=512, tn=2048, w_single=True)
    (q,) = _mm([h], [(wt, o_q, True)], kdim, lambda a, c, t: (a[0] * scale,), [f32], name="proj_q", **wide)
    log_f, k = _mm([h], [(wt, o_f, True)], kdim, _epi_forget, [f32, f32], cols=[hg_lb_logits], name="proj_f", **wide)
    (v,) = _mm([h], [(wt, o_v, True)], vdim, ident, [bf16], name="proj_v", **wide)
    (sg,) = _mm([h], [(wt, o_g, True)], vdim, silu, [f32], name="proj_g", **wide)
    (zs,) = _mm([h], [(wt, o_z, True)], inner, silu, [f32], name="proj_z", **wide)
    (xbc,) = _mm([h], [(wt, o_xbc, True)], conv_dim, ident, [f32], tm=512, tn=conv_dim // 2, w_single=True, name="proj_xbc")
    pad_heads = lambda p: jnp.pad(p.reshape(M_GROUPS, hpg, -1), ((0, 0), (0, LANES - hpg), (0, 0))).reshape(M_GROUPS * LANES, -1)
    (dt,) = _mm([h], [(pad_heads(wt[o_dt:o_gate]), 0, True)], M_GROUPS * LANES,
                lambda a, c, t: (_softplus(a[0] + c[0]),), [f32], cols=[pad_heads(dt_bias[0]).T], name="proj_dt")
    (gates,) = _mm([h], [(wt, o_gate, True)], 2 * d, lambda a, c, t: (jax.nn.sigmoid(a[0]),), [f32], name="proj_gate", **wide)

    o_hg, shp, shs = _hgrn(q, log_f, k, v, sg, hg_norm[0], state_hgrn[0], n_prompt, batch, seq, dec_seq)
    dsk = jnp.repeat(d_skip[0], M_HEADDIM).reshape(1, inner)
    o_m, smp, sms = _ssd(xbc, dt, zs, conv_w[0], conv_b[0].reshape(1, conv_dim), pad_heads(a_log[0]).T, dsk,
                         ssm_norm[0].reshape(1, inner), state_ssm[0].reshape(dec_batch, inner, M_DSTATE),
                         state_conv[0], n_prompt, batch, seq, dec_seq)

    tn = 1024
    (merged,) = _mm([o_hg, o_m], [(w_branch_hg[0], 0, False), (w_branch_ssm[0], 0, False)], d,
                    lambda a, c, t: (t[0] * a[0] + t[1] * a[1],), [bf16],
                    tiles=[(gates, lambda j, i: (i, j)), (gates, lambda j, i: (i, j + d // tn))],
                    tm=512, tn=tn, w_single=True, name="merge")
    tm = 512
    n_p_tiles = n_prompt // tm

    def add_x_norm(a, c, t):
        x1 = jnp.where(pl.program_id(1) < n_p_tiles, t[0], t[1]) + a[0]
        ms = jnp.mean(x1 * x1, axis=-1, keepdims=True)
        return x1, x1 * lax.rsqrt(ms + EPS) * c[0]

    x1, h2 = _mm([merged], [(w_out[0], 0, False)], d, add_x_norm, [f32, bf16], cols=[norm_ffn],
                 tiles=[(xp2, lambda j, i: (jnp.minimum(i, n_p_tiles - 1), j)),
                        (xs2, lambda j, i: (jnp.maximum(i - n_p_tiles, 0), j), pl.Buffered(1))],
                 tm=tm, tn=d, w_single=True, name="out_proj")
    ffn = _ffn(h2, w_ffn_gate[0], w_ffn_up[0], w_ffn_down[0])
    y_p, y_s = _rmsnorm([x1], norm_final, f32, n_prompt, split_out=True, res=ffn)

    hist = M_CONV - 1
    new_conv_p = jnp.stack([xbc[(b + 1) * seq - hist:(b + 1) * seq] for b in range(batch)])[None]
    new_conv_s = xbc[n_prompt:].reshape(dec_batch, dec_seq, conv_dim)[:, dec_seq - hist:][None]
    return (y_p.reshape(batch, seq, d), y_s.reshape(dec_batch, dec_seq, d),
            shp[None], smp.reshape(1, batch, m_heads, M_HEADDIM, M_DSTATE), new_conv_p,
            shs[None], sms.reshape(1, dec_batch, m_heads, M_HEADDIM, M_DSTATE), new_conv_s)
```

```python
import functools

import numpy as np
import jax
import jax.numpy as jnp
from jax import lax
from jax.experimental import pallas as pl
from jax.experimental.pallas import tpu as pltpu

f32 = jnp.float32
bf16 = jnp.bfloat16

EPS = 1e-6
LANES = 128
CHUNK = 64
HG_DK = 128
M_HEADDIM = 64
M_DSTATE = 128
M_GROUPS = 4
M_CONV = 4
VMEM_LIMIT = 56 * 1024 * 1024


def _cparams(sem):
    return pltpu.CompilerParams(dimension_semantics=sem, vmem_limit_bytes=VMEM_LIMIT)


def _dot(a, b):
    return jnp.dot(a, b, preferred_element_type=f32)


def _dot_nt(a, b):
    return lax.dot_general(a, b, (((1,), (1,)), ((), ())), preferred_element_type=f32)


def _dot_tn(a, b):
    return lax.dot_general(a, b, (((0,), (0,)), ((), ())), preferred_element_type=f32)


def _split(x, n):
    pieces = []
    for _ in range(n - 1):
        p = x.astype(bf16)
        pieces.append(p)
        x = x - p.astype(f32)
    return pieces + [x.astype(bf16)]


def _tile3(m, n=3):
    return jnp.asarray(np.tile(m, (1, n)), bf16)


def _dot3(mn, x):
    return _dot(mn, jnp.concatenate(_split(x, mn.shape[1] // x.shape[0]), axis=0))


def _mxu_operand(x, small):
    xb = x.astype(bf16)
    return xb.astype(f32) if small else xb


def _rmsnorm_kernel(*refs, n_in, n_out, split, has_res):
    x_refs, w_ref, o_refs = refs[:n_in], refs[n_in], refs[n_in + 1 + has_res:]

    def run(x_ref, o_ref):
        x = x_ref[...]
        if has_res:
            x = x + refs[n_in + 1][...]
        ms = jnp.mean(x * x, axis=-1, keepdims=True)
        o_ref[...] = (x * lax.rsqrt(ms + EPS) * w_ref[...]).astype(o_ref.dtype)

    if n_in == 1 and n_out == 1:
        run(x_refs[0], o_refs[0])
    else:
        i = pl.program_id(0)
        pl.when(i < split)(lambda: run(x_refs[0], o_refs[0]))
        pl.when(i >= split)(lambda: run(x_refs[-1], o_refs[-1]))


def _rmsnorm(xs, w, out_dtype, n_first, split_out=False, res=None, tm=512):
    d = xs[0].shape[1]
    m = sum(x.shape[0] for x in xs)
    split = n_first // tm
    first = lambda i: (jnp.minimum(i, split - 1), 0)
    second = lambda i: (jnp.maximum(i - split, 0), 0)
    whole = lambda i: (i, 0)
    blk = lambda fn: pl.BlockSpec((tm, d), fn)
    in_specs = [blk(whole)] if len(xs) == 1 else [blk(first), blk(second)]
    if split_out:
        out_specs = [blk(first), blk(second)]
        out_shape = [jax.ShapeDtypeStruct((n_first, d), out_dtype), jax.ShapeDtypeStruct((m - n_first, d), out_dtype)]
    else:
        out_specs = [blk(whole)]
        out_shape = [jax.ShapeDtypeStruct((m, d), out_dtype)]
    extra = [] if res is None else [res]
    outs = pl.pallas_call(
        functools.partial(_rmsnorm_kernel, n_in=len(xs), n_out=len(out_shape), split=split, has_res=len(extra)),
        grid=(m // tm,),
        in_specs=in_specs + [pl.BlockSpec((1, d), lambda i: (0, 0))] + [blk(whole) for _ in extra],
        out_specs=out_specs,
        out_shape=out_shape,
        compiler_params=_cparams(("arbitrary",)),
        name="rmsnorm",
    )(*xs, w.reshape(1, d), *extra)
    return outs if split_out else outs[0]


def _mm_kernel(*refs, a_parts, split, nd, nc, nt, no, epi, w_t):
    a_vals = []
    for parts in a_parts:
        if parts == 2:
            a_vals.append(jnp.where(pl.program_id(1) < split, refs[0][...], refs[1][...]))
        else:
            a_vals.append(refs[0][...])
        refs = refs[parts:]
    na = len(a_vals)
    w_refs = refs[:nd]
    c_refs = refs[nd:nd + nc]
    t_refs = refs[nd + nc:nd + nc + nt]
    o_refs = refs[nd + nc + nt:nd + nc + nt + no]
    wb_refs = refs[nd + nc + nt + no:]

    @pl.when(pl.program_id(1) == 0)
    def _():
        for w_ref, wb_ref in zip(w_refs, wb_refs):
            wb_ref[...] = w_ref[...].astype(bf16)

    accs = [(_dot_nt if t else _dot)(a_vals[min(i, na - 1)], wb_ref[...])
            for i, (wb_ref, t) in enumerate(zip(wb_refs, w_t))]
    outs = epi(accs, [c[...] for c in c_refs], [t[...] for t in t_refs])
    for o_ref, o in zip(o_refs, outs):
        o_ref[...] = o.astype(o_ref.dtype)


def _col_map(j, i, off):
    return (0, j + off)


def _tile_map(j, i, off):
    return (i, j + off)


def _row_elem_map(j, i, off, tn):
    return (pl.multiple_of(off + j * tn, 8), 0)


def _mm(a_list, w_list, n_cols, epi, out_dtypes, cols=(), tiles=(), tm=1024, tn=512, w_single=False, name="mm"):
    na, nd, nc, nt, no = len(a_list), len(w_list), len(cols), len(tiles), len(out_dtypes)
    assert na in (1, nd)
    a_list = [a if isinstance(a, tuple) else (a,) for a in a_list]
    m = sum(p.shape[0] for p in a_list[0])
    split = a_list[0][0].shape[0] // tm
    in_specs, a_flat = [], []
    for parts in a_list:
        k = parts[0].shape[1]
        if len(parts) == 2:
            in_specs.append(pl.BlockSpec((tm, k), lambda j, i: (jnp.minimum(i, split - 1), 0)))
            in_specs.append(pl.BlockSpec((tm, k), lambda j, i: (jnp.maximum(i - split, 0), 0)))
        else:
            in_specs.append(pl.BlockSpec((tm, k), lambda j, i: (i, 0)))
        a_flat += list(parts)
    scratch = []
    mode = dict(pipeline_mode=pl.Buffered(1)) if w_single else {}
    for w, off, transposed in w_list:
        if transposed:
            k = w.shape[1]
            assert off % 8 == 0
            in_specs.append(pl.BlockSpec((pl.Element(tn), pl.Element(k)), functools.partial(_row_elem_map, off=off, tn=tn), **mode))
            scratch.append(pltpu.VMEM((tn, k), bf16))
        else:
            k = w.shape[0]
            in_specs.append(pl.BlockSpec((k, tn), functools.partial(_col_map, off=off // tn), **mode))
            scratch.append(pltpu.VMEM((k, tn), bf16))
    in_specs += [pl.BlockSpec((c.shape[0], tn), functools.partial(_col_map, off=0)) for c in cols]
    in_specs += [pl.BlockSpec((tm, tn), t[1], **(dict(pipeline_mode=t[2]) if len(t) > 2 else {})) for t in tiles]
    outs = pl.pallas_call(
        functools.partial(_mm_kernel, a_parts=tuple(len(p) for p in a_list), split=split, nd=nd, nc=nc, nt=nt, no=no,
                          epi=epi, w_t=tuple(t for _, _, t in w_list)),
        grid=(n_cols // tn, m // tm),
        in_specs=in_specs,
        out_specs=[pl.BlockSpec((tm, tn), functools.partial(_tile_map, off=0)) for _ in out_dtypes],
        out_shape=[jax.ShapeDtypeStruct((m, n_cols), dt) for dt in out_dtypes],
        scratch_shapes=scratch,
        compiler_params=_cparams(("arbitrary", "arbitrary")),
        name=name,
    )(*a_flat, *[w for w, _, _ in w_list], *cols, *[t[0] for t in tiles])
    return outs


def _ffn_kernel(h_ref, wg_ref, wu_ref, wd_ref, o_ref):
    @pl.when(pl.program_id(1) == 0)
    def _():
        o_ref[...] = jnp.zeros_like(o_ref)

    h = h_ref[...]
    gate = _dot(h, wg_ref[...].astype(bf16))
    up = _dot(h, wu_ref[...].astype(bf16))
    act = (jax.nn.silu(gate) * up).astype(bf16)
    o_ref[...] += _dot(act, wd_ref[...].astype(bf16))


def _ffn(h, w_gate, w_up, w_down, tm=1536, tf=256):
    m, d = h.shape
    hidden = w_gate.shape[1]
    return pl.pallas_call(
        _ffn_kernel,
        grid=(m // tm, hidden // tf),
        in_specs=[pl.BlockSpec((tm, d), lambda i, f: (i, 0), pipeline_mode=pl.Buffered(1)),
                  pl.BlockSpec((d, tf), lambda i, f: (0, f)),
                  pl.BlockSpec((d, tf), lambda i, f: (0, f)),
                  pl.BlockSpec((tf, d), lambda i, f: (f, 0))],
        out_specs=pl.BlockSpec((tm, d), lambda i, f: (i, 0)),
        out_shape=jax.ShapeDtypeStruct((m, d), f32),
        compiler_params=_cparams(("arbitrary", "arbitrary")),
        name="ffn",
    )(h, w_gate, w_up, w_down)


def _softplus(x):
    return jnp.maximum(x, 0.0) + jnp.log1p(jnp.exp(-jnp.abs(x)))


def _epi_forget(accs, cols, tiles):
    logits = cols[0]
    mx = jnp.max(logits, axis=0, keepdims=True)
    e = jnp.exp(logits - mx)
    lb = e[0:1, :] / jnp.sum(e, axis=0, keepdims=True)
    sig = jax.nn.sigmoid(accs[0])
    log_f = jnp.log(lb + (1.0 - lb) * sig)
    k = (1.0 - lb) * (1.0 - sig)
    return log_f, k


HG_SPLIT = 2


def _hgrn_consts(c, seg):
    nlev = int(np.log2(seg))
    mat = np.zeros(((nlev + 2) * c, c), np.float32)
    masks = np.zeros((nlev + 1, c, c), np.float32)
    for t in range(c):
        tl = t % seg
        base = t - tl
        for l in range(nlev):
            h = 1 << l
            pos = tl % (2 * h)
            ref = base + tl - pos + h - 1
            if pos >= h:
                mat[l * c + t, ref + 1:t + 1] = 1
            else:
                mat[l * c + t, t + 1:ref + 1] = 1
            for s in range(base, base + seg):
                sl = s % seg
                if sl // (2 * h) == tl // (2 * h) and pos >= h and sl % (2 * h) < h:
                    masks[l, t, s] = 1
        mat[nlev * c + t, base:t + 1] = 1
        mat[(nlev + 1) * c + t, t + 1:base + seg] = 1
        masks[nlev, t, t] = 1
    return mat, masks, nlev


def _hgrn_chunks(q_ref, k_ref, g_ref, v_ref, mat, masks, nlev, c):
    n_chunks = q_ref.shape[0] // c
    heads = q_ref.shape[1] // LANES
    rows = lambda ci: slice(ci * c, (ci + 1) * c)
    cols = lambda h: slice(h * LANES, (h + 1) * LANES)
    items = [(ci, h) for ci in range(n_chunks) for h in range(heads)]
    e_all = [jnp.exp(_dot3(mat, g_ref[rows(ci), :])) for ci in range(n_chunks)]
    q = {(ci, h): q_ref[rows(ci), cols(h)] for ci, h in items}
    k = {(ci, h): k_ref[rows(ci), cols(h)] for ci, h in items}
    v = {(ci, h): v_ref[rows(ci), cols(h)].astype(bf16) for ci, h in items}
    s = {it: jnp.where(masks[nlev], _dot_nt(q[it].astype(bf16), k[it].astype(bf16)), 0.0) for it in items}
    for l in range(nlev):
        for ci, h in items:
            e = e_all[ci][l * c:(l + 1) * c, cols(h)]
            s[ci, h] = jnp.where(masks[l], _dot_nt((q[ci, h] * e).astype(bf16), (k[ci, h] * e).astype(bf16)), s[ci, h])
    res = {}
    for ci, h in items:
        o = _dot(s[ci, h].astype(bf16), v[ci, h])
        eb = e_all[ci][nlev * c:(nlev + 1) * c, cols(h)]
        er = e_all[ci][(nlev + 1) * c:, cols(h)]
        res[ci, h] = (o, q[ci, h] * eb, k[ci, h] * er, eb, v[ci, h])
    return res


def _hgrn_out(o, sg, nw):
    ms = jnp.mean(o * o, axis=-1, keepdims=True)
    return o * lax.rsqrt(ms + EPS) * nw * sg


def _hgrn_prompt_kernel(q_ref, g_ref, k_ref, v_ref, sg_ref, nw_ref, mat_ref, mask_ref, o_ref, s_ref, st_ref,
                        *, c, nlev, n_chunks):
    t_blk = pl.program_id(2)

    @pl.when(t_blk == 0)
    def _():
        st_ref[...] = jnp.zeros_like(st_ref)

    masks = [mask_ref[l] > 0.5 for l in range(nlev + 1)]
    mat = mat_ref[...]
    nw = nw_ref[...]

    res = _hgrn_chunks(q_ref, k_ref, g_ref, v_ref, mat, masks, nlev, c)
    upd = {it: _dot_tn(r[4], r[2].astype(bf16)) for it, r in res.items()}
    for h in range(st_ref.shape[0]):
        cs = slice(h * LANES, (h + 1) * LANES)
        st = st_ref[h]
        for ci in range(n_chunks):
            rows = slice(ci * c, (ci + 1) * c)
            o, qe, _, eb, _ = res[ci, h]
            o = o + _dot_nt(qe.astype(bf16), st.astype(bf16))
            st = st * eb[c - 1:c, :] + upd[ci, h]
            o_ref[rows, cs] = _hgrn_out(o, sg_ref[rows, cs], nw[:, cs]).astype(o_ref.dtype)
        st_ref[h] = st

    @pl.when(t_blk == pl.num_programs(2) - 1)
    def _():
        for h in range(st_ref.shape[0]):
            s_ref[0, h] = st_ref[h].T


def _hgrn_sample_kernel(q_ref, g_ref, k_ref, v_ref, sg_ref, nw_ref, mat_ref, mask_ref, s_in_ref,
                        o_ref, s_out_ref, *, c, nlev, seg):
    masks = [mask_ref[l] > 0.5 for l in range(nlev + 1)]
    mat = mat_ref[...]
    nw = nw_ref[...]
    small = seg < 16
    per = c // seg
    res = _hgrn_chunks(q_ref, k_ref, g_ref, v_ref, mat, masks, nlev, c)
    for (ci, h), (o, qe, kd, eb, v) in res.items():
        rows = slice(ci * c, (ci + 1) * c)
        cs = slice(h * LANES, (h + 1) * LANES)
        v = v.astype(f32)
        eb_t = jnp.concatenate([eb, eb], axis=0).T
        parts = []
        for j in range(per):
            sl = slice(j * seg, (j + 1) * seg)
            s0 = s_in_ref[ci * per + j, h]
            parts.append(_dot(_mxu_operand(qe[sl], small), _mxu_operand(s0, small)))
            last = j * seg + seg - 1
            upd = _dot_tn(_mxu_operand(kd[sl], small), _mxu_operand(v[sl], small))
            s_out_ref[ci * per + j, h] = s0 * eb_t[:, last:last + 1] + upd
        o = o + jnp.concatenate(parts, axis=0)
        o_ref[rows, cs] = _hgrn_out(o, sg_ref[rows, cs], nw[:, cs]).astype(o_ref.dtype)


def _hgrn(q, g, k, v, sg, nw, state, n_prompt, batch, seq, dec_seq):
    m, width = q.shape
    heads = width // LANES
    nw = nw.reshape(1, width)
    c = CHUNK
    hp = 4
    hw = hp * LANES
    tc = 1024
    nt = seq // tc
    mat, masks, nlev = _hgrn_consts(c, c)
    row_spec = pl.BlockSpec((tc, hw), lambda b, h, t: (b * nt + t, h))
    const2 = lambda shape: pl.BlockSpec(shape, lambda b, h, t: (0,) * len(shape))
    o_p, s_p = pl.pallas_call(
        functools.partial(_hgrn_prompt_kernel, c=c, nlev=nlev, n_chunks=tc // c),
        grid=(batch, heads // hp, nt),
        in_specs=[row_spec] * 5 + [pl.BlockSpec((1, hw), lambda b, h, t: (0, h)), const2((mat.shape[0], HG_SPLIT * c)), const2(masks.shape)],
        out_specs=[row_spec, pl.BlockSpec((1, hp, HG_DK, LANES), lambda b, h, t: (b, h, 0, 0))],
        out_shape=[jax.ShapeDtypeStruct((n_prompt, width), bf16),
                   jax.ShapeDtypeStruct((batch, heads, HG_DK, LANES), f32)],
        scratch_shapes=[pltpu.VMEM((hp, LANES, HG_DK), f32)],
        compiler_params=_cparams(("arbitrary", "arbitrary", "arbitrary")),
        name="hgrn_prompt",
    )(q, g, k, v, sg, nw, _tile3(mat, HG_SPLIT), jnp.asarray(masks))
    n_sample = m - n_prompt
    rows = 2 * c
    per = rows // dec_seq
    mat, masks, nlev = _hgrn_consts(c, dec_seq)
    base = n_prompt // rows
    row_spec = pl.BlockSpec((rows, hw), lambda jb, h: (base + jb, h))
    st_spec = pl.BlockSpec((per, hp, HG_DK, LANES), lambda jb, h: (jb, h, 0, 0))
    const2 = lambda shape: pl.BlockSpec(shape, lambda jb, h: (0,) * len(shape))
    o_s, s_s = pl.pallas_call(
        functools.partial(_hgrn_sample_kernel, c=c, nlev=nlev, seg=dec_seq),
        grid=(n_sample // rows, heads // hp),
        in_specs=[row_spec] * 5 + [pl.BlockSpec((1, hw), lambda jb, h: (0, h)), const2((mat.shape[0], HG_SPLIT * c)),
                                   const2(masks.shape), st_spec],
        out_specs=[pl.BlockSpec((rows, hw), lambda jb, h: (jb, h)), st_spec],
        out_shape=[jax.ShapeDtypeStruct((n_sample, width), bf16), jax.ShapeDtypeStruct(state.shape, f32)],
        compiler_params=_cparams(("arbitrary", "arbitrary")),
        name="hgrn_sample",
    )(q, g, k, v, sg, nw, _tile3(mat, HG_SPLIT), jnp.asarray(masks), state)
    return (o_p, o_s), s_p, s_s


GROUP_W = 512
PAIRS = GROUP_W // LANES
XBC_W = GROUP_W + 2 * M_DSTATE


def _ssd_consts(c, seg):
    t = np.arange(c)
    same = (t[:, None] // seg) == (t[None, :] // seg)
    tril = (same & (t[None, :] <= t[:, None])).astype(np.float32)
    return tril


def _ssd_chunks(chunks, a_row, dsk, nw, lc, tril, get_state, set_state, c, seg, state_t):
    nseg = c // seg
    small = seg < 16
    assert 2 * c == LANES and 2 * M_HEADDIM == LANES and not (state_t and nseg > 1)
    lo_half = lax.broadcasted_iota(jnp.int32, (c, LANES), 1) < M_HEADDIM
    lo_row = lo_half[0:1, :]
    n = len(chunks)
    items = [(ci, p) for ci in range(n) for p in range(PAIRS)]
    pc = lambda p: slice(p * LANES, (p + 1) * LANES)
    acum = [_dot3(lc, dt * a_row) for _, _, _, dt, _ in chunks]
    bmb = [bm.astype(bf16) for _, bm, _, _, _ in chunks]
    cmb = [cm.astype(bf16) for _, _, cm, _, _ in chunks]
    cb2 = [_dot_nt(cmb[ci], jnp.concatenate([bmb[ci], bmb[ci]], axis=0)) for ci in range(n)]
    acum_t = [jnp.concatenate([a, a], axis=0).T for a in acum]
    dt_t = [jnp.concatenate([ch[3], ch[3]], axis=0).T for ch in chunks]
    if state_t:
        bm_t = [ch[1].T.astype(bf16) for ch in chunks]
    else:
        ea_t = [jnp.exp(a[0:8, 0:c]) for a in acum_t]
    y, xw, ea = {}, {}, {}
    for ci, p in items:
        h0, h1 = 2 * p, 2 * p + 1
        xp, dt = chunks[ci][0][:, pc(p)], chunks[ci][3]
        acp = jnp.where(lo_half, acum[ci][:, h0:h0 + 1], acum[ci][:, h1:h1 + 1])
        dtp = jnp.where(lo_half, dt[:, h0:h0 + 1], dt[:, h1:h1 + 1])
        a_src = jnp.where(lo_row, acum_t[ci][h0:h0 + 1, :], acum_t[ci][h1:h1 + 1, :])
        dt_src = jnp.where(lo_row, dt_t[ci][h0:h0 + 1, :], dt_t[ci][h1:h1 + 1, :])
        lm = jnp.exp(jnp.where(tril, acp - a_src, -1e30))
        sc = (cb2[ci] * lm * dt_src).astype(bf16)
        x_blk = jnp.concatenate([jnp.where(lo_half, xp, 0.0), jnp.where(lo_half, 0.0, xp)], axis=0).astype(bf16)
        y[ci, p] = _dot(sc, x_blk)
        if nseg == 1:
            alast = acp[c - 1:c, :]
        else:
            alast = jnp.concatenate(
                [jnp.broadcast_to(acp[j * seg + seg - 1:j * seg + seg, :], (seg, LANES)) for j in range(nseg)], axis=0)
        xw[ci, p] = xp * (jnp.exp(alast - acp) * dtp)
        ea[ci, p] = jnp.exp(acp)
    cs = {}
    if state_t:
        upd = {(ci, p): _dot(bm_t[ci], xw[ci, p].astype(bf16)) for ci, p in items}
        for p in range(PAIRS):
            st = get_state(0, 0, p)
            for ci in range(n):
                cs[ci, p] = _dot(cmb[ci], st.astype(bf16))
                st = st * ea[ci, p][c - 1:c, :] + upd[ci, p]
            set_state(0, 0, p, st)
    else:
        for ci, p in items:
            h0, h1 = 2 * p, 2 * p + 1
            bm, cm = chunks[ci][1], chunks[ci][2]
            parts = []
            for j in range(nseg):
                sl = slice(j * seg, (j + 1) * seg)
                s0 = get_state(ci, j, p)
                parts.append(_dot_nt(_mxu_operand(cm[sl], small), _mxu_operand(s0, small)))
                upd = _dot_tn(_mxu_operand(xw[ci, p][sl], small), _mxu_operand(bm[sl], small))
                last = j * seg + seg - 1
                decay = jnp.concatenate(
                    [jnp.broadcast_to(ea_t[ci][h0:h0 + 1, last:last + 1], (M_HEADDIM, LANES)),
                     jnp.broadcast_to(ea_t[ci][h1:h1 + 1, last:last + 1], (M_HEADDIM, LANES))], axis=0)
                set_state(ci, j, p, s0 * decay + upd)
            cs[ci, p] = parts[0] if nseg == 1 else jnp.concatenate(parts, axis=0)
    outs = []
    for ci in range(n):
        xs, zs = chunks[ci][0], chunks[ci][4]
        ssq = jnp.zeros((c, 1), f32)
        ys = []
        for p in range(PAIRS):
            yp = (y[ci, p] + cs[ci, p] * ea[ci, p] + dsk[:, pc(p)] * xs[:, pc(p)]) * zs[:, pc(p)]
            ssq = ssq + jnp.sum(yp * yp, axis=-1, keepdims=True)
            ys.append(yp)
        scale = lax.rsqrt(ssq * (1.0 / GROUP_W) + EPS)
        outs.append([ys[p] * scale * nw[:, pc(p)] for p in range(PAIRS)])
    return outs


def _conv_taps(xpad_ref, lead, rows, w, b):
    acc = None
    for j in range(M_CONV):
        term = xpad_ref[lead + (slice(5 + j + rows[0], 5 + j + rows[1]), slice(None))] * w[j:j + 1, :]
        acc = term if acc is None else acc + term
    return jax.nn.silu(b + acc)


def _ssd_prompt_kernel(xr_ref, br_ref, cr_ref, dt_ref, zs_ref, wx_ref, wb_ref, wc_ref, bx_ref, bb_ref, bc_ref,
                       alog_ref, dsk_ref, nw_ref, lc_ref, tril_ref, o_ref, s_ref, xpad_ref, xc_ref, st_ref,
                       *, c, rows):
    t_blk = pl.program_id(2)

    @pl.when(t_blk == 0)
    def _():
        st_ref[...] = jnp.zeros_like(st_ref)
        xpad_ref[0:8, :] = jnp.zeros((8, XBC_W), f32)

    @pl.when(t_blk > 0)
    def _():
        xpad_ref[0:8, :] = xpad_ref[rows:rows + 8, :]

    xpad_ref[8:8 + rows, 0:GROUP_W] = xr_ref[...]
    xpad_ref[8:8 + rows, GROUP_W:GROUP_W + M_DSTATE] = br_ref[...]
    xpad_ref[8:8 + rows, GROUP_W + M_DSTATE:XBC_W] = cr_ref[...]
    w = jnp.concatenate([wx_ref[...], wb_ref[...], wc_ref[...]], axis=1)
    b = jnp.concatenate([bx_ref[...], bb_ref[...], bc_ref[...]], axis=1)
    for i in range(rows // c):
        xc_ref[i * c:(i + 1) * c, :] = _conv_taps(xpad_ref, (), (i * c, (i + 1) * c), w, b)

    a_row = -jnp.exp(alog_ref[...])
    dsk = dsk_ref[...]
    nw = nw_ref[...]
    lc = lc_ref[...]
    tril = tril_ref[...] > 0.5

    def get_state(ci, j, p):
        return st_ref[p]

    def set_state(ci, j, p, val):
        st_ref[p] = val

    rs = lambda ci: slice(ci * c, (ci + 1) * c)
    chunks = [(xc_ref[rs(ci), 0:GROUP_W], xc_ref[rs(ci), GROUP_W:GROUP_W + M_DSTATE], xc_ref[rs(ci), GROUP_W + M_DSTATE:XBC_W],
               dt_ref[rs(ci), :], zs_ref[rs(ci), :]) for ci in range(rows // c)]
    outs = _ssd_chunks(chunks, a_row, dsk, nw, lc, tril, get_state, set_state, c, c, state_t=True)
    for ci, out in enumerate(outs):
        for p in range(PAIRS):
            o_ref[rs(ci), p * LANES:(p + 1) * LANES] = out[p].astype(o_ref.dtype)

    @pl.when(t_blk == pl.num_programs(2) - 1)
    def _():
        for p in range(PAIRS):
            s_ref[0, p * LANES:(p + 1) * LANES, :] = st_ref[p].T


def _ssd_sample_kernel(xr_ref, br_ref, cr_ref, dt_ref, zs_ref, wx_ref, wb_ref, wc_ref, bx_ref, bb_ref, bc_ref,
                       alog_ref, dsk_ref, nw_ref, lc_ref, tril_ref, hx_ref, hb_ref, hc_ref, s_in_ref,
                       o_ref, s_out_ref, xpad_ref, *, c, seg):
    nseg = c // seg
    w = jnp.concatenate([wx_ref[...], wb_ref[...], wc_ref[...]], axis=1)
    b = jnp.concatenate([bx_ref[...], bb_ref[...], bc_ref[...]], axis=1)
    conv = []
    for j in range(nseg):
        sl = slice(j * seg, (j + 1) * seg)
        xpad_ref[j, 5:8, 0:GROUP_W] = hx_ref[j]
        xpad_ref[j, 5:8, GROUP_W:GROUP_W + M_DSTATE] = hb_ref[j]
        xpad_ref[j, 5:8, GROUP_W + M_DSTATE:XBC_W] = hc_ref[j]
        xpad_ref[j, 8:8 + seg, 0:GROUP_W] = xr_ref[sl, :]
        xpad_ref[j, 8:8 + seg, GROUP_W:GROUP_W + M_DSTATE] = br_ref[sl, :]
        xpad_ref[j, 8:8 + seg, GROUP_W + M_DSTATE:XBC_W] = cr_ref[sl, :]
        conv.append(_conv_taps(xpad_ref, (j,), (0, seg), w, b))
    xc = jnp.concatenate(conv, axis=0)

    def get_state(ci, j, p):
        return s_in_ref[j, p * LANES:(p + 1) * LANES, :]

    def set_state(ci, j, p, val):
        s_out_ref[j, p * LANES:(p + 1) * LANES, :] = val

    chunks = [(xc[:, 0:GROUP_W], xc[:, GROUP_W:GROUP_W + M_DSTATE], xc[:, GROUP_W + M_DSTATE:XBC_W], dt_ref[...], zs_ref[...])]
    (out,) = _ssd_chunks(chunks, -jnp.exp(alog_ref[...]), dsk_ref[...], nw_ref[...], lc_ref[...],
                         tril_ref[...] > 0.5, get_state, set_state, c, seg, state_t=False)
    for p in range(PAIRS):
        o_ref[:, p * LANES:(p + 1) * LANES] = out[p].astype(o_ref.dtype)


def _ssd(xbc, dt, zs, conv_w, conv_b, alog_p, dsk, nw, state, hist, n_prompt, batch, seq, dec_seq):
    m = xbc.shape[0]
    inner = zs.shape[1]
    c = CHUNK
    xb_blk = inner // M_DSTATE
    cb_blk = xb_blk + M_GROUPS

    def specs(row_map, nrow):
        def rm(fn):
            return lambda *ix: fn(row_map(*ix), ix[1])
        zero = lambda fn: (lambda *ix: fn(0, ix[1]))
        return [
            pl.BlockSpec((nrow, GROUP_W), rm(lambda r, g: (r, g))),
            pl.BlockSpec((nrow, M_DSTATE), rm(lambda r, g: (r, xb_blk + g))),
            pl.BlockSpec((nrow, M_DSTATE), rm(lambda r, g: (r, cb_blk + g))),
            pl.BlockSpec((nrow, LANES), rm(lambda r, g: (r, g))),
            pl.BlockSpec((nrow, GROUP_W), rm(lambda r, g: (r, g))),
            pl.BlockSpec((M_CONV, GROUP_W), zero(lambda r, g: (0, g))),
            pl.BlockSpec((M_CONV, M_DSTATE), zero(lambda r, g: (0, xb_blk + g))),
            pl.BlockSpec((M_CONV, M_DSTATE), zero(lambda r, g: (0, cb_blk + g))),
            pl.BlockSpec((1, GROUP_W), zero(lambda r, g: (0, g))),
            pl.BlockSpec((1, M_DSTATE), zero(lambda r, g: (0, xb_blk + g))),
            pl.BlockSpec((1, M_DSTATE), zero(lambda r, g: (0, cb_blk + g))),
            pl.BlockSpec((1, LANES), zero(lambda r, g: (0, g))),
            pl.BlockSpec((1, GROUP_W), zero(lambda r, g: (0, g))),
            pl.BlockSpec((1, GROUP_W), zero(lambda r, g: (0, g))),
            pl.BlockSpec((c, 3 * c), zero(lambda r, g: (0, 0))),
            pl.BlockSpec((c, 2 * c), zero(lambda r, g: (0, 0))),
        ]

    common = (xbc, xbc, xbc, dt, zs, conv_w, conv_w, conv_w, conv_b, conv_b, conv_b, alog_p, dsk, nw)
    rows = 1024
    nt = seq // rows
    tril = _ssd_consts(c, c)
    in_specs = specs(lambda b, g, t: b * nt + t, rows)
    o_p, s_p = pl.pallas_call(
        functools.partial(_ssd_prompt_kernel, c=c, rows=rows),
        grid=(batch, M_GROUPS, nt),
        in_specs=in_specs,
        out_specs=[pl.BlockSpec((rows, GROUP_W), lambda b, g, t: (b * nt + t, g)),
                   pl.BlockSpec((1, GROUP_W, M_DSTATE), lambda b, g, t: (b, g, 0))],
        out_shape=[jax.ShapeDtypeStruct((n_prompt, inner), bf16),
                   jax.ShapeDtypeStruct((batch, inner, M_DSTATE), f32)],
        scratch_shapes=[pltpu.VMEM((rows + 8, XBC_W), f32), pltpu.VMEM((rows, XBC_W), f32),
                        pltpu.VMEM((PAIRS, LANES, M_DSTATE), f32)],
        compiler_params=_cparams(("arbitrary", "arbitrary", "arbitrary")),
        name="ssd_prompt",
    )(*common, _tile3(tril), jnp.asarray(np.tile(tril, (1, 2))))
    n_sample = m - n_prompt
    per = c // dec_seq
    base = n_prompt // c
    tril = _ssd_consts(c, dec_seq)
    in_specs = specs(lambda jb, g: base + jb, c)
    in_specs += [
        pl.BlockSpec((per, M_CONV - 1, GROUP_W), lambda jb, g: (jb, 0, g)),
        pl.BlockSpec((per, M_CONV - 1, M_DSTATE), lambda jb, g: (jb, 0, xb_blk + g)),
        pl.BlockSpec((per, M_CONV - 1, M_DSTATE), lambda jb, g: (jb, 0, cb_blk + g)),
        pl.BlockSpec((per, GROUP_W, M_DSTATE), lambda jb, g: (jb, g, 0)),
    ]
    o_s, s_s = pl.pallas_call(
        functools.partial(_ssd_sample_kernel, c=c, seg=dec_seq),
        grid=(n_sample // c, M_GROUPS),
        in_specs=in_specs,
        out_specs=[pl.BlockSpec((c, GROUP_W), lambda jb, g: (jb, g)),
                   pl.BlockSpec((per, GROUP_W, M_DSTATE), lambda jb, g: (jb, g, 0))],
        out_shape=[jax.ShapeDtypeStruct((n_sample, inner), bf16), jax.ShapeDtypeStruct(state.shape, f32)],
        scratch_shapes=[pltpu.VMEM((per, 16, XBC_W), f32)],
        compiler_params=_cparams(("arbitrary", "arbitrary")),
        name="ssd_sample",
    )(*common, _tile3(tril), jnp.asarray(np.tile(tril, (1, 2))), hist, hist, hist, state)
    return (o_p, o_s), s_p, s_s


def kernel(x_prompt, x_sample, state_hgrn, state_ssm, state_conv, norm_mix, w_in, hg_lb_logits, hg_norm, conv_w, conv_b,
           dt_bias, a_log, d_skip, ssm_norm, w_branch_hg, w_branch_ssm, w_out, norm_ffn, w_ffn_gate, w_ffn_up,
           w_ffn_down, norm_final):
    batch, seq, d = x_prompt.shape
    dec_batch, dec_seq, _ = x_sample.shape
    n_prompt, n_sample = batch * seq, dec_batch * dec_seq
    hg_heads = state_hgrn.shape[2]
    kdim = hg_heads * HG_DK
    vdim = d
    inner = d
    m_heads = state_ssm.shape[2]
    conv_dim = conv_w.shape[2]
    hpg = m_heads // M_GROUPS

    xp2, xs2 = x_prompt.reshape(n_prompt, d), x_sample.reshape(n_sample, d)
    h = _rmsnorm([xp2, xs2], norm_mix[0], bf16, n_prompt)

    wt = jnp.swapaxes(w_in, 1, 2)[0]
    o_q, o_f, o_v, o_g, o_z, o_xbc = 0, kdim, 2 * kdim, 2 * kdim + vdim, 2 * kdim + 2 * vdim, 2 * kdim + 2 * vdim + inner
    o_dt = o_xbc + conv_dim
    o_gate = o_dt + m_heads
    scale = HG_DK ** -0.5
    ident = lambda a, c, t: (a[0],)
    silu = lambda a, c, t: (jax.nn.silu(a[0]),)
    wide = dict(tm=1024, tn=2048, w_single=True)
    (q,) = _mm([h], [(wt, o_q, True)], kdim, lambda a, c, t: (a[0] * scale,), [f32], name="proj_q", **wide)
    log_f, k = _mm([h], [(wt, o_f, True)], kdim, _epi_forget, [f32, f32], cols=[hg_lb_logits], name="proj_f",
                   tm=512, tn=2048, w_single=True)
    (v,) = _mm([h], [(wt, o_v, True)], vdim, ident, [bf16], name="proj_v", **wide)
    (sg,) = _mm([h], [(wt, o_g, True)], vdim, silu, [f32], name="proj_g", **wide)
    (zs,) = _mm([h], [(wt, o_z, True)], inner, silu, [f32], name="proj_z", **wide)
    (xbc,) = _mm([h], [(wt, o_xbc, True)], conv_dim, ident, [f32], tm=1024, tn=conv_dim // 2, w_single=True, name="proj_xbc")
    pad_heads = lambda p: jnp.pad(p.reshape(M_GROUPS, hpg, -1), ((0, 0), (0, LANES - hpg), (0, 0))).reshape(M_GROUPS * LANES, -1)
    (dt,) = _mm([h], [(pad_heads(wt[o_dt:o_gate]), 0, True)], M_GROUPS * LANES,
                lambda a, c, t: (_softplus(a[0] + c[0]),), [f32], cols=[pad_heads(dt_bias[0]).T], name="proj_dt")
    (gates,) = _mm([h], [(wt, o_gate, True)], 2 * d, lambda a, c, t: (jax.nn.sigmoid(a[0]),), [f32], name="proj_gate", **wide)

    o_hg, shp, shs = _hgrn(q, log_f, k, v, sg, hg_norm[0], state_hgrn[0], n_prompt, batch, seq, dec_seq)
    dsk = jnp.repeat(d_skip[0], M_HEADDIM).reshape(1, inner)
    o_m, smp, sms = _ssd(xbc, dt, zs, conv_w[0], conv_b[0].reshape(1, conv_dim), pad_heads(a_log[0]).T, dsk,
                         ssm_norm[0].reshape(1, inner), state_ssm[0].reshape(dec_batch, inner, M_DSTATE),
                         state_conv[0], n_prompt, batch, seq, dec_seq)

    tn = 1024
    (merged,) = _mm([o_hg, o_m], [(w_branch_hg[0], 0, False), (w_branch_ssm[0], 0, False)], d,
                    lambda a, c, t: (t[0] * a[0] + t[1] * a[1],), [bf16],
                    tiles=[(gates, lambda j, i: (i, j)), (gates, lambda j, i: (i, j + d // tn))],
                    tm=512, tn=tn, w_single=True, name="merge")
    tm = 512
    n_p_tiles = n_prompt // tm

    def add_x_norm(a, c, t):
        x1 = jnp.where(pl.program_id(1) < n_p_tiles, t[0], t[1]) + a[0]
        ms = jnp.mean(x1 * x1, axis=-1, keepdims=True)
        return x1, x1 * lax.rsqrt(ms + EPS) * c[0]

    x1, h2 = _mm([merged], [(w_out[0], 0, False)], d, add_x_norm, [f32, bf16], cols=[norm_ffn],
                 tiles=[(xp2, lambda j, i: (jnp.minimum(i, n_p_tiles - 1), j)),
                        (xs2, lambda j, i: (jnp.maximum(i - n_p_tiles, 0), j), pl.Buffered(1))],
                 tm=tm, tn=d, w_single=True, name="out_proj")
    ffn = _ffn(h2, w_ffn_gate[0], w_ffn_up[0], w_ffn_down[0])
    y_p, y_s = _rmsnorm([x1], norm_final, f32, n_prompt, split_out=True, res=ffn)

    hist = M_CONV - 1
    new_conv_p = jnp.stack([xbc[(b + 1) * seq - hist:(b + 1) * seq] for b in range(batch)])[None]
    new_conv_s = xbc[n_prompt:].reshape(dec_batch, dec_seq, conv_dim)[:, dec_seq - hist:][None]
    return (y_p.reshape(batch, seq, d), y_s.reshape(dec_batch, dec_seq, d),
            shp[None], smp.reshape(1, batch, m_heads, M_HEADDIM, M_DSTATE), new_conv_p,
            shs[None], sms.reshape(1, dec_batch, m_heads, M_HEADDIM, M_DSTATE), new_conv_s)
```

```python
import functools

import numpy as np
import jax
import jax.numpy as jnp
from jax import lax
from jax.experimental import pallas as pl
from jax.experimental.pallas import tpu as pltpu

f32 = jnp.float32
bf16 = jnp.bfloat16

EPS = 1e-6
LANES = 128
CHUNK = 64
HG_DK = 128
M_HEADDIM = 64
M_DSTATE = 128
M_GROUPS = 4
M_CONV = 4
VMEM_LIMIT = 56 * 1024 * 1024


def _cparams(sem):
    return pltpu.CompilerParams(dimension_semantics=sem, vmem_limit_bytes=VMEM_LIMIT)


def _dot(a, b):
    return jnp.dot(a, b, preferred_element_type=f32)


def _dot_nt(a, b):
    return lax.dot_general(a, b, (((1,), (1,)), ((), ())), preferred_element_type=f32)


def _dot_tn(a, b):
    return lax.dot_general(a, b, (((0,), (0,)), ((), ())), preferred_element_type=f32)


def _split(x, n):
    pieces = []
    for _ in range(n - 1):
        p = x.astype(bf16)
        pieces.append(p)
        x = x - p.astype(f32)
    return pieces + [x.astype(bf16)]


def _tile3(m, n=3):
    return jnp.asarray(np.tile(m, (1, n)), bf16)


def _dot3(mn, x):
    return _dot(mn, jnp.concatenate(_split(x, mn.shape[1] // x.shape[0]), axis=0))


def _mxu_operand(x, small):
    xb = x.astype(bf16)
    return xb.astype(f32) if small else xb


def _rmsnorm_kernel(*refs, n_in, n_out, split, has_res):
    x_refs, w_ref, o_refs = refs[:n_in], refs[n_in], refs[n_in + 1 + has_res:]

    def run(x_ref, o_ref):
        x = x_ref[...]
        if has_res:
            x = x + refs[n_in + 1][...]
        ms = jnp.mean(x * x, axis=-1, keepdims=True)
        o_ref[...] = (x * lax.rsqrt(ms + EPS) * w_ref[...]).astype(o_ref.dtype)

    if n_in == 1 and n_out == 1:
        run(x_refs[0], o_refs[0])
    else:
        i = pl.program_id(0)
        pl.when(i < split)(lambda: run(x_refs[0], o_refs[0]))
        pl.when(i >= split)(lambda: run(x_refs[-1], o_refs[-1]))


def _rmsnorm(xs, w, out_dtype, n_first, split_out=False, res=None, tm=512):
    d = xs[0].shape[1]
    m = sum(x.shape[0] for x in xs)
    split = n_first // tm
    first = lambda i: (jnp.minimum(i, split - 1), 0)
    second = lambda i: (jnp.maximum(i - split, 0), 0)
    whole = lambda i: (i, 0)
    blk = lambda fn: pl.BlockSpec((tm, d), fn)
    in_specs = [blk(whole)] if len(xs) == 1 else [blk(first), blk(second)]
    if split_out:
        out_specs = [blk(first), blk(second)]
        out_shape = [jax.ShapeDtypeStruct((n_first, d), out_dtype), jax.ShapeDtypeStruct((m - n_first, d), out_dtype)]
    else:
        out_specs = [blk(whole)]
        out_shape = [jax.ShapeDtypeStruct((m, d), out_dtype)]
    extra = [] if res is None else [res]
    outs = pl.pallas_call(
        functools.partial(_rmsnorm_kernel, n_in=len(xs), n_out=len(out_shape), split=split, has_res=len(extra)),
        grid=(m // tm,),
        in_specs=in_specs + [pl.BlockSpec((1, d), lambda i: (0, 0))] + [blk(whole) for _ in extra],
        out_specs=out_specs,
        out_shape=out_shape,
        compiler_params=_cparams(("arbitrary",)),
        name="rmsnorm",
    )(*xs, w.reshape(1, d), *extra)
    return outs if split_out else outs[0]


def _mm_kernel(*refs, a_parts, split, nd, nc, nt, no, epi, w_t):
    a_vals = []
    for parts in a_parts:
        if parts == 2:
            a_vals.append(jnp.where(pl.program_id(1) < split, refs[0][...], refs[1][...]))
        else:
            a_vals.append(refs[0][...])
        refs = refs[parts:]
    na = len(a_vals)
    w_refs = refs[:nd]
    c_refs = refs[nd:nd + nc]
    t_refs = refs[nd + nc:nd + nc + nt]
    o_refs = refs[nd + nc + nt:nd + nc + nt + no]
    wb_refs = refs[nd + nc + nt + no:]

    @pl.when(pl.program_id(1) == 0)
    def _():
        for w_ref, wb_ref in zip(w_refs, wb_refs):
            wb_ref[...] = w_ref[...].astype(bf16)

    accs = [(_dot_nt if t else _dot)(a_vals[min(i, na - 1)], wb_ref[...])
            for i, (wb_ref, t) in enumerate(zip(wb_refs, w_t))]
    outs = epi(accs, [c[...] for c in c_refs], [t[...] for t in t_refs])
    for o_ref, o in zip(o_refs, outs):
        o_ref[...] = o.astype(o_ref.dtype)


def _col_map(j, i, off):
    return (0, j + off)


def _tile_map(j, i, off):
    return (i, j + off)


def _row_elem_map(j, i, off, tn):
    return (pl.multiple_of(off + j * tn, 8), 0)


def _mm(a_list, w_list, n_cols, epi, out_dtypes, cols=(), tiles=(), tm=1024, tn=512, w_single=False, name="mm"):
    na, nd, nc, nt, no = len(a_list), len(w_list), len(cols), len(tiles), len(out_dtypes)
    assert na in (1, nd)
    a_list = [a if isinstance(a, tuple) else (a,) for a in a_list]
    m = sum(p.shape[0] for p in a_list[0])
    split = a_list[0][0].shape[0] // tm
    in_specs, a_flat = [], []
    for parts in a_list:
        k = parts[0].shape[1]
        if len(parts) == 2:
            in_specs.append(pl.BlockSpec((tm, k), lambda j, i: (jnp.minimum(i, split - 1), 0)))
            in_specs.append(pl.BlockSpec((tm, k), lambda j, i: (jnp.maximum(i - split, 0), 0)))
        else:
            in_specs.append(pl.BlockSpec((tm, k), lambda j, i: (i, 0)))
        a_flat += list(parts)
    scratch = []
    mode = dict(pipeline_mode=pl.Buffered(1)) if w_single else {}
    for w, off, transposed in w_list:
        if transposed:
            k = w.shape[1]
            assert off % 8 == 0
            in_specs.append(pl.BlockSpec((pl.Element(tn), pl.Element(k)), functools.partial(_row_elem_map, off=off, tn=tn), **mode))
            scratch.append(pltpu.VMEM((tn, k), bf16))
        else:
            k = w.shape[0]
            in_specs.append(pl.BlockSpec((k, tn), functools.partial(_col_map, off=off // tn), **mode))
            scratch.append(pltpu.VMEM((k, tn), bf16))
    in_specs += [pl.BlockSpec((c.shape[0], tn), functools.partial(_col_map, off=0)) for c in cols]
    in_specs += [pl.BlockSpec((tm, tn), t[1], **(dict(pipeline_mode=t[2]) if len(t) > 2 else {})) for t in tiles]
    outs = pl.pallas_call(
        functools.partial(_mm_kernel, a_parts=tuple(len(p) for p in a_list), split=split, nd=nd, nc=nc, nt=nt, no=no,
                          epi=epi, w_t=tuple(t for _, _, t in w_list)),
        grid=(n_cols // tn, m // tm),
        in_specs=in_specs,
        out_specs=[pl.BlockSpec((tm, tn), functools.partial(_tile_map, off=0)) for _ in out_dtypes],
        out_shape=[jax.ShapeDtypeStruct((m, n_cols), dt) for dt in out_dtypes],
        scratch_shapes=scratch,
        compiler_params=_cparams(("arbitrary", "arbitrary")),
        name=name,
    )(*a_flat, *[w for w, _, _ in w_list], *cols, *[t[0] for t in tiles])
    return outs


def _ffn_kernel(h_ref, wg_ref, wu_ref, wd_ref, o_ref):
    @pl.when(pl.program_id(1) == 0)
    def _():
        o_ref[...] = jnp.zeros_like(o_ref)

    h = h_ref[...]
    gate = _dot(h, wg_ref[...].astype(bf16))
    up = _dot(h, wu_ref[...].astype(bf16))
    act = (jax.nn.silu(gate) * up).astype(bf16)
    o_ref[...] += _dot(act, wd_ref[...].astype(bf16))


def _ffn(h, w_gate, w_up, w_down, tm=1536, tf=256):
    m, d = h.shape
    hidden = w_gate.shape[1]
    return pl.pallas_call(
        _ffn_kernel,
        grid=(m // tm, hidden // tf),
        in_specs=[pl.BlockSpec((tm, d), lambda i, f: (i, 0), pipeline_mode=pl.Buffered(1)),
                  pl.BlockSpec((d, tf), lambda i, f: (0, f)),
                  pl.BlockSpec((d, tf), lambda i, f: (0, f)),
                  pl.BlockSpec((tf, d), lambda i, f: (f, 0))],
        out_specs=pl.BlockSpec((tm, d), lambda i, f: (i, 0)),
        out_shape=jax.ShapeDtypeStruct((m, d), f32),
        compiler_params=_cparams(("arbitrary", "arbitrary")),
        name="ffn",
    )(h, w_gate, w_up, w_down)


def _softplus(x):
    return jnp.maximum(x, 0.0) + jnp.log1p(jnp.exp(-jnp.abs(x)))


def _epi_forget(accs, cols, tiles):
    logits = cols[0]
    mx = jnp.max(logits, axis=0, keepdims=True)
    e = jnp.exp(logits - mx)
    lb = e[0:1, :] / jnp.sum(e, axis=0, keepdims=True)
    sig = jax.nn.sigmoid(accs[0])
    log_f = jnp.log(lb + (1.0 - lb) * sig)
    k = (1.0 - lb) * (1.0 - sig)
    return log_f, k


HG_SPLIT = 2


def _hgrn_consts(c, seg):
    nlev = int(np.log2(seg))
    mat = np.zeros(((nlev + 2) * c, c), np.float32)
    masks = np.zeros((nlev + 1, c, c), np.float32)
    for t in range(c):
        tl = t % seg
        base = t - tl
        for l in range(nlev):
            h = 1 << l
            pos = tl % (2 * h)
            ref = base + tl - pos + h - 1
            if pos >= h:
                mat[l * c + t, ref + 1:t + 1] = 1
            else:
                mat[l * c + t, t + 1:ref + 1] = 1
            for s in range(base, base + seg):
                sl = s % seg
                if sl // (2 * h) == tl // (2 * h) and pos >= h and sl % (2 * h) < h:
                    masks[l, t, s] = 1
        mat[nlev * c + t, base:t + 1] = 1
        mat[(nlev + 1) * c + t, t + 1:base + seg] = 1
        masks[nlev, t, t] = 1
    return mat, masks, nlev


def _hgrn_chunks(q_ref, k_ref, g_ref, v_ref, mat, masks, nlev, c):
    n_chunks = q_ref.shape[0] // c
    heads = q_ref.shape[1] // LANES
    rows = lambda ci: slice(ci * c, (ci + 1) * c)
    cols = lambda h: slice(h * LANES, (h + 1) * LANES)
    items = [(ci, h) for ci in range(n_chunks) for h in range(heads)]
    e_all = [jnp.exp(_dot3(mat, g_ref[rows(ci), :])) for ci in range(n_chunks)]
    q = {(ci, h): q_ref[rows(ci), cols(h)] for ci, h in items}
    k = {(ci, h): k_ref[rows(ci), cols(h)] for ci, h in items}
    v = {(ci, h): v_ref[rows(ci), cols(h)].astype(bf16) for ci, h in items}
    s = {it: jnp.where(masks[nlev], _dot_nt(q[it].astype(bf16), k[it].astype(bf16)), 0.0) for it in items}
    for l in range(nlev):
        for ci, h in items:
            e = e_all[ci][l * c:(l + 1) * c, cols(h)]
            s[ci, h] = jnp.where(masks[l], _dot_nt((q[ci, h] * e).astype(bf16), (k[ci, h] * e).astype(bf16)), s[ci, h])
    res = {}
    for ci, h in items:
        o = _dot(s[ci, h].astype(bf16), v[ci, h])
        eb = e_all[ci][nlev * c:(nlev + 1) * c, cols(h)]
        er = e_all[ci][(nlev + 1) * c:, cols(h)]
        res[ci, h] = (o, q[ci, h] * eb, k[ci, h] * er, eb, v[ci, h])
    return res


def _hgrn_out(o, sg, nw):
    ms = jnp.mean(o * o, axis=-1, keepdims=True)
    return o * lax.rsqrt(ms + EPS) * nw * sg


def _hgrn_prompt_kernel(q_ref, g_ref, k_ref, v_ref, sg_ref, nw_ref, mat_ref, mask_ref, o_ref, s_ref, st_ref,
                        *, c, nlev, n_chunks):
    t_blk = pl.program_id(2)

    @pl.when(t_blk == 0)
    def _():
        st_ref[...] = jnp.zeros_like(st_ref)

    masks = [mask_ref[l] > 0.5 for l in range(nlev + 1)]
    mat = mat_ref[...]
    nw = nw_ref[...]

    res = _hgrn_chunks(q_ref, k_ref, g_ref, v_ref, mat, masks, nlev, c)
    upd = {it: _dot_tn(r[4], r[2].astype(bf16)) for it, r in res.items()}
    for h in range(st_ref.shape[0]):
        cs = slice(h * LANES, (h + 1) * LANES)
        st = st_ref[h]
        for ci in range(n_chunks):
            rows = slice(ci * c, (ci + 1) * c)
            o, qe, _, eb, _ = res[ci, h]
            o = o + _dot_nt(qe.astype(bf16), st.astype(bf16))
            st = st * eb[c - 1:c, :] + upd[ci, h]
            o_ref[rows, cs] = _hgrn_out(o, sg_ref[rows, cs], nw[:, cs]).astype(o_ref.dtype)
        st_ref[h] = st

    @pl.when(t_blk == pl.num_programs(2) - 1)
    def _():
        for h in range(st_ref.shape[0]):
            s_ref[0, h] = st_ref[h].T


def _hgrn_sample_kernel(q_ref, g_ref, k_ref, v_ref, sg_ref, nw_ref, mat_ref, mask_ref, s_in_ref,
                        o_ref, s_out_ref, *, c, nlev, seg):
    masks = [mask_ref[l] > 0.5 for l in range(nlev + 1)]
    mat = mat_ref[...]
    nw = nw_ref[...]
    small = seg < 16
    per = c // seg
    res = _hgrn_chunks(q_ref, k_ref, g_ref, v_ref, mat, masks, nlev, c)
    for (ci, h), (o, qe, kd, eb, v) in res.items():
        rows = slice(ci * c, (ci + 1) * c)
        cs = slice(h * LANES, (h + 1) * LANES)
        v = v.astype(f32)
        eb_t = jnp.concatenate([eb, eb], axis=0).T
        parts = []
        for j in range(per):
            sl = slice(j * seg, (j + 1) * seg)
            s0 = s_in_ref[ci * per + j, h]
            parts.append(_dot(_mxu_operand(qe[sl], small), _mxu_operand(s0, small)))
            last = j * seg + seg - 1
            upd = _dot_tn(_mxu_operand(kd[sl], small), _mxu_operand(v[sl], small))
            s_out_ref[ci * per + j, h] = s0 * eb_t[:, last:last + 1] + upd
        o = o + jnp.concatenate(parts, axis=0)
        o_ref[rows, cs] = _hgrn_out(o, sg_ref[rows, cs], nw[:, cs]).astype(o_ref.dtype)


def _hgrn(q, g, k, v, sg, nw, state, n_prompt, batch, seq, dec_seq):
    m, width = q.shape
    heads = width // LANES
    nw = nw.reshape(1, width)
    c = CHUNK
    hp = 4
    hw = hp * LANES
    tc = 1024
    nt = seq // tc
    mat, masks, nlev = _hgrn_consts(c, c)
    row_spec = pl.BlockSpec((tc, hw), lambda b, h, t: (b * nt + t, h))
    const2 = lambda shape: pl.BlockSpec(shape, lambda b, h, t: (0,) * len(shape))
    o_p, s_p = pl.pallas_call(
        functools.partial(_hgrn_prompt_kernel, c=c, nlev=nlev, n_chunks=tc // c),
        grid=(batch, heads // hp, nt),
        in_specs=[row_spec] * 5 + [pl.BlockSpec((1, hw), lambda b, h, t: (0, h)), const2((mat.shape[0], HG_SPLIT * c)), const2(masks.shape)],
        out_specs=[row_spec, pl.BlockSpec((1, hp, HG_DK, LANES), lambda b, h, t: (b, h, 0, 0))],
        out_shape=[jax.ShapeDtypeStruct((n_prompt, width), bf16),
                   jax.ShapeDtypeStruct((batch, heads, HG_DK, LANES), f32)],
        scratch_shapes=[pltpu.VMEM((hp, LANES, HG_DK), f32)],
        compiler_params=_cparams(("arbitrary", "arbitrary", "arbitrary")),
        name="hgrn_prompt",
    )(q, g, k, v, sg, nw, _tile3(mat, HG_SPLIT), jnp.asarray(masks))
    n_sample = m - n_prompt
    rows = 2 * c
    per = rows // dec_seq
    mat, masks, nlev = _hgrn_consts(c, dec_seq)
    base = n_prompt // rows
    row_spec = pl.BlockSpec((rows, hw), lambda jb, h: (base + jb, h))
    st_spec = pl.BlockSpec((per, hp, HG_DK, LANES), lambda jb, h: (jb, h, 0, 0))
    const2 = lambda shape: pl.BlockSpec(shape, lambda jb, h: (0,) * len(shape))
    o_s, s_s = pl.pallas_call(
        functools.partial(_hgrn_sample_kernel, c=c, nlev=nlev, seg=dec_seq),
        grid=(n_sample // rows, heads // hp),
        in_specs=[row_spec] * 5 + [pl.BlockSpec((1, hw), lambda jb, h: (0, h)), const2((mat.shape[0], HG_SPLIT * c)),
                                   const2(masks.shape), st_spec],
        out_specs=[pl.BlockSpec((rows, hw), lambda jb, h: (jb, h)), st_spec],
        out_shape=[jax.ShapeDtypeStruct((n_sample, width), bf16), jax.ShapeDtypeStruct(state.shape, f32)],
        compiler_params=_cparams(("arbitrary", "arbitrary")),
        name="hgrn_sample",
    )(q, g, k, v, sg, nw, _tile3(mat, HG_SPLIT), jnp.asarray(masks), state)
    return (o_p, o_s), s_p, s_s


GROUP_W = 512
PAIRS = GROUP_W // LANES
XBC_W = GROUP_W + 2 * M_DSTATE


def _ssd_consts(c, seg):
    t = np.arange(c)
    same = (t[:, None] // seg) == (t[None, :] // seg)
    tril = (same & (t[None, :] <= t[:, None])).astype(np.float32)
    return tril


def _ssd_chunks(chunks, a_row, dsk, nw, lc, tril, get_state, set_state, c, seg, state_t):
    nseg = c // seg
    small = seg < 16
    assert 2 * c == LANES and 2 * M_HEADDIM == LANES and not (state_t and nseg > 1)
    lo_half = lax.broadcasted_iota(jnp.int32, (c, LANES), 1) < M_HEADDIM
    lo_row = lo_half[0:1, :]
    n = len(chunks)
    items = [(ci, p) for ci in range(n) for p in range(PAIRS)]
    pc = lambda p: slice(p * LANES, (p + 1) * LANES)
    acum = [_dot3(lc, dt * a_row) for _, _, _, dt, _ in chunks]
    bmb = [bm.astype(bf16) for _, bm, _, _, _ in chunks]
    cmb = [cm.astype(bf16) for _, _, cm, _, _ in chunks]
    cb2 = [_dot_nt(cmb[ci], jnp.concatenate([bmb[ci], bmb[ci]], axis=0)) for ci in range(n)]
    acum_t = [jnp.concatenate([a, a], axis=0).T for a in acum]
    dt_t = [jnp.concatenate([ch[3], ch[3]], axis=0).T for ch in chunks]
    if state_t:
        bm_t = [ch[1].T.astype(bf16) for ch in chunks]
    else:
        ea_t = [jnp.exp(a[0:8, 0:c]) for a in acum_t]
    y, xw, ea = {}, {}, {}
    for ci, p in items:
        h0, h1 = 2 * p, 2 * p + 1
        xp, dt = chunks[ci][0][:, pc(p)], chunks[ci][3]
        acp = jnp.where(lo_half, acum[ci][:, h0:h0 + 1], acum[ci][:, h1:h1 + 1])
        dtp = jnp.where(lo_half, dt[:, h0:h0 + 1], dt[:, h1:h1 + 1])
        a_src = jnp.where(lo_row, acum_t[ci][h0:h0 + 1, :], acum_t[ci][h1:h1 + 1, :])
        dt_src = jnp.where(lo_row, dt_t[ci][h0:h0 + 1, :], dt_t[ci][h1:h1 + 1, :])
        lm = jnp.exp(jnp.where(tril, acp - a_src, -1e30))
        sc = (cb2[ci] * lm * dt_src).astype(bf16)
        x_blk = jnp.concatenate([jnp.where(lo_half, xp, 0.0), jnp.where(lo_half, 0.0, xp)], axis=0).astype(bf16)
        y[ci, p] = _dot(sc, x_blk)
        if nseg == 1:
            alast = acp[c - 1:c, :]
        else:
            alast = jnp.concatenate(
                [jnp.broadcast_to(acp[j * seg + seg - 1:j * seg + seg, :], (seg, LANES)) for j in range(nseg)], axis=0)
        xw[ci, p] = xp * (jnp.exp(alast - acp) * dtp)
        ea[ci, p] = jnp.exp(acp)
    cs = {}
    if state_t:
        upd = {(ci, p): _dot(bm_t[ci], xw[ci, p].astype(bf16)) for ci, p in items}
        for p in range(PAIRS):
            st = get_state(0, 0, p)
            for ci in range(n):
                cs[ci, p] = _dot(cmb[ci], st.astype(bf16))
                st = st * ea[ci, p][c - 1:c, :] + upd[ci, p]
            set_state(0, 0, p, st)
    else:
        for ci, p in items:
            h0, h1 = 2 * p, 2 * p + 1
            bm, cm = chunks[ci][1], chunks[ci][2]
            parts = []
            for j in range(nseg):
                sl = slice(j * seg, (j + 1) * seg)
                s0 = get_state(ci, j, p)
                parts.append(_dot_nt(_mxu_operand(cm[sl], small), _mxu_operand(s0, small)))
                upd = _dot_tn(_mxu_operand(xw[ci, p][sl], small), _mxu_operand(bm[sl], small))
                last = j * seg + seg - 1
                decay = jnp.concatenate(
                    [jnp.broadcast_to(ea_t[ci][h0:h0 + 1, last:last + 1], (M_HEADDIM, LANES)),
                     jnp.broadcast_to(ea_t[ci][h1:h1 + 1, last:last + 1], (M_HEADDIM, LANES))], axis=0)
                set_state(ci, j, p, s0 * decay + upd)
            cs[ci, p] = parts[0] if nseg == 1 else jnp.concatenate(parts, axis=0)
    outs = []
    for ci in range(n):
        xs, zs = chunks[ci][0], chunks[ci][4]
        ssq = jnp.zeros((c, 1), f32)
        ys = []
        for p in range(PAIRS):
            yp = (y[ci, p] + cs[ci, p] * ea[ci, p] + dsk[:, pc(p)] * xs[:, pc(p)]) * zs[:, pc(p)]
            ssq = ssq + jnp.sum(yp * yp, axis=-1, keepdims=True)
            ys.append(yp)
        scale = lax.rsqrt(ssq * (1.0 / GROUP_W) + EPS)
        outs.append([ys[p] * scale * nw[:, pc(p)] for p in range(PAIRS)])
    return outs


def _conv_taps(xpad_ref, lead, rows, w, b):
    acc = None
    for j in range(M_CONV):
        term = xpad_ref[lead + (slice(5 + j + rows[0], 5 + j + rows[1]), slice(None))] * w[j:j + 1, :]
        acc = term if acc is None else acc + term
    return jax.nn.silu(b + acc)


def _ssd_prompt_kernel(xr_ref, br_ref, cr_ref, dt_ref, zs_ref, wx_ref, wb_ref, wc_ref, bx_ref, bb_ref, bc_ref,
                       alog_ref, dsk_ref, nw_ref, lc_ref, tril_ref, o_ref, s_ref, xpad_ref, xc_ref, st_ref,
                       *, c, rows):
    t_blk = pl.program_id(2)

    @pl.when(t_blk == 0)
    def _():
        st_ref[...] = jnp.zeros_like(st_ref)
        xpad_ref[0:8, :] = jnp.zeros((8, XBC_W), f32)

    @pl.when(t_blk > 0)
    def _():
        xpad_ref[0:8, :] = xpad_ref[rows:rows + 8, :]

    xpad_ref[8:8 + rows, 0:GROUP_W] = xr_ref[...]
    xpad_ref[8:8 + rows, GROUP_W:GROUP_W + M_DSTATE] = br_ref[...]
    xpad_ref[8:8 + rows, GROUP_W + M_DSTATE:XBC_W] = cr_ref[...]
    w = jnp.concatenate([wx_ref[...], wb_ref[...], wc_ref[...]], axis=1)
    b = jnp.concatenate([bx_ref[...], bb_ref[...], bc_ref[...]], axis=1)
    for i in range(rows // c):
        xc_ref[i * c:(i + 1) * c, :] = _conv_taps(xpad_ref, (), (i * c, (i + 1) * c), w, b)

    a_row = -jnp.exp(alog_ref[...])
    dsk = dsk_ref[...]
    nw = nw_ref[...]
    lc = lc_ref[...]
    tril = tril_ref[...] > 0.5

    def get_state(ci, j, p):
        return st_ref[p]

    def set_state(ci, j, p, val):
        st_ref[p] = val

    rs = lambda ci: slice(ci * c, (ci + 1) * c)
    chunks = [(xc_ref[rs(ci), 0:GROUP_W], xc_ref[rs(ci), GROUP_W:GROUP_W + M_DSTATE], xc_ref[rs(ci), GROUP_W + M_DSTATE:XBC_W],
               dt_ref[rs(ci), :], zs_ref[rs(ci), :]) for ci in range(rows // c)]
    outs = _ssd_chunks(chunks, a_row, dsk, nw, lc, tril, get_state, set_state, c, c, state_t=True)
    for ci, out in enumerate(outs):
        for p in range(PAIRS):
            o_ref[rs(ci), p * LANES:(p + 1) * LANES] = out[p].astype(o_ref.dtype)

    @pl.when(t_blk == pl.num_programs(2) - 1)
    def _():
        for p in range(PAIRS):
            s_ref[0, p * LANES:(p + 1) * LANES, :] = st_ref[p].T


def _ssd_sample_kernel(xr_ref, br_ref, cr_ref, dt_ref, zs_ref, wx_ref, wb_ref, wc_ref, bx_ref, bb_ref, bc_ref,
                       alog_ref, dsk_ref, nw_ref, lc_ref, tril_ref, hx_ref, hb_ref, hc_ref, s_in_ref,
                       o_ref, s_out_ref, xpad_ref, *, c, seg):
    nseg = c // seg
    n_chunks = xr_ref.shape[0] // c
    w = jnp.concatenate([wx_ref[...], wb_ref[...], wc_ref[...]], axis=1)
    b = jnp.concatenate([bx_ref[...], bb_ref[...], bc_ref[...]], axis=1)
    conv = []
    for j in range(n_chunks * nseg):
        sl = slice(j * seg, (j + 1) * seg)
        xpad_ref[j, 5:8, 0:GROUP_W] = hx_ref[j]
        xpad_ref[j, 5:8, GROUP_W:GROUP_W + M_DSTATE] = hb_ref[j]
        xpad_ref[j, 5:8, GROUP_W + M_DSTATE:XBC_W] = hc_ref[j]
        xpad_ref[j, 8:8 + seg, 0:GROUP_W] = xr_ref[sl, :]
        xpad_ref[j, 8:8 + seg, GROUP_W:GROUP_W + M_DSTATE] = br_ref[sl, :]
        xpad_ref[j, 8:8 + seg, GROUP_W + M_DSTATE:XBC_W] = cr_ref[sl, :]
        conv.append(_conv_taps(xpad_ref, (j,), (0, seg), w, b))

    def get_state(ci, j, p):
        return s_in_ref[ci * nseg + j, p * LANES:(p + 1) * LANES, :]

    def set_state(ci, j, p, val):
        s_out_ref[ci * nseg + j, p * LANES:(p + 1) * LANES, :] = val

    chunks = []
    for ci in range(n_chunks):
        xc = jnp.concatenate(conv[ci * nseg:(ci + 1) * nseg], axis=0)
        rs = slice(ci * c, (ci + 1) * c)
        chunks.append((xc[:, 0:GROUP_W], xc[:, GROUP_W:GROUP_W + M_DSTATE], xc[:, GROUP_W + M_DSTATE:XBC_W],
                       dt_ref[rs, :], zs_ref[rs, :]))
    outs = _ssd_chunks(chunks, -jnp.exp(alog_ref[...]), dsk_ref[...], nw_ref[...], lc_ref[...],
                       tril_ref[...] > 0.5, get_state, set_state, c, seg, state_t=False)
    for ci, out in enumerate(outs):
        for p in range(PAIRS):
            o_ref[ci * c:(ci + 1) * c, p * LANES:(p + 1) * LANES] = out[p].astype(o_ref.dtype)


def _ssd(xbc, dt, zs, conv_w, conv_b, alog_p, dsk, nw, state, hist, n_prompt, batch, seq, dec_seq):
    m = xbc.shape[0]
    inner = zs.shape[1]
    c = CHUNK
    xb_blk = inner // M_DSTATE
    cb_blk = xb_blk + M_GROUPS

    def specs(row_map, nrow):
        def rm(fn):
            return lambda *ix: fn(row_map(*ix), ix[1])
        zero = lambda fn: (lambda *ix: fn(0, ix[1]))
        return [
            pl.BlockSpec((nrow, GROUP_W), rm(lambda r, g: (r, g))),
            pl.BlockSpec((nrow, M_DSTATE), rm(lambda r, g: (r, xb_blk + g))),
            pl.BlockSpec((nrow, M_DSTATE), rm(lambda r, g: (r, cb_blk + g))),
            pl.BlockSpec((nrow, LANES), rm(lambda r, g: (r, g))),
            pl.BlockSpec((nrow, GROUP_W), rm(lambda r, g: (r, g))),
            pl.BlockSpec((M_CONV, GROUP_W), zero(lambda r, g: (0, g))),
            pl.BlockSpec((M_CONV, M_DSTATE), zero(lambda r, g: (0, xb_blk + g))),
            pl.BlockSpec((M_CONV, M_DSTATE), zero(lambda r, g: (0, cb_blk + g))),
            pl.BlockSpec((1, GROUP_W), zero(lambda r, g: (0, g))),
            pl.BlockSpec((1, M_DSTATE), zero(lambda r, g: (0, xb_blk + g))),
            pl.BlockSpec((1, M_DSTATE), zero(lambda r, g: (0, cb_blk + g))),
            pl.BlockSpec((1, LANES), zero(lambda r, g: (0, g))),
            pl.BlockSpec((1, GROUP_W), zero(lambda r, g: (0, g))),
            pl.BlockSpec((1, GROUP_W), zero(lambda r, g: (0, g))),
            pl.BlockSpec((c, 3 * c), zero(lambda r, g: (0, 0))),
            pl.BlockSpec((c, 2 * c), zero(lambda r, g: (0, 0))),
        ]

    common = (xbc, xbc, xbc, dt, zs, conv_w, conv_w, conv_w, conv_b, conv_b, conv_b, alog_p, dsk, nw)
    rows = 1024
    nt = seq // rows
    tril = _ssd_consts(c, c)
    in_specs = specs(lambda b, g, t: b * nt + t, rows)
    o_p, s_p = pl.pallas_call(
        functools.partial(_ssd_prompt_kernel, c=c, rows=rows),
        grid=(batch, M_GROUPS, nt),
        in_specs=in_specs,
        out_specs=[pl.BlockSpec((rows, GROUP_W), lambda b, g, t: (b * nt + t, g)),
                   pl.BlockSpec((1, GROUP_W, M_DSTATE), lambda b, g, t: (b, g, 0))],
        out_shape=[jax.ShapeDtypeStruct((n_prompt, inner), bf16),
                   jax.ShapeDtypeStruct((batch, inner, M_DSTATE), f32)],
        scratch_shapes=[pltpu.VMEM((rows + 8, XBC_W), f32), pltpu.VMEM((rows, XBC_W), f32),
                        pltpu.VMEM((PAIRS, LANES, M_DSTATE), f32)],
        compiler_params=_cparams(("arbitrary", "arbitrary", "arbitrary")),
        name="ssd_prompt",
    )(*common, _tile3(tril), jnp.asarray(np.tile(tril, (1, 2))))
    n_sample = m - n_prompt
    rows = 2 * c
    per = rows // dec_seq
    base = n_prompt // rows
    tril = _ssd_consts(c, dec_seq)
    in_specs = specs(lambda jb, g: base + jb, rows)
    in_specs += [
        pl.BlockSpec((per, M_CONV - 1, GROUP_W), lambda jb, g: (jb, 0, g)),
        pl.BlockSpec((per, M_CONV - 1, M_DSTATE), lambda jb, g: (jb, 0, xb_blk + g)),
        pl.BlockSpec((per, M_CONV - 1, M_DSTATE), lambda jb, g: (jb, 0, cb_blk + g)),
        pl.BlockSpec((per, GROUP_W, M_DSTATE), lambda jb, g: (jb, g, 0)),
    ]
    o_s, s_s = pl.pallas_call(
        functools.partial(_ssd_sample_kernel, c=c, seg=dec_seq),
        grid=(n_sample // rows, M_GROUPS),
        in_specs=in_specs,
        out_specs=[pl.BlockSpec((rows, GROUP_W), lambda jb, g: (jb, g)),
                   pl.BlockSpec((per, GROUP_W, M_DSTATE), lambda jb, g: (jb, g, 0))],
        out_shape=[jax.ShapeDtypeStruct((n_sample, inner), bf16), jax.ShapeDtypeStruct(state.shape, f32)],
        scratch_shapes=[pltpu.VMEM((per, 16, XBC_W), f32)],
        compiler_params=_cparams(("arbitrary", "arbitrary")),
        name="ssd_sample",
    )(*common, _tile3(tril), jnp.asarray(np.tile(tril, (1, 2))), hist, hist, hist, state)
    return (o_p, o_s), s_p, s_s


def kernel(x_prompt, x_sample, state_hgrn, state_ssm, state_conv, norm_mix, w_in, hg_lb_logits, hg_norm, conv_w, conv_b,
           dt_bias, a_log, d_skip, ssm_norm, w_branch_hg, w_branch_ssm, w_out, norm_ffn, w_ffn_gate, w_ffn_up,
           w_ffn_down, norm_final):
    batch, seq, d = x_prompt.shape
    dec_batch, dec_seq, _ = x_sample.shape
    n_prompt, n_sample = batch * seq, dec_batch * dec_seq
    hg_heads = state_hgrn.shape[2]
    kdim = hg_heads * HG_DK
    vdim = d
    inner = d
    m_heads = state_ssm.shape[2]
    conv_dim = conv_w.shape[2]
    hpg = m_heads // M_GROUPS

    xp2, xs2 = x_prompt.reshape(n_prompt, d), x_sample.reshape(n_sample, d)
    h = _rmsnorm([xp2, xs2], norm_mix[0], bf16, n_prompt)

    wt = jnp.swapaxes(w_in, 1, 2)[0]
    o_q, o_f, o_v, o_g, o_z, o_xbc = 0, kdim, 2 * kdim, 2 * kdim + vdim, 2 * kdim + 2 * vdim, 2 * kdim + 2 * vdim + inner
    o_dt = o_xbc + conv_dim
    o_gate = o_dt + m_heads
    scale = HG_DK ** -0.5
    ident = lambda a, c, t: (a[0],)
    silu = lambda a, c, t: (jax.nn.silu(a[0]),)
    wide = dict(tm=1024, tn=2048, w_single=True)
    (q,) = _mm([h], [(wt, o_q, True)], kdim, lambda a, c, t: (a[0] * scale,), [f32], name="proj_q", **wide)
    log_f, k = _mm([h], [(wt, o_f, True)], kdim, _epi_forget, [f32, f32], cols=[hg_lb_logits], name="proj_f",
                   tm=512, tn=2048, w_single=True)
    (v,) = _mm([h], [(wt, o_v, True)], vdim, ident, [bf16], name="proj_v", **wide)
    (sg,) = _mm([h], [(wt, o_g, True)], vdim, silu, [f32], name="proj_g", **wide)
    (zs,) = _mm([h], [(wt, o_z, True)], inner, silu, [f32], name="proj_z", **wide)
    (xbc,) = _mm([h], [(wt, o_xbc, True)], conv_dim, ident, [f32], tm=1024, tn=conv_dim // 2, w_single=True, name="proj_xbc")
    pad_heads = lambda p: jnp.pad(p.reshape(M_GROUPS, hpg, -1), ((0, 0), (0, LANES - hpg), (0, 0))).reshape(M_GROUPS * LANES, -1)
    (dt,) = _mm([h], [(pad_heads(wt[o_dt:o_gate]), 0, True)], M_GROUPS * LANES,
                lambda a, c, t: (_softplus(a[0] + c[0]),), [f32], cols=[pad_heads(dt_bias[0]).T], name="proj_dt")
    (gates,) = _mm([h], [(wt, o_gate, True)], 2 * d, lambda a, c, t: (jax.nn.sigmoid(a[0]),), [bf16], name="proj_gate", **wide)

    o_hg, shp, shs = _hgrn(q, log_f, k, v, sg, hg_norm[0], state_hgrn[0], n_prompt, batch, seq, dec_seq)
    dsk = jnp.repeat(d_skip[0], M_HEADDIM).reshape(1, inner)
    o_m, smp, sms = _ssd(xbc, dt, zs, conv_w[0], conv_b[0].reshape(1, conv_dim), pad_heads(a_log[0]).T, dsk,
                         ssm_norm[0].reshape(1, inner), state_ssm[0].reshape(dec_batch, inner, M_DSTATE),
                         state_conv[0], n_prompt, batch, seq, dec_seq)

    tn = 1024
    (merged,) = _mm([o_hg, o_m], [(w_branch_hg[0], 0, False), (w_branch_ssm[0], 0, False)], d,
                    lambda a, c, t: (t[0].astype(f32) * a[0] + t[1].astype(f32) * a[1],), [bf16],
                    tiles=[(gates, lambda j, i: (i, j)), (gates, lambda j, i: (i, j + d // tn))],
                    tm=512, tn=tn, w_single=True, name="merge")
    tm = 512
    n_p_tiles = n_prompt // tm

    def add_x_norm(a, c, t):
        x1 = jnp.where(pl.program_id(1) < n_p_tiles, t[0], t[1]) + a[0]
        ms = jnp.mean(x1 * x1, axis=-1, keepdims=True)
        return x1, x1 * lax.rsqrt(ms + EPS) * c[0]

    x1, h2 = _mm([merged], [(w_out[0], 0, False)], d, add_x_norm, [f32, bf16], cols=[norm_ffn],
                 tiles=[(xp2, lambda j, i: (jnp.minimum(i, n_p_tiles - 1), j)),
                        (xs2, lambda j, i: (jnp.maximum(i - n_p_tiles, 0), j), pl.Buffered(1))],
                 tm=tm, tn=d, w_single=True, name="out_proj")
    ffn = _ffn(h2, w_ffn_gate[0], w_ffn_up[0], w_ffn_down[0])
    y_p, y_s = _rmsnorm([x1], norm_final, f32, n_prompt, split_out=True, res=ffn)

    hist = M_CONV - 1
    new_conv_p = jnp.stack([xbc[(b + 1) * seq - hist:(b + 1) * seq] for b in range(batch)])[None]
    new_conv_s = xbc[n_prompt:].reshape(dec_batch, dec_seq, conv_dim)[:, dec_seq - hist:][None]
    return (y_p.reshape(batch, seq, d), y_s.reshape(dec_batch, dec_seq, d),
            shp[None], smp.reshape(1, batch, m_heads, M_HEADDIM, M_DSTATE), new_conv_p,
            shs[None], sms.reshape(1, dec_batch, m_heads, M_HEADDIM, M_DSTATE), new_conv_s)
```

```python
import functools

import numpy as np
import jax
import jax.numpy as jnp
from jax import lax
from jax.experimental import pallas as pl
from jax.experimental.pallas import tpu as pltpu

f32 = jnp.float32
bf16 = jnp.bfloat16

EPS = 1e-6
LANES = 128
CHUNK = 64
HG_DK = 128
M_HEADDIM = 64
M_DSTATE = 128
M_GROUPS = 4
M_CONV = 4
VMEM_LIMIT = 56 * 1024 * 1024


def _cparams(sem):
    return pltpu.CompilerParams(dimension_semantics=sem, vmem_limit_bytes=VMEM_LIMIT)


def _dot(a, b):
    return jnp.dot(a, b, preferred_element_type=f32)


def _dot_nt(a, b):
    return lax.dot_general(a, b, (((1,), (1,)), ((), ())), preferred_element_type=f32)


def _dot_tn(a, b):
    return lax.dot_general(a, b, (((0,), (0,)), ((), ())), preferred_element_type=f32)


def _split(x, n):
    pieces = []
    for _ in range(n - 1):
        p = x.astype(bf16)
        pieces.append(p)
        x = x - p.astype(f32)
    return pieces + [x.astype(bf16)]


def _tile3(m, n=3):
    return jnp.asarray(np.tile(m, (1, n)), bf16)


def _dot3(mn, x):
    return _dot(mn, jnp.concatenate(_split(x, mn.shape[1] // x.shape[0]), axis=0))


def _mxu_operand(x, small):
    xb = x.astype(bf16)
    return xb.astype(f32) if small else xb


def _rmsnorm_kernel(*refs, n_in, n_out, split, has_res):
    x_refs, w_ref, o_refs = refs[:n_in], refs[n_in], refs[n_in + 1 + has_res:]

    def run(x_ref, o_ref):
        x = x_ref[...]
        if has_res:
            x = x + refs[n_in + 1][...]
        ms = jnp.mean(x * x, axis=-1, keepdims=True)
        o_ref[...] = (x * lax.rsqrt(ms + EPS) * w_ref[...]).astype(o_ref.dtype)

    if n_in == 1 and n_out == 1:
        run(x_refs[0], o_refs[0])
    else:
        i = pl.program_id(0)
        pl.when(i < split)(lambda: run(x_refs[0], o_refs[0]))
        pl.when(i >= split)(lambda: run(x_refs[-1], o_refs[-1]))


def _rmsnorm(xs, w, out_dtype, n_first, split_out=False, res=None, tm=512):
    d = xs[0].shape[1]
    m = sum(x.shape[0] for x in xs)
    split = n_first // tm
    first = lambda i: (jnp.minimum(i, split - 1), 0)
    second = lambda i: (jnp.maximum(i - split, 0), 0)
    whole = lambda i: (i, 0)
    blk = lambda fn: pl.BlockSpec((tm, d), fn)
    in_specs = [blk(whole)] if len(xs) == 1 else [blk(first), blk(second)]
    if split_out:
        out_specs = [blk(first), blk(second)]
        out_shape = [jax.ShapeDtypeStruct((n_first, d), out_dtype), jax.ShapeDtypeStruct((m - n_first, d), out_dtype)]
    else:
        out_specs = [blk(whole)]
        out_shape = [jax.ShapeDtypeStruct((m, d), out_dtype)]
    extra = [] if res is None else [res]
    outs = pl.pallas_call(
        functools.partial(_rmsnorm_kernel, n_in=len(xs), n_out=len(out_shape), split=split, has_res=len(extra)),
        grid=(m // tm,),
        in_specs=in_specs + [pl.BlockSpec((1, d), lambda i: (0, 0))] + [blk(whole) for _ in extra],
        out_specs=out_specs,
        out_shape=out_shape,
        compiler_params=_cparams(("arbitrary",)),
        name="rmsnorm",
    )(*xs, w.reshape(1, d), *extra)
    return outs if split_out else outs[0]


def _mm_kernel(*refs, a_parts, split, nd, nc, nt, no, epi, w_t):
    a_refs, refs = refs[:sum(a_parts)], refs[sum(a_parts):]
    w_refs = refs[:nd]
    c_refs = refs[nd:nd + nc]
    t_refs = refs[nd + nc:nd + nc + nt]
    o_refs = refs[nd + nc + nt:nd + nc + nt + no]
    wb_refs = refs[nd + nc + nt + no:]

    @pl.when(pl.program_id(1) == 0)
    def _():
        for w_ref, wb_ref in zip(w_refs, wb_refs):
            wb_ref[...] = w_ref[...].astype(bf16)

    a_vals = []
    for parts in a_parts:
        if parts == 2:
            a_vals.append(jnp.where(pl.program_id(1) < split, a_refs[0][...], a_refs[1][...]))
        else:
            a_vals.append(a_refs[0][...])
        a_refs = a_refs[parts:]
    na = len(a_vals)
    accs = [(_dot_nt if t else _dot)(a_vals[min(i, na - 1)], wb_ref[...])
            for i, (wb_ref, t) in enumerate(zip(wb_refs, w_t))]
    outs = epi(accs, [c[...] for c in c_refs], [t[...] for t in t_refs])
    for o_ref, o in zip(o_refs, outs):
        o_ref[...] = o.astype(o_ref.dtype)


def _col_map(j, i, off):
    return (0, j + off)


def _tile_map(j, i, off):
    return (i, j + off)


def _row_elem_map(j, i, off, tn):
    return (pl.multiple_of(off + j * tn, 8), 0)


def _mm(a_list, w_list, n_cols, epi, out_dtypes, cols=(), tiles=(), tm=1024, tn=512, w_single=False, name="mm"):
    na, nd, nc, nt, no = len(a_list), len(w_list), len(cols), len(tiles), len(out_dtypes)
    assert na in (1, nd)
    a_list = [a if isinstance(a, tuple) else (a,) for a in a_list]
    m = sum(p.shape[0] for p in a_list[0])
    split = a_list[0][0].shape[0] // tm
    in_specs, a_flat = [], []
    for parts in a_list:
        k = parts[0].shape[1]
        if len(parts) == 2:
            in_specs.append(pl.BlockSpec((tm, k), lambda j, i: (jnp.minimum(i, split - 1), 0)))
            in_specs.append(pl.BlockSpec((tm, k), lambda j, i: (jnp.maximum(i - split, 0), 0),
                                         pipeline_mode=pl.Buffered(1)))
        else:
            in_specs.append(pl.BlockSpec((tm, k), lambda j, i: (i, 0)))
        a_flat += list(parts)
    scratch = []
    mode = dict(pipeline_mode=pl.Buffered(1)) if w_single else {}
    for w, off, transposed in w_list:
        if transposed:
            k = w.shape[1]
            assert off % 8 == 0
            in_specs.append(pl.BlockSpec((pl.Element(tn), pl.Element(k)), functools.partial(_row_elem_map, off=off, tn=tn), **mode))
            scratch.append(pltpu.VMEM((tn, k), bf16))
        else:
            k = w.shape[0]
            in_specs.append(pl.BlockSpec((k, tn), functools.partial(_col_map, off=off // tn), **mode))
            scratch.append(pltpu.VMEM((k, tn), bf16))
    in_specs += [pl.BlockSpec((c.shape[0], tn), functools.partial(_col_map, off=0)) for c in cols]
    in_specs += [pl.BlockSpec((tm, tn), t[1], **(dict(pipeline_mode=t[2]) if len(t) > 2 else {})) for t in tiles]
    outs = pl.pallas_call(
        functools.partial(_mm_kernel, a_parts=tuple(len(p) for p in a_list), split=split, nd=nd, nc=nc, nt=nt, no=no,
                          epi=epi, w_t=tuple(t for _, _, t in w_list)),
        grid=(n_cols // tn, m // tm),
        in_specs=in_specs,
        out_specs=[pl.BlockSpec((tm, tn), functools.partial(_tile_map, off=0)) for _ in out_dtypes],
        out_shape=[jax.ShapeDtypeStruct((m, n_cols), dt) for dt in out_dtypes],
        scratch_shapes=scratch,
        compiler_params=_cparams(("arbitrary", "arbitrary")),
        name=name,
    )(*a_flat, *[w for w, _, _ in w_list], *cols, *[t[0] for t in tiles])
    return outs


def _ffn_kernel(h_ref, wg_ref, wu_ref, wd_ref, o_ref):
    @pl.when(pl.program_id(1) == 0)
    def _():
        o_ref[...] = jnp.zeros_like(o_ref)

    h = h_ref[...]
    gate = _dot(h, wg_ref[...].astype(bf16))
    up = _dot(h, wu_ref[...].astype(bf16))
    act = (jax.nn.silu(gate) * up).astype(bf16)
    o_ref[...] += _dot(act, wd_ref[...].astype(bf16))


def _ffn(h, w_gate, w_up, w_down, tm=1536, tf=256):
    m, d = h.shape
    hidden = w_gate.shape[1]
    return pl.pallas_call(
        _ffn_kernel,
        grid=(m // tm, hidden // tf),
        in_specs=[pl.BlockSpec((tm, d), lambda i, f: (i, 0), pipeline_mode=pl.Buffered(1)),
                  pl.BlockSpec((d, tf), lambda i, f: (0, f)),
                  pl.BlockSpec((d, tf), lambda i, f: (0, f)),
                  pl.BlockSpec((tf, d), lambda i, f: (f, 0))],
        out_specs=pl.BlockSpec((tm, d), lambda i, f: (i, 0)),
        out_shape=jax.ShapeDtypeStruct((m, d), f32),
        compiler_params=_cparams(("arbitrary", "arbitrary")),
        name="ffn",
    )(h, w_gate, w_up, w_down)


def _softplus(x):
    return jnp.maximum(x, 0.0) + jnp.log1p(jnp.exp(-jnp.abs(x)))


def _epi_forget(accs, cols, tiles):
    logits = cols[0]
    mx = jnp.max(logits, axis=0, keepdims=True)
    e = jnp.exp(logits - mx)
    lb = e[0:1, :] / jnp.sum(e, axis=0, keepdims=True)
    sig = jax.nn.sigmoid(accs[0])
    log_f = jnp.log(lb + (1.0 - lb) * sig)
    k = (1.0 - lb) * (1.0 - sig)
    return log_f, k


HG_SPLIT = 2


def _hgrn_consts(c, seg):
    nlev = int(np.log2(seg))
    mat = np.zeros(((nlev + 2) * c, c), np.float32)
    masks = np.zeros((nlev + 1, c, c), np.float32)
    for t in range(c):
        tl = t % seg
        base = t - tl
        for l in range(nlev):
            h = 1 << l
            pos = tl % (2 * h)
            ref = base + tl - pos + h - 1
            if pos >= h:
                mat[l * c + t, ref + 1:t + 1] = 1
            else:
                mat[l * c + t, t + 1:ref + 1] = 1
            for s in range(base, base + seg):
                sl = s % seg
                if sl // (2 * h) == tl // (2 * h) and pos >= h and sl % (2 * h) < h:
                    masks[l, t, s] = 1
        mat[nlev * c + t, base:t + 1] = 1
        mat[(nlev + 1) * c + t, t + 1:base + seg] = 1
        masks[nlev, t, t] = 1
    return mat, masks, nlev


def _hgrn_chunks(q_ref, k_ref, g_ref, v_ref, mat, masks, nlev, c):
    n_chunks = q_ref.shape[0] // c
    heads = q_ref.shape[1] // LANES
    rows = lambda ci: slice(ci * c, (ci + 1) * c)
    cols = lambda h: slice(h * LANES, (h + 1) * LANES)
    items = [(ci, h) for ci in range(n_chunks) for h in range(heads)]
    e_all = [jnp.exp(_dot3(mat, g_ref[rows(ci), :])) for ci in range(n_chunks)]
    q = {(ci, h): q_ref[rows(ci), cols(h)] for ci, h in items}
    k = {(ci, h): k_ref[rows(ci), cols(h)] for ci, h in items}
    v = {(ci, h): v_ref[rows(ci), cols(h)].astype(bf16) for ci, h in items}
    s = {it: jnp.where(masks[nlev], _dot_nt(q[it].astype(bf16), k[it].astype(bf16)), 0.0) for it in items}
    for l in range(nlev):
        for ci, h in items:
            e = e_all[ci][l * c:(l + 1) * c, cols(h)]
            s[ci, h] = jnp.where(masks[l], _dot_nt((q[ci, h] * e).astype(bf16), (k[ci, h] * e).astype(bf16)), s[ci, h])
    res = {}
    for ci, h in items:
        o = _dot(s[ci, h].astype(bf16), v[ci, h])
        eb = e_all[ci][nlev * c:(nlev + 1) * c, cols(h)]
        er = e_all[ci][(nlev + 1) * c:, cols(h)]
        res[ci, h] = (o, q[ci, h] * eb, k[ci, h] * er, eb, v[ci, h])
    return res


def _hgrn_out(o, sg, nw):
    ms = jnp.mean(o * o, axis=-1, keepdims=True)
    return o * lax.rsqrt(ms + EPS) * nw * sg


def _hgrn_prompt_kernel(q_ref, g_ref, k_ref, v_ref, sg_ref, nw_ref, mat_ref, mask_ref, o_ref, s_ref, st_ref,
                        *, c, nlev, n_chunks):
    t_blk = pl.program_id(2)

    @pl.when(t_blk == 0)
    def _():
        st_ref[...] = jnp.zeros_like(st_ref)

    masks = [mask_ref[l] > 0.5 for l in range(nlev + 1)]
    mat = mat_ref[...]
    nw = nw_ref[...]

    res = _hgrn_chunks(q_ref, k_ref, g_ref, v_ref, mat, masks, nlev, c)
    upd = {it: _dot_tn(r[4], r[2].astype(bf16)) for it, r in res.items()}
    for h in range(st_ref.shape[0]):
        cs = slice(h * LANES, (h + 1) * LANES)
        st = st_ref[h]
        for ci in range(n_chunks):
            rows = slice(ci * c, (ci + 1) * c)
            o, qe, _, eb, _ = res[ci, h]
            o = o + _dot_nt(qe.astype(bf16), st.astype(bf16))
            st = st * eb[c - 1:c, :] + upd[ci, h]
            o_ref[rows, cs] = _hgrn_out(o, sg_ref[rows, cs], nw[:, cs]).astype(o_ref.dtype)
        st_ref[h] = st

    @pl.when(t_blk == pl.num_programs(2) - 1)
    def _():
        for h in range(st_ref.shape[0]):
            s_ref[0, h] = st_ref[h].T


def _hgrn_sample_kernel(q_ref, g_ref, k_ref, v_ref, sg_ref, nw_ref, mat_ref, mask_ref, s_in_ref,
                        o_ref, s_out_ref, *, c, nlev, seg):
    masks = [mask_ref[l] > 0.5 for l in range(nlev + 1)]
    mat = mat_ref[...]
    nw = nw_ref[...]
    small = seg < 16
    per = c // seg
    res = _hgrn_chunks(q_ref, k_ref, g_ref, v_ref, mat, masks, nlev, c)
    for (ci, h), (o, qe, kd, eb, v) in res.items():
        rows = slice(ci * c, (ci + 1) * c)
        cs = slice(h * LANES, (h + 1) * LANES)
        v = v.astype(f32)
        eb_t = jnp.concatenate([eb, eb], axis=0).T
        parts = []
        for j in range(per):
            sl = slice(j * seg, (j + 1) * seg)
            s0 = s_in_ref[ci * per + j, h]
            parts.append(_dot(_mxu_operand(qe[sl], small), _mxu_operand(s0, small)))
            last = j * seg + seg - 1
            upd = _dot_tn(_mxu_operand(kd[sl], small), _mxu_operand(v[sl], small))
            s_out_ref[ci * per + j, h] = s0 * eb_t[:, last:last + 1] + upd
        o = o + jnp.concatenate(parts, axis=0)
        o_ref[rows, cs] = _hgrn_out(o, sg_ref[rows, cs], nw[:, cs]).astype(o_ref.dtype)


def _hgrn(q, g, k, v, sg, nw, state, n_prompt, batch, seq, dec_seq):
    m, width = q.shape
    heads = width // LANES
    nw = nw.reshape(1, width)
    c = CHUNK
    hp = 4
    hw = hp * LANES
    tc = 1024
    nt = seq // tc
    mat, masks, nlev = _hgrn_consts(c, c)
    row_spec = pl.BlockSpec((tc, hw), lambda b, h, t: (b * nt + t, h))
    const2 = lambda shape: pl.BlockSpec(shape, lambda b, h, t: (0,) * len(shape))
    o_p, s_p = pl.pallas_call(
        functools.partial(_hgrn_prompt_kernel, c=c, nlev=nlev, n_chunks=tc // c),
        grid=(batch, heads // hp, nt),
        in_specs=[row_spec] * 5 + [pl.BlockSpec((1, hw), lambda b, h, t: (0, h)), const2((mat.shape[0], HG_SPLIT * c)), const2(masks.shape)],
        out_specs=[row_spec, pl.BlockSpec((1, hp, HG_DK, LANES), lambda b, h, t: (b, h, 0, 0))],
        out_shape=[jax.ShapeDtypeStruct((n_prompt, width), bf16),
                   jax.ShapeDtypeStruct((batch, heads, HG_DK, LANES), f32)],
        scratch_shapes=[pltpu.VMEM((hp, LANES, HG_DK), f32)],
        compiler_params=_cparams(("arbitrary", "arbitrary", "arbitrary")),
        name="hgrn_prompt",
    )(q, g, k, v, sg, nw, _tile3(mat, HG_SPLIT), jnp.asarray(masks))
    n_sample = m - n_prompt
    rows = 2 * c
    per = rows // dec_seq
    mat, masks, nlev = _hgrn_consts(c, dec_seq)
    base = n_prompt // rows
    row_spec = pl.BlockSpec((rows, hw), lambda jb, h: (base + jb, h))
    st_spec = pl.BlockSpec((per, hp, HG_DK, LANES), lambda jb, h: (jb, h, 0, 0))
    const2 = lambda shape: pl.BlockSpec(shape, lambda jb, h: (0,) * len(shape))
    o_s, s_s = pl.pallas_call(
        functools.partial(_hgrn_sample_kernel, c=c, nlev=nlev, seg=dec_seq),
        grid=(n_sample // rows, heads // hp),
        in_specs=[row_spec] * 5 + [pl.BlockSpec((1, hw), lambda jb, h: (0, h)), const2((mat.shape[0], HG_SPLIT * c)),
                                   const2(masks.shape), st_spec],
        out_specs=[pl.BlockSpec((rows, hw), lambda jb, h: (jb, h)), st_spec],
        out_shape=[jax.ShapeDtypeStruct((n_sample, width), bf16), jax.ShapeDtypeStruct(state.shape, f32)],
        compiler_params=_cparams(("arbitrary", "arbitrary")),
        name="hgrn_sample",
    )(q, g, k, v, sg, nw, _tile3(mat, HG_SPLIT), jnp.asarray(masks), state)
    return (o_p, o_s), s_p, s_s


GROUP_W = 512
PAIRS = GROUP_W // LANES
XBC_W = GROUP_W + 2 * M_DSTATE


def _ssd_consts(c, seg):
    t = np.arange(c)
    same = (t[:, None] // seg) == (t[None, :] // seg)
    tril = (same & (t[None, :] <= t[:, None])).astype(np.float32)
    return tril


def _ssd_chunks(chunks, a_row, dsk, nw, lc, tril, get_state, set_state, c, seg, state_t, after_first_stage=None):
    nseg = c // seg
    small = seg < 16
    assert 2 * c == LANES and 2 * M_HEADDIM == LANES and not (state_t and nseg > 1)
    lo_half = lax.broadcasted_iota(jnp.int32, (c, LANES), 1) < M_HEADDIM
    lo_row = lo_half[0:1, :]
    n = len(chunks)
    items = [(ci, p) for ci in range(n) for p in range(PAIRS)]
    pc = lambda p: slice(p * LANES, (p + 1) * LANES)
    acum = [_dot3(lc, dt * a_row) for _, _, _, dt, _ in chunks]
    bmb = [bm.astype(bf16) for _, bm, _, _, _ in chunks]
    cmb = [cm.astype(bf16) for _, _, cm, _, _ in chunks]
    cb2 = [_dot_nt(cmb[ci], jnp.concatenate([bmb[ci], bmb[ci]], axis=0)) for ci in range(n)]
    acum_t = [jnp.concatenate([a, a], axis=0).T for a in acum]
    dt_t = [jnp.concatenate([ch[3], ch[3]], axis=0).T for ch in chunks]
    if state_t:
        bm_t = [ch[1].T.astype(bf16) for ch in chunks]
    else:
        ea_t = [jnp.exp(a[0:8, 0:c]) for a in acum_t]
    if after_first_stage is not None:
        after_first_stage()
    y, xw, ea = {}, {}, {}
    for ci, p in items:
        h0, h1 = 2 * p, 2 * p + 1
        xp, dt = chunks[ci][0][:, pc(p)], chunks[ci][3]
        acp = jnp.where(lo_half, acum[ci][:, h0:h0 + 1], acum[ci][:, h1:h1 + 1])
        dtp = jnp.where(lo_half, dt[:, h0:h0 + 1], dt[:, h1:h1 + 1])
        a_src = jnp.where(lo_row, acum_t[ci][h0:h0 + 1, :], acum_t[ci][h1:h1 + 1, :])
        dt_src = jnp.where(lo_row, dt_t[ci][h0:h0 + 1, :], dt_t[ci][h1:h1 + 1, :])
        lm = jnp.exp(jnp.where(tril, acp - a_src, -1e30))
        sc = (cb2[ci] * lm * dt_src).astype(bf16)
        x_blk = jnp.concatenate([jnp.where(lo_half, xp, 0.0), jnp.where(lo_half, 0.0, xp)], axis=0).astype(bf16)
        y[ci, p] = _dot(sc, x_blk)
        if nseg == 1:
            alast = acp[c - 1:c, :]
        else:
            alast = jnp.concatenate(
                [jnp.broadcast_to(acp[j * seg + seg - 1:j * seg + seg, :], (seg, LANES)) for j in range(nseg)], axis=0)
        xw[ci, p] = xp * (jnp.exp(alast - acp) * dtp)
        ea[ci, p] = jnp.exp(acp)
    cs = {}
    if state_t:
        upd = {(ci, p): _dot(bm_t[ci], xw[ci, p].astype(bf16)) for ci, p in items}
        for p in range(PAIRS):
            st = get_state(0, 0, p)
            for ci in range(n):
                cs[ci, p] = _dot(cmb[ci], st.astype(bf16))
                st = st * ea[ci, p][c - 1:c, :] + upd[ci, p]
            set_state(0, 0, p, st)
    else:
        for ci, p in items:
            h0, h1 = 2 * p, 2 * p + 1
            bm, cm = chunks[ci][1], chunks[ci][2]
            parts = []
            for j in range(nseg):
                sl = slice(j * seg, (j + 1) * seg)
                s0 = get_state(ci, j, p)
                parts.append(_dot_nt(_mxu_operand(cm[sl], small), _mxu_operand(s0, small)))
                upd = _dot_tn(_mxu_operand(xw[ci, p][sl], small), _mxu_operand(bm[sl], small))
                last = j * seg + seg - 1
                decay = jnp.concatenate(
                    [jnp.broadcast_to(ea_t[ci][h0:h0 + 1, last:last + 1], (M_HEADDIM, LANES)),
                     jnp.broadcast_to(ea_t[ci][h1:h1 + 1, last:last + 1], (M_HEADDIM, LANES))], axis=0)
                set_state(ci, j, p, s0 * decay + upd)
            cs[ci, p] = parts[0] if nseg == 1 else jnp.concatenate(parts, axis=0)
    outs = []
    for ci in range(n):
        xs, zs = chunks[ci][0], chunks[ci][4]
        ssq = jnp.zeros((c, 1), f32)
        ys = []
        for p in range(PAIRS):
            yp = (y[ci, p] + cs[ci, p] * ea[ci, p] + dsk[:, pc(p)] * xs[:, pc(p)]) * zs[:, pc(p)]
            ssq = ssq + jnp.sum(yp * yp, axis=-1, keepdims=True)
            ys.append(yp)
        scale = lax.rsqrt(ssq * (1.0 / GROUP_W) + EPS)
        outs.append([ys[p] * scale * nw[:, pc(p)] for p in range(PAIRS)])
    return outs


def _conv_taps(xpad_ref, lead, rows, w, b):
    acc = None
    for j in range(M_CONV):
        term = xpad_ref[lead + (slice(5 + j + rows[0], 5 + j + rows[1]), slice(None))] * w[j:j + 1, :]
        acc = term if acc is None else acc + term
    return jax.nn.silu(b + acc)


def _ssd_prompt_kernel(xr_ref, br_ref, cr_ref, dt_ref, zs_ref, wx_ref, wb_ref, wc_ref, bx_ref, bb_ref, bc_ref,
                       alog_ref, dsk_ref, nw_ref, lc_ref, tril_ref, h_ref, wg_ref, o_ref, s_ref, gate_ref,
                       xpad_ref, xc_ref, st_ref, wgb_ref, *, c, rows):
    t_blk = pl.program_id(2)

    @pl.when(t_blk == 0)
    def _():
        st_ref[...] = jnp.zeros_like(st_ref)
        xpad_ref[0:8, :] = jnp.zeros((8, XBC_W), f32)
        wgb_ref[...] = wg_ref[...].astype(bf16)

    @pl.when(t_blk > 0)
    def _():
        xpad_ref[0:8, :] = xpad_ref[rows:rows + 8, :]

    half = rows // 2

    def gate_half(i):
        rs = slice(i * half, (i + 1) * half)
        gate_ref[rs, :] = _dot_nt(h_ref[rs, :], wgb_ref[...])

    gate_half(0)
    xpad_ref[8:8 + rows, 0:GROUP_W] = xr_ref[...]
    xpad_ref[8:8 + rows, GROUP_W:GROUP_W + M_DSTATE] = br_ref[...]
    xpad_ref[8:8 + rows, GROUP_W + M_DSTATE:XBC_W] = cr_ref[...]
    w = jnp.concatenate([wx_ref[...], wb_ref[...], wc_ref[...]], axis=1)
    b = jnp.concatenate([bx_ref[...], bb_ref[...], bc_ref[...]], axis=1)
    for i in range(rows // c):
        xc_ref[i * c:(i + 1) * c, :] = _conv_taps(xpad_ref, (), (i * c, (i + 1) * c), w, b)

    a_row = -jnp.exp(alog_ref[...])
    dsk = dsk_ref[...]
    nw = nw_ref[...]
    lc = lc_ref[...]
    tril = tril_ref[...] > 0.5

    def get_state(ci, j, p):
        return st_ref[p]

    def set_state(ci, j, p, val):
        st_ref[p] = val

    rs = lambda ci: slice(ci * c, (ci + 1) * c)
    chunks = [(xc_ref[rs(ci), 0:GROUP_W], xc_ref[rs(ci), GROUP_W:GROUP_W + M_DSTATE], xc_ref[rs(ci), GROUP_W + M_DSTATE:XBC_W],
               dt_ref[rs(ci), :], zs_ref[rs(ci), :]) for ci in range(rows // c)]
    outs = _ssd_chunks(chunks, a_row, dsk, nw, lc, tril, get_state, set_state, c, c, state_t=True,
                       after_first_stage=lambda: gate_half(1))
    for ci, out in enumerate(outs):
        for p in range(PAIRS):
            o_ref[rs(ci), p * LANES:(p + 1) * LANES] = out[p].astype(o_ref.dtype)

    @pl.when(t_blk == pl.num_programs(2) - 1)
    def _():
        for p in range(PAIRS):
            s_ref[0, p * LANES:(p + 1) * LANES, :] = st_ref[p].T


def _ssd_sample_kernel(xr_ref, br_ref, cr_ref, dt_ref, zs_ref, wx_ref, wb_ref, wc_ref, bx_ref, bb_ref, bc_ref,
                       alog_ref, dsk_ref, nw_ref, lc_ref, tril_ref, hx_ref, hb_ref, hc_ref, s_in_ref,
                       o_ref, s_out_ref, xpad_ref, *, c, seg):
    nseg = c // seg
    n_chunks = xr_ref.shape[0] // c
    w = jnp.concatenate([wx_ref[...], wb_ref[...], wc_ref[...]], axis=1)
    b = jnp.concatenate([bx_ref[...], bb_ref[...], bc_ref[...]], axis=1)
    conv = []
    for j in range(n_chunks * nseg):
        sl = slice(j * seg, (j + 1) * seg)
        xpad_ref[j, 5:8, 0:GROUP_W] = hx_ref[j]
        xpad_ref[j, 5:8, GROUP_W:GROUP_W + M_DSTATE] = hb_ref[j]
        xpad_ref[j, 5:8, GROUP_W + M_DSTATE:XBC_W] = hc_ref[j]
        xpad_ref[j, 8:8 + seg, 0:GROUP_W] = xr_ref[sl, :]
        xpad_ref[j, 8:8 + seg, GROUP_W:GROUP_W + M_DSTATE] = br_ref[sl, :]
        xpad_ref[j, 8:8 + seg, GROUP_W + M_DSTATE:XBC_W] = cr_ref[sl, :]
        conv.append(_conv_taps(xpad_ref, (j,), (0, seg), w, b))

    def get_state(ci, j, p):
        return s_in_ref[ci * nseg + j, p * LANES:(p + 1) * LANES, :]

    def set_state(ci, j, p, val):
        s_out_ref[ci * nseg + j, p * LANES:(p + 1) * LANES, :] = val

    chunks = []
    for ci in range(n_chunks):
        xc = jnp.concatenate(conv[ci * nseg:(ci + 1) * nseg], axis=0)
        rs = slice(ci * c, (ci + 1) * c)
        chunks.append((xc[:, 0:GROUP_W], xc[:, GROUP_W:GROUP_W + M_DSTATE], xc[:, GROUP_W + M_DSTATE:XBC_W],
                       dt_ref[rs, :], zs_ref[rs, :]))
    outs = _ssd_chunks(chunks, -jnp.exp(alog_ref[...]), dsk_ref[...], nw_ref[...], lc_ref[...],
                       tril_ref[...] > 0.5, get_state, set_state, c, seg, state_t=False)
    for ci, out in enumerate(outs):
        for p in range(PAIRS):
            o_ref[ci * c:(ci + 1) * c, p * LANES:(p + 1) * LANES] = out[p].astype(o_ref.dtype)


def _ssd(xbc, dt, zs, conv_w, conv_b, alog_p, dsk, nw, state, hist, n_prompt, batch, seq, dec_seq, h, wt, o_gate, gate_cols):
    m = xbc.shape[0]
    inner = zs.shape[1]
    c = CHUNK
    xb_blk = inner // M_DSTATE
    cb_blk = xb_blk + M_GROUPS

    def specs(row_map, nrow):
        def rm(fn):
            return lambda *ix: fn(row_map(*ix), ix[1])
        zero = lambda fn: (lambda *ix: fn(0, ix[1]))
        return [
            pl.BlockSpec((nrow, GROUP_W), rm(lambda r, g: (r, g))),
            pl.BlockSpec((nrow, M_DSTATE), rm(lambda r, g: (r, xb_blk + g))),
            pl.BlockSpec((nrow, M_DSTATE), rm(lambda r, g: (r, cb_blk + g))),
            pl.BlockSpec((nrow, LANES), rm(lambda r, g: (r, g))),
            pl.BlockSpec((nrow, GROUP_W), rm(lambda r, g: (r, g))),
            pl.BlockSpec((M_CONV, GROUP_W), zero(lambda r, g: (0, g))),
            pl.BlockSpec((M_CONV, M_DSTATE), zero(lambda r, g: (0, xb_blk + g))),
            pl.BlockSpec((M_CONV, M_DSTATE), zero(lambda r, g: (0, cb_blk + g))),
            pl.BlockSpec((1, GROUP_W), zero(lambda r, g: (0, g))),
            pl.BlockSpec((1, M_DSTATE), zero(lambda r, g: (0, xb_blk + g))),
            pl.BlockSpec((1, M_DSTATE), zero(lambda r, g: (0, cb_blk + g))),
            pl.BlockSpec((1, LANES), zero(lambda r, g: (0, g))),
            pl.BlockSpec((1, GROUP_W), zero(lambda r, g: (0, g))),
            pl.BlockSpec((1, GROUP_W), zero(lambda r, g: (0, g))),
            pl.BlockSpec((c, 3 * c), zero(lambda r, g: (0, 0))),
            pl.BlockSpec((c, 2 * c), zero(lambda r, g: (0, 0))),
        ]

    common = (xbc, xbc, xbc, dt, zs, conv_w, conv_w, conv_w, conv_b, conv_b, conv_b, alog_p, dsk, nw)
    rows = 1024
    nt = seq // rows
    tril = _ssd_consts(c, c)
    in_specs = specs(lambda b, g, t: b * nt + t, rows)
    d_model = h.shape[1]
    gw = gate_cols // M_GROUPS
    in_specs += [
        pl.BlockSpec((rows, d_model), lambda b, g, t: (b * nt + t, 0)),
        pl.BlockSpec((pl.Element(gw), pl.Element(d_model)), lambda b, g, t: (pl.multiple_of(o_gate + g * gw, 8), 0),
                     pipeline_mode=pl.Buffered(1)),
    ]
    o_p, s_p, gates_p = pl.pallas_call(
        functools.partial(_ssd_prompt_kernel, c=c, rows=rows),
        grid=(batch, M_GROUPS, nt),
        in_specs=in_specs,
        out_specs=[pl.BlockSpec((rows, GROUP_W), lambda b, g, t: (b * nt + t, g)),
                   pl.BlockSpec((1, GROUP_W, M_DSTATE), lambda b, g, t: (b, g, 0)),
                   pl.BlockSpec((rows, gw), lambda b, g, t: (b * nt + t, g))],
        out_shape=[jax.ShapeDtypeStruct((n_prompt, inner), bf16),
                   jax.ShapeDtypeStruct((batch, inner, M_DSTATE), f32),
                   jax.ShapeDtypeStruct((n_prompt, gate_cols), f32)],
        scratch_shapes=[pltpu.VMEM((rows + 8, XBC_W), f32), pltpu.VMEM((rows, XBC_W), f32),
                        pltpu.VMEM((PAIRS, LANES, M_DSTATE), f32), pltpu.VMEM((gw, d_model), bf16)],
        compiler_params=_cparams(("arbitrary", "arbitrary", "arbitrary")),
        name="ssd_prompt",
    )(*common, _tile3(tril), jnp.asarray(np.tile(tril, (1, 2))), h, wt)
    n_sample = m - n_prompt
    rows = 2 * c
    per = rows // dec_seq
    base = n_prompt // rows
    tril = _ssd_consts(c, dec_seq)
    in_specs = specs(lambda jb, g: base + jb, rows)
    in_specs += [
        pl.BlockSpec((per, M_CONV - 1, GROUP_W), lambda jb, g: (jb, 0, g)),
        pl.BlockSpec((per, M_CONV - 1, M_DSTATE), lambda jb, g: (jb, 0, xb_blk + g)),
        pl.BlockSpec((per, M_CONV - 1, M_DSTATE), lambda jb, g: (jb, 0, cb_blk + g)),
        pl.BlockSpec((per, GROUP_W, M_DSTATE), lambda jb, g: (jb, g, 0)),
    ]
    o_s, s_s = pl.pallas_call(
        functools.partial(_ssd_sample_kernel, c=c, seg=dec_seq),
        grid=(n_sample // rows, M_GROUPS),
        in_specs=in_specs,
        out_specs=[pl.BlockSpec((rows, GROUP_W), lambda jb, g: (jb, g)),
                   pl.BlockSpec((per, GROUP_W, M_DSTATE), lambda jb, g: (jb, g, 0))],
        out_shape=[jax.ShapeDtypeStruct((n_sample, inner), bf16), jax.ShapeDtypeStruct(state.shape, f32)],
        scratch_shapes=[pltpu.VMEM((per, 16, XBC_W), f32)],
        compiler_params=_cparams(("arbitrary", "arbitrary")),
        name="ssd_sample",
    )(*common, _tile3(tril), jnp.asarray(np.tile(tril, (1, 2))), hist, hist, hist, state)
    return (o_p, o_s), s_p, s_s, gates_p


def kernel(x_prompt, x_sample, state_hgrn, state_ssm, state_conv, norm_mix, w_in, hg_lb_logits, hg_norm, conv_w, conv_b,
           dt_bias, a_log, d_skip, ssm_norm, w_branch_hg, w_branch_ssm, w_out, norm_ffn, w_ffn_gate, w_ffn_up,
           w_ffn_down, norm_final):
    batch, seq, d = x_prompt.shape
    dec_batch, dec_seq, _ = x_sample.shape
    n_prompt, n_sample = batch * seq, dec_batch * dec_seq
    hg_heads = state_hgrn.shape[2]
    kdim = hg_heads * HG_DK
    vdim = d
    inner = d
    m_heads = state_ssm.shape[2]
    conv_dim = conv_w.shape[2]
    hpg = m_heads // M_GROUPS

    xp2, xs2 = x_prompt.reshape(n_prompt, d), x_sample.reshape(n_sample, d)
    h = _rmsnorm([xp2, xs2], norm_mix[0], bf16, n_prompt)

    wt = jnp.swapaxes(w_in, 1, 2)[0]
    o_q, o_f, o_v, o_g, o_z, o_xbc = 0, kdim, 2 * kdim, 2 * kdim + vdim, 2 * kdim + 2 * vdim, 2 * kdim + 2 * vdim + inner
    o_dt = o_xbc + conv_dim
    o_gate = o_dt + m_heads
    scale = HG_DK ** -0.5
    ident = lambda a, c, t: (a[0],)
    silu = lambda a, c, t: (jax.nn.silu(a[0]),)
    wide = dict(tm=1024, tn=2048, w_single=True)
    (q,) = _mm([h], [(wt, o_q, True)], kdim, lambda a, c, t: (a[0] * scale,), [f32], name="proj_q", **wide)
    log_f, k = _mm([h], [(wt, o_f, True)], kdim, _epi_forget, [f32, f32], cols=[hg_lb_logits], name="proj_f",
                   tm=512, tn=2048, w_single=True)
    (v,) = _mm([h], [(wt, o_v, True)], vdim, ident, [bf16], name="proj_v", **wide)
    (sg,) = _mm([h], [(wt, o_g, True)], vdim, silu, [f32], name="proj_g", **wide)
    (zs,) = _mm([h], [(wt, o_z, True)], inner, silu, [f32], name="proj_z", **wide)
    (xbc,) = _mm([h], [(wt, o_xbc, True)], conv_dim, ident, [f32], tm=1024, tn=conv_dim // 2, w_single=True, name="proj_xbc")
    pad_heads = lambda p: jnp.pad(p.reshape(M_GROUPS, hpg, -1), ((0, 0), (0, LANES - hpg), (0, 0))).reshape(M_GROUPS * LANES, -1)
    (dt,) = _mm([h], [(pad_heads(wt[o_dt:o_gate]), 0, True)], M_GROUPS * LANES,
                lambda a, c, t: (_softplus(a[0] + c[0]),), [f32], cols=[pad_heads(dt_bias[0]).T], name="proj_dt")
    (gates_s,) = _mm([h[n_prompt:]], [(wt, o_gate, True)], 2 * d, ident, [f32], name="proj_gate_sample", **wide)

    o_hg, shp, shs = _hgrn(q, log_f, k, v, sg, hg_norm[0], state_hgrn[0], n_prompt, batch, seq, dec_seq)
    dsk = jnp.repeat(d_skip[0], M_HEADDIM).reshape(1, inner)
    o_m, smp, sms, gates_p = _ssd(xbc, dt, zs, conv_w[0], conv_b[0].reshape(1, conv_dim), pad_heads(a_log[0]).T, dsk,
                                  ssm_norm[0].reshape(1, inner), state_ssm[0].reshape(dec_batch, inner, M_DSTATE),
                                  state_conv[0], n_prompt, batch, seq, dec_seq, h, wt, o_gate, 2 * d)

    tm, tn = 512, 1024
    n_p_tiles = n_prompt // tm
    first = lambda off: (lambda j, i: (jnp.minimum(i, n_p_tiles - 1), j + off))
    second = lambda off: (lambda j, i: (jnp.maximum(i - n_p_tiles, 0), j + off))

    def gated_sum(a, c, t):
        is_prompt = pl.program_id(1) < n_p_tiles
        g_hg = jax.nn.sigmoid(jnp.where(is_prompt, t[0], t[1]))
        g_m = jax.nn.sigmoid(jnp.where(is_prompt, t[2], t[3]))
        return (g_hg * a[0] + g_m * a[1],)

    (merged,) = _mm([o_hg, o_m], [(w_branch_hg[0], 0, False), (w_branch_ssm[0], 0, False)], d, gated_sum, [bf16],
                    tiles=[(gates_p, first(0)), (gates_s, second(0), pl.Buffered(1)),
                           (gates_p, first(d // tn)), (gates_s, second(d // tn), pl.Buffered(1))],
                    tm=tm, tn=tn, w_single=True, name="merge")
    tm = 512
    n_p_tiles = n_prompt // tm

    def add_x_norm(a, c, t):
        x1 = jnp.where(pl.program_id(1) < n_p_tiles, t[0], t[1]) + a[0]
        ms = jnp.mean(x1 * x1, axis=-1, keepdims=True)
        return x1, x1 * lax.rsqrt(ms + EPS) * c[0]

    x1, h2 = _mm([merged], [(w_out[0], 0, False)], d, add_x_norm, [f32, bf16], cols=[norm_ffn],
                 tiles=[(xp2, lambda j, i: (jnp.minimum(i, n_p_tiles - 1), j)),
                        (xs2, lambda j, i: (jnp.maximum(i - n_p_tiles, 0), j), pl.Buffered(1))],
                 tm=tm, tn=d, w_single=True, name="out_proj")
    ffn = _ffn(h2, w_ffn_gate[0], w_ffn_up[0], w_ffn_down[0])
    y_p, y_s = _rmsnorm([x1], norm_final, f32, n_prompt, split_out=True, res=ffn)

    hist = M_CONV - 1
    new_conv_p = jnp.stack([xbc[(b + 1) * seq - hist:(b + 1) * seq] for b in range(batch)])[None]
    new_conv_s = xbc[n_prompt:].reshape(dec_batch, dec_seq, conv_dim)[:, dec_seq - hist:][None]
    return (y_p.reshape(batch, seq, d), y_s.reshape(dec_batch, dec_seq, d),
            shp[None], smp.reshape(1, batch, m_heads, M_HEADDIM, M_DSTATE), new_conv_p,
            shs[None], sms.reshape(1, dec_batch, m_heads, M_HEADDIM, M_DSTATE), new_conv_s)
```

```python
import functools

import numpy as np
import jax
import jax.numpy as jnp
from jax import lax
from jax.experimental import pallas as pl
from jax.experimental.pallas import tpu as pltpu

f32 = jnp.float32
bf16 = jnp.bfloat16

EPS = 1e-6
LANES = 128
CHUNK = 64
HG_DK = 128
M_HEADDIM = 64
M_DSTATE = 128
M_GROUPS = 4
M_CONV = 4
VMEM_LIMIT = 56 * 1024 * 1024


def _cparams(sem):
    return pltpu.CompilerParams(dimension_semantics=sem, vmem_limit_bytes=VMEM_LIMIT)


def _dot(a, b):
    return jnp.dot(a, b, preferred_element_type=f32)


def _dot_nt(a, b):
    return lax.dot_general(a, b, (((1,), (1,)), ((), ())), preferred_element_type=f32)


def _dot_tn(a, b):
    return lax.dot_general(a, b, (((0,), (0,)), ((), ())), preferred_element_type=f32)


def _split(x, n):
    pieces = []
    for _ in range(n - 1):
        p = x.astype(bf16)
        pieces.append(p)
        x = x - p.astype(f32)
    return pieces + [x.astype(bf16)]


def _tile3(m, n=3):
    return jnp.asarray(np.tile(m, (1, n)), bf16)


def _dot3(mn, x):
    return _dot(mn, jnp.concatenate(_split(x, mn.shape[1] // x.shape[0]), axis=0))


def _mxu_operand(x, small):
    xb = x.astype(bf16)
    return xb.astype(f32) if small else xb


def _rmsnorm_kernel(*refs, n_in, n_out, split, has_res):
    x_refs, w_ref, o_refs = refs[:n_in], refs[n_in], refs[n_in + 1 + has_res:]

    def run(x_ref, o_ref):
        x = x_ref[...]
        if has_res:
            x = x + refs[n_in + 1][...]
        ms = jnp.mean(x * x, axis=-1, keepdims=True)
        o_ref[...] = (x * lax.rsqrt(ms + EPS) * w_ref[...]).astype(o_ref.dtype)

    if n_in == 1 and n_out == 1:
        run(x_refs[0], o_refs[0])
    else:
        i = pl.program_id(0)
        pl.when(i < split)(lambda: run(x_refs[0], o_refs[0]))
        pl.when(i >= split)(lambda: run(x_refs[-1], o_refs[-1]))


def _rmsnorm(xs, w, out_dtype, n_first, split_out=False, res=None, tm=512):
    d = xs[0].shape[1]
    m = sum(x.shape[0] for x in xs)
    split = n_first // tm
    first = lambda i: (jnp.minimum(i, split - 1), 0)
    second = lambda i: (jnp.maximum(i - split, 0), 0)
    whole = lambda i: (i, 0)
    blk = lambda fn: pl.BlockSpec((tm, d), fn)
    in_specs = [blk(whole)] if len(xs) == 1 else [blk(first), blk(second)]
    if split_out:
        out_specs = [blk(first), blk(second)]
        out_shape = [jax.ShapeDtypeStruct((n_first, d), out_dtype), jax.ShapeDtypeStruct((m - n_first, d), out_dtype)]
    else:
        out_specs = [blk(whole)]
        out_shape = [jax.ShapeDtypeStruct((m, d), out_dtype)]
    extra = [] if res is None else [res]
    outs = pl.pallas_call(
        functools.partial(_rmsnorm_kernel, n_in=len(xs), n_out=len(out_shape), split=split, has_res=len(extra)),
        grid=(m // tm,),
        in_specs=in_specs + [pl.BlockSpec((1, d), lambda i: (0, 0))] + [blk(whole) for _ in extra],
        out_specs=out_specs,
        out_shape=out_shape,
        compiler_params=_cparams(("arbitrary",)),
        name="rmsnorm",
    )(*xs, w.reshape(1, d), *extra)
    return outs if split_out else outs[0]


def _mm_kernel(*refs, a_parts, split, nd, nc, nt, no, epi, w_t):
    a_refs, refs = refs[:sum(a_parts)], refs[sum(a_parts):]
    w_refs = refs[:nd]
    c_refs = refs[nd:nd + nc]
    t_refs = refs[nd + nc:nd + nc + nt]
    o_refs = refs[nd + nc + nt:nd + nc + nt + no]
    wb_refs = refs[nd + nc + nt + no:]

    @pl.when(pl.program_id(1) == 0)
    def _():
        for w_ref, wb_ref in zip(w_refs, wb_refs):
            wb_ref[...] = w_ref[...].astype(bf16)

    a_vals = []
    for parts in a_parts:
        if parts == 2:
            a_vals.append(jnp.where(pl.program_id(1) < split, a_refs[0][...], a_refs[1][...]))
        else:
            a_vals.append(a_refs[0][...])
        a_refs = a_refs[parts:]
    na = len(a_vals)
    accs = [(_dot_nt if t else _dot)(a_vals[min(i, na - 1)], wb_ref[...])
            for i, (wb_ref, t) in enumerate(zip(wb_refs, w_t))]
    outs = epi(accs, [c[...] for c in c_refs], [t[...] for t in t_refs])
    for o_ref, o in zip(o_refs, outs):
        o_ref[...] = o.astype(o_ref.dtype)


def _col_map(j, i, off):
    return (0, j + off)


def _tile_map(j, i, off):
    return (i, j + off)


def _row_elem_map(j, i, off, tn):
    return (pl.multiple_of(off + j * tn, 8), 0)


def _mm(a_list, w_list, n_cols, epi, out_dtypes, cols=(), tiles=(), tm=1024, tn=512, w_single=False, name="mm"):
    na, nd, nc, nt, no = len(a_list), len(w_list), len(cols), len(tiles), len(out_dtypes)
    assert na in (1, nd)
    a_list = [a if isinstance(a, tuple) else (a,) for a in a_list]
    m = sum(p.shape[0] for p in a_list[0])
    split = a_list[0][0].shape[0] // tm
    in_specs, a_flat = [], []
    for parts in a_list:
        k = parts[0].shape[1]
        if len(parts) == 2:
            in_specs.append(pl.BlockSpec((tm, k), lambda j, i: (jnp.minimum(i, split - 1), 0)))
            in_specs.append(pl.BlockSpec((tm, k), lambda j, i: (jnp.maximum(i - split, 0), 0),
                                         pipeline_mode=pl.Buffered(1)))
        else:
            in_specs.append(pl.BlockSpec((tm, k), lambda j, i: (i, 0)))
        a_flat += list(parts)
    scratch = []
    mode = dict(pipeline_mode=pl.Buffered(1)) if w_single else {}
    for w, off, transposed in w_list:
        if transposed:
            k = w.shape[1]
            assert off % 8 == 0
            in_specs.append(pl.BlockSpec((pl.Element(tn), pl.Element(k)), functools.partial(_row_elem_map, off=off, tn=tn), **mode))
            scratch.append(pltpu.VMEM((tn, k), bf16))
        else:
            k = w.shape[0]
            in_specs.append(pl.BlockSpec((k, tn), functools.partial(_col_map, off=off // tn), **mode))
            scratch.append(pltpu.VMEM((k, tn), bf16))
    in_specs += [pl.BlockSpec((c.shape[0], tn), functools.partial(_col_map, off=0)) for c in cols]
    in_specs += [pl.BlockSpec((tm, tn), t[1], **(dict(pipeline_mode=t[2]) if len(t) > 2 else {})) for t in tiles]
    outs = pl.pallas_call(
        functools.partial(_mm_kernel, a_parts=tuple(len(p) for p in a_list), split=split, nd=nd, nc=nc, nt=nt, no=no,
                          epi=epi, w_t=tuple(t for _, _, t in w_list)),
        grid=(n_cols // tn, m // tm),
        in_specs=in_specs,
        out_specs=[pl.BlockSpec((tm, tn), functools.partial(_tile_map, off=0)) for _ in out_dtypes],
        out_shape=[jax.ShapeDtypeStruct((m, n_cols), dt) for dt in out_dtypes],
        scratch_shapes=scratch,
        compiler_params=_cparams(("arbitrary", "arbitrary")),
        name=name,
    )(*a_flat, *[w for w, _, _ in w_list], *cols, *[t[0] for t in tiles])
    return outs


def _ffn_kernel(h_ref, wg_ref, wu_ref, wd_ref, o_ref):
    @pl.when(pl.program_id(1) == 0)
    def _():
        o_ref[...] = jnp.zeros_like(o_ref)

    h = h_ref[...]
    gate = _dot(h, wg_ref[...].astype(bf16))
    up = _dot(h, wu_ref[...].astype(bf16))
    act = (jax.nn.silu(gate) * up).astype(bf16)
    o_ref[...] += _dot(act, wd_ref[...].astype(bf16))


def _ffn(h, w_gate, w_up, w_down, tm=1536, tf=256):
    m, d = h.shape
    hidden = w_gate.shape[1]
    return pl.pallas_call(
        _ffn_kernel,
        grid=(m // tm, hidden // tf),
        in_specs=[pl.BlockSpec((tm, d), lambda i, f: (i, 0), pipeline_mode=pl.Buffered(1)),
                  pl.BlockSpec((d, tf), lambda i, f: (0, f)),
                  pl.BlockSpec((d, tf), lambda i, f: (0, f)),
                  pl.BlockSpec((tf, d), lambda i, f: (f, 0))],
        out_specs=pl.BlockSpec((tm, d), lambda i, f: (i, 0)),
        out_shape=jax.ShapeDtypeStruct((m, d), f32),
        compiler_params=_cparams(("arbitrary", "arbitrary")),
        name="ffn",
    )(h, w_gate, w_up, w_down)


def _softplus(x):
    return jnp.maximum(x, 0.0) + jnp.log1p(jnp.exp(-jnp.abs(x)))


def _epi_forget(accs, cols, tiles):
    logits = cols[0]
    mx = jnp.max(logits, axis=0, keepdims=True)
    e = jnp.exp(logits - mx)
    lb = e[0:1, :] / jnp.sum(e, axis=0, keepdims=True)
    sig = jax.nn.sigmoid(accs[0])
    log_f = jnp.log(lb + (1.0 - lb) * sig)
    k = (1.0 - lb) * (1.0 - sig)
    return log_f, k


HG_SPLIT = 2


def _hgrn_consts(c, seg):
    nlev = int(np.log2(seg))
    mat = np.zeros(((nlev + 2) * c, c), np.float32)
    masks = np.zeros((nlev + 1, c, c), np.float32)
    for t in range(c):
        tl = t % seg
        base = t - tl
        for l in range(nlev):
            h = 1 << l
            pos = tl % (2 * h)
            ref = base + tl - pos + h - 1
            if pos >= h:
                mat[l * c + t, ref + 1:t + 1] = 1
            else:
                mat[l * c + t, t + 1:ref + 1] = 1
            for s in range(base, base + seg):
                sl = s % seg
                if sl // (2 * h) == tl // (2 * h) and pos >= h and sl % (2 * h) < h:
                    masks[l, t, s] = 1
        mat[nlev * c + t, base:t + 1] = 1
        mat[(nlev + 1) * c + t, t + 1:base + seg] = 1
        masks[nlev, t, t] = 1
    return mat, masks, nlev


def _hgrn_chunks(q_ref, k_ref, g_ref, v_ref, mat, masks, nlev, c):
    n_chunks = q_ref.shape[0] // c
    heads = q_ref.shape[1] // LANES
    rows = lambda ci: slice(ci * c, (ci + 1) * c)
    cols = lambda h: slice(h * LANES, (h + 1) * LANES)
    items = [(ci, h) for ci in range(n_chunks) for h in range(heads)]
    e_all = [jnp.exp(_dot3(mat, g_ref[rows(ci), :])) for ci in range(n_chunks)]
    q = {(ci, h): q_ref[rows(ci), cols(h)] for ci, h in items}
    k = {(ci, h): k_ref[rows(ci), cols(h)] for ci, h in items}
    v = {(ci, h): v_ref[rows(ci), cols(h)].astype(bf16) for ci, h in items}
    s = {it: jnp.where(masks[nlev], _dot_nt(q[it].astype(bf16), k[it].astype(bf16)), 0.0) for it in items}
    for l in range(nlev):
        for ci, h in items:
            e = e_all[ci][l * c:(l + 1) * c, cols(h)]
            s[ci, h] = jnp.where(masks[l], _dot_nt((q[ci, h] * e).astype(bf16), (k[ci, h] * e).astype(bf16)), s[ci, h])
    res = {}
    for ci, h in items:
        o = _dot(s[ci, h].astype(bf16), v[ci, h])
        eb = e_all[ci][nlev * c:(nlev + 1) * c, cols(h)]
        er = e_all[ci][(nlev + 1) * c:, cols(h)]
        res[ci, h] = (o, q[ci, h] * eb, k[ci, h] * er, eb, v[ci, h])
    return res


def _hgrn_out(o, sg, nw):
    ms = jnp.mean(o * o, axis=-1, keepdims=True)
    return o * lax.rsqrt(ms + EPS) * nw * sg


def _hgrn_prompt_kernel(q_ref, g_ref, k_ref, v_ref, sg_ref, nw_ref, mat_ref, mask_ref, o_ref, s_ref, st_ref,
                        *, c, nlev, n_chunks):
    t_blk = pl.program_id(2)

    @pl.when(t_blk == 0)
    def _():
        st_ref[...] = jnp.zeros_like(st_ref)

    masks = [mask_ref[l] > 0.5 for l in range(nlev + 1)]
    mat = mat_ref[...]
    nw = nw_ref[...]

    res = _hgrn_chunks(q_ref, k_ref, g_ref, v_ref, mat, masks, nlev, c)
    upd = {it: _dot_tn(r[4], r[2].astype(bf16)) for it, r in res.items()}
    for h in range(st_ref.shape[0]):
        cs = slice(h * LANES, (h + 1) * LANES)
        st = st_ref[h]
        for ci in range(n_chunks):
            rows = slice(ci * c, (ci + 1) * c)
            o, qe, _, eb, _ = res[ci, h]
            o = o + _dot_nt(qe.astype(bf16), st.astype(bf16))
            st = st * eb[c - 1:c, :] + upd[ci, h]
            o_ref[rows, cs] = _hgrn_out(o, sg_ref[rows, cs], nw[:, cs]).astype(o_ref.dtype)
        st_ref[h] = st

    @pl.when(t_blk == pl.num_programs(2) - 1)
    def _():
        for h in range(st_ref.shape[0]):
            s_ref[0, h] = st_ref[h].T


def _hgrn_sample_kernel(q_ref, g_ref, k_ref, v_ref, sg_ref, nw_ref, mat_ref, mask_ref, s_in_ref,
                        o_ref, s_out_ref, *, c, nlev, seg):
    masks = [mask_ref[l] > 0.5 for l in range(nlev + 1)]
    mat = mat_ref[...]
    nw = nw_ref[...]
    small = seg < 16
    per = c // seg
    res = _hgrn_chunks(q_ref, k_ref, g_ref, v_ref, mat, masks, nlev, c)
    for (ci, h), (o, qe, kd, eb, v) in res.items():
        rows = slice(ci * c, (ci + 1) * c)
        cs = slice(h * LANES, (h + 1) * LANES)
        v = v.astype(f32)
        eb_t = jnp.concatenate([eb, eb], axis=0).T
        parts = []
        for j in range(per):
            sl = slice(j * seg, (j + 1) * seg)
            s0 = s_in_ref[ci * per + j, h]
            parts.append(_dot(_mxu_operand(qe[sl], small), _mxu_operand(s0, small)))
            last = j * seg + seg - 1
            upd = _dot_tn(_mxu_operand(kd[sl], small), _mxu_operand(v[sl], small))
            s_out_ref[ci * per + j, h] = s0 * eb_t[:, last:last + 1] + upd
        o = o + jnp.concatenate(parts, axis=0)
        o_ref[rows, cs] = _hgrn_out(o, sg_ref[rows, cs], nw[:, cs]).astype(o_ref.dtype)


def _hgrn(q, g, k, v, sg, nw, state, n_prompt, batch, seq, dec_seq):
    m, width = q.shape
    heads = width // LANES
    nw = nw.reshape(1, width)
    c = CHUNK
    hp = 4
    hw = hp * LANES
    tc = 1024
    nt = seq // tc
    mat, masks, nlev = _hgrn_consts(c, c)
    row_spec = pl.BlockSpec((tc, hw), lambda b, h, t: (b * nt + t, h))
    const2 = lambda shape: pl.BlockSpec(shape, lambda b, h, t: (0,) * len(shape))
    o_p, s_p = pl.pallas_call(
        functools.partial(_hgrn_prompt_kernel, c=c, nlev=nlev, n_chunks=tc // c),
        grid=(batch, heads // hp, nt),
        in_specs=[row_spec] * 5 + [pl.BlockSpec((1, hw), lambda b, h, t: (0, h)), const2((mat.shape[0], HG_SPLIT * c)), const2(masks.shape)],
        out_specs=[row_spec, pl.BlockSpec((1, hp, HG_DK, LANES), lambda b, h, t: (b, h, 0, 0))],
        out_shape=[jax.ShapeDtypeStruct((n_prompt, width), bf16),
                   jax.ShapeDtypeStruct((batch, heads, HG_DK, LANES), f32)],
        scratch_shapes=[pltpu.VMEM((hp, LANES, HG_DK), f32)],
        compiler_params=_cparams(("arbitrary", "arbitrary", "arbitrary")),
        name="hgrn_prompt",
    )(q, g, k, v, sg, nw, _tile3(mat, HG_SPLIT), jnp.asarray(masks))
    n_sample = m - n_prompt
    rows = 2 * c
    per = rows // dec_seq
    mat, masks, nlev = _hgrn_consts(c, dec_seq)
    base = n_prompt // rows
    row_spec = pl.BlockSpec((rows, hw), lambda jb, h: (base + jb, h))
    st_spec = pl.BlockSpec((per, hp, HG_DK, LANES), lambda jb, h: (jb, h, 0, 0))
    const2 = lambda shape: pl.BlockSpec(shape, lambda jb, h: (0,) * len(shape))
    o_s, s_s = pl.pallas_call(
        functools.partial(_hgrn_sample_kernel, c=c, nlev=nlev, seg=dec_seq),
        grid=(n_sample // rows, heads // hp),
        in_specs=[row_spec] * 5 + [pl.BlockSpec((1, hw), lambda jb, h: (0, h)), const2((mat.shape[0], HG_SPLIT * c)),
                                   const2(masks.shape), st_spec],
        out_specs=[pl.BlockSpec((rows, hw), lambda jb, h: (jb, h)), st_spec],
        out_shape=[jax.ShapeDtypeStruct((n_sample, width), bf16), jax.ShapeDtypeStruct(state.shape, f32)],
        compiler_params=_cparams(("arbitrary", "arbitrary")),
        name="hgrn_sample",
    )(q, g, k, v, sg, nw, _tile3(mat, HG_SPLIT), jnp.asarray(masks), state)
    return (o_p, o_s), s_p, s_s


GROUP_W = 512
PAIRS = GROUP_W // LANES
XBC_W = GROUP_W + 2 * M_DSTATE


def _ssd_consts(c, seg):
    t = np.arange(c)
    same = (t[:, None] // seg) == (t[None, :] // seg)
    tril = (same & (t[None, :] <= t[:, None])).astype(np.float32)
    return tril


def _ssd_chunks(chunks, a_row, dsk, nw, lc, tril, get_state, set_state, c, seg, state_t, after_first_stage=None):
    nseg = c // seg
    small = seg < 16
    assert 2 * c == LANES and 2 * M_HEADDIM == LANES and not (state_t and nseg > 1)
    lo_half = lax.broadcasted_iota(jnp.int32, (c, LANES), 1) < M_HEADDIM
    lo_row = lo_half[0:1, :]
    n = len(chunks)
    items = [(ci, p) for ci in range(n) for p in range(PAIRS)]
    pc = lambda p: slice(p * LANES, (p + 1) * LANES)
    acum = [_dot3(lc, dt * a_row) for _, _, _, dt, _ in chunks]
    bmb = [bm.astype(bf16) for _, bm, _, _, _ in chunks]
    cmb = [cm.astype(bf16) for _, _, cm, _, _ in chunks]
    cb2 = [_dot_nt(cmb[ci], jnp.concatenate([bmb[ci], bmb[ci]], axis=0)) for ci in range(n)]
    acum_t = [jnp.concatenate([a, a], axis=0).T for a in acum]
    dt_t = [jnp.concatenate([ch[3], ch[3]], axis=0).T for ch in chunks]
    if state_t:
        bm_t = [ch[1].T.astype(bf16) for ch in chunks]
    else:
        ea_t = [jnp.exp(a[0:8, 0:c]) for a in acum_t]
    if after_first_stage is not None:
        after_first_stage()
    y, xw, ea = {}, {}, {}
    for ci, p in items:
        h0, h1 = 2 * p, 2 * p + 1
        xp, dt = chunks[ci][0][:, pc(p)], chunks[ci][3]
        acp = jnp.where(lo_half, acum[ci][:, h0:h0 + 1], acum[ci][:, h1:h1 + 1])
        dtp = jnp.where(lo_half, dt[:, h0:h0 + 1], dt[:, h1:h1 + 1])
        a_src = jnp.where(lo_row, acum_t[ci][h0:h0 + 1, :], acum_t[ci][h1:h1 + 1, :])
        dt_src = jnp.where(lo_row, dt_t[ci][h0:h0 + 1, :], dt_t[ci][h1:h1 + 1, :])
        lm = jnp.exp(jnp.where(tril, acp - a_src, -1e30))
        sc = (cb2[ci] * lm * dt_src).astype(bf16)
        x_blk = jnp.concatenate([jnp.where(lo_half, xp, 0.0), jnp.where(lo_half, 0.0, xp)], axis=0).astype(bf16)
        y[ci, p] = _dot(sc, x_blk)
        if nseg == 1:
            alast = acp[c - 1:c, :]
        else:
            alast = jnp.concatenate(
                [jnp.broadcast_to(acp[j * seg + seg - 1:j * seg + seg, :], (seg, LANES)) for j in range(nseg)], axis=0)
        xw[ci, p] = xp * (jnp.exp(alast - acp) * dtp)
        ea[ci, p] = jnp.exp(acp)
    cs = {}
    if state_t:
        upd = {(ci, p): _dot(bm_t[ci], xw[ci, p].astype(bf16)) for ci, p in items}
        for p in range(PAIRS):
            st = get_state(0, 0, p)
            for ci in range(n):
                cs[ci, p] = _dot(cmb[ci], st.astype(bf16))
                st = st * ea[ci, p][c - 1:c, :] + upd[ci, p]
            set_state(0, 0, p, st)
    else:
        for ci, p in items:
            h0, h1 = 2 * p, 2 * p + 1
            bm, cm = chunks[ci][1], chunks[ci][2]
            parts = []
            for j in range(nseg):
                sl = slice(j * seg, (j + 1) * seg)
                s0 = get_state(ci, j, p)
                parts.append(_dot_nt(_mxu_operand(cm[sl], small), _mxu_operand(s0, small)))
                upd = _dot_tn(_mxu_operand(xw[ci, p][sl], small), _mxu_operand(bm[sl], small))
                last = j * seg + seg - 1
                decay = jnp.concatenate(
                    [jnp.broadcast_to(ea_t[ci][h0:h0 + 1, last:last + 1], (M_HEADDIM, LANES)),
                     jnp.broadcast_to(ea_t[ci][h1:h1 + 1, last:last + 1], (M_HEADDIM, LANES))], axis=0)
                set_state(ci, j, p, s0 * decay + upd)
            cs[ci, p] = parts[0] if nseg == 1 else jnp.concatenate(parts, axis=0)
    outs = []
    for ci in range(n):
        xs, zs = chunks[ci][0], chunks[ci][4]
        ssq = jnp.zeros((c, 1), f32)
        ys = []
        for p in range(PAIRS):
            yp = (y[ci, p] + cs[ci, p] * ea[ci, p] + dsk[:, pc(p)] * xs[:, pc(p)]) * zs[:, pc(p)]
            ssq = ssq + jnp.sum(yp * yp, axis=-1, keepdims=True)
            ys.append(yp)
        scale = lax.rsqrt(ssq * (1.0 / GROUP_W) + EPS)
        outs.append([ys[p] * scale * nw[:, pc(p)] for p in range(PAIRS)])
    return outs


def _conv_taps(xpad_ref, lead, rows, w, b):
    acc = None
    for j in range(M_CONV):
        term = xpad_ref[lead + (slice(5 + j + rows[0], 5 + j + rows[1]), slice(None))] * w[j:j + 1, :]
        acc = term if acc is None else acc + term
    return jax.nn.silu(b + acc)


def _ssd_prompt_kernel(xr_ref, br_ref, cr_ref, dt_ref, zs_ref, wx_ref, wb_ref, wc_ref, bx_ref, bb_ref, bc_ref,
                       alog_ref, dsk_ref, nw_ref, lc_ref, tril_ref, h_ref, wg_ref, o_ref, s_ref, gate_ref,
                       xpad_ref, xc_ref, st_ref, wgb_ref, *, c, rows):
    t_blk = pl.program_id(2)

    @pl.when(t_blk == 0)
    def _():
        st_ref[...] = jnp.zeros_like(st_ref)
        xpad_ref[0:8, :] = jnp.zeros((8, XBC_W), f32)

    @pl.when(jnp.logical_and(t_blk == 0, pl.program_id(1) == 0))
    def _():
        wgb_ref[...] = wg_ref[...].astype(bf16)

    @pl.when(t_blk > 0)
    def _():
        xpad_ref[0:8, :] = xpad_ref[rows:rows + 8, :]

    half = rows // 2

    def gate_half(i):
        rs = slice(i * half, (i + 1) * half)
        gate_ref[rs, :] = _dot_nt(h_ref[rs, :], wgb_ref[...])

    gate_half(0)
    xpad_ref[8:8 + rows, 0:GROUP_W] = xr_ref[...]
    xpad_ref[8:8 + rows, GROUP_W:GROUP_W + M_DSTATE] = br_ref[...]
    xpad_ref[8:8 + rows, GROUP_W + M_DSTATE:XBC_W] = cr_ref[...]
    w = jnp.concatenate([wx_ref[...], wb_ref[...], wc_ref[...]], axis=1)
    b = jnp.concatenate([bx_ref[...], bb_ref[...], bc_ref[...]], axis=1)
    for i in range(rows // c):
        xc_ref[i * c:(i + 1) * c, :] = _conv_taps(xpad_ref, (), (i * c, (i + 1) * c), w, b)

    a_row = -jnp.exp(alog_ref[...])
    dsk = dsk_ref[...]
    nw = nw_ref[...]
    lc = lc_ref[...]
    tril = tril_ref[...] > 0.5

    def get_state(ci, j, p):
        return st_ref[p]

    def set_state(ci, j, p, val):
        st_ref[p] = val

    rs = lambda ci: slice(ci * c, (ci + 1) * c)
    chunks = [(xc_ref[rs(ci), 0:GROUP_W], xc_ref[rs(ci), GROUP_W:GROUP_W + M_DSTATE], xc_ref[rs(ci), GROUP_W + M_DSTATE:XBC_W],
               dt_ref[rs(ci), :], zs_ref[rs(ci), :]) for ci in range(rows // c)]
    outs = _ssd_chunks(chunks, a_row, dsk, nw, lc, tril, get_state, set_state, c, c, state_t=True,
                       after_first_stage=lambda: gate_half(1))
    for ci, out in enumerate(outs):
        for p in range(PAIRS):
            o_ref[rs(ci), p * LANES:(p + 1) * LANES] = out[p].astype(o_ref.dtype)

    @pl.when(t_blk == pl.num_programs(2) - 1)
    def _():
        for p in range(PAIRS):
            s_ref[0, p * LANES:(p + 1) * LANES, :] = st_ref[p].T


def _ssd_sample_kernel(xr_ref, br_ref, cr_ref, dt_ref, zs_ref, wx_ref, wb_ref, wc_ref, bx_ref, bb_ref, bc_ref,
                       alog_ref, dsk_ref, nw_ref, lc_ref, tril_ref, hx_ref, hb_ref, hc_ref, s_in_ref,
                       o_ref, s_out_ref, xpad_ref, *, c, seg):
    nseg = c // seg
    n_chunks = xr_ref.shape[0] // c
    w = jnp.concatenate([wx_ref[...], wb_ref[...], wc_ref[...]], axis=1)
    b = jnp.concatenate([bx_ref[...], bb_ref[...], bc_ref[...]], axis=1)
    conv = []
    for j in range(n_chunks * nseg):
        sl = slice(j * seg, (j + 1) * seg)
        xpad_ref[j, 5:8, 0:GROUP_W] = hx_ref[j]
        xpad_ref[j, 5:8, GROUP_W:GROUP_W + M_DSTATE] = hb_ref[j]
        xpad_ref[j, 5:8, GROUP_W + M_DSTATE:XBC_W] = hc_ref[j]
        xpad_ref[j, 8:8 + seg, 0:GROUP_W] = xr_ref[sl, :]
        xpad_ref[j, 8:8 + seg, GROUP_W:GROUP_W + M_DSTATE] = br_ref[sl, :]
        xpad_ref[j, 8:8 + seg, GROUP_W + M_DSTATE:XBC_W] = cr_ref[sl, :]
        conv.append(_conv_taps(xpad_ref, (j,), (0, seg), w, b))

    def get_state(ci, j, p):
        return s_in_ref[ci * nseg + j, p * LANES:(p + 1) * LANES, :]

    def set_state(ci, j, p, val):
        s_out_ref[ci * nseg + j, p * LANES:(p + 1) * LANES, :] = val

    chunks = []
    for ci in range(n_chunks):
        xc = jnp.concatenate(conv[ci * nseg:(ci + 1) * nseg], axis=0)
        rs = slice(ci * c, (ci + 1) * c)
        chunks.append((xc[:, 0:GROUP_W], xc[:, GROUP_W:GROUP_W + M_DSTATE], xc[:, GROUP_W + M_DSTATE:XBC_W],
                       dt_ref[rs, :], zs_ref[rs, :]))
    outs = _ssd_chunks(chunks, -jnp.exp(alog_ref[...]), dsk_ref[...], nw_ref[...], lc_ref[...],
                       tril_ref[...] > 0.5, get_state, set_state, c, seg, state_t=False)
    for ci, out in enumerate(outs):
        for p in range(PAIRS):
            o_ref[ci * c:(ci + 1) * c, p * LANES:(p + 1) * LANES] = out[p].astype(o_ref.dtype)


def _ssd(xbc, dt, zs, conv_w, conv_b, alog_p, dsk, nw, state, hist, n_prompt, batch, seq, dec_seq, h, wt, o_gate, gate_cols):
    m = xbc.shape[0]
    inner = zs.shape[1]
    c = CHUNK
    xb_blk = inner // M_DSTATE
    cb_blk = xb_blk + M_GROUPS

    def specs(row_map, nrow, gpos):
        def rm(fn):
            return lambda *ix: fn(row_map(*ix), ix[gpos])
        zero = lambda fn: (lambda *ix: fn(0, ix[gpos]))
        return [
            pl.BlockSpec((nrow, GROUP_W), rm(lambda r, g: (r, g))),
            pl.BlockSpec((nrow, M_DSTATE), rm(lambda r, g: (r, xb_blk + g))),
            pl.BlockSpec((nrow, M_DSTATE), rm(lambda r, g: (r, cb_blk + g))),
            pl.BlockSpec((nrow, LANES), rm(lambda r, g: (r, g))),
            pl.BlockSpec((nrow, GROUP_W), rm(lambda r, g: (r, g))),
            pl.BlockSpec((M_CONV, GROUP_W), zero(lambda r, g: (0, g))),
            pl.BlockSpec((M_CONV, M_DSTATE), zero(lambda r, g: (0, xb_blk + g))),
            pl.BlockSpec((M_CONV, M_DSTATE), zero(lambda r, g: (0, cb_blk + g))),
            pl.BlockSpec((1, GROUP_W), zero(lambda r, g: (0, g))),
            pl.BlockSpec((1, M_DSTATE), zero(lambda r, g: (0, xb_blk + g))),
            pl.BlockSpec((1, M_DSTATE), zero(lambda r, g: (0, cb_blk + g))),
            pl.BlockSpec((1, LANES), zero(lambda r, g: (0, g))),
            pl.BlockSpec((1, GROUP_W), zero(lambda r, g: (0, g))),
            pl.BlockSpec((1, GROUP_W), zero(lambda r, g: (0, g))),
            pl.BlockSpec((c, 3 * c), zero(lambda r, g: (0, 0))),
            pl.BlockSpec((c, 2 * c), zero(lambda r, g: (0, 0))),
        ]

    common = (xbc, xbc, xbc, dt, zs, conv_w, conv_w, conv_w, conv_b, conv_b, conv_b, alog_p, dsk, nw)
    rows = 1024
    nt = seq // rows
    tril = _ssd_consts(c, c)
    in_specs = specs(lambda g, b, t: b * nt + t, rows, 0)
    d_model = h.shape[1]
    gw = gate_cols // M_GROUPS
    in_specs += [
        pl.BlockSpec((rows, d_model), lambda g, b, t: (b * nt + t, 0)),
        pl.BlockSpec((pl.Element(gw), pl.Element(d_model)), lambda g, b, t: (pl.multiple_of(o_gate + g * gw, 8), 0),
                     pipeline_mode=pl.Buffered(1)),
    ]
    o_p, s_p, gates_p = pl.pallas_call(
        functools.partial(_ssd_prompt_kernel, c=c, rows=rows),
        grid=(M_GROUPS, batch, nt),
        in_specs=in_specs,
        out_specs=[pl.BlockSpec((rows, GROUP_W), lambda g, b, t: (b * nt + t, g)),
                   pl.BlockSpec((1, GROUP_W, M_DSTATE), lambda g, b, t: (b, g, 0)),
                   pl.BlockSpec((rows, gw), lambda g, b, t: (b * nt + t, g))],
        out_shape=[jax.ShapeDtypeStruct((n_prompt, inner), bf16),
                   jax.ShapeDtypeStruct((batch, inner, M_DSTATE), f32),
                   jax.ShapeDtypeStruct((n_prompt, gate_cols), f32)],
        scratch_shapes=[pltpu.VMEM((rows + 8, XBC_W), f32), pltpu.VMEM((rows, XBC_W), f32),
                        pltpu.VMEM((PAIRS, LANES, M_DSTATE), f32), pltpu.VMEM((gw, d_model), bf16)],
        compiler_params=_cparams(("arbitrary", "arbitrary", "arbitrary")),
        name="ssd_prompt",
    )(*common, _tile3(tril), jnp.asarray(np.tile(tril, (1, 2))), h, wt)
    n_sample = m - n_prompt
    rows = 2 * c
    per = rows // dec_seq
    base = n_prompt // rows
    tril = _ssd_consts(c, dec_seq)
    in_specs = specs(lambda jb, g: base + jb, rows, 1)
    in_specs += [
        pl.BlockSpec((per, M_CONV - 1, GROUP_W), lambda jb, g: (jb, 0, g)),
        pl.BlockSpec((per, M_CONV - 1, M_DSTATE), lambda jb, g: (jb, 0, xb_blk + g)),
        pl.BlockSpec((per, M_CONV - 1, M_DSTATE), lambda jb, g: (jb, 0, cb_blk + g)),
        pl.BlockSpec((per, GROUP_W, M_DSTATE), lambda jb, g: (jb, g, 0)),
    ]
    o_s, s_s = pl.pallas_call(
        functools.partial(_ssd_sample_kernel, c=c, seg=dec_seq),
        grid=(n_sample // rows, M_GROUPS),
        in_specs=in_specs,
        out_specs=[pl.BlockSpec((rows, GROUP_W), lambda jb, g: (jb, g)),
                   pl.BlockSpec((per, GROUP_W, M_DSTATE), lambda jb, g: (jb, g, 0))],
        out_shape=[jax.ShapeDtypeStruct((n_sample, inner), bf16), jax.ShapeDtypeStruct(state.shape, f32)],
        scratch_shapes=[pltpu.VMEM((per, 16, XBC_W), f32)],
        compiler_params=_cparams(("arbitrary", "arbitrary")),
        name="ssd_sample",
    )(*common, _tile3(tril), jnp.asarray(np.tile(tril, (1, 2))), hist, hist, hist, state)
    return (o_p, o_s), s_p, s_s, gates_p


def kernel(x_prompt, x_sample, state_hgrn, state_ssm, state_conv, norm_mix, w_in, hg_lb_logits, hg_norm, conv_w, conv_b,
           dt_bias, a_log, d_skip, ssm_norm, w_branch_hg, w_branch_ssm, w_out, norm_ffn, w_ffn_gate, w_ffn_up,
           w_ffn_down, norm_final):
    batch, seq, d = x_prompt.shape
    dec_batch, dec_seq, _ = x_sample.shape
    n_prompt, n_sample = batch * seq, dec_batch * dec_seq
    hg_heads = state_hgrn.shape[2]
    kdim = hg_heads * HG_DK
    vdim = d
    inner = d
    m_heads = state_ssm.shape[2]
    conv_dim = conv_w.shape[2]
    hpg = m_heads // M_GROUPS

    xp2, xs2 = x_prompt.reshape(n_prompt, d), x_sample.reshape(n_sample, d)
    h = _rmsnorm([xp2, xs2], norm_mix[0], bf16, n_prompt)

    wt = jnp.swapaxes(w_in, 1, 2)[0]
    o_q, o_f, o_v, o_g, o_z, o_xbc = 0, kdim, 2 * kdim, 2 * kdim + vdim, 2 * kdim + 2 * vdim, 2 * kdim + 2 * vdim + inner
    o_dt = o_xbc + conv_dim
    o_gate = o_dt + m_heads
    scale = HG_DK ** -0.5
    ident = lambda a, c, t: (a[0],)
    silu = lambda a, c, t: (jax.nn.silu(a[0]),)
    wide = dict(tm=1024, tn=2048, w_single=True)
    (q,) = _mm([h], [(wt, o_q, True)], kdim, lambda a, c, t: (a[0] * scale,), [f32], name="proj_q", **wide)
    log_f, k = _mm([h], [(wt, o_f, True)], kdim, _epi_forget, [f32, f32], cols=[hg_lb_logits], name="proj_f",
                   tm=512, tn=2048, w_single=True)
    (v,) = _mm([h], [(wt, o_v, True)], vdim, ident, [bf16], name="proj_v", **wide)
    (sg,) = _mm([h], [(wt, o_g, True)], vdim, silu, [f32], name="proj_g", **wide)
    (zs,) = _mm([h], [(wt, o_z, True)], inner, silu, [f32], name="proj_z", **wide)
    (xbc,) = _mm([h], [(wt, o_xbc, True)], conv_dim, ident, [f32], tm=1024, tn=conv_dim // 2, w_single=True, name="proj_xbc")
    pad_heads = lambda p: jnp.pad(p.reshape(M_GROUPS, hpg, -1), ((0, 0), (0, LANES - hpg), (0, 0))).reshape(M_GROUPS * LANES, -1)
    (dt,) = _mm([h], [(pad_heads(wt[o_dt:o_gate]), 0, True)], M_GROUPS * LANES,
                lambda a, c, t: (_softplus(a[0] + c[0]),), [f32], cols=[pad_heads(dt_bias[0]).T], name="proj_dt")
    (gates_s,) = _mm([h[n_prompt:]], [(wt, o_gate, True)], 2 * d, ident, [f32], name="proj_gate_sample", **wide)

    o_hg, shp, shs = _hgrn(q, log_f, k, v, sg, hg_norm[0], state_hgrn[0], n_prompt, batch, seq, dec_seq)
    dsk = jnp.repeat(d_skip[0], M_HEADDIM).reshape(1, inner)
    o_m, smp, sms, gates_p = _ssd(xbc, dt, zs, conv_w[0], conv_b[0].reshape(1, conv_dim), pad_heads(a_log[0]).T, dsk,
                                  ssm_norm[0].reshape(1, inner), state_ssm[0].reshape(dec_batch, inner, M_DSTATE),
                                  state_conv[0], n_prompt, batch, seq, dec_seq, h, wt, o_gate, 2 * d)

    tm, tn = 512, 1024
    n_p_tiles = n_prompt // tm
    first = lambda off: (lambda j, i: (jnp.minimum(i, n_p_tiles - 1), j + off))
    second = lambda off: (lambda j, i: (jnp.maximum(i - n_p_tiles, 0), j + off))

    def gated_sum(a, c, t):
        is_prompt = pl.program_id(1) < n_p_tiles
        g_hg = jax.nn.sigmoid(jnp.where(is_prompt, t[0], t[1]))
        g_m = jax.nn.sigmoid(jnp.where(is_prompt, t[2], t[3]))
        return (g_hg * a[0] + g_m * a[1],)

    (merged,) = _mm([o_hg, o_m], [(w_branch_hg[0], 0, False), (w_branch_ssm[0], 0, False)], d, gated_sum, [bf16],
                    tiles=[(gates_p, first(0)), (gates_s, second(0), pl.Buffered(1)),
                           (gates_p, first(d // tn)), (gates_s, second(d // tn), pl.Buffered(1))],
                    tm=tm, tn=tn, w_single=True, name="merge")
    tm = 512
    n_p_tiles = n_prompt // tm

    def add_x_norm(a, c, t):
        x1 = jnp.where(pl.program_id(1) < n_p_tiles, t[0], t[1]) + a[0]
        ms = jnp.mean(x1 * x1, axis=-1, keepdims=True)
        return x1, x1 * lax.rsqrt(ms + EPS) * c[0]

    x1, h2 = _mm([merged], [(w_out[0], 0, False)], d, add_x_norm, [f32, bf16], cols=[norm_ffn],
                 tiles=[(xp2, lambda j, i: (jnp.minimum(i, n_p_tiles - 1), j)),
                        (xs2, lambda j, i: (jnp.maximum(i - n_p_tiles, 0), j), pl.Buffered(1))],
                 tm=tm, tn=d, w_single=True, name="out_proj")
    ffn = _ffn(h2, w_ffn_gate[0], w_ffn_up[0], w_ffn_down[0])
    y_p, y_s = _rmsnorm([x1], norm_final, f32, n_prompt, split_out=True, res=ffn)

    hist = M_CONV - 1
    new_conv_p = jnp.stack([xbc[(b + 1) * seq - hist:(b + 1) * seq] for b in range(batch)])[None]
    new_conv_s = xbc[n_prompt:].reshape(dec_batch, dec_seq, conv_dim)[:, dec_seq - hist:][None]
    return (y_p.reshape(batch, seq, d), y_s.reshape(dec_batch, dec_seq, d),
            shp[None], smp.reshape(1, batch, m_heads, M_HEADDIM, M_DSTATE), new_conv_p,
            shs[None], sms.reshape(1, dec_batch, m_heads, M_HEADDIM, M_DSTATE), new_conv_s)
```

```python
import functools

import numpy as np
import jax
import jax.numpy as jnp
from jax import lax
from jax.experimental import pallas as pl
from jax.experimental.pallas import tpu as pltpu

f32 = jnp.float32
bf16 = jnp.bfloat16

EPS = 1e-6
LANES = 128
CHUNK = 64
HG_DK = 128
M_HEADDIM = 64
M_DSTATE = 128
M_GROUPS = 4
M_CONV = 4
VMEM_LIMIT = 56 * 1024 * 1024


def _cparams(sem):
    return pltpu.CompilerParams(dimension_semantics=sem, vmem_limit_bytes=VMEM_LIMIT)


def _dot(a, b):
    return jnp.dot(a, b, preferred_element_type=f32)


def _dot_nt(a, b):
    return lax.dot_general(a, b, (((1,), (1,)), ((), ())), preferred_element_type=f32)


def _dot_tn(a, b):
    return lax.dot_general(a, b, (((0,), (0,)), ((), ())), preferred_element_type=f32)


def _split(x, n):
    pieces = []
    for _ in range(n - 1):
        p = x.astype(bf16)
        pieces.append(p)
        x = x - p.astype(f32)
    return pieces + [x.astype(bf16)]


def _tile3(m, n=3):
    return jnp.asarray(np.tile(m, (1, n)), bf16)


def _dot3(mn, x):
    return _dot(mn, jnp.concatenate(_split(x, mn.shape[1] // x.shape[0]), axis=0))


def _mxu_operand(x, small):
    xb = x.astype(bf16)
    return xb.astype(f32) if small else xb


def _rmsnorm_kernel(*refs, n_in, n_out, split, has_res):
    x_refs, w_ref, o_refs = refs[:n_in], refs[n_in], refs[n_in + 1 + has_res:]

    def run(x_ref, o_ref):
        x = x_ref[...]
        if has_res:
            x = x + refs[n_in + 1][...]
        ms = jnp.mean(x * x, axis=-1, keepdims=True)
        o_ref[...] = (x * lax.rsqrt(ms + EPS) * w_ref[...]).astype(o_ref.dtype)

    if n_in == 1 and n_out == 1:
        run(x_refs[0], o_refs[0])
    else:
        i = pl.program_id(0)
        pl.when(i < split)(lambda: run(x_refs[0], o_refs[0]))
        pl.when(i >= split)(lambda: run(x_refs[-1], o_refs[-1]))


def _rmsnorm(xs, w, out_dtype, n_first, split_out=False, res=None, tm=512):
    d = xs[0].shape[1]
    m = sum(x.shape[0] for x in xs)
    split = n_first // tm
    first = lambda i: (jnp.minimum(i, split - 1), 0)
    second = lambda i: (jnp.maximum(i - split, 0), 0)
    whole = lambda i: (i, 0)
    blk = lambda fn: pl.BlockSpec((tm, d), fn)
    in_specs = [blk(whole)] if len(xs) == 1 else [blk(first), blk(second)]
    if split_out:
        out_specs = [blk(first), blk(second)]
        out_shape = [jax.ShapeDtypeStruct((n_first, d), out_dtype), jax.ShapeDtypeStruct((m - n_first, d), out_dtype)]
    else:
        out_specs = [blk(whole)]
        out_shape = [jax.ShapeDtypeStruct((m, d), out_dtype)]
    extra = [] if res is None else [res]
    outs = pl.pallas_call(
        functools.partial(_rmsnorm_kernel, n_in=len(xs), n_out=len(out_shape), split=split, has_res=len(extra)),
        grid=(m // tm,),
        in_specs=in_specs + [pl.BlockSpec((1, d), lambda i: (0, 0))] + [blk(whole) for _ in extra],
        out_specs=out_specs,
        out_shape=out_shape,
        compiler_params=_cparams(("arbitrary",)),
        name="rmsnorm",
    )(*xs, w.reshape(1, d), *extra)
    return outs if split_out else outs[0]


def _prenorm_dt_kernel(xp_ref, xs_ref, w_ref, wdt_ref, bias_ref, h_ref, dt_ref, wdtb_ref, *, split):
    i = pl.program_id(0)

    @pl.when(i == 0)
    def _():
        wdtb_ref[...] = wdt_ref[...].astype(bf16)

    def run(x_ref):
        x = x_ref[...]
        ms = jnp.mean(x * x, axis=-1, keepdims=True)
        hb = (x * lax.rsqrt(ms + EPS) * w_ref[...]).astype(bf16)
        h_ref[...] = hb
        dt_ref[...] = _softplus(_dot_nt(hb, wdtb_ref[...]) + bias_ref[...])

    pl.when(i < split)(lambda: run(xp_ref))
    pl.when(i >= split)(lambda: run(xs_ref))


def _prenorm_dt(xp, xs, w, w_dt_t, dt_bias, tm=512):
    d = xp.shape[1]
    m = xp.shape[0] + xs.shape[0]
    n_dt = w_dt_t.shape[0]
    split = xp.shape[0] // tm
    return pl.pallas_call(
        functools.partial(_prenorm_dt_kernel, split=split),
        grid=(m // tm,),
        in_specs=[pl.BlockSpec((tm, d), lambda i: (jnp.minimum(i, split - 1), 0)),
                  pl.BlockSpec((tm, d), lambda i: (jnp.maximum(i - split, 0), 0)),
                  pl.BlockSpec((1, d), lambda i: (0, 0)),
                  pl.BlockSpec((n_dt, d), lambda i: (0, 0)),
                  pl.BlockSpec((1, n_dt), lambda i: (0, 0))],
        out_specs=[pl.BlockSpec((tm, d), lambda i: (i, 0)), pl.BlockSpec((tm, n_dt), lambda i: (i, 0))],
        out_shape=[jax.ShapeDtypeStruct((m, d), bf16), jax.ShapeDtypeStruct((m, n_dt), f32)],
        scratch_shapes=[pltpu.VMEM((n_dt, d), bf16)],
        compiler_params=_cparams(("arbitrary",)),
        name="prenorm_dt",
    )(xp, xs, w.reshape(1, d), w_dt_t, dt_bias)


def _mm_kernel(*refs, a_parts, split, nd, nc, nt, no, epi, w_t):
    a_refs, refs = refs[:sum(a_parts)], refs[sum(a_parts):]
    w_refs = refs[:nd]
    c_refs = refs[nd:nd + nc]
    t_refs = refs[nd + nc:nd + nc + nt]
    o_refs = refs[nd + nc + nt:nd + nc + nt + no]
    wb_refs = refs[nd + nc + nt + no:]

    @pl.when(pl.program_id(1) == 0)
    def _():
        for w_ref, wb_ref in zip(w_refs, wb_refs):
            wb_ref[...] = w_ref[...].astype(bf16)

    a_vals = []
    for parts in a_parts:
        if parts == 2:
            a_vals.append(jnp.where(pl.program_id(1) < split, a_refs[0][...], a_refs[1][...]))
        else:
            a_vals.append(a_refs[0][...])
        a_refs = a_refs[parts:]
    na = len(a_vals)
    accs = [(_dot_nt if t else _dot)(a_vals[min(i, na - 1)], wb_ref[...])
            for i, (wb_ref, t) in enumerate(zip(wb_refs, w_t))]
    outs = epi(accs, [c[...] for c in c_refs], [t[...] for t in t_refs])
    for o_ref, o in zip(o_refs, outs):
        o_ref[...] = o.astype(o_ref.dtype)


def _col_map(j, i, off):
    return (0, j + off)


def _tile_map(j, i, off):
    return (i, j + off)


def _row_elem_map(j, i, off, tn):
    return (pl.multiple_of(off + j * tn, 8), 0)


def _mm(a_list, w_list, n_cols, epi, out_dtypes, cols=(), tiles=(), tm=1024, tn=512, w_single=False, name="mm"):
    na, nd, nc, nt, no = len(a_list), len(w_list), len(cols), len(tiles), len(out_dtypes)
    assert na in (1, nd)
    a_list = [a if isinstance(a, tuple) else (a,) for a in a_list]
    m = sum(p.shape[0] for p in a_list[0])
    split = a_list[0][0].shape[0] // tm
    in_specs, a_flat = [], []
    for parts in a_list:
        k = parts[0].shape[1]
        if len(parts) == 2:
            in_specs.append(pl.BlockSpec((tm, k), lambda j, i: (jnp.minimum(i, split - 1), 0)))
            in_specs.append(pl.BlockSpec((tm, k), lambda j, i: (jnp.maximum(i - split, 0), 0),
                                         pipeline_mode=pl.Buffered(1)))
        else:
            in_specs.append(pl.BlockSpec((tm, k), lambda j, i: (i, 0)))
        a_flat += list(parts)
    scratch = []
    mode = dict(pipeline_mode=pl.Buffered(1)) if w_single else {}
    for w, off, transposed in w_list:
        if transposed:
            k = w.shape[1]
            assert off % 8 == 0
            in_specs.append(pl.BlockSpec((pl.Element(tn), pl.Element(k)), functools.partial(_row_elem_map, off=off, tn=tn), **mode))
            scratch.append(pltpu.VMEM((tn, k), bf16))
        else:
            k = w.shape[0]
            in_specs.append(pl.BlockSpec((k, tn), functools.partial(_col_map, off=off // tn), **mode))
            scratch.append(pltpu.VMEM((k, tn), bf16))
    in_specs += [pl.BlockSpec((c.shape[0], tn), functools.partial(_col_map, off=0)) for c in cols]
    in_specs += [pl.BlockSpec((tm, tn), t[1], **(dict(pipeline_mode=t[2]) if len(t) > 2 else {})) for t in tiles]
    outs = pl.pallas_call(
        functools.partial(_mm_kernel, a_parts=tuple(len(p) for p in a_list), split=split, nd=nd, nc=nc, nt=nt, no=no,
                          epi=epi, w_t=tuple(t for _, _, t in w_list)),
        grid=(n_cols // tn, m // tm),
        in_specs=in_specs,
        out_specs=[pl.BlockSpec((tm, tn), functools.partial(_tile_map, off=0)) for _ in out_dtypes],
        out_shape=[jax.ShapeDtypeStruct((m, n_cols), dt) for dt in out_dtypes],
        scratch_shapes=scratch,
        compiler_params=_cparams(("arbitrary", "arbitrary")),
        name=name,
    )(*a_flat, *[w for w, _, _ in w_list], *cols, *[t[0] for t in tiles])
    return outs


def _ffn_kernel(h_ref, wg_ref, wu_ref, wd_ref, o_ref):
    @pl.when(pl.program_id(1) == 0)
    def _():
        o_ref[...] = jnp.zeros_like(o_ref)

    h = h_ref[...]
    gate = _dot(h, wg_ref[...].astype(bf16))
    up = _dot(h, wu_ref[...].astype(bf16))
    act = (jax.nn.silu(gate) * up).astype(bf16)
    o_ref[...] += _dot(act, wd_ref[...].astype(bf16))


def _ffn(h, w_gate, w_up, w_down, tm=1536, tf=256):
    m, d = h.shape
    hidden = w_gate.shape[1]
    return pl.pallas_call(
        _ffn_kernel,
        grid=(m // tm, hidden // tf),
        in_specs=[pl.BlockSpec((tm, d), lambda i, f: (i, 0), pipeline_mode=pl.Buffered(1)),
                  pl.BlockSpec((d, tf), lambda i, f: (0, f)),
                  pl.BlockSpec((d, tf), lambda i, f: (0, f)),
                  pl.BlockSpec((tf, d), lambda i, f: (f, 0))],
        out_specs=pl.BlockSpec((tm, d), lambda i, f: (i, 0)),
        out_shape=jax.ShapeDtypeStruct((m, d), f32),
        compiler_params=_cparams(("arbitrary", "arbitrary")),
        name="ffn",
    )(h, w_gate, w_up, w_down)


def _softplus(x):
    return jnp.maximum(x, 0.0) + jnp.log1p(jnp.exp(-jnp.abs(x)))


def _epi_forget(accs, cols, tiles):
    logits = cols[0]
    mx = jnp.max(logits, axis=0, keepdims=True)
    e = jnp.exp(logits - mx)
    lb = e[0:1, :] / jnp.sum(e, axis=0, keepdims=True)
    sig = jax.nn.sigmoid(accs[0])
    log_f = jnp.log(lb + (1.0 - lb) * sig)
    k = (1.0 - lb) * (1.0 - sig)
    return log_f, k


HG_SPLIT = 2


def _hgrn_consts(c, seg):
    nlev = int(np.log2(seg))
    mat = np.zeros(((nlev + 2) * c, c), np.float32)
    masks = np.zeros((nlev + 1, c, c), np.float32)
    for t in range(c):
        tl = t % seg
        base = t - tl
        for l in range(nlev):
            h = 1 << l
            pos = tl % (2 * h)
            ref = base + tl - pos + h - 1
            if pos >= h:
                mat[l * c + t, ref + 1:t + 1] = 1
            else:
                mat[l * c + t, t + 1:ref + 1] = 1
            for s in range(base, base + seg):
                sl = s % seg
                if sl // (2 * h) == tl // (2 * h) and pos >= h and sl % (2 * h) < h:
                    masks[l, t, s] = 1
        mat[nlev * c + t, base:t + 1] = 1
        mat[(nlev + 1) * c + t, t + 1:base + seg] = 1
        masks[nlev, t, t] = 1
    return mat, masks, nlev


def _hgrn_chunks(q_ref, k_ref, g_ref, v_ref, mat, masks, nlev, c):
    n_chunks = q_ref.shape[0] // c
    heads = q_ref.shape[1] // LANES
    rows = lambda ci: slice(ci * c, (ci + 1) * c)
    cols = lambda h: slice(h * LANES, (h + 1) * LANES)
    items = [(ci, h) for ci in range(n_chunks) for h in range(heads)]
    e_all = [jnp.exp(_dot3(mat, g_ref[rows(ci), :])) for ci in range(n_chunks)]
    q = {(ci, h): q_ref[rows(ci), cols(h)] for ci, h in items}
    k = {(ci, h): k_ref[rows(ci), cols(h)] for ci, h in items}
    v = {(ci, h): v_ref[rows(ci), cols(h)].astype(bf16) for ci, h in items}
    s = {it: jnp.where(masks[nlev], _dot_nt(q[it].astype(bf16), k[it].astype(bf16)), 0.0) for it in items}
    for l in range(nlev):
        for ci, h in items:
            e = e_all[ci][l * c:(l + 1) * c, cols(h)]
            s[ci, h] = jnp.where(masks[l], _dot_nt((q[ci, h] * e).astype(bf16), (k[ci, h] * e).astype(bf16)), s[ci, h])
    res = {}
    for ci, h in items:
        o = _dot(s[ci, h].astype(bf16), v[ci, h])
        eb = e_all[ci][nlev * c:(nlev + 1) * c, cols(h)]
        er = e_all[ci][(nlev + 1) * c:, cols(h)]
        res[ci, h] = (o, q[ci, h] * eb, k[ci, h] * er, eb, v[ci, h])
    return res


def _hgrn_out(o, sg, nw):
    ms = jnp.mean(o * o, axis=-1, keepdims=True)
    return o * lax.rsqrt(ms + EPS) * nw * sg


def _hgrn_prompt_kernel(q_ref, g_ref, k_ref, v_ref, sg_ref, nw_ref, mat_ref, mask_ref, o_ref, s_ref, st_ref,
                        *, c, nlev, n_chunks):
    t_blk = pl.program_id(2)

    @pl.when(t_blk == 0)
    def _():
        st_ref[...] = jnp.zeros_like(st_ref)

    masks = [mask_ref[l] > 0.5 for l in range(nlev + 1)]
    mat = mat_ref[...]
    nw = nw_ref[...]

    res = _hgrn_chunks(q_ref, k_ref, g_ref, v_ref, mat, masks, nlev, c)
    upd = {it: _dot_tn(r[4], r[2].astype(bf16)) for it, r in res.items()}
    for h in range(st_ref.shape[0]):
        cs = slice(h * LANES, (h + 1) * LANES)
        st = st_ref[h]
        for ci in range(n_chunks):
            rows = slice(ci * c, (ci + 1) * c)
            o, qe, _, eb, _ = res[ci, h]
            o = o + _dot_nt(qe.astype(bf16), st.astype(bf16))
            st = st * eb[c - 1:c, :] + upd[ci, h]
            o_ref[rows, cs] = _hgrn_out(o, sg_ref[rows, cs], nw[:, cs]).astype(o_ref.dtype)
        st_ref[h] = st

    @pl.when(t_blk == pl.num_programs(2) - 1)
    def _():
        for h in range(st_ref.shape[0]):
            s_ref[0, h] = st_ref[h].T


def _hgrn_sample_kernel(q_ref, g_ref, k_ref, v_ref, sg_ref, nw_ref, mat_ref, mask_ref, s_in_ref,
                        o_ref, s_out_ref, *, c, nlev, seg):
    masks = [mask_ref[l] > 0.5 for l in range(nlev + 1)]
    mat = mat_ref[...]
    nw = nw_ref[...]
    small = seg < 16
    per = c // seg
    res = _hgrn_chunks(q_ref, k_ref, g_ref, v_ref, mat, masks, nlev, c)
    for (ci, h), (o, qe, kd, eb, v) in res.items():
        rows = slice(ci * c, (ci + 1) * c)
        cs = slice(h * LANES, (h + 1) * LANES)
        v = v.astype(f32)
        eb_t = jnp.concatenate([eb, eb], axis=0).T
        parts = []
        for j in range(per):
            sl = slice(j * seg, (j + 1) * seg)
            s0 = s_in_ref[ci * per + j, h]
            parts.append(_dot(_mxu_operand(qe[sl], small), _mxu_operand(s0, small)))
            last = j * seg + seg - 1
            upd = _dot_tn(_mxu_operand(kd[sl], small), _mxu_operand(v[sl], small))
            s_out_ref[ci * per + j, h] = s0 * eb_t[:, last:last + 1] + upd
        o = o + jnp.concatenate(parts, axis=0)
        o_ref[rows, cs] = _hgrn_out(o, sg_ref[rows, cs], nw[:, cs]).astype(o_ref.dtype)


def _hgrn(q, g, k, v, sg, nw, state, n_prompt, batch, seq, dec_seq):
    m, width = q.shape
    heads = width // LANES
    nw = nw.reshape(1, width)
    c = CHUNK
    hp = 4
    hw = hp * LANES
    tc = 1024
    nt = seq // tc
    mat, masks, nlev = _hgrn_consts(c, c)
    row_spec = pl.BlockSpec((tc, hw), lambda b, h, t: (b * nt + t, h))
    const2 = lambda shape: pl.BlockSpec(shape, lambda b, h, t: (0,) * len(shape))
    o_p, s_p = pl.pallas_call(
        functools.partial(_hgrn_prompt_kernel, c=c, nlev=nlev, n_chunks=tc // c),
        grid=(batch, heads // hp, nt),
        in_specs=[row_spec] * 5 + [pl.BlockSpec((1, hw), lambda b, h, t: (0, h)), const2((mat.shape[0], HG_SPLIT * c)), const2(masks.shape)],
        out_specs=[row_spec, pl.BlockSpec((1, hp, HG_DK, LANES), lambda b, h, t: (b, h, 0, 0))],
        out_shape=[jax.ShapeDtypeStruct((n_prompt, width), bf16),
                   jax.ShapeDtypeStruct((batch, heads, HG_DK, LANES), f32)],
        scratch_shapes=[pltpu.VMEM((hp, LANES, HG_DK), f32)],
        compiler_params=_cparams(("arbitrary", "arbitrary", "arbitrary")),
        name="hgrn_prompt",
    )(q, g, k, v, sg, nw, _tile3(mat, HG_SPLIT), jnp.asarray(masks))
    n_sample = m - n_prompt
    rows = 2 * c
    per = rows // dec_seq
    mat, masks, nlev = _hgrn_consts(c, dec_seq)
    base = n_prompt // rows
    row_spec = pl.BlockSpec((rows, hw), lambda jb, h: (base + jb, h))
    st_spec = pl.BlockSpec((per, hp, HG_DK, LANES), lambda jb, h: (jb, h, 0, 0))
    const2 = lambda shape: pl.BlockSpec(shape, lambda jb, h: (0,) * len(shape))
    o_s, s_s = pl.pallas_call(
        functools.partial(_hgrn_sample_kernel, c=c, nlev=nlev, seg=dec_seq),
        grid=(n_sample // rows, heads // hp),
        in_specs=[row_spec] * 5 + [pl.BlockSpec((1, hw), lambda jb, h: (0, h)), const2((mat.shape[0], HG_SPLIT * c)),
                                   const2(masks.shape), st_spec],
        out_specs=[pl.BlockSpec((rows, hw), lambda jb, h: (jb, h)), st_spec],
        out_shape=[jax.ShapeDtypeStruct((n_sample, width), bf16), jax.ShapeDtypeStruct(state.shape, f32)],
        compiler_params=_cparams(("arbitrary", "arbitrary")),
        name="hgrn_sample",
    )(q, g, k, v, sg, nw, _tile3(mat, HG_SPLIT), jnp.asarray(masks), state)
    return (o_p, o_s), s_p, s_s


GROUP_W = 512
PAIRS = GROUP_W // LANES
XBC_W = GROUP_W + 2 * M_DSTATE


def _ssd_consts(c, seg):
    t = np.arange(c)
    same = (t[:, None] // seg) == (t[None, :] // seg)
    tril = (same & (t[None, :] <= t[:, None])).astype(np.float32)
    return tril


def _ssd_chunks(chunks, a_row, dsk, nw, lc, tril, get_state, set_state, c, seg, state_t, after_first_stage=None):
    nseg = c // seg
    small = seg < 16
    assert 2 * c == LANES and 2 * M_HEADDIM == LANES and not (state_t and nseg > 1)
    lo_half = lax.broadcasted_iota(jnp.int32, (c, LANES), 1) < M_HEADDIM
    lo_row = lo_half[0:1, :]
    n = len(chunks)
    items = [(ci, p) for ci in range(n) for p in range(PAIRS)]
    pc = lambda p: slice(p * LANES, (p + 1) * LANES)
    acum = [_dot3(lc, dt * a_row) for _, _, _, dt, _ in chunks]
    bmb = [bm.astype(bf16) for _, bm, _, _, _ in chunks]
    cmb = [cm.astype(bf16) for _, _, cm, _, _ in chunks]
    cb2 = [_dot_nt(cmb[ci], jnp.concatenate([bmb[ci], bmb[ci]], axis=0)) for ci in range(n)]
    acum_t = [jnp.concatenate([a, a], axis=0).T for a in acum]
    dt_t = [jnp.concatenate([ch[3], ch[3]], axis=0).T for ch in chunks]
    if state_t:
        bm_t = [ch[1].T.astype(bf16) for ch in chunks]
    else:
        ea_t = [jnp.exp(a[0:8, 0:c]) for a in acum_t]
    if after_first_stage is not None:
        after_first_stage()
    y, xw, ea = {}, {}, {}
    for ci, p in items:
        h0, h1 = 2 * p, 2 * p + 1
        xp, dt = chunks[ci][0][:, pc(p)], chunks[ci][3]
        acp = jnp.where(lo_half, acum[ci][:, h0:h0 + 1], acum[ci][:, h1:h1 + 1])
        dtp = jnp.where(lo_half, dt[:, h0:h0 + 1], dt[:, h1:h1 + 1])
        a_src = jnp.where(lo_row, acum_t[ci][h0:h0 + 1, :], acum_t[ci][h1:h1 + 1, :])
        dt_src = jnp.where(lo_row, dt_t[ci][h0:h0 + 1, :], dt_t[ci][h1:h1 + 1, :])
        lm = jnp.exp(jnp.where(tril, acp - a_src, -1e30))
        sc = (cb2[ci] * lm * dt_src).astype(bf16)
        x_blk = jnp.concatenate([jnp.where(lo_half, xp, 0.0), jnp.where(lo_half, 0.0, xp)], axis=0).astype(bf16)
        y[ci, p] = _dot(sc, x_blk)
        if nseg == 1:
            alast = acp[c - 1:c, :]
        else:
            alast = jnp.concatenate(
                [jnp.broadcast_to(acp[j * seg + seg - 1:j * seg + seg, :], (seg, LANES)) for j in range(nseg)], axis=0)
        xw[ci, p] = xp * (jnp.exp(alast - acp) * dtp)
        ea[ci, p] = jnp.exp(acp)
    cs = {}
    if state_t:
        upd = {(ci, p): _dot(bm_t[ci], xw[ci, p].astype(bf16)) for ci, p in items}
        for p in range(PAIRS):
            st = get_state(0, 0, p)
            for ci in range(n):
                cs[ci, p] = _dot(cmb[ci], st.astype(bf16))
                st = st * ea[ci, p][c - 1:c, :] + upd[ci, p]
            set_state(0, 0, p, st)
    else:
        for ci, p in items:
            h0, h1 = 2 * p, 2 * p + 1
            bm, cm = chunks[ci][1], chunks[ci][2]
            parts = []
            for j in range(nseg):
                sl = slice(j * seg, (j + 1) * seg)
                s0 = get_state(ci, j, p)
                parts.append(_dot_nt(_mxu_operand(cm[sl], small), _mxu_operand(s0, small)))
                upd = _dot_tn(_mxu_operand(xw[ci, p][sl], small), _mxu_operand(bm[sl], small))
                last = j * seg + seg - 1
                decay = jnp.concatenate(
                    [jnp.broadcast_to(ea_t[ci][h0:h0 + 1, last:last + 1], (M_HEADDIM, LANES)),
                     jnp.broadcast_to(ea_t[ci][h1:h1 + 1, last:last + 1], (M_HEADDIM, LANES))], axis=0)
                set_state(ci, j, p, s0 * decay + upd)
            cs[ci, p] = parts[0] if nseg == 1 else jnp.concatenate(parts, axis=0)
    outs = []
    for ci in range(n):
        xs, zs = chunks[ci][0], chunks[ci][4]
        ssq = jnp.zeros((c, 1), f32)
        ys = []
        for p in range(PAIRS):
            yp = (y[ci, p] + cs[ci, p] * ea[ci, p] + dsk[:, pc(p)] * xs[:, pc(p)]) * zs[:, pc(p)]
            ssq = ssq + jnp.sum(yp * yp, axis=-1, keepdims=True)
            ys.append(yp)
        scale = lax.rsqrt(ssq * (1.0 / GROUP_W) + EPS)
        outs.append([ys[p] * scale * nw[:, pc(p)] for p in range(PAIRS)])
    return outs


def _conv_taps(xpad_ref, lead, rows, w, b):
    acc = None
    for j in range(M_CONV):
        term = xpad_ref[lead + (slice(5 + j + rows[0], 5 + j + rows[1]), slice(None))] * w[j:j + 1, :]
        acc = term if acc is None else acc + term
    return jax.nn.silu(b + acc)


def _ssd_prompt_kernel(*refs, c, rows, n_batch):
    h_ref, gate_ref, wgb_ref = refs[16], refs[20], refs[24]
    pl.when(pl.program_id(1) < n_batch)(lambda: _ssd_prompt_step(*refs, c=c, rows=rows))

    @pl.when(jnp.logical_and(pl.program_id(1) == n_batch, pl.program_id(2) == 0))
    def _():
        gate_ref[...] = _dot_nt(h_ref[...], wgb_ref[...])


def _ssd_prompt_step(xr_ref, br_ref, cr_ref, dt_ref, zs_ref, wx_ref, wb_ref, wc_ref, bx_ref, bb_ref, bc_ref,
                     alog_ref, dsk_ref, nw_ref, lc_ref, tril_ref, h_ref, wg_ref, o_ref, s_ref, gate_ref,
                     xpad_ref, xc_ref, st_ref, wgb_ref, *, c, rows):
    t_blk = pl.program_id(2)

    @pl.when(t_blk == 0)
    def _():
        st_ref[...] = jnp.zeros_like(st_ref)
        xpad_ref[0:8, :] = jnp.zeros((8, XBC_W), f32)

    @pl.when(jnp.logical_and(t_blk == 0, pl.program_id(1) == 0))
    def _():
        wgb_ref[...] = wg_ref[...].astype(bf16)

    @pl.when(t_blk > 0)
    def _():
        xpad_ref[0:8, :] = xpad_ref[rows:rows + 8, :]

    half = rows // 2

    def gate_half(i):
        rs = slice(i * half, (i + 1) * half)
        gate_ref[rs, :] = _dot_nt(h_ref[rs, :], wgb_ref[...])

    gate_half(0)
    xpad_ref[8:8 + rows, 0:GROUP_W] = xr_ref[...]
    xpad_ref[8:8 + rows, GROUP_W:GROUP_W + M_DSTATE] = br_ref[...]
    xpad_ref[8:8 + rows, GROUP_W + M_DSTATE:XBC_W] = cr_ref[...]
    w = jnp.concatenate([wx_ref[...], wb_ref[...], wc_ref[...]], axis=1)
    b = jnp.concatenate([bx_ref[...], bb_ref[...], bc_ref[...]], axis=1)
    for i in range(rows // c):
        xc_ref[i * c:(i + 1) * c, :] = _conv_taps(xpad_ref, (), (i * c, (i + 1) * c), w, b)

    a_row = -jnp.exp(alog_ref[...])
    dsk = dsk_ref[...]
    nw = nw_ref[...]
    lc = lc_ref[...]
    tril = tril_ref[...] > 0.5

    def get_state(ci, j, p):
        return st_ref[p]

    def set_state(ci, j, p, val):
        st_ref[p] = val

    rs = lambda ci: slice(ci * c, (ci + 1) * c)
    chunks = [(xc_ref[rs(ci), 0:GROUP_W], xc_ref[rs(ci), GROUP_W:GROUP_W + M_DSTATE], xc_ref[rs(ci), GROUP_W + M_DSTATE:XBC_W],
               dt_ref[rs(ci), :], zs_ref[rs(ci), :]) for ci in range(rows // c)]
    outs = _ssd_chunks(chunks, a_row, dsk, nw, lc, tril, get_state, set_state, c, c, state_t=True,
                       after_first_stage=lambda: gate_half(1))
    for ci, out in enumerate(outs):
        for p in range(PAIRS):
            o_ref[rs(ci), p * LANES:(p + 1) * LANES] = out[p].astype(o_ref.dtype)

    @pl.when(t_blk == pl.num_programs(2) - 1)
    def _():
        for p in range(PAIRS):
            s_ref[0, p * LANES:(p + 1) * LANES, :] = st_ref[p].T


def _ssd_sample_kernel(xr_ref, br_ref, cr_ref, dt_ref, zs_ref, wx_ref, wb_ref, wc_ref, bx_ref, bb_ref, bc_ref,
                       alog_ref, dsk_ref, nw_ref, lc_ref, tril_ref, hx_ref, hb_ref, hc_ref, s_in_ref,
                       o_ref, s_out_ref, xpad_ref, *, c, seg):
    nseg = c // seg
    n_chunks = xr_ref.shape[0] // c
    w = jnp.concatenate([wx_ref[...], wb_ref[...], wc_ref[...]], axis=1)
    b = jnp.concatenate([bx_ref[...], bb_ref[...], bc_ref[...]], axis=1)
    conv = []
    for j in range(n_chunks * nseg):
        sl = slice(j * seg, (j + 1) * seg)
        xpad_ref[j, 5:8, 0:GROUP_W] = hx_ref[j]
        xpad_ref[j, 5:8, GROUP_W:GROUP_W + M_DSTATE] = hb_ref[j]
        xpad_ref[j, 5:8, GROUP_W + M_DSTATE:XBC_W] = hc_ref[j]
        xpad_ref[j, 8:8 + seg, 0:GROUP_W] = xr_ref[sl, :]
        xpad_ref[j, 8:8 + seg, GROUP_W:GROUP_W + M_DSTATE] = br_ref[sl, :]
        xpad_ref[j, 8:8 + seg, GROUP_W + M_DSTATE:XBC_W] = cr_ref[sl, :]
        conv.append(_conv_taps(xpad_ref, (j,), (0, seg), w, b))

    def get_state(ci, j, p):
        return s_in_ref[ci * nseg + j, p * LANES:(p + 1) * LANES, :]

    def set_state(ci, j, p, val):
        s_out_ref[ci * nseg + j, p * LANES:(p + 1) * LANES, :] = val

    chunks = []
    for ci in range(n_chunks):
        xc = jnp.concatenate(conv[ci * nseg:(ci + 1) * nseg], axis=0)
        rs = slice(ci * c, (ci + 1) * c)
        chunks.append((xc[:, 0:GROUP_W], xc[:, GROUP_W:GROUP_W + M_DSTATE], xc[:, GROUP_W + M_DSTATE:XBC_W],
                       dt_ref[rs, :], zs_ref[rs, :]))
    outs = _ssd_chunks(chunks, -jnp.exp(alog_ref[...]), dsk_ref[...], nw_ref[...], lc_ref[...],
                       tril_ref[...] > 0.5, get_state, set_state, c, seg, state_t=False)
    for ci, out in enumerate(outs):
        for p in range(PAIRS):
            o_ref[ci * c:(ci + 1) * c, p * LANES:(p + 1) * LANES] = out[p].astype(o_ref.dtype)


def _ssd(xbc, dt, zs, conv_w, conv_b, alog_p, dsk, nw, state, hist, n_prompt, batch, seq, dec_seq, h, wt, o_gate, gate_cols):
    m = xbc.shape[0]
    inner = zs.shape[1]
    c = CHUNK
    xb_blk = inner // M_DSTATE
    cb_blk = xb_blk + M_GROUPS

    def specs(row_map, nrow, gpos):
        def rm(fn):
            return lambda *ix: fn(row_map(*ix), ix[gpos])
        zero = lambda fn: (lambda *ix: fn(0, ix[gpos]))
        return [
            pl.BlockSpec((nrow, GROUP_W), rm(lambda r, g: (r, g))),
            pl.BlockSpec((nrow, M_DSTATE), rm(lambda r, g: (r, xb_blk + g))),
            pl.BlockSpec((nrow, M_DSTATE), rm(lambda r, g: (r, cb_blk + g))),
            pl.BlockSpec((nrow, LANES), rm(lambda r, g: (r, g))),
            pl.BlockSpec((nrow, GROUP_W), rm(lambda r, g: (r, g))),
            pl.BlockSpec((M_CONV, GROUP_W), zero(lambda r, g: (0, g))),
            pl.BlockSpec((M_CONV, M_DSTATE), zero(lambda r, g: (0, xb_blk + g))),
            pl.BlockSpec((M_CONV, M_DSTATE), zero(lambda r, g: (0, cb_blk + g))),
            pl.BlockSpec((1, GROUP_W), zero(lambda r, g: (0, g))),
            pl.BlockSpec((1, M_DSTATE), zero(lambda r, g: (0, xb_blk + g))),
            pl.BlockSpec((1, M_DSTATE), zero(lambda r, g: (0, cb_blk + g))),
            pl.BlockSpec((1, LANES), zero(lambda r, g: (0, g))),
            pl.BlockSpec((1, GROUP_W), zero(lambda r, g: (0, g))),
            pl.BlockSpec((1, GROUP_W), zero(lambda r, g: (0, g))),
            pl.BlockSpec((c, 3 * c), zero(lambda r, g: (0, 0))),
            pl.BlockSpec((c, 2 * c), zero(lambda r, g: (0, 0))),
        ]

    common = (xbc, xbc, xbc, dt, zs, conv_w, conv_w, conv_w, conv_b, conv_b, conv_b, alog_p, dsk, nw)
    rows = 1024
    nt = seq // rows
    tril = _ssd_consts(c, c)
    assert m - n_prompt == rows
    last = batch * nt - 1
    scan_row = lambda g, b, t: jnp.minimum(b * nt + t, last)
    gate_row = lambda g, b, t: jnp.minimum(b * nt + t, last + 1)
    in_specs = specs(scan_row, rows, 0)
    d_model = h.shape[1]
    gw = gate_cols // M_GROUPS
    in_specs += [
        pl.BlockSpec((rows, d_model), lambda g, b, t: (gate_row(g, b, t), 0)),
        pl.BlockSpec((pl.Element(gw), pl.Element(d_model)), lambda g, b, t: (pl.multiple_of(o_gate + g * gw, 8), 0),
                     pipeline_mode=pl.Buffered(1)),
    ]
    o_p, s_p, gates = pl.pallas_call(
        functools.partial(_ssd_prompt_kernel, c=c, rows=rows, n_batch=batch),
        grid=(M_GROUPS, batch + 1, nt),
        in_specs=in_specs,
        out_specs=[pl.BlockSpec((rows, GROUP_W), lambda g, b, t: (scan_row(g, b, t), g)),
                   pl.BlockSpec((1, GROUP_W, M_DSTATE), lambda g, b, t: (jnp.minimum(b, batch - 1), g, 0)),
                   pl.BlockSpec((rows, gw), lambda g, b, t: (gate_row(g, b, t), g))],
        out_shape=[jax.ShapeDtypeStruct((n_prompt, inner), bf16),
                   jax.ShapeDtypeStruct((batch, inner, M_DSTATE), f32),
                   jax.ShapeDtypeStruct((m, gate_cols), f32)],
        scratch_shapes=[pltpu.VMEM((rows + 8, XBC_W), f32), pltpu.VMEM((rows, XBC_W), f32),
                        pltpu.VMEM((PAIRS, LANES, M_DSTATE), f32), pltpu.VMEM((gw, d_model), bf16)],
        compiler_params=_cparams(("arbitrary", "arbitrary", "arbitrary")),
        name="ssd_prompt",
    )(*common, _tile3(tril), jnp.asarray(np.tile(tril, (1, 2))), h, wt)
    n_sample = m - n_prompt
    rows = 2 * c
    per = rows // dec_seq
    base = n_prompt // rows
    tril = _ssd_consts(c, dec_seq)
    in_specs = specs(lambda jb, g: base + jb, rows, 1)
    in_specs += [
        pl.BlockSpec((per, M_CONV - 1, GROUP_W), lambda jb, g: (jb, 0, g)),
        pl.BlockSpec((per, M_CONV - 1, M_DSTATE), lambda jb, g: (jb, 0, xb_blk + g)),
        pl.BlockSpec((per, M_CONV - 1, M_DSTATE), lambda jb, g: (jb, 0, cb_blk + g)),
        pl.BlockSpec((per, GROUP_W, M_DSTATE), lambda jb, g: (jb, g, 0)),
    ]
    o_s, s_s = pl.pallas_call(
        functools.partial(_ssd_sample_kernel, c=c, seg=dec_seq),
        grid=(n_sample // rows, M_GROUPS),
        in_specs=in_specs,
        out_specs=[pl.BlockSpec((rows, GROUP_W), lambda jb, g: (jb, g)),
                   pl.BlockSpec((per, GROUP_W, M_DSTATE), lambda jb, g: (jb, g, 0))],
        out_shape=[jax.ShapeDtypeStruct((n_sample, inner), bf16), jax.ShapeDtypeStruct(state.shape, f32)],
        scratch_shapes=[pltpu.VMEM((per, 16, XBC_W), f32)],
        compiler_params=_cparams(("arbitrary", "arbitrary")),
        name="ssd_sample",
    )(*common, _tile3(tril), jnp.asarray(np.tile(tril, (1, 2))), hist, hist, hist, state)
    return (o_p, o_s), s_p, s_s, gates


def kernel(x_prompt, x_sample, state_hgrn, state_ssm, state_conv, norm_mix, w_in, hg_lb_logits, hg_norm, conv_w, conv_b,
           dt_bias, a_log, d_skip, ssm_norm, w_branch_hg, w_branch_ssm, w_out, norm_ffn, w_ffn_gate, w_ffn_up,
           w_ffn_down, norm_final):
    batch, seq, d = x_prompt.shape
    dec_batch, dec_seq, _ = x_sample.shape
    n_prompt, n_sample = batch * seq, dec_batch * dec_seq
    hg_heads = state_hgrn.shape[2]
    kdim = hg_heads * HG_DK
    vdim = d
    inner = d
    m_heads = state_ssm.shape[2]
    conv_dim = conv_w.shape[2]
    hpg = m_heads // M_GROUPS

    xp2, xs2 = x_prompt.reshape(n_prompt, d), x_sample.reshape(n_sample, d)

    wt = jnp.swapaxes(w_in, 1, 2)[0]
    o_q, o_f, o_v, o_g, o_z, o_xbc = 0, kdim, 2 * kdim, 2 * kdim + vdim, 2 * kdim + 2 * vdim, 2 * kdim + 2 * vdim + inner
    o_dt = o_xbc + conv_dim
    o_gate = o_dt + m_heads
    pad_heads = lambda p: jnp.pad(p.reshape(M_GROUPS, hpg, -1), ((0, 0), (0, LANES - hpg), (0, 0))).reshape(M_GROUPS * LANES, -1)
    h, dt = _prenorm_dt(xp2, xs2, norm_mix[0], pad_heads(wt[o_dt:o_gate]), pad_heads(dt_bias[0]).T)
    scale = HG_DK ** -0.5
    ident = lambda a, c, t: (a[0],)
    silu = lambda a, c, t: (jax.nn.silu(a[0]),)
    wide = dict(tm=1024, tn=2048, w_single=True)
    (q,) = _mm([h], [(wt, o_q, True)], kdim, lambda a, c, t: (a[0] * scale,), [f32], name="proj_q", **wide)
    log_f, k = _mm([h], [(wt, o_f, True)], kdim, _epi_forget, [f32, f32], cols=[hg_lb_logits], name="proj_f",
                   tm=512, tn=2048, w_single=True)
    (v,) = _mm([h], [(wt, o_v, True)], vdim, ident, [bf16], name="proj_v", **wide)
    (sg,) = _mm([h], [(wt, o_g, True)], vdim, silu, [f32], name="proj_g", **wide)
    (zs,) = _mm([h], [(wt, o_z, True)], inner, silu, [f32], name="proj_z", **wide)
    (xbc,) = _mm([h], [(wt, o_xbc, True)], conv_dim, ident, [f32], tm=1024, tn=conv_dim // 2, w_single=True, name="proj_xbc")
    o_hg, shp, shs = _hgrn(q, log_f, k, v, sg, hg_norm[0], state_hgrn[0], n_prompt, batch, seq, dec_seq)
    dsk = jnp.repeat(d_skip[0], M_HEADDIM).reshape(1, inner)
    o_m, smp, sms, gates = _ssd(xbc, dt, zs, conv_w[0], conv_b[0].reshape(1, conv_dim), pad_heads(a_log[0]).T, dsk,
                                ssm_norm[0].reshape(1, inner), state_ssm[0].reshape(dec_batch, inner, M_DSTATE),
                                state_conv[0], n_prompt, batch, seq, dec_seq, h, wt, o_gate, 2 * d)

    tm, tn = 512, 1024
    (merged,) = _mm([o_hg, o_m], [(w_branch_hg[0], 0, False), (w_branch_ssm[0], 0, False)], d,
                    lambda a, c, t: (jax.nn.sigmoid(t[0]) * a[0] + jax.nn.sigmoid(t[1]) * a[1],), [bf16],
                    tiles=[(gates, lambda j, i: (i, j)), (gates, lambda j, i: (i, j + d // tn))],
                    tm=tm, tn=tn, w_single=True, name="merge")
    tm = 512
    n_p_tiles = n_prompt // tm

    def add_x_norm(a, c, t):
        x1 = jnp.where(pl.program_id(1) < n_p_tiles, t[0], t[1]) + a[0]
        ms = jnp.mean(x1 * x1, axis=-1, keepdims=True)
        return x1, x1 * lax.rsqrt(ms + EPS) * c[0]

    x1, h2 = _mm([merged], [(w_out[0], 0, False)], d, add_x_norm, [f32, bf16], cols=[norm_ffn],
                 tiles=[(xp2, lambda j, i: (jnp.minimum(i, n_p_tiles - 1), j)),
                        (xs2, lambda j, i: (jnp.maximum(i - n_p_tiles, 0), j), pl.Buffered(1))],
                 tm=tm, tn=d, w_single=True, name="out_proj")
    ffn = _ffn(h2, w_ffn_gate[0], w_ffn_up[0], w_ffn_down[0])
    y_p, y_s = _rmsnorm([x1], norm_final, f32, n_prompt, split_out=True, res=ffn)

    hist = M_CONV - 1
    new_conv_p = jnp.stack([xbc[(b + 1) * seq - hist:(b + 1) * seq] for b in range(batch)])[None]
    new_conv_s = xbc[n_prompt:].reshape(dec_batch, dec_seq, conv_dim)[:, dec_seq - hist:][None]
    return (y_p.reshape(batch, seq, d), y_s.reshape(dec_batch, dec_seq, d),
            shp[None], smp.reshape(1, batch, m_heads, M_HEADDIM, M_DSTATE), new_conv_p,
            shs[None], sms.reshape(1, dec_batch, m_heads, M_HEADDIM, M_DSTATE), new_conv_s)
```

```python
import functools

import numpy as np
import jax
import jax.numpy as jnp
from jax import lax
from jax.experimental import pallas as pl
from jax.experimental.pallas import tpu as pltpu

f32 = jnp.float32
bf16 = jnp.bfloat16

EPS = 1e-6
LANES = 128
CHUNK = 64
HG_DK = 128
M_HEADDIM = 64
M_DSTATE = 128
M_GROUPS = 4
M_CONV = 4
VMEM_LIMIT = 56 * 1024 * 1024


def _cparams(sem, vmem=VMEM_LIMIT):
    return pltpu.CompilerParams(dimension_semantics=sem, vmem_limit_bytes=vmem)


def _dot(a, b):
    return jnp.dot(a, b, preferred_element_type=f32)


def _dot_nt(a, b):
    return lax.dot_general(a, b, (((1,), (1,)), ((), ())), preferred_element_type=f32)


def _dot_tn(a, b):
    return lax.dot_general(a, b, (((0,), (0,)), ((), ())), preferred_element_type=f32)


def _split(x, n):
    pieces = []
    for _ in range(n - 1):
        p = x.astype(bf16)
        pieces.append(p)
        x = x - p.astype(f32)
    return pieces + [x.astype(bf16)]


def _tile3(m, n=3):
    return jnp.asarray(np.tile(m, (1, n)), bf16)


def _dot3(mn, x):
    return _dot(mn, jnp.concatenate(_split(x, mn.shape[1] // x.shape[0]), axis=0))


def _mxu_operand(x, small):
    xb = x.astype(bf16)
    return xb.astype(f32) if small else xb


def _rmsnorm_kernel(*refs, n_in, n_out, split, has_res):
    x_refs, w_ref, o_refs = refs[:n_in], refs[n_in], refs[n_in + 1 + has_res:]

    def run(x_ref, o_ref):
        x = x_ref[...]
        if has_res:
            x = x + refs[n_in + 1][...]
        ms = jnp.mean(x * x, axis=-1, keepdims=True)
        o_ref[...] = (x * lax.rsqrt(ms + EPS) * w_ref[...]).astype(o_ref.dtype)

    if n_in == 1 and n_out == 1:
        run(x_refs[0], o_refs[0])
    else:
        i = pl.program_id(0)
        pl.when(i < split)(lambda: run(x_refs[0], o_refs[0]))
        pl.when(i >= split)(lambda: run(x_refs[-1], o_refs[-1]))


def _rmsnorm(xs, w, out_dtype, n_first, split_out=False, res=None, tm=512):
    d = xs[0].shape[1]
    m = sum(x.shape[0] for x in xs)
    split = n_first // tm
    first = lambda i: (jnp.minimum(i, split - 1), 0)
    second = lambda i: (jnp.maximum(i - split, 0), 0)
    whole = lambda i: (i, 0)
    blk = lambda fn: pl.BlockSpec((tm, d), fn)
    in_specs = [blk(whole)] if len(xs) == 1 else [blk(first), blk(second)]
    if split_out:
        out_specs = [blk(first), blk(second)]
        out_shape = [jax.ShapeDtypeStruct((n_first, d), out_dtype), jax.ShapeDtypeStruct((m - n_first, d), out_dtype)]
    else:
        out_specs = [blk(whole)]
        out_shape = [jax.ShapeDtypeStruct((m, d), out_dtype)]
    extra = [] if res is None else [res]
    outs = pl.pallas_call(
        functools.partial(_rmsnorm_kernel, n_in=len(xs), n_out=len(out_shape), split=split, has_res=len(extra)),
        grid=(m // tm,),
        in_specs=in_specs + [pl.BlockSpec((1, d), lambda i: (0, 0))] + [blk(whole) for _ in extra],
        out_specs=out_specs,
        out_shape=out_shape,
        compiler_params=_cparams(("arbitrary",)),
        name="rmsnorm",
    )(*xs, w.reshape(1, d), *extra)
    return outs if split_out else outs[0]


def _prenorm_dt_kernel(xp_ref, xs_ref, w_ref, wdt_ref, bias_ref, h_ref, dt_ref, wdtb_ref, *, split):
    i = pl.program_id(0)

    @pl.when(i == 0)
    def _():
        wdtb_ref[...] = wdt_ref[...].astype(bf16)

    def run(x_ref):
        x = x_ref[...]
        ms = jnp.mean(x * x, axis=-1, keepdims=True)
        hb = (x * lax.rsqrt(ms + EPS) * w_ref[...]).astype(bf16)
        h_ref[...] = hb
        dt_ref[...] = _softplus(_dot_nt(hb, wdtb_ref[...]) + bias_ref[...])

    pl.when(i < split)(lambda: run(xp_ref))
    pl.when(i >= split)(lambda: run(xs_ref))


def _prenorm_dt(xp, xs, w, w_dt_t, dt_bias, tm=512):
    d = xp.shape[1]
    m = xp.shape[0] + xs.shape[0]
    n_dt = w_dt_t.shape[0]
    split = xp.shape[0] // tm
    return pl.pallas_call(
        functools.partial(_prenorm_dt_kernel, split=split),
        grid=(m // tm,),
        in_specs=[pl.BlockSpec((tm, d), lambda i: (jnp.minimum(i, split - 1), 0)),
                  pl.BlockSpec((tm, d), lambda i: (jnp.maximum(i - split, 0), 0)),
                  pl.BlockSpec((1, d), lambda i: (0, 0)),
                  pl.BlockSpec((n_dt, d), lambda i: (0, 0)),
                  pl.BlockSpec((1, n_dt), lambda i: (0, 0))],
        out_specs=[pl.BlockSpec((tm, d), lambda i: (i, 0)), pl.BlockSpec((tm, n_dt), lambda i: (i, 0))],
        out_shape=[jax.ShapeDtypeStruct((m, d), bf16), jax.ShapeDtypeStruct((m, n_dt), f32)],
        scratch_shapes=[pltpu.VMEM((n_dt, d), bf16)],
        compiler_params=_cparams(("arbitrary",)),
        name="prenorm_dt",
    )(xp, xs, w.reshape(1, d), w_dt_t, dt_bias)


def _mm_kernel(*refs, a_parts, split, nd, nc, nt, no, epi, w_t):
    a_refs, refs = refs[:sum(a_parts)], refs[sum(a_parts):]
    w_refs = refs[:nd]
    c_refs = refs[nd:nd + nc]
    t_refs = refs[nd + nc:nd + nc + nt]
    o_refs = refs[nd + nc + nt:nd + nc + nt + no]
    wb_refs = refs[nd + nc + nt + no:]

    @pl.when(pl.program_id(1) == 0)
    def _():
        for w_ref, wb_ref in zip(w_refs, wb_refs):
            wb_ref[...] = w_ref[...].astype(bf16)

    a_vals = []
    for parts in a_parts:
        if parts == 2:
            a_vals.append(jnp.where(pl.program_id(1) < split, a_refs[0][...], a_refs[1][...]))
        else:
            a_vals.append(a_refs[0][...])
        a_refs = a_refs[parts:]
    na = len(a_vals)
    accs = [(_dot_nt if t else _dot)(a_vals[min(i, na - 1)], wb_ref[...])
            for i, (wb_ref, t) in enumerate(zip(wb_refs, w_t))]
    outs = epi(accs, [c[...] for c in c_refs], [t[...] for t in t_refs])
    for o_ref, o in zip(o_refs, outs):
        o_ref[...] = o.astype(o_ref.dtype)


def _col_map(j, i, off):
    return (0, j + off)


def _tile_map(j, i, off):
    return (i, j + off)


def _row_elem_map(j, i, off, tn):
    return (pl.multiple_of(off + j * tn, 8), 0)


def _mm(a_list, w_list, n_cols, epi, out_dtypes, cols=(), tiles=(), tm=1024, tn=512, w_single=False, name="mm"):
    na, nd, nc, nt, no = len(a_list), len(w_list), len(cols), len(tiles), len(out_dtypes)
    assert na in (1, nd)
    a_list = [a if isinstance(a, tuple) else (a,) for a in a_list]
    m = sum(p.shape[0] for p in a_list[0])
    split = a_list[0][0].shape[0] // tm
    in_specs, a_flat = [], []
    for parts in a_list:
        k = parts[0].shape[1]
        if len(parts) == 2:
            in_specs.append(pl.BlockSpec((tm, k), lambda j, i: (jnp.minimum(i, split - 1), 0)))
            in_specs.append(pl.BlockSpec((tm, k), lambda j, i: (jnp.maximum(i - split, 0), 0),
                                         pipeline_mode=pl.Buffered(1)))
        else:
            in_specs.append(pl.BlockSpec((tm, k), lambda j, i: (i, 0)))
        a_flat += list(parts)
    scratch = []
    mode = dict(pipeline_mode=pl.Buffered(1)) if w_single else {}
    for w, off, transposed in w_list:
        if transposed:
            k = w.shape[1]
            assert off % 8 == 0
            in_specs.append(pl.BlockSpec((pl.Element(tn), pl.Element(k)), functools.partial(_row_elem_map, off=off, tn=tn), **mode))
            scratch.append(pltpu.VMEM((tn, k), bf16))
        else:
            k = w.shape[0]
            in_specs.append(pl.BlockSpec((k, tn), functools.partial(_col_map, off=off // tn), **mode))
            scratch.append(pltpu.VMEM((k, tn), bf16))
    in_specs += [pl.BlockSpec((c.shape[0], tn), functools.partial(_col_map, off=0)) for c in cols]
    in_specs += [pl.BlockSpec((tm, tn), t[1], **(dict(pipeline_mode=t[2]) if len(t) > 2 else {})) for t in tiles]
    outs = pl.pallas_call(
        functools.partial(_mm_kernel, a_parts=tuple(len(p) for p in a_list), split=split, nd=nd, nc=nc, nt=nt, no=no,
                          epi=epi, w_t=tuple(t for _, _, t in w_list)),
        grid=(n_cols // tn, m // tm),
        in_specs=in_specs,
        out_specs=[pl.BlockSpec((tm, tn), functools.partial(_tile_map, off=0)) for _ in out_dtypes],
        out_shape=[jax.ShapeDtypeStruct((m, n_cols), dt) for dt in out_dtypes],
        scratch_shapes=scratch,
        compiler_params=_cparams(("arbitrary", "arbitrary")),
        name=name,
    )(*a_flat, *[w for w, _, _ in w_list], *cols, *[t[0] for t in tiles])
    return outs


def _ffn_kernel(h_ref, wg_ref, wu_ref, wd_ref, o_ref):
    @pl.when(pl.program_id(1) == 0)
    def _():
        o_ref[...] = jnp.zeros_like(o_ref)

    h = h_ref[...]
    gate = _dot(h, wg_ref[...].astype(bf16))
    up = _dot(h, wu_ref[...].astype(bf16))
    act = (jax.nn.silu(gate) * up).astype(bf16)
    o_ref[...] += _dot(act, wd_ref[...].astype(bf16))


def _ffn(h, w_gate, w_up, w_down, tm=1152, tf=512):
    m, d = h.shape
    hidden = w_gate.shape[1]
    return pl.pallas_call(
        _ffn_kernel,
        grid=(m // tm, hidden // tf),
        in_specs=[pl.BlockSpec((tm, d), lambda i, f: (i, 0), pipeline_mode=pl.Buffered(1)),
                  pl.BlockSpec((d, tf), lambda i, f: (0, f)),
                  pl.BlockSpec((d, tf), lambda i, f: (0, f)),
                  pl.BlockSpec((tf, d), lambda i, f: (f, 0))],
        out_specs=pl.BlockSpec((tm, d), lambda i, f: (i, 0)),
        out_shape=jax.ShapeDtypeStruct((m, d), f32),
        compiler_params=_cparams(("arbitrary", "arbitrary"), VMEM_LIMIT + 4 * 1024 * 1024),
        name="ffn",
    )(h, w_gate, w_up, w_down)


def _softplus(x):
    return jnp.maximum(x, 0.0) + jnp.log1p(jnp.exp(-jnp.abs(x)))


def _epi_forget(accs, cols, tiles):
    logits = cols[0]
    mx = jnp.max(logits, axis=0, keepdims=True)
    e = jnp.exp(logits - mx)
    lb = e[0:1, :] / jnp.sum(e, axis=0, keepdims=True)
    sig = jax.nn.sigmoid(accs[0])
    log_f = jnp.log(lb + (1.0 - lb) * sig)
    k = (1.0 - lb) * (1.0 - sig)
    return log_f, k


HG_SPLIT = 2


def _hgrn_consts(c, seg):
    nlev = int(np.log2(seg))
    mat = np.zeros(((nlev + 2) * c, c), np.float32)
    masks = np.zeros((nlev + 1, c, c), np.float32)
    for t in range(c):
        tl = t % seg
        base = t - tl
        for l in range(nlev):
            h = 1 << l
            pos = tl % (2 * h)
            ref = base + tl - pos + h - 1
            if pos >= h:
                mat[l * c + t, ref + 1:t + 1] = 1
            else:
                mat[l * c + t, t + 1:ref + 1] = 1
            for s in range(base, base + seg):
                sl = s % seg
                if sl // (2 * h) == tl // (2 * h) and pos >= h and sl % (2 * h) < h:
                    masks[l, t, s] = 1
        mat[nlev * c + t, base:t + 1] = 1
        mat[(nlev + 1) * c + t, t + 1:base + seg] = 1
        masks[nlev, t, t] = 1
    return mat, masks, nlev


def _hgrn_chunks(q_ref, k_ref, g_ref, v_ref, mat, masks, nlev, c):
    n_chunks = q_ref.shape[0] // c
    heads = q_ref.shape[1] // LANES
    rows = lambda ci: slice(ci * c, (ci + 1) * c)
    cols = lambda h: slice(h * LANES, (h + 1) * LANES)
    items = [(ci, h) for ci in range(n_chunks) for h in range(heads)]
    e_all = [jnp.exp(_dot3(mat, g_ref[rows(ci), :])) for ci in range(n_chunks)]
    q = {(ci, h): q_ref[rows(ci), cols(h)] for ci, h in items}
    k = {(ci, h): k_ref[rows(ci), cols(h)] for ci, h in items}
    v = {(ci, h): v_ref[rows(ci), cols(h)].astype(bf16) for ci, h in items}
    s = {it: jnp.where(masks[nlev], _dot_nt(q[it].astype(bf16), k[it].astype(bf16)), 0.0) for it in items}
    for l in range(nlev):
        for ci, h in items:
            e = e_all[ci][l * c:(l + 1) * c, cols(h)]
            s[ci, h] = jnp.where(masks[l], _dot_nt((q[ci, h] * e).astype(bf16), (k[ci, h] * e).astype(bf16)), s[ci, h])
    res = {}
    for ci, h in items:
        o = _dot(s[ci, h].astype(bf16), v[ci, h])
        eb = e_all[ci][nlev * c:(nlev + 1) * c, cols(h)]
        er = e_all[ci][(nlev + 1) * c:, cols(h)]
        res[ci, h] = (o, q[ci, h] * eb, k[ci, h] * er, eb, v[ci, h])
    return res


def _hgrn_out(o, sg, nw):
    ms = jnp.mean(o * o, axis=-1, keepdims=True)
    return o * lax.rsqrt(ms + EPS) * nw * sg


def _hgrn_prompt_kernel(q_ref, g_ref, k_ref, v_ref, sg_ref, nw_ref, mat_ref, mask_ref, o_ref, s_ref, st_ref,
                        *, c, nlev, n_chunks):
    t_blk = pl.program_id(2)

    @pl.when(t_blk == 0)
    def _():
        st_ref[...] = jnp.zeros_like(st_ref)

    masks = [mask_ref[l] > 0.5 for l in range(nlev + 1)]
    mat = mat_ref[...]
    nw = nw_ref[...]

    res = _hgrn_chunks(q_ref, k_ref, g_ref, v_ref, mat, masks, nlev, c)
    upd = {it: _dot_tn(r[4], r[2].astype(bf16)) for it, r in res.items()}
    for h in range(st_ref.shape[0]):
        cs = slice(h * LANES, (h + 1) * LANES)
        st = st_ref[h]
        for ci in range(n_chunks):
            rows = slice(ci * c, (ci + 1) * c)
            o, qe, _, eb, _ = res[ci, h]
            o = o + _dot_nt(qe.astype(bf16), st.astype(bf16))
            st = st * eb[c - 1:c, :] + upd[ci, h]
            o_ref[rows, cs] = _hgrn_out(o, sg_ref[rows, cs], nw[:, cs]).astype(o_ref.dtype)
        st_ref[h] = st

    @pl.when(t_blk == pl.num_programs(2) - 1)
    def _():
        for h in range(st_ref.shape[0]):
            s_ref[0, h] = st_ref[h].T


def _hgrn_sample_kernel(q_ref, g_ref, k_ref, v_ref, sg_ref, nw_ref, mat_ref, mask_ref, s_in_ref,
                        o_ref, s_out_ref, *, c, nlev, seg):
    masks = [mask_ref[l] > 0.5 for l in range(nlev + 1)]
    mat = mat_ref[...]
    nw = nw_ref[...]
    small = seg < 16
    per = c // seg
    res = _hgrn_chunks(q_ref, k_ref, g_ref, v_ref, mat, masks, nlev, c)
    for (ci, h), (o, qe, kd, eb, v) in res.items():
        rows = slice(ci * c, (ci + 1) * c)
        cs = slice(h * LANES, (h + 1) * LANES)
        v = v.astype(f32)
        eb_t = jnp.concatenate([eb, eb], axis=0).T
        parts = []
        for j in range(per):
            sl = slice(j * seg, (j + 1) * seg)
            s0 = s_in_ref[ci * per + j, h]
            parts.append(_dot(_mxu_operand(qe[sl], small), _mxu_operand(s0, small)))
            last = j * seg + seg - 1
            upd = _dot_tn(_mxu_operand(kd[sl], small), _mxu_operand(v[sl], small))
            s_out_ref[ci * per + j, h] = s0 * eb_t[:, last:last + 1] + upd
        o = o + jnp.concatenate(parts, axis=0)
        o_ref[rows, cs] = _hgrn_out(o, sg_ref[rows, cs], nw[:, cs]).astype(o_ref.dtype)


def _hgrn(q, g, k, v, sg, nw, state, n_prompt, batch, seq, dec_seq):
    m, width = q.shape
    heads = width // LANES
    nw = nw.reshape(1, width)
    c = CHUNK
    hp = 4
    hw = hp * LANES
    tc = 1024
    nt = seq // tc
    mat, masks, nlev = _hgrn_consts(c, c)
    row_spec = pl.BlockSpec((tc, hw), lambda b, h, t: (b * nt + t, h))
    const2 = lambda shape: pl.BlockSpec(shape, lambda b, h, t: (0,) * len(shape))
    o_p, s_p = pl.pallas_call(
        functools.partial(_hgrn_prompt_kernel, c=c, nlev=nlev, n_chunks=tc // c),
        grid=(batch, heads // hp, nt),
        in_specs=[row_spec] * 5 + [pl.BlockSpec((1, hw), lambda b, h, t: (0, h)), const2((mat.shape[0], HG_SPLIT * c)), const2(masks.shape)],
        out_specs=[row_spec, pl.BlockSpec((1, hp, HG_DK, LANES), lambda b, h, t: (b, h, 0, 0))],
        out_shape=[jax.ShapeDtypeStruct((n_prompt, width), bf16),
                   jax.ShapeDtypeStruct((batch, heads, HG_DK, LANES), f32)],
        scratch_shapes=[pltpu.VMEM((hp, LANES, HG_DK), f32)],
        compiler_params=_cparams(("arbitrary", "arbitrary", "arbitrary")),
        name="hgrn_prompt",
    )(q, g, k, v, sg, nw, _tile3(mat, HG_SPLIT), jnp.asarray(masks))
    n_sample = m - n_prompt
    rows = 2 * c
    per = rows // dec_seq
    mat, masks, nlev = _hgrn_consts(c, dec_seq)
    base = n_prompt // rows
    row_spec = pl.BlockSpec((rows, hw), lambda jb, h: (base + jb, h))
    st_spec = pl.BlockSpec((per, hp, HG_DK, LANES), lambda jb, h: (jb, h, 0, 0))
    const2 = lambda shape: pl.BlockSpec(shape, lambda jb, h: (0,) * len(shape))
    o_s, s_s = pl.pallas_call(
        functools.partial(_hgrn_sample_kernel, c=c, nlev=nlev, seg=dec_seq),
        grid=(n_sample // rows, heads // hp),
        in_specs=[row_spec] * 5 + [pl.BlockSpec((1, hw), lambda jb, h: (0, h)), const2((mat.shape[0], HG_SPLIT * c)),
                                   const2(masks.shape), st_spec],
        out_specs=[pl.BlockSpec((rows, hw), lambda jb, h: (jb, h)), st_spec],
        out_shape=[jax.ShapeDtypeStruct((n_sample, width), bf16), jax.ShapeDtypeStruct(state.shape, f32)],
        compiler_params=_cparams(("arbitrary", "arbitrary")),
        name="hgrn_sample",
    )(q, g, k, v, sg, nw, _tile3(mat, HG_SPLIT), jnp.asarray(masks), state)
    return (o_p, o_s), s_p, s_s


GROUP_W = 512
PAIRS = GROUP_W // LANES
XBC_W = GROUP_W + 2 * M_DSTATE


def _ssd_consts(c, seg):
    t = np.arange(c)
    same = (t[:, None] // seg) == (t[None, :] // seg)
    tril = (same & (t[None, :] <= t[:, None])).astype(np.float32)
    return tril


def _ssd_chunks(chunks, a_row, dsk, nw, lc, tril, get_state, set_state, c, seg, state_t, after_first_stage=None):
    nseg = c // seg
    small = seg < 16
    assert 2 * c == LANES and 2 * M_HEADDIM == LANES and not (state_t and nseg > 1)
    lo_half = lax.broadcasted_iota(jnp.int32, (c, LANES), 1) < M_HEADDIM
    lo_row = lo_half[0:1, :]
    n = len(chunks)
    items = [(ci, p) for ci in range(n) for p in range(PAIRS)]
    pc = lambda p: slice(p * LANES, (p + 1) * LANES)
    acum = [_dot3(lc, dt * a_row) for _, _, _, dt, _ in chunks]
    bmb = [bm.astype(bf16) for _, bm, _, _, _ in chunks]
    cmb = [cm.astype(bf16) for _, _, cm, _, _ in chunks]
    cb2 = [_dot_nt(cmb[ci], jnp.concatenate([bmb[ci], bmb[ci]], axis=0)) for ci in range(n)]
    acum_t = [jnp.concatenate([a, a], axis=0).T for a in acum]
    dt_t = [jnp.concatenate([ch[3], ch[3]], axis=0).T for ch in chunks]
    if state_t:
        bm_t = [ch[1].T.astype(bf16) for ch in chunks]
    else:
        ea_t = [jnp.exp(a[0:8, 0:c]) for a in acum_t]
    if after_first_stage is not None:
        after_first_stage()
    y, xw, ea = {}, {}, {}
    for ci, p in items:
        h0, h1 = 2 * p, 2 * p + 1
        xp, dt = chunks[ci][0][:, pc(p)], chunks[ci][3]
        acp = jnp.where(lo_half, acum[ci][:, h0:h0 + 1], acum[ci][:, h1:h1 + 1])
        dtp = jnp.where(lo_half, dt[:, h0:h0 + 1], dt[:, h1:h1 + 1])
        a_src = jnp.where(lo_row, acum_t[ci][h0:h0 + 1, :], acum_t[ci][h1:h1 + 1, :])
        dt_src = jnp.where(lo_row, dt_t[ci][h0:h0 + 1, :], dt_t[ci][h1:h1 + 1, :])
        lm = jnp.exp(jnp.where(tril, acp - a_src, -1e30))
        sc = (cb2[ci] * lm * dt_src).astype(bf16)
        x_blk = jnp.concatenate([jnp.where(lo_half, xp, 0.0), jnp.where(lo_half, 0.0, xp)], axis=0).astype(bf16)
        y[ci, p] = _dot(sc, x_blk)
        if nseg == 1:
            alast = acp[c - 1:c, :]
        else:
            alast = jnp.concatenate(
                [jnp.broadcast_to(acp[j * seg + seg - 1:j * seg + seg, :], (seg, LANES)) for j in range(nseg)], axis=0)
        xw[ci, p] = xp * (jnp.exp(alast - acp) * dtp)
        ea[ci, p] = jnp.exp(acp)
    cs = {}
    if state_t:
        upd = {(ci, p): _dot(bm_t[ci], xw[ci, p].astype(bf16)) for ci, p in items}
        for p in range(PAIRS):
            st = get_state(0, 0, p)
            for ci in range(n):
                cs[ci, p] = _dot(cmb[ci], st.astype(bf16))
                st = st * ea[ci, p][c - 1:c, :] + upd[ci, p]
            set_state(0, 0, p, st)
    else:
        for ci, p in items:
            h0, h1 = 2 * p, 2 * p + 1
            bm, cm = chunks[ci][1], chunks[ci][2]
            parts = []
            for j in range(nseg):
                sl = slice(j * seg, (j + 1) * seg)
                s0 = get_state(ci, j, p)
                parts.append(_dot_nt(_mxu_operand(cm[sl], small), _mxu_operand(s0, small)))
                upd = _dot_tn(_mxu_operand(xw[ci, p][sl], small), _mxu_operand(bm[sl], small))
                last = j * seg + seg - 1
                decay = jnp.concatenate(
                    [jnp.broadcast_to(ea_t[ci][h0:h0 + 1, last:last + 1], (M_HEADDIM, LANES)),
                     jnp.broadcast_to(ea_t[ci][h1:h1 + 1, last:last + 1], (M_HEADDIM, LANES))], axis=0)
                set_state(ci, j, p, s0 * decay + upd)
            cs[ci, p] = parts[0] if nseg == 1 else jnp.concatenate(parts, axis=0)
    outs = []
    for ci in range(n):
        xs, zs = chunks[ci][0], chunks[ci][4]
        ssq = jnp.zeros((c, 1), f32)
        ys = []
        for p in range(PAIRS):
            yp = (y[ci, p] + cs[ci, p] * ea[ci, p] + dsk[:, pc(p)] * xs[:, pc(p)]) * zs[:, pc(p)]
            ssq = ssq + jnp.sum(yp * yp, axis=-1, keepdims=True)
            ys.append(yp)
        scale = lax.rsqrt(ssq * (1.0 / GROUP_W) + EPS)
        outs.append([ys[p] * scale * nw[:, pc(p)] for p in range(PAIRS)])
    return outs


def _conv_taps(xpad_ref, lead, rows, w, b):
    acc = None
    for j in range(M_CONV):
        term = xpad_ref[lead + (slice(5 + j + rows[0], 5 + j + rows[1]), slice(None))] * w[j:j + 1, :]
        acc = term if acc is None else acc + term
    return jax.nn.silu(b + acc)


def _ssd_prompt_kernel(*refs, c, rows, n_batch):
    h_ref, gate_ref, wgb_ref = refs[16], refs[20], refs[24]
    pl.when(pl.program_id(1) < n_batch)(lambda: _ssd_prompt_step(*refs, c=c, rows=rows))

    @pl.when(jnp.logical_and(pl.program_id(1) == n_batch, pl.program_id(2) == 0))
    def _():
        gate_ref[...] = _dot_nt(h_ref[...], wgb_ref[...])


def _ssd_prompt_step(xr_ref, br_ref, cr_ref, dt_ref, zs_ref, wx_ref, wb_ref, wc_ref, bx_ref, bb_ref, bc_ref,
                     alog_ref, dsk_ref, nw_ref, lc_ref, tril_ref, h_ref, wg_ref, o_ref, s_ref, gate_ref,
                     xpad_ref, xc_ref, st_ref, wgb_ref, *, c, rows):
    t_blk = pl.program_id(2)

    @pl.when(t_blk == 0)
    def _():
        st_ref[...] = jnp.zeros_like(st_ref)
        xpad_ref[0:8, :] = jnp.zeros((8, XBC_W), f32)

    @pl.when(jnp.logical_and(t_blk == 0, pl.program_id(1) == 0))
    def _():
        wgb_ref[...] = wg_ref[...].astype(bf16)

    @pl.when(t_blk > 0)
    def _():
        xpad_ref[0:8, :] = xpad_ref[rows:rows + 8, :]

    half = rows // 2

    def gate_half(i):
        rs = slice(i * half, (i + 1) * half)
        gate_ref[rs, :] = _dot_nt(h_ref[rs, :], wgb_ref[...])

    gate_half(0)
    xpad_ref[8:8 + rows, 0:GROUP_W] = xr_ref[...]
    xpad_ref[8:8 + rows, GROUP_W:GROUP_W + M_DSTATE] = br_ref[...]
    xpad_ref[8:8 + rows, GROUP_W + M_DSTATE:XBC_W] = cr_ref[...]
    w = jnp.concatenate([wx_ref[...], wb_ref[...], wc_ref[...]], axis=1)
    b = jnp.concatenate([bx_ref[...], bb_ref[...], bc_ref[...]], axis=1)
    for i in range(rows // c):
        xc_ref[i * c:(i + 1) * c, :] = _conv_taps(xpad_ref, (), (i * c, (i + 1) * c), w, b)

    a_row = -jnp.exp(alog_ref[...])
    dsk = dsk_ref[...]
    nw = nw_ref[...]
    lc = lc_ref[...]
    tril = tril_ref[...] > 0.5

    def get_state(ci, j, p):
        return st_ref[p]

    def set_state(ci, j, p, val):
        st_ref[p] = val

    rs = lambda ci: slice(ci * c, (ci + 1) * c)
    chunks = [(xc_ref[rs(ci), 0:GROUP_W], xc_ref[rs(ci), GROUP_W:GROUP_W + M_DSTATE], xc_ref[rs(ci), GROUP_W + M_DSTATE:XBC_W],
               dt_ref[rs(ci), :], zs_ref[rs(ci), :]) for ci in range(rows // c)]
    outs = _ssd_chunks(chunks, a_row, dsk, nw, lc, tril, get_state, set_state, c, c, state_t=True,
                       after_first_stage=lambda: gate_half(1))
    for ci, out in enumerate(outs):
        for p in range(PAIRS):
            o_ref[rs(ci), p * LANES:(p + 1) * LANES] = out[p].astype(o_ref.dtype)

    @pl.when(t_blk == pl.num_programs(2) - 1)
    def _():
        for p in range(PAIRS):
            s_ref[0, p * LANES:(p + 1) * LANES, :] = st_ref[p].T


def _ssd_sample_kernel(xr_ref, br_ref, cr_ref, dt_ref, zs_ref, wx_ref, wb_ref, wc_ref, bx_ref, bb_ref, bc_ref,
                       alog_ref, dsk_ref, nw_ref, lc_ref, tril_ref, hx_ref, hb_ref, hc_ref, s_in_ref,
                       o_ref, s_out_ref, xpad_ref, *, c, seg):
    nseg = c // seg
    n_chunks = xr_ref.shape[0] // c
    w = jnp.concatenate([wx_ref[...], wb_ref[...], wc_ref[...]], axis=1)
    b = jnp.concatenate([bx_ref[...], bb_ref[...], bc_ref[...]], axis=1)
    conv = []
    for j in range(n_chunks * nseg):
        sl = slice(j * seg, (j + 1) * seg)
        xpad_ref[j, 5:8, 0:GROUP_W] = hx_ref[j]
        xpad_ref[j, 5:8, GROUP_W:GROUP_W + M_DSTATE] = hb_ref[j]
        xpad_ref[j, 5:8, GROUP_W + M_DSTATE:XBC_W] = hc_ref[j]
        xpad_ref[j, 8:8 + seg, 0:GROUP_W] = xr_ref[sl, :]
        xpad_ref[j, 8:8 + seg, GROUP_W:GROUP_W + M_DSTATE] = br_ref[sl, :]
        xpad_ref[j, 8:8 + seg, GROUP_W + M_DSTATE:XBC_W] = cr_ref[sl, :]
        conv.append(_conv_taps(xpad_ref, (j,), (0, seg), w, b))

    def get_state(ci, j, p):
        return s_in_ref[ci * nseg + j, p * LANES:(p + 1) * LANES, :]

    def set_state(ci, j, p, val):
        s_out_ref[ci * nseg + j, p * LANES:(p + 1) * LANES, :] = val

    chunks = []
    for ci in range(n_chunks):
        xc = jnp.concatenate(conv[ci * nseg:(ci + 1) * nseg], axis=0)
        rs = slice(ci * c, (ci + 1) * c)
        chunks.append((xc[:, 0:GROUP_W], xc[:, GROUP_W:GROUP_W + M_DSTATE], xc[:, GROUP_W + M_DSTATE:XBC_W],
                       dt_ref[rs, :], zs_ref[rs, :]))
    outs = _ssd_chunks(chunks, -jnp.exp(alog_ref[...]), dsk_ref[...], nw_ref[...], lc_ref[...],
                       tril_ref[...] > 0.5, get_state, set_state, c, seg, state_t=False)
    for ci, out in enumerate(outs):
        for p in range(PAIRS):
            o_ref[ci * c:(ci + 1) * c, p * LANES:(p + 1) * LANES] = out[p].astype(o_ref.dtype)


def _ssd(xbc, dt, zs, conv_w, conv_b, alog_p, dsk, nw, state, hist, n_prompt, batch, seq, dec_seq, h, wt, o_gate, gate_cols):
    m = xbc.shape[0]
    inner = zs.shape[1]
    c = CHUNK
    xb_blk = inner // M_DSTATE
    cb_blk = xb_blk + M_GROUPS

    def specs(row_map, nrow, gpos):
        def rm(fn):
            return lambda *ix: fn(row_map(*ix), ix[gpos])
        zero = lambda fn: (lambda *ix: fn(0, ix[gpos]))
        return [
            pl.BlockSpec((nrow, GROUP_W), rm(lambda r, g: (r, g))),
            pl.BlockSpec((nrow, M_DSTATE), rm(lambda r, g: (r, xb_blk + g))),
            pl.BlockSpec((nrow, M_DSTATE), rm(lambda r, g: (r, cb_blk + g))),
            pl.BlockSpec((nrow, LANES), rm(lambda r, g: (r, g))),
            pl.BlockSpec((nrow, GROUP_W), rm(lambda r, g: (r, g))),
            pl.BlockSpec((M_CONV, GROUP_W), zero(lambda r, g: (0, g))),
            pl.BlockSpec((M_CONV, M_DSTATE), zero(lambda r, g: (0, xb_blk + g))),
            pl.BlockSpec((M_CONV, M_DSTATE), zero(lambda r, g: (0, cb_blk + g))),
            pl.BlockSpec((1, GROUP_W), zero(lambda r, g: (0, g))),
            pl.BlockSpec((1, M_DSTATE), zero(lambda r, g: (0, xb_blk + g))),
            pl.BlockSpec((1, M_DSTATE), zero(lambda r, g: (0, cb_blk + g))),
            pl.BlockSpec((1, LANES), zero(lambda r, g: (0, g))),
            pl.BlockSpec((1, GROUP_W), zero(lambda r, g: (0, g))),
            pl.BlockSpec((1, GROUP_W), zero(lambda r, g: (0, g))),
            pl.BlockSpec((c, 3 * c), zero(lambda r, g: (0, 0))),
            pl.BlockSpec((c, 2 * c), zero(lambda r, g: (0, 0))),
        ]

    common = (xbc, xbc, xbc, dt, zs, conv_w, conv_w, conv_w, conv_b, conv_b, conv_b, alog_p, dsk, nw)
    rows = 1024
    nt = seq // rows
    tril = _ssd_consts(c, c)
    assert m - n_prompt == rows
    last = batch * nt - 1
    scan_row = lambda g, b, t: jnp.minimum(b * nt + t, last)
    gate_row = lambda g, b, t: jnp.minimum(b * nt + t, last + 1)
    in_specs = specs(scan_row, rows, 0)
    d_model = h.shape[1]
    gw = gate_cols // M_GROUPS
    in_specs += [
        pl.BlockSpec((rows, d_model), lambda g, b, t: (gate_row(g, b, t), 0)),
        pl.BlockSpec((pl.Element(gw), pl.Element(d_model)), lambda g, b, t: (pl.multiple_of(o_gate + g * gw, 8), 0),
                     pipeline_mode=pl.Buffered(1)),
    ]
    o_p, s_p, gates = pl.pallas_call(
        functools.partial(_ssd_prompt_kernel, c=c, rows=rows, n_batch=batch),
        grid=(M_GROUPS, batch + 1, nt),
        in_specs=in_specs,
        out_specs=[pl.BlockSpec((rows, GROUP_W), lambda g, b, t: (scan_row(g, b, t), g)),
                   pl.BlockSpec((1, GROUP_W, M_DSTATE), lambda g, b, t: (jnp.minimum(b, batch - 1), g, 0)),
                   pl.BlockSpec((rows, gw), lambda g, b, t: (gate_row(g, b, t), g))],
        out_shape=[jax.ShapeDtypeStruct((n_prompt, inner), bf16),
                   jax.ShapeDtypeStruct((batch, inner, M_DSTATE), f32),
                   jax.ShapeDtypeStruct((m, gate_cols), f32)],
        scratch_shapes=[pltpu.VMEM((rows + 8, XBC_W), f32), pltpu.VMEM((rows, XBC_W), f32),
                        pltpu.VMEM((PAIRS, LANES, M_DSTATE), f32), pltpu.VMEM((gw, d_model), bf16)],
        compiler_params=_cparams(("arbitrary", "arbitrary", "arbitrary")),
        name="ssd_prompt",
    )(*common, _tile3(tril), jnp.asarray(np.tile(tril, (1, 2))), h, wt)
    n_sample = m - n_prompt
    rows = 2 * c
    per = rows // dec_seq
    base = n_prompt // rows
    tril = _ssd_consts(c, dec_seq)
    in_specs = specs(lambda jb, g: base + jb, rows, 1)
    in_specs += [
        pl.BlockSpec((per, M_CONV - 1, GROUP_W), lambda jb, g: (jb, 0, g)),
        pl.BlockSpec((per, M_CONV - 1, M_DSTATE), lambda jb, g: (jb, 0, xb_blk + g)),
        pl.BlockSpec((per, M_CONV - 1, M_DSTATE), lambda jb, g: (jb, 0, cb_blk + g)),
        pl.BlockSpec((per, GROUP_W, M_DSTATE), lambda jb, g: (jb, g, 0)),
    ]
    o_s, s_s = pl.pallas_call(
        functools.partial(_ssd_sample_kernel, c=c, seg=dec_seq),
        grid=(n_sample // rows, M_GROUPS),
        in_specs=in_specs,
        out_specs=[pl.BlockSpec((rows, GROUP_W), lambda jb, g: (jb, g)),
                   pl.BlockSpec((per, GROUP_W, M_DSTATE), lambda jb, g: (jb, g, 0))],
        out_shape=[jax.ShapeDtypeStruct((n_sample, inner), bf16), jax.ShapeDtypeStruct(state.shape, f32)],
        scratch_shapes=[pltpu.VMEM((per, 16, XBC_W), f32)],
        compiler_params=_cparams(("arbitrary", "arbitrary")),
        name="ssd_sample",
    )(*common, _tile3(tril), jnp.asarray(np.tile(tril, (1, 2))), hist, hist, hist, state)
    return (o_p, o_s), s_p, s_s, gates


def kernel(x_prompt, x_sample, state_hgrn, state_ssm, state_conv, norm_mix, w_in, hg_lb_logits, hg_norm, conv_w, conv_b,
           dt_bias, a_log, d_skip, ssm_norm, w_branch_hg, w_branch_ssm, w_out, norm_ffn, w_ffn_gate, w_ffn_up,
           w_ffn_down, norm_final):
    batch, seq, d = x_prompt.shape
    dec_batch, dec_seq, _ = x_sample.shape
    n_prompt, n_sample = batch * seq, dec_batch * dec_seq
    hg_heads = state_hgrn.shape[2]
    kdim = hg_heads * HG_DK
    vdim = d
    inner = d
    m_heads = state_ssm.shape[2]
    conv_dim = conv_w.shape[2]
    hpg = m_heads // M_GROUPS

    xp2, xs2 = x_prompt.reshape(n_prompt, d), x_sample.reshape(n_sample, d)

    wt = jnp.swapaxes(w_in, 1, 2)[0]
    o_q, o_f, o_v, o_g, o_z, o_xbc = 0, kdim, 2 * kdim, 2 * kdim + vdim, 2 * kdim + 2 * vdim, 2 * kdim + 2 * vdim + inner
    o_dt = o_xbc + conv_dim
    o_gate = o_dt + m_heads
    pad_heads = lambda p: jnp.pad(p.reshape(M_GROUPS, hpg, -1), ((0, 0), (0, LANES - hpg), (0, 0))).reshape(M_GROUPS * LANES, -1)
    h, dt = _prenorm_dt(xp2, xs2, norm_mix[0], pad_heads(wt[o_dt:o_gate]), pad_heads(dt_bias[0]).T)
    scale = HG_DK ** -0.5
    ident = lambda a, c, t: (a[0],)
    silu = lambda a, c, t: (jax.nn.silu(a[0]),)
    wide = dict(tm=1024, tn=2048, w_single=True)
    (q,) = _mm([h], [(wt, o_q, True)], kdim, lambda a, c, t: (a[0] * scale,), [f32], name="proj_q", **wide)
    log_f, k = _mm([h], [(wt, o_f, True)], kdim, _epi_forget, [f32, f32], cols=[hg_lb_logits], name="proj_f",
                   tm=512, tn=2048, w_single=True)
    (v,) = _mm([h], [(wt, o_v, True)], vdim, ident, [bf16], name="proj_v", **wide)
    (sg,) = _mm([h], [(wt, o_g, True)], vdim, silu, [f32], name="proj_g", **wide)
    (zs,) = _mm([h], [(wt, o_z, True)], inner, silu, [f32], name="proj_z", **wide)
    (xbc,) = _mm([h], [(wt, o_xbc, True)], conv_dim, ident, [f32], tm=1024, tn=conv_dim // 2, w_single=True, name="proj_xbc")
    o_hg, shp, shs = _hgrn(q, log_f, k, v, sg, hg_norm[0], state_hgrn[0], n_prompt, batch, seq, dec_seq)
    dsk = jnp.repeat(d_skip[0], M_HEADDIM).reshape(1, inner)
    o_m, smp, sms, gates = _ssd(xbc, dt, zs, conv_w[0], conv_b[0].reshape(1, conv_dim), pad_heads(a_log[0]).T, dsk,
                                ssm_norm[0].reshape(1, inner), state_ssm[0].reshape(dec_batch, inner, M_DSTATE),
                                state_conv[0], n_prompt, batch, seq, dec_seq, h, wt, o_gate, 2 * d)

    tm, tn = 512, 1024
    (merged,) = _mm([o_hg, o_m], [(w_branch_hg[0], 0, False), (w_branch_ssm[0], 0, False)], d,
                    lambda a, c, t: (jax.nn.sigmoid(t[0]) * a[0] + jax.nn.sigmoid(t[1]) * a[1],), [bf16],
                    tiles=[(gates, lambda j, i: (i, j)), (gates, lambda j, i: (i, j + d // tn))],
                    tm=tm, tn=tn, w_single=True, name="merge")
    tm = 512
    n_p_tiles = n_prompt // tm

    def add_x_norm(a, c, t):
        x1 = jnp.where(pl.program_id(1) < n_p_tiles, t[0], t[1]) + a[0]
        ms = jnp.mean(x1 * x1, axis=-1, keepdims=True)
        return x1, x1 * lax.rsqrt(ms + EPS) * c[0]

    x1, h2 = _mm([merged], [(w_out[0], 0, False)], d, add_x_norm, [f32, bf16], cols=[norm_ffn],
                 tiles=[(xp2, lambda j, i: (jnp.minimum(i, n_p_tiles - 1), j)),
                        (xs2, lambda j, i: (jnp.maximum(i - n_p_tiles, 0), j), pl.Buffered(1))],
                 tm=tm, tn=d, w_single=True, name="out_proj")
    ffn = _ffn(h2, w_ffn_gate[0], w_ffn_up[0], w_ffn_down[0])
    y_p, y_s = _rmsnorm([x1], norm_final, f32, n_prompt, split_out=True, res=ffn)

    hist = M_CONV - 1
    new_conv_p = jnp.stack([xbc[(b + 1) * seq - hist:(b + 1) * seq] for b in range(batch)])[None]
    new_conv_s = xbc[n_prompt:].reshape(dec_batch, dec_seq, conv_dim)[:, dec_seq - hist:][None]
    return (y_p.reshape(batch, seq, d), y_s.reshape(dec_batch, dec_seq, d),
            shp[None], smp.reshape(1, batch, m_heads, M_HEADDIM, M_DSTATE), new_conv_p,
            shs[None], sms.reshape(1, dec_batch, m_heads, M_HEADDIM, M_DSTATE), new_conv_s)
```

```python
import functools

import numpy as np
import jax
import jax.numpy as jnp
from jax import lax
from jax.experimental import pallas as pl
from jax.experimental.pallas import tpu as pltpu

f32 = jnp.float32
bf16 = jnp.bfloat16

EPS = 1e-6
LANES = 128
CHUNK = 64
HG_DK = 128
M_HEADDIM = 64
M_DSTATE = 128
M_GROUPS = 4
M_CONV = 4
VMEM_LIMIT = 56 * 1024 * 1024


def _cparams(sem):
    return pltpu.CompilerParams(dimension_semantics=sem, vmem_limit_bytes=VMEM_LIMIT)


def _dot(a, b):
    return jnp.dot(a, b, preferred_element_type=f32)


def _dot_nt(a, b):
    return lax.dot_general(a, b, (((1,), (1,)), ((), ())), preferred_element_type=f32)


def _dot_tn(a, b):
    return lax.dot_general(a, b, (((0,), (0,)), ((), ())), preferred_element_type=f32)


def _split(x, n):
    pieces = []
    for _ in range(n - 1):
        p = x.astype(bf16)
        pieces.append(p)
        x = x - p.astype(f32)
    return pieces + [x.astype(bf16)]


def _tile3(m, n=3):
    return jnp.asarray(np.tile(m, (1, n)), bf16)


def _dot3(mn, x):
    return _dot(mn, jnp.concatenate(_split(x, mn.shape[1] // x.shape[0]), axis=0))


def _mxu_operand(x, small):
    xb = x.astype(bf16)
    return xb.astype(f32) if small else xb


def _rmsnorm_kernel(*refs, n_in, n_out, split, has_res):
    x_refs, w_ref, o_refs = refs[:n_in], refs[n_in], refs[n_in + 1 + has_res:]

    def run(x_ref, o_ref):
        x = x_ref[...]
        if has_res:
            x = x + refs[n_in + 1][...]
        ms = jnp.mean(x * x, axis=-1, keepdims=True)
        o_ref[...] = (x * lax.rsqrt(ms + EPS) * w_ref[...]).astype(o_ref.dtype)

    if n_in == 1 and n_out == 1:
        run(x_refs[0], o_refs[0])
    else:
        i = pl.program_id(0)
        pl.when(i < split)(lambda: run(x_refs[0], o_refs[0]))
        pl.when(i >= split)(lambda: run(x_refs[-1], o_refs[-1]))


def _rmsnorm(xs, w, out_dtype, n_first, split_out=False, res=None, tm=512):
    d = xs[0].shape[1]
    m = sum(x.shape[0] for x in xs)
    split = n_first // tm
    first = lambda i: (jnp.minimum(i, split - 1), 0)
    second = lambda i: (jnp.maximum(i - split, 0), 0)
    whole = lambda i: (i, 0)
    blk = lambda fn: pl.BlockSpec((tm, d), fn)
    in_specs = [blk(whole)] if len(xs) == 1 else [blk(first), blk(second)]
    if split_out:
        out_specs = [blk(first), blk(second)]
        out_shape = [jax.ShapeDtypeStruct((n_first, d), out_dtype), jax.ShapeDtypeStruct((m - n_first, d), out_dtype)]
    else:
        out_specs = [blk(whole)]
        out_shape = [jax.ShapeDtypeStruct((m, d), out_dtype)]
    extra = [] if res is None else [res]
    outs = pl.pallas_call(
        functools.partial(_rmsnorm_kernel, n_in=len(xs), n_out=len(out_shape), split=split, has_res=len(extra)),
        grid=(m // tm,),
        in_specs=in_specs + [pl.BlockSpec((1, d), lambda i: (0, 0))] + [blk(whole) for _ in extra],
        out_specs=out_specs,
        out_shape=out_shape,
        compiler_params=_cparams(("arbitrary",)),
        name="rmsnorm",
    )(*xs, w.reshape(1, d), *extra)
    return outs if split_out else outs[0]


def _prenorm_dt_kernel(xp_ref, xs_ref, w_ref, wdt_ref, bias_ref, h_ref, dt_ref, wdtb_ref, *, split):
    i = pl.program_id(0)

    @pl.when(i == 0)
    def _():
        wdtb_ref[...] = wdt_ref[...].astype(bf16)

    def run(x_ref):
        x = x_ref[...]
        ms = jnp.mean(x * x, axis=-1, keepdims=True)
        hb = (x * lax.rsqrt(ms + EPS) * w_ref[...]).astype(bf16)
        h_ref[...] = hb
        dt_ref[...] = _softplus(_dot_nt(hb, wdtb_ref[...]) + bias_ref[...])

    pl.when(i < split)(lambda: run(xp_ref))
    pl.when(i >= split)(lambda: run(xs_ref))


def _prenorm_dt(xp, xs, w, w_dt_t, dt_bias, tm=512):
    d = xp.shape[1]
    m = xp.shape[0] + xs.shape[0]
    n_dt = w_dt_t.shape[0]
    split = xp.shape[0] // tm
    return pl.pallas_call(
        functools.partial(_prenorm_dt_kernel, split=split),
        grid=(m // tm,),
        in_specs=[pl.BlockSpec((tm, d), lambda i: (jnp.minimum(i, split - 1), 0)),
                  pl.BlockSpec((tm, d), lambda i: (jnp.maximum(i - split, 0), 0)),
                  pl.BlockSpec((1, d), lambda i: (0, 0)),
                  pl.BlockSpec((n_dt, d), lambda i: (0, 0)),
                  pl.BlockSpec((1, n_dt), lambda i: (0, 0))],
        out_specs=[pl.BlockSpec((tm, d), lambda i: (i, 0)), pl.BlockSpec((tm, n_dt), lambda i: (i, 0))],
        out_shape=[jax.ShapeDtypeStruct((m, d), bf16), jax.ShapeDtypeStruct((m, n_dt), f32)],
        scratch_shapes=[pltpu.VMEM((n_dt, d), bf16)],
        compiler_params=_cparams(("arbitrary",)),
        name="prenorm_dt",
    )(xp, xs, w.reshape(1, d), w_dt_t, dt_bias)


def _mm_kernel(*refs, a_parts, split, nd, nc, nt, no, epi, w_t):
    a_refs, refs = refs[:sum(a_parts)], refs[sum(a_parts):]
    w_refs = refs[:nd]
    c_refs = refs[nd:nd + nc]
    t_refs = refs[nd + nc:nd + nc + nt]
    o_refs = refs[nd + nc + nt:nd + nc + nt + no]
    wb_refs = refs[nd + nc + nt + no:]

    @pl.when(pl.program_id(1) == 0)
    def _():
        for w_ref, wb_ref in zip(w_refs, wb_refs):
            wb_ref[...] = w_ref[...].astype(bf16)

    a_vals = []
    for parts in a_parts:
        if parts == 2:
            a_vals.append(jnp.where(pl.program_id(1) < split, a_refs[0][...], a_refs[1][...]))
        else:
            a_vals.append(a_refs[0][...])
        a_refs = a_refs[parts:]
    na = len(a_vals)
    accs = [(_dot_nt if t else _dot)(a_vals[min(i, na - 1)], wb_ref[...])
            for i, (wb_ref, t) in enumerate(zip(wb_refs, w_t))]
    outs = epi(accs, [c[...] for c in c_refs], [t[...] for t in t_refs])
    for o_ref, o in zip(o_refs, outs):
        o_ref[...] = o.astype(o_ref.dtype)


def _col_map(j, i, off):
    return (0, j + off)


def _tile_map(j, i, off):
    return (i, j + off)


def _row_elem_map(j, i, off, tn):
    return (pl.multiple_of(off + j * tn, 8), 0)


def _mm(a_list, w_list, n_cols, epi, out_dtypes, cols=(), tiles=(), tm=1024, tn=512, w_single=False, name="mm"):
    na, nd, nc, nt, no = len(a_list), len(w_list), len(cols), len(tiles), len(out_dtypes)
    assert na in (1, nd)
    a_list = [a if isinstance(a, tuple) else (a,) for a in a_list]
    m = sum(p.shape[0] for p in a_list[0])
    split = a_list[0][0].shape[0] // tm
    in_specs, a_flat = [], []
    for parts in a_list:
        k = parts[0].shape[1]
        if len(parts) == 2:
            in_specs.append(pl.BlockSpec((tm, k), lambda j, i: (jnp.minimum(i, split - 1), 0)))
            in_specs.append(pl.BlockSpec((tm, k), lambda j, i: (jnp.maximum(i - split, 0), 0),
                                         pipeline_mode=pl.Buffered(1)))
        else:
            in_specs.append(pl.BlockSpec((tm, k), lambda j, i: (i, 0)))
        a_flat += list(parts)
    scratch = []
    mode = dict(pipeline_mode=pl.Buffered(1)) if w_single else {}
    for w, off, transposed in w_list:
        if transposed:
            k = w.shape[1]
            assert off % 8 == 0
            in_specs.append(pl.BlockSpec((pl.Element(tn), pl.Element(k)), functools.partial(_row_elem_map, off=off, tn=tn), **mode))
            scratch.append(pltpu.VMEM((tn, k), bf16))
        else:
            k = w.shape[0]
            in_specs.append(pl.BlockSpec((k, tn), functools.partial(_col_map, off=off // tn), **mode))
            scratch.append(pltpu.VMEM((k, tn), bf16))
    in_specs += [pl.BlockSpec((c.shape[0], tn), functools.partial(_col_map, off=0)) for c in cols]
    in_specs += [pl.BlockSpec((tm, tn), t[1], **(dict(pipeline_mode=t[2]) if len(t) > 2 else {})) for t in tiles]
    outs = pl.pallas_call(
        functools.partial(_mm_kernel, a_parts=tuple(len(p) for p in a_list), split=split, nd=nd, nc=nc, nt=nt, no=no,
                          epi=epi, w_t=tuple(t for _, _, t in w_list)),
        grid=(n_cols // tn, m // tm),
        in_specs=in_specs,
        out_specs=[pl.BlockSpec((tm, tn), functools.partial(_tile_map, off=0)) for _ in out_dtypes],
        out_shape=[jax.ShapeDtypeStruct((m, n_cols), dt) for dt in out_dtypes],
        scratch_shapes=scratch,
        compiler_params=_cparams(("arbitrary", "arbitrary")),
        name=name,
    )(*a_flat, *[w for w, _, _ in w_list], *cols, *[t[0] for t in tiles])
    return outs


def _ffn_kernel(h_ref, wg_ref, wu_ref, wd_ref, o_ref):
    @pl.when(pl.program_id(1) == 0)
    def _():
        o_ref[...] = jnp.zeros_like(o_ref)

    h = h_ref[...]
    gate = _dot(h, wg_ref[...].astype(bf16))
    up = _dot(h, wu_ref[...].astype(bf16))
    act = (jax.nn.silu(gate) * up).astype(bf16)
    o_ref[...] += _dot(act, wd_ref[...].astype(bf16))


def _ffn(h, w_gate, w_up, w_down, tm=1536, tf=256):
    m, d = h.shape
    hidden = w_gate.shape[1]
    return pl.pallas_call(
        _ffn_kernel,
        grid=(m // tm, hidden // tf),
        in_specs=[pl.BlockSpec((tm, d), lambda i, f: (i, 0), pipeline_mode=pl.Buffered(1)),
                  pl.BlockSpec((d, tf), lambda i, f: (0, f)),
                  pl.BlockSpec((d, tf), lambda i, f: (0, f)),
                  pl.BlockSpec((tf, d), lambda i, f: (f, 0))],
        out_specs=pl.BlockSpec((tm, d), lambda i, f: (i, 0)),
        out_shape=jax.ShapeDtypeStruct((m, d), f32),
        compiler_params=_cparams(("arbitrary", "arbitrary")),
        name="ffn",
    )(h, w_gate, w_up, w_down)


def _softplus(x):
    return jnp.maximum(x, 0.0) + jnp.log1p(jnp.exp(-jnp.abs(x)))


def _epi_forget(accs, cols, tiles):
    logits = cols[0]
    mx = jnp.max(logits, axis=0, keepdims=True)
    e = jnp.exp(logits - mx)
    lb = e[0:1, :] / jnp.sum(e, axis=0, keepdims=True)
    sig = jax.nn.sigmoid(accs[0])
    log_f = jnp.log(lb + (1.0 - lb) * sig)
    k = (1.0 - lb) * (1.0 - sig)
    return log_f, k


HG_SPLIT = 2


def _hgrn_consts(c, seg):
    nlev = int(np.log2(seg))
    mat = np.zeros(((nlev + 2) * c, c), np.float32)
    masks = np.zeros((nlev + 1, c, c), np.float32)
    for t in range(c):
        tl = t % seg
        base = t - tl
        for l in range(nlev):
            h = 1 << l
            pos = tl % (2 * h)
            ref = base + tl - pos + h - 1
            if pos >= h:
                mat[l * c + t, ref + 1:t + 1] = 1
            else:
                mat[l * c + t, t + 1:ref + 1] = 1
            for s in range(base, base + seg):
                sl = s % seg
                if sl // (2 * h) == tl // (2 * h) and pos >= h and sl % (2 * h) < h:
                    masks[l, t, s] = 1
        mat[nlev * c + t, base:t + 1] = 1
        mat[(nlev + 1) * c + t, t + 1:base + seg] = 1
        masks[nlev, t, t] = 1
    return mat, masks, nlev


def _hgrn_chunks(q_ref, k_ref, g_ref, v_ref, mat, masks, nlev, c):
    n_chunks = q_ref.shape[0] // c
    heads = q_ref.shape[1] // LANES
    rows = lambda ci: slice(ci * c, (ci + 1) * c)
    cols = lambda h: slice(h * LANES, (h + 1) * LANES)
    items = [(ci, h) for ci in range(n_chunks) for h in range(heads)]
    e_all = [jnp.exp(_dot3(mat, g_ref[rows(ci), :])) for ci in range(n_chunks)]
    q = {(ci, h): q_ref[rows(ci), cols(h)] for ci, h in items}
    k = {(ci, h): k_ref[rows(ci), cols(h)] for ci, h in items}
    v = {(ci, h): v_ref[rows(ci), cols(h)].astype(bf16) for ci, h in items}
    s = {it: jnp.where(masks[nlev], _dot_nt(q[it].astype(bf16), k[it].astype(bf16)), 0.0) for it in items}
    for l in range(nlev):
        for ci, h in items:
            e = e_all[ci][l * c:(l + 1) * c, cols(h)]
            s[ci, h] = jnp.where(masks[l], _dot_nt((q[ci, h] * e).astype(bf16), (k[ci, h] * e).astype(bf16)), s[ci, h])
    res = {}
    for ci, h in items:
        o = _dot(s[ci, h].astype(bf16), v[ci, h])
        eb = e_all[ci][nlev * c:(nlev + 1) * c, cols(h)]
        er = e_all[ci][(nlev + 1) * c:, cols(h)]
        res[ci, h] = (o, q[ci, h] * eb, k[ci, h] * er, eb, v[ci, h])
    return res


def _hgrn_out(o, sg, nw):
    ms = jnp.mean(o * o, axis=-1, keepdims=True)
    return o * lax.rsqrt(ms + EPS) * nw * sg


def _hgrn_prompt_kernel(q_ref, g_ref, k_ref, v_ref, sg_ref, nw_ref, mat_ref, mask_ref, o_ref, s_ref, st_ref,
                        *, c, nlev, n_chunks):
    t_blk = pl.program_id(2)

    @pl.when(t_blk == 0)
    def _():
        st_ref[...] = jnp.zeros_like(st_ref)

    masks = [mask_ref[l] > 0.5 for l in range(nlev + 1)]
    mat = mat_ref[...]
    nw = nw_ref[...]

    res = _hgrn_chunks(q_ref, k_ref, g_ref, v_ref, mat, masks, nlev, c)
    upd = {it: _dot_tn(r[4], r[2].astype(bf16)) for it, r in res.items()}
    for h in range(st_ref.shape[0]):
        cs = slice(h * LANES, (h + 1) * LANES)
        st = st_ref[h]
        for ci in range(n_chunks):
            rows = slice(ci * c, (ci + 1) * c)
            o, qe, _, eb, _ = res[ci, h]
            o = o + _dot_nt(qe.astype(bf16), st.astype(bf16))
            st = st * eb[c - 1:c, :] + upd[ci, h]
            o_ref[rows, cs] = _hgrn_out(o, sg_ref[rows, cs], nw[:, cs]).astype(o_ref.dtype)
        st_ref[h] = st

    @pl.when(t_blk == pl.num_programs(2) - 1)
    def _():
        for h in range(st_ref.shape[0]):
            s_ref[0, h] = st_ref[h].T


def _hgrn_sample_kernel(q_ref, g_ref, k_ref, v_ref, sg_ref, nw_ref, mat_ref, mask_ref, s_in_ref,
                        o_ref, s_out_ref, *, c, nlev, seg):
    masks = [mask_ref[l] > 0.5 for l in range(nlev + 1)]
    mat = mat_ref[...]
    nw = nw_ref[...]
    small = seg < 16
    per = c // seg
    res = _hgrn_chunks(q_ref, k_ref, g_ref, v_ref, mat, masks, nlev, c)
    for (ci, h), (o, qe, kd, eb, v) in res.items():
        rows = slice(ci * c, (ci + 1) * c)
        cs = slice(h * LANES, (h + 1) * LANES)
        v = v.astype(f32)
        eb_t = jnp.concatenate([eb, eb], axis=0).T
        parts = []
        for j in range(per):
            sl = slice(j * seg, (j + 1) * seg)
            s0 = s_in_ref[ci * per + j, h]
            parts.append(_dot(_mxu_operand(qe[sl], small), _mxu_operand(s0, small)))
            last = j * seg + seg - 1
            upd = _dot_tn(_mxu_operand(kd[sl], small), _mxu_operand(v[sl], small))
            s_out_ref[ci * per + j, h] = s0 * eb_t[:, last:last + 1] + upd
        o = o + jnp.concatenate(parts, axis=0)
        o_ref[rows, cs] = _hgrn_out(o, sg_ref[rows, cs], nw[:, cs]).astype(o_ref.dtype)


def _hgrn(q, g, k, v, sg, nw, state, n_prompt, batch, seq, dec_seq):
    m, width = q.shape
    heads = width // LANES
    nw = nw.reshape(1, width)
    c = CHUNK
    hp = 4
    hw = hp * LANES
    tc = 1024
    nt = seq // tc
    mat, masks, nlev = _hgrn_consts(c, c)
    row_spec = pl.BlockSpec((tc, hw), lambda b, h, t: (b * nt + t, h))
    const2 = lambda shape: pl.BlockSpec(shape, lambda b, h, t: (0,) * len(shape))
    o_p, s_p = pl.pallas_call(
        functools.partial(_hgrn_prompt_kernel, c=c, nlev=nlev, n_chunks=tc // c),
        grid=(batch, heads // hp, nt),
        in_specs=[row_spec] * 5 + [pl.BlockSpec((1, hw), lambda b, h, t: (0, h)), const2((mat.shape[0], HG_SPLIT * c)), const2(masks.shape)],
        out_specs=[row_spec, pl.BlockSpec((1, hp, HG_DK, LANES), lambda b, h, t: (b, h, 0, 0))],
        out_shape=[jax.ShapeDtypeStruct((n_prompt, width), bf16),
                   jax.ShapeDtypeStruct((batch, heads, HG_DK, LANES), f32)],
        scratch_shapes=[pltpu.VMEM((hp, LANES, HG_DK), f32)],
        compiler_params=_cparams(("arbitrary", "arbitrary", "arbitrary")),
        name="hgrn_prompt",
    )(q, g, k, v, sg, nw, _tile3(mat, HG_SPLIT), jnp.asarray(masks))
    n_sample = m - n_prompt
    rows = 2 * c
    per = rows // dec_seq
    mat, masks, nlev = _hgrn_consts(c, dec_seq)
    base = n_prompt // rows
    row_spec = pl.BlockSpec((rows, hw), lambda jb, h: (base + jb, h))
    st_spec = pl.BlockSpec((per, hp, HG_DK, LANES), lambda jb, h: (jb, h, 0, 0))
    const2 = lambda shape: pl.BlockSpec(shape, lambda jb, h: (0,) * len(shape))
    o_s, s_s = pl.pallas_call(
        functools.partial(_hgrn_sample_kernel, c=c, nlev=nlev, seg=dec_seq),
        grid=(n_sample // rows, heads // hp),
        in_specs=[row_spec] * 5 + [pl.BlockSpec((1, hw), lambda jb, h: (0, h)), const2((mat.shape[0], HG_SPLIT * c)),
                                   const2(masks.shape), st_spec],
        out_specs=[pl.BlockSpec((rows, hw), lambda jb, h: (jb, h)), st_spec],
        out_shape=[jax.ShapeDtypeStruct((n_sample, width), bf16), jax.ShapeDtypeStruct(state.shape, f32)],
        compiler_params=_cparams(("arbitrary", "arbitrary")),
        name="hgrn_sample",
    )(q, g, k, v, sg, nw, _tile3(mat, HG_SPLIT), jnp.asarray(masks), state)
    return (o_p, o_s), s_p, s_s


GROUP_W = 512
PAIRS = GROUP_W // LANES
XBC_W = GROUP_W + 2 * M_DSTATE


def _ssd_consts(c, seg):
    t = np.arange(c)
    same = (t[:, None] // seg) == (t[None, :] // seg)
    tril = (same & (t[None, :] <= t[:, None])).astype(np.float32)
    return tril


def _ssd_chunks(chunks, a_row, dsk, nw, lc, tril, get_state, set_state, c, seg, state_t, after_first_stage=None):
    nseg = c // seg
    small = seg < 16
    assert 2 * c == LANES and 2 * M_HEADDIM == LANES and not (state_t and nseg > 1)
    lo_half = lax.broadcasted_iota(jnp.int32, (c, LANES), 1) < M_HEADDIM
    lo_row = lo_half[0:1, :]
    n = len(chunks)
    items = [(ci, p) for ci in range(n) for p in range(PAIRS)]
    pc = lambda p: slice(p * LANES, (p + 1) * LANES)
    acum = [_dot3(lc, dt * a_row) for _, _, _, dt, _ in chunks]
    bmb = [bm.astype(bf16) for _, bm, _, _, _ in chunks]
    cmb = [cm.astype(bf16) for _, _, cm, _, _ in chunks]
    cb2 = [_dot_nt(cmb[ci], jnp.concatenate([bmb[ci], bmb[ci]], axis=0)) for ci in range(n)]
    acum_t = [jnp.concatenate([a, a], axis=0).T for a in acum]
    dt_t = [jnp.concatenate([ch[3], ch[3]], axis=0).T for ch in chunks]
    if state_t:
        bm_t = [ch[1].T.astype(bf16) for ch in chunks]
    else:
        ea_t = [jnp.exp(a[0:8, 0:c]) for a in acum_t]
    if after_first_stage is not None:
        after_first_stage()
    y, xw, ea = {}, {}, {}
    for ci, p in items:
        h0, h1 = 2 * p, 2 * p + 1
        xp, dt = chunks[ci][0][:, pc(p)], chunks[ci][3]
        acp = jnp.where(lo_half, acum[ci][:, h0:h0 + 1], acum[ci][:, h1:h1 + 1])
        dtp = jnp.where(lo_half, dt[:, h0:h0 + 1], dt[:, h1:h1 + 1])
        a_src = jnp.where(lo_row, acum_t[ci][h0:h0 + 1, :], acum_t[ci][h1:h1 + 1, :])
        dt_src = jnp.where(lo_row, dt_t[ci][h0:h0 + 1, :], dt_t[ci][h1:h1 + 1, :])
        lm = jnp.exp(jnp.where(tril, acp - a_src, -1e30))
        sc = (cb2[ci] * lm * dt_src).astype(bf16)
        x_blk = jnp.concatenate([jnp.where(lo_half, xp, 0.0), jnp.where(lo_half, 0.0, xp)], axis=0).astype(bf16)
        y[ci, p] = _dot(sc, x_blk)
        if nseg == 1:
            alast = acp[c - 1:c, :]
        else:
            alast = jnp.concatenate(
                [jnp.broadcast_to(acp[j * seg + seg - 1:j * seg + seg, :], (seg, LANES)) for j in range(nseg)], axis=0)
        xw[ci, p] = xp * (jnp.exp(alast - acp) * dtp)
        ea[ci, p] = jnp.exp(acp)
    cs = {}
    if state_t:
        upd = {(ci, p): _dot(bm_t[ci], xw[ci, p].astype(bf16)) for ci, p in items}
        for p in range(PAIRS):
            st = get_state(0, 0, p)
            for ci in range(n):
                cs[ci, p] = _dot(cmb[ci], st.astype(bf16))
                st = st * ea[ci, p][c - 1:c, :] + upd[ci, p]
            set_state(0, 0, p, st)
    else:
        for ci, p in items:
            h0, h1 = 2 * p, 2 * p + 1
            bm, cm = chunks[ci][1], chunks[ci][2]
            parts = []
            for j in range(nseg):
                sl = slice(j * seg, (j + 1) * seg)
                s0 = get_state(ci, j, p)
                parts.append(_dot_nt(_mxu_operand(cm[sl], small), _mxu_operand(s0, small)))
                upd = _dot_tn(_mxu_operand(xw[ci, p][sl], small), _mxu_operand(bm[sl], small))
                last = j * seg + seg - 1
                decay = jnp.concatenate(
                    [jnp.broadcast_to(ea_t[ci][h0:h0 + 1, last:last + 1], (M_HEADDIM, LANES)),
                     jnp.broadcast_to(ea_t[ci][h1:h1 + 1, last:last + 1], (M_HEADDIM, LANES))], axis=0)
                set_state(ci, j, p, s0 * decay + upd)
            cs[ci, p] = parts[0] if nseg == 1 else jnp.concatenate(parts, axis=0)
    outs = []
    for ci in range(n):
        xs, zs = chunks[ci][0], chunks[ci][4]
        ssq = jnp.zeros((c, 1), f32)
        ys = []
        for p in range(PAIRS):
            yp = (y[ci, p] + cs[ci, p] * ea[ci, p] + dsk[:, pc(p)] * xs[:, pc(p)]) * zs[:, pc(p)]
            ssq = ssq + jnp.sum(yp * yp, axis=-1, keepdims=True)
            ys.append(yp)
        scale = lax.rsqrt(ssq * (1.0 / GROUP_W) + EPS)
        outs.append([ys[p] * scale * nw[:, pc(p)] for p in range(PAIRS)])
    return outs


def _conv_taps(xpad_ref, lead, rows, w, b):
    acc = None
    for j in range(M_CONV):
        term = xpad_ref[lead + (slice(5 + j + rows[0], 5 + j + rows[1]), slice(None))] * w[j:j + 1, :]
        acc = term if acc is None else acc + term
    return jax.nn.silu(b + acc)


def _ssd_prompt_kernel(*refs, c, rows, n_batch):
    h_ref, gate_ref, wgb_ref = refs[16], refs[20], refs[24]
    pl.when(pl.program_id(1) < n_batch)(lambda: _ssd_prompt_step(*refs, c=c, rows=rows))

    @pl.when(jnp.logical_and(pl.program_id(1) == n_batch, pl.program_id(2) == 0))
    def _():
        gate_ref[...] = _dot_nt(h_ref[...], wgb_ref[...])


def _ssd_prompt_step(xr_ref, br_ref, cr_ref, dt_ref, zs_ref, wx_ref, wb_ref, wc_ref, bx_ref, bb_ref, bc_ref,
                     alog_ref, dsk_ref, nw_ref, lc_ref, tril_ref, h_ref, wg_ref, o_ref, s_ref, gate_ref,
                     xpad_ref, xc_ref, st_ref, wgb_ref, *, c, rows):
    t_blk = pl.program_id(2)

    @pl.when(t_blk == 0)
    def _():
        st_ref[...] = jnp.zeros_like(st_ref)
        xpad_ref[0:8, :] = jnp.zeros((8, XBC_W), f32)

    @pl.when(jnp.logical_and(t_blk == 0, pl.program_id(1) == 0))
    def _():
        wgb_ref[...] = wg_ref[...].astype(bf16)

    @pl.when(t_blk > 0)
    def _():
        xpad_ref[0:8, :] = xpad_ref[rows:rows + 8, :]

    half = rows // 2

    def gate_half(i):
        rs = slice(i * half, (i + 1) * half)
        gate_ref[rs, :] = _dot_nt(h_ref[rs, :], wgb_ref[...])

    gate_half(0)
    xpad_ref[8:8 + rows, 0:GROUP_W] = xr_ref[...]
    xpad_ref[8:8 + rows, GROUP_W:GROUP_W + M_DSTATE] = br_ref[...]
    xpad_ref[8:8 + rows, GROUP_W + M_DSTATE:XBC_W] = cr_ref[...]
    w = jnp.concatenate([wx_ref[...], wb_ref[...], wc_ref[...]], axis=1)
    b = jnp.concatenate([bx_ref[...], bb_ref[...], bc_ref[...]], axis=1)
    for i in range(rows // c):
        xc_ref[i * c:(i + 1) * c, :] = _conv_taps(xpad_ref, (), (i * c, (i + 1) * c), w, b)

    a_row = -jnp.exp(alog_ref[...])
    dsk = dsk_ref[...]
    nw = nw_ref[...]
    lc = lc_ref[...]
    tril = tril_ref[...] > 0.5

    def get_state(ci, j, p):
        return st_ref[p]

    def set_state(ci, j, p, val):
        st_ref[p] = val

    rs = lambda ci: slice(ci * c, (ci + 1) * c)
    chunks = [(xc_ref[rs(ci), 0:GROUP_W], xc_ref[rs(ci), GROUP_W:GROUP_W + M_DSTATE], xc_ref[rs(ci), GROUP_W + M_DSTATE:XBC_W],
               dt_ref[rs(ci), :], zs_ref[rs(ci), :]) for ci in range(rows // c)]
    outs = _ssd_chunks(chunks, a_row, dsk, nw, lc, tril, get_state, set_state, c, c, state_t=True,
                       after_first_stage=lambda: gate_half(1))
    for ci, out in enumerate(outs):
        for p in range(PAIRS):
            o_ref[rs(ci), p * LANES:(p + 1) * LANES] = out[p].astype(o_ref.dtype)

    @pl.when(t_blk == pl.num_programs(2) - 1)
    def _():
        for p in range(PAIRS):
            s_ref[0, p * LANES:(p + 1) * LANES, :] = st_ref[p].T


def _ssd_sample_kernel(xr_ref, br_ref, cr_ref, dt_ref, zs_ref, wx_ref, wb_ref, wc_ref, bx_ref, bb_ref, bc_ref,
                       alog_ref, dsk_ref, nw_ref, lc_ref, tril_ref, hx_ref, hb_ref, hc_ref, s_in_ref,
                       o_ref, s_out_ref, xpad_ref, *, c, seg):
    nseg = c // seg
    n_chunks = xr_ref.shape[0] // c
    w = jnp.concatenate([wx_ref[...], wb_ref[...], wc_ref[...]], axis=1)
    b = jnp.concatenate([bx_ref[...], bb_ref[...], bc_ref[...]], axis=1)
    conv = []
    for j in range(n_chunks * nseg):
        sl = slice(j * seg, (j + 1) * seg)
        xpad_ref[j, 5:8, 0:GROUP_W] = hx_ref[j]
        xpad_ref[j, 5:8, GROUP_W:GROUP_W + M_DSTATE] = hb_ref[j]
        xpad_ref[j, 5:8, GROUP_W + M_DSTATE:XBC_W] = hc_ref[j]
        xpad_ref[j, 8:8 + seg, 0:GROUP_W] = xr_ref[sl, :]
        xpad_ref[j, 8:8 + seg, GROUP_W:GROUP_W + M_DSTATE] = br_ref[sl, :]
        xpad_ref[j, 8:8 + seg, GROUP_W + M_DSTATE:XBC_W] = cr_ref[sl, :]
        conv.append(_conv_taps(xpad_ref, (j,), (0, seg), w, b))

    def get_state(ci, j, p):
        return s_in_ref[ci * nseg + j, p * LANES:(p + 1) * LANES, :]

    def set_state(ci, j, p, val):
        s_out_ref[ci * nseg + j, p * LANES:(p + 1) * LANES, :] = val

    chunks = []
    for ci in range(n_chunks):
        xc = jnp.concatenate(conv[ci * nseg:(ci + 1) * nseg], axis=0)
        rs = slice(ci * c, (ci + 1) * c)
        chunks.append((xc[:, 0:GROUP_W], xc[:, GROUP_W:GROUP_W + M_DSTATE], xc[:, GROUP_W + M_DSTATE:XBC_W],
                       dt_ref[rs, :], zs_ref[rs, :]))
    outs = _ssd_chunks(chunks, -jnp.exp(alog_ref[...]), dsk_ref[...], nw_ref[...], lc_ref[...],
                       tril_ref[...] > 0.5, get_state, set_state, c, seg, state_t=False)
    for ci, out in enumerate(outs):
        for p in range(PAIRS):
            o_ref[ci * c:(ci + 1) * c, p * LANES:(p + 1) * LANES] = out[p].astype(o_ref.dtype)


def _ssd(xbc, dt, zs, conv_w, conv_b, alog_p, dsk, nw, state, hist, n_prompt, batch, seq, dec_seq, h, wt, o_gate, gate_cols):
    m = xbc.shape[0]
    inner = zs.shape[1]
    c = CHUNK
    xb_blk = inner // M_DSTATE
    cb_blk = xb_blk + M_GROUPS

    def specs(row_map, nrow, gpos):
        def rm(fn):
            return lambda *ix: fn(row_map(*ix), ix[gpos])
        zero = lambda fn: (lambda *ix: fn(0, ix[gpos]))
        return [
            pl.BlockSpec((nrow, GROUP_W), rm(lambda r, g: (r, g))),
            pl.BlockSpec((nrow, M_DSTATE), rm(lambda r, g: (r, xb_blk + g))),
            pl.BlockSpec((nrow, M_DSTATE), rm(lambda r, g: (r, cb_blk + g))),
            pl.BlockSpec((nrow, LANES), rm(lambda r, g: (r, g))),
            pl.BlockSpec((nrow, GROUP_W), rm(lambda r, g: (r, g))),
            pl.BlockSpec((M_CONV, GROUP_W), zero(lambda r, g: (0, g))),
            pl.BlockSpec((M_CONV, M_DSTATE), zero(lambda r, g: (0, xb_blk + g))),
            pl.BlockSpec((M_CONV, M_DSTATE), zero(lambda r, g: (0, cb_blk + g))),
            pl.BlockSpec((1, GROUP_W), zero(lambda r, g: (0, g))),
            pl.BlockSpec((1, M_DSTATE), zero(lambda r, g: (0, xb_blk + g))),
            pl.BlockSpec((1, M_DSTATE), zero(lambda r, g: (0, cb_blk + g))),
            pl.BlockSpec((1, LANES), zero(lambda r, g: (0, g))),
            pl.BlockSpec((1, GROUP_W), zero(lambda r, g: (0, g))),
            pl.BlockSpec((1, GROUP_W), zero(lambda r, g: (0, g))),
            pl.BlockSpec((c, 3 * c), zero(lambda r, g: (0, 0))),
            pl.BlockSpec((c, 2 * c), zero(lambda r, g: (0, 0))),
        ]

    common = (xbc, xbc, xbc, dt, zs, conv_w, conv_w, conv_w, conv_b, conv_b, conv_b, alog_p, dsk, nw)
    rows = 1024
    nt = seq // rows
    tril = _ssd_consts(c, c)
    assert m - n_prompt == rows
    last = batch * nt - 1
    scan_row = lambda g, b, t: jnp.minimum(b * nt + t, last)
    gate_row = lambda g, b, t: jnp.minimum(b * nt + t, last + 1)
    in_specs = specs(scan_row, rows, 0)
    d_model = h.shape[1]
    gw = gate_cols // M_GROUPS
    in_specs += [
        pl.BlockSpec((rows, d_model), lambda g, b, t: (gate_row(g, b, t), 0)),
        pl.BlockSpec((pl.Element(gw), pl.Element(d_model)), lambda g, b, t: (pl.multiple_of(o_gate + g * gw, 8), 0),
                     pipeline_mode=pl.Buffered(1)),
    ]
    o_p, s_p, gates = pl.pallas_call(
        functools.partial(_ssd_prompt_kernel, c=c, rows=rows, n_batch=batch),
        grid=(M_GROUPS, batch + 1, nt),
        in_specs=in_specs,
        out_specs=[pl.BlockSpec((rows, GROUP_W), lambda g, b, t: (scan_row(g, b, t), g)),
                   pl.BlockSpec((1, GROUP_W, M_DSTATE), lambda g, b, t: (jnp.minimum(b, batch - 1), g, 0)),
                   pl.BlockSpec((rows, gw), lambda g, b, t: (gate_row(g, b, t), g))],
        out_shape=[jax.ShapeDtypeStruct((n_prompt, inner), bf16),
                   jax.ShapeDtypeStruct((batch, inner, M_DSTATE), f32),
                   jax.ShapeDtypeStruct((m, gate_cols), f32)],
        scratch_shapes=[pltpu.VMEM((rows + 8, XBC_W), f32), pltpu.VMEM((rows, XBC_W), f32),
                        pltpu.VMEM((PAIRS, LANES, M_DSTATE), f32), pltpu.VMEM((gw, d_model), bf16)],
        compiler_params=_cparams(("arbitrary", "arbitrary", "arbitrary")),
        name="ssd_prompt",
    )(*common, _tile3(tril), jnp.asarray(np.tile(tril, (1, 2))), h, wt)
    n_sample = m - n_prompt
    rows = 2 * c
    per = rows // dec_seq
    base = n_prompt // rows
    tril = _ssd_consts(c, dec_seq)
    in_specs = specs(lambda jb, g: base + jb, rows, 1)
    in_specs += [
        pl.BlockSpec((per, M_CONV - 1, GROUP_W), lambda jb, g: (jb, 0, g)),
        pl.BlockSpec((per, M_CONV - 1, M_DSTATE), lambda jb, g: (jb, 0, xb_blk + g)),
        pl.BlockSpec((per, M_CONV - 1, M_DSTATE), lambda jb, g: (jb, 0, cb_blk + g)),
        pl.BlockSpec((per, GROUP_W, M_DSTATE), lambda jb, g: (jb, g, 0)),
    ]
    o_s, s_s = pl.pallas_call(
        functools.partial(_ssd_sample_kernel, c=c, seg=dec_seq),
        grid=(n_sample // rows, M_GROUPS),
        in_specs=in_specs,
        out_specs=[pl.BlockSpec((rows, GROUP_W), lambda jb, g: (jb, g)),
                   pl.BlockSpec((per, GROUP_W, M_DSTATE), lambda jb, g: (jb, g, 0))],
        out_shape=[jax.ShapeDtypeStruct((n_sample, inner), bf16), jax.ShapeDtypeStruct(state.shape, f32)],
        scratch_shapes=[pltpu.VMEM((per, 16, XBC_W), f32)],
        compiler_params=_cparams(("arbitrary", "arbitrary")),
        name="ssd_sample",
    )(*common, _tile3(tril), jnp.asarray(np.tile(tril, (1, 2))), hist, hist, hist, state)
    return (o_p, o_s), s_p, s_s, gates


def kernel(x_prompt, x_sample, state_hgrn, state_ssm, state_conv, norm_mix, w_in, hg_lb_logits, hg_norm, conv_w, conv_b,
           dt_bias, a_log, d_skip, ssm_norm, w_branch_hg, w_branch_ssm, w_out, norm_ffn, w_ffn_gate, w_ffn_up,
           w_ffn_down, norm_final):
    batch, seq, d = x_prompt.shape
    dec_batch, dec_seq, _ = x_sample.shape
    n_prompt, n_sample = batch * seq, dec_batch * dec_seq
    hg_heads = state_hgrn.shape[2]
    kdim = hg_heads * HG_DK
    vdim = d
    inner = d
    m_heads = state_ssm.shape[2]
    conv_dim = conv_w.shape[2]
    hpg = m_heads // M_GROUPS

    xp2, xs2 = x_prompt.reshape(n_prompt, d), x_sample.reshape(n_sample, d)

    wt = jnp.swapaxes(w_in, 1, 2)[0]
    o_q, o_f, o_v, o_g, o_z, o_xbc = 0, kdim, 2 * kdim, 2 * kdim + vdim, 2 * kdim + 2 * vdim, 2 * kdim + 2 * vdim + inner
    o_dt = o_xbc + conv_dim
    o_gate = o_dt + m_heads
    pad_heads = lambda p: jnp.pad(p.reshape(M_GROUPS, hpg, -1), ((0, 0), (0, LANES - hpg), (0, 0))).reshape(M_GROUPS * LANES, -1)
    h, dt = _prenorm_dt(xp2, xs2, norm_mix[0], pad_heads(wt[o_dt:o_gate]), pad_heads(dt_bias[0]).T)
    scale = HG_DK ** -0.5
    ident = lambda a, c, t: (a[0],)
    silu = lambda a, c, t: (jax.nn.silu(a[0]),)
    wide = dict(tm=1024, tn=2048, w_single=True)
    (q,) = _mm([h], [(wt, o_q, True)], kdim, lambda a, c, t: (a[0] * scale,), [f32], name="proj_q", **wide)
    log_f, k = _mm([h], [(wt, o_f, True)], kdim, _epi_forget, [f32, f32], cols=[hg_lb_logits], name="proj_f",
                   tm=512, tn=2048, w_single=True)
    (v,) = _mm([h], [(wt, o_v, True)], vdim, ident, [bf16], name="proj_v", **wide)
    (sg,) = _mm([h], [(wt, o_g, True)], vdim, silu, [f32], name="proj_g", **wide)
    (zs,) = _mm([h], [(wt, o_z, True)], inner, silu, [f32], name="proj_z", **wide)
    (xbc,) = _mm([h], [(wt, o_xbc, True)], conv_dim, ident, [f32], tm=1024, tn=conv_dim // 2, w_single=True, name="proj_xbc")
    o_hg, shp, shs = _hgrn(q, log_f, k, v, sg, hg_norm[0], state_hgrn[0], n_prompt, batch, seq, dec_seq)
    dsk = jnp.repeat(d_skip[0], M_HEADDIM).reshape(1, inner)
    o_m, smp, sms, gates = _ssd(xbc, dt, zs, conv_w[0], conv_b[0].reshape(1, conv_dim), pad_heads(a_log[0]).T, dsk,
                                ssm_norm[0].reshape(1, inner), state_ssm[0].reshape(dec_batch, inner, M_DSTATE),
                                state_conv[0], n_prompt, batch, seq, dec_seq, h, wt, o_gate, 2 * d)

    tm, tn = 512, 1024
    (merged,) = _mm([o_hg, o_m], [(w_branch_hg[0], 0, False), (w_branch_ssm[0], 0, False)], d,
                    lambda a, c, t: (jax.nn.sigmoid(t[0]) * a[0] + jax.nn.sigmoid(t[1]) * a[1],), [bf16],
                    tiles=[(gates, lambda j, i: (i, j)), (gates, lambda j, i: (i, j + d // tn))],
                    tm=tm, tn=tn, w_single=True, name="merge")
    tm = 512
    n_p_tiles = n_prompt // tm

    def add_x_norm(a, c, t):
        x1 = jnp.where(pl.program_id(1) < n_p_tiles, t[0], t[1]) + a[0]
        ms = jnp.mean(x1 * x1, axis=-1, keepdims=True)
        return x1, x1 * lax.rsqrt(ms + EPS) * c[0]

    x1, h2 = _mm([merged], [(w_out[0], 0, False)], d, add_x_norm, [f32, bf16], cols=[norm_ffn],
                 tiles=[(xp2, lambda j, i: (jnp.minimum(i, n_p_tiles - 1), j)),
                        (xs2, lambda j, i: (jnp.maximum(i - n_p_tiles, 0), j), pl.Buffered(1))],
                 tm=tm, tn=d, w_single=True, name="out_proj")
    ffn = _ffn(h2, w_ffn_gate[0], w_ffn_up[0], w_ffn_down[0])
    y_p, y_s = _rmsnorm([x1], norm_final, f32, n_prompt, split_out=True, res=ffn)

    hist = M_CONV - 1
    new_conv_p = jnp.stack([xbc[(b + 1) * seq - hist:(b + 1) * seq] for b in range(batch)])[None]
    new_conv_s = xbc.reshape(-1, dec_seq, conv_dim)[n_prompt // dec_seq:, dec_seq - hist:][None]
    return (y_p.reshape(batch, seq, d), y_s.reshape(dec_batch, dec_seq, d),
            shp[None], smp.reshape(1, batch, m_heads, M_HEADDIM, M_DSTATE), new_conv_p,
            shs[None], sms.reshape(1, dec_batch, m_heads, M_HEADDIM, M_DSTATE), new_conv_s)
```

```python
import functools

import numpy as np
import jax
import jax.numpy as jnp
from jax import lax
from jax.experimental import pallas as pl
from jax.experimental.pallas import tpu as pltpu

f32 = jnp.float32
bf16 = jnp.bfloat16

EPS = 1e-6
LANES = 128
CHUNK = 64
HG_DK = 128
M_HEADDIM = 64
M_DSTATE = 128
M_GROUPS = 4
M_CONV = 4
VMEM_LIMIT = 56 * 1024 * 1024


def _cparams(sem):
    return pltpu.CompilerParams(dimension_semantics=sem, vmem_limit_bytes=VMEM_LIMIT)


def _dot(a, b):
    return jnp.dot(a, b, preferred_element_type=f32)


def _dot_nt(a, b):
    return lax.dot_general(a, b, (((1,), (1,)), ((), ())), preferred_element_type=f32)


def _dot_tn(a, b):
    return lax.dot_general(a, b, (((0,), (0,)), ((), ())), preferred_element_type=f32)


def _split(x, n):
    pieces = []
    for _ in range(n - 1):
        p = x.astype(bf16)
        pieces.append(p)
        x = x - p.astype(f32)
    return pieces + [x.astype(bf16)]


def _tile3(m, n=3):
    return jnp.asarray(np.tile(m, (1, n)), bf16)


def _dot3(mn, x):
    return _dot(mn, jnp.concatenate(_split(x, mn.shape[1] // x.shape[0]), axis=0))


def _mxu_operand(x, small):
    xb = x.astype(bf16)
    return xb.astype(f32) if small else xb


def _rmsnorm_kernel(*refs, n_in, n_out, split, has_res):
    x_refs, w_ref, o_refs = refs[:n_in], refs[n_in], refs[n_in + 1 + has_res:]

    def run(x_ref, o_ref):
        x = x_ref[...]
        if has_res:
            x = x + refs[n_in + 1][...]
        ms = jnp.mean(x * x, axis=-1, keepdims=True)
        o_ref[...] = (x * lax.rsqrt(ms + EPS) * w_ref[...]).astype(o_ref.dtype)

    if n_in == 1 and n_out == 1:
        run(x_refs[0], o_refs[0])
    else:
        i = pl.program_id(0)
        pl.when(i < split)(lambda: run(x_refs[0], o_refs[0]))
        pl.when(i >= split)(lambda: run(x_refs[-1], o_refs[-1]))


def _rmsnorm(xs, w, out_dtype, n_first, split_out=False, res=None, tm=512):
    d = xs[0].shape[1]
    m = sum(x.shape[0] for x in xs)
    split = n_first // tm
    first = lambda i: (jnp.minimum(i, split - 1), 0)
    second = lambda i: (jnp.maximum(i - split, 0), 0)
    whole = lambda i: (i, 0)
    blk = lambda fn: pl.BlockSpec((tm, d), fn)
    in_specs = [blk(whole)] if len(xs) == 1 else [blk(first), blk(second)]
    if split_out:
        out_specs = [blk(first), blk(second)]
        out_shape = [jax.ShapeDtypeStruct((n_first, d), out_dtype), jax.ShapeDtypeStruct((m - n_first, d), out_dtype)]
    else:
        out_specs = [blk(whole)]
        out_shape = [jax.ShapeDtypeStruct((m, d), out_dtype)]
    extra = [] if res is None else [res]
    outs = pl.pallas_call(
        functools.partial(_rmsnorm_kernel, n_in=len(xs), n_out=len(out_shape), split=split, has_res=len(extra)),
        grid=(m // tm,),
        in_specs=in_specs + [pl.BlockSpec((1, d), lambda i: (0, 0))] + [blk(whole) for _ in extra],
        out_specs=out_specs,
        out_shape=out_shape,
        compiler_params=_cparams(("arbitrary",)),
        name="rmsnorm",
    )(*xs, w.reshape(1, d), *extra)
    return outs if split_out else outs[0]


def _prenorm_dt_kernel(xp_ref, xs_ref, w_ref, wdt_ref, bias_ref, h_ref, dt_ref, wdtb_ref, *, split):
    i = pl.program_id(0)

    @pl.when(i == 0)
    def _():
        wdtb_ref[...] = wdt_ref[...].astype(bf16)

    def run(x_ref):
        x = x_ref[...]
        ms = jnp.mean(x * x, axis=-1, keepdims=True)
        hb = (x * lax.rsqrt(ms + EPS) * w_ref[...]).astype(bf16)
        h_ref[...] = hb
        dt_ref[...] = _softplus(_dot_nt(hb, wdtb_ref[...]) + bias_ref[...])

    pl.when(i < split)(lambda: run(xp_ref))
    pl.when(i >= split)(lambda: run(xs_ref))


def _prenorm_dt(xp, xs, w, w_dt_t, dt_bias, tm=512):
    d = xp.shape[1]
    m = xp.shape[0] + xs.shape[0]
    n_dt = w_dt_t.shape[0]
    split = xp.shape[0] // tm
    return pl.pallas_call(
        functools.partial(_prenorm_dt_kernel, split=split),
        grid=(m // tm,),
        in_specs=[pl.BlockSpec((tm, d), lambda i: (jnp.minimum(i, split - 1), 0)),
                  pl.BlockSpec((tm, d), lambda i: (jnp.maximum(i - split, 0), 0)),
                  pl.BlockSpec((1, d), lambda i: (0, 0)),
                  pl.BlockSpec((n_dt, d), lambda i: (0, 0)),
                  pl.BlockSpec((1, n_dt), lambda i: (0, 0))],
        out_specs=[pl.BlockSpec((tm, d), lambda i: (i, 0)), pl.BlockSpec((tm, n_dt), lambda i: (i, 0))],
        out_shape=[jax.ShapeDtypeStruct((m, d), bf16), jax.ShapeDtypeStruct((m, n_dt), f32)],
        scratch_shapes=[pltpu.VMEM((n_dt, d), bf16)],
        compiler_params=_cparams(("arbitrary",)),
        name="prenorm_dt",
    )(xp, xs, w.reshape(1, d), w_dt_t, dt_bias)


def _mm_kernel(*refs, a_parts, split, nd, nc, nt, no, epi, w_t):
    a_refs, refs = refs[:sum(a_parts)], refs[sum(a_parts):]
    w_refs = refs[:nd]
    c_refs = refs[nd:nd + nc]
    t_refs = refs[nd + nc:nd + nc + nt]
    o_refs = refs[nd + nc + nt:nd + nc + nt + no]
    wb_refs = refs[nd + nc + nt + no:]

    @pl.when(pl.program_id(1) == 0)
    def _():
        for w_ref, wb_ref in zip(w_refs, wb_refs):
            wb_ref[...] = w_ref[...].astype(bf16)

    a_vals = []
    for parts in a_parts:
        if parts == 2:
            a_vals.append(jnp.where(pl.program_id(1) < split, a_refs[0][...], a_refs[1][...]))
        else:
            a_vals.append(a_refs[0][...])
        a_refs = a_refs[parts:]
    na = len(a_vals)
    accs = [(_dot_nt if t else _dot)(a_vals[min(i, na - 1)], wb_ref[...])
            for i, (wb_ref, t) in enumerate(zip(wb_refs, w_t))]
    outs = epi(accs, [c[...] for c in c_refs], [t[...] for t in t_refs])
    for o_ref, o in zip(o_refs, outs):
        o_ref[...] = o.astype(o_ref.dtype)


def _col_map(j, i, off):
    return (0, j + off)


def _tile_map(j, i, off):
    return (i, j + off)


def _row_elem_map(j, i, off, tn):
    return (pl.multiple_of(off + j * tn, 8), 0)


def _mm(a_list, w_list, n_cols, epi, out_dtypes, cols=(), tiles=(), tm=1024, tn=512, w_single=False, name="mm"):
    na, nd, nc, nt, no = len(a_list), len(w_list), len(cols), len(tiles), len(out_dtypes)
    assert na in (1, nd)
    a_list = [a if isinstance(a, tuple) else (a,) for a in a_list]
    m = sum(p.shape[0] for p in a_list[0])
    split = a_list[0][0].shape[0] // tm
    in_specs, a_flat = [], []
    for parts in a_list:
        k = parts[0].shape[1]
        if len(parts) == 2:
            in_specs.append(pl.BlockSpec((tm, k), lambda j, i: (jnp.minimum(i, split - 1), 0)))
            in_specs.append(pl.BlockSpec((tm, k), lambda j, i: (jnp.maximum(i - split, 0), 0),
                                         pipeline_mode=pl.Buffered(1)))
        else:
            in_specs.append(pl.BlockSpec((tm, k), lambda j, i: (i, 0)))
        a_flat += list(parts)
    scratch = []
    mode = dict(pipeline_mode=pl.Buffered(1)) if w_single else {}
    for w, off, transposed in w_list:
        if transposed:
            k = w.shape[1]
            assert off % 8 == 0
            in_specs.append(pl.BlockSpec((pl.Element(tn), pl.Element(k)), functools.partial(_row_elem_map, off=off, tn=tn), **mode))
            scratch.append(pltpu.VMEM((tn, k), bf16))
        else:
            k = w.shape[0]
            in_specs.append(pl.BlockSpec((k, tn), functools.partial(_col_map, off=off // tn), **mode))
            scratch.append(pltpu.VMEM((k, tn), bf16))
    in_specs += [pl.BlockSpec((c.shape[0], tn), functools.partial(_col_map, off=0)) for c in cols]
    in_specs += [pl.BlockSpec((tm, tn), t[1], **(dict(pipeline_mode=t[2]) if len(t) > 2 else {})) for t in tiles]
    outs = pl.pallas_call(
        functools.partial(_mm_kernel, a_parts=tuple(len(p) for p in a_list), split=split, nd=nd, nc=nc, nt=nt, no=no,
                          epi=epi, w_t=tuple(t for _, _, t in w_list)),
        grid=(n_cols // tn, m // tm),
        in_specs=in_specs,
        out_specs=[pl.BlockSpec((tm, tn), functools.partial(_tile_map, off=0)) for _ in out_dtypes],
        out_shape=[jax.ShapeDtypeStruct((m, n_cols), dt) for dt in out_dtypes],
        scratch_shapes=scratch,
        compiler_params=_cparams(("arbitrary", "arbitrary")),
        name=name,
    )(*a_flat, *[w for w, _, _ in w_list], *cols, *[t[0] for t in tiles])
    return outs


def _ffn_kernel(h_ref, wg_ref, wu_ref, wd_ref, o_ref):
    @pl.when(pl.program_id(1) == 0)
    def _():
        o_ref[...] = jnp.zeros_like(o_ref)

    h = h_ref[...]
    gate = _dot(h, wg_ref[...].astype(bf16))
    up = _dot(h, wu_ref[...].astype(bf16))
    act = (jax.nn.silu(gate) * up).astype(bf16)
    o_ref[...] += _dot(act, wd_ref[...].astype(bf16))


def _ffn(h, w_gate, w_up, w_down, tm=1536, tf=256):
    m, d = h.shape
    hidden = w_gate.shape[1]
    return pl.pallas_call(
        _ffn_kernel,
        grid=(m // tm, hidden // tf),
        in_specs=[pl.BlockSpec((tm, d), lambda i, f: (i, 0), pipeline_mode=pl.Buffered(1)),
                  pl.BlockSpec((d, tf), lambda i, f: (0, f)),
                  pl.BlockSpec((d, tf), lambda i, f: (0, f)),
                  pl.BlockSpec((tf, d), lambda i, f: (f, 0))],
        out_specs=pl.BlockSpec((tm, d), lambda i, f: (i, 0)),
        out_shape=jax.ShapeDtypeStruct((m, d), f32),
        compiler_params=_cparams(("arbitrary", "arbitrary")),
        name="ffn",
    )(h, w_gate, w_up, w_down)


def _softplus(x):
    return jnp.maximum(x, 0.0) + jnp.log1p(jnp.exp(-jnp.abs(x)))


def _epi_forget(accs, cols, tiles):
    logits = cols[0]
    mx = jnp.max(logits, axis=0, keepdims=True)
    e = jnp.exp(logits - mx)
    lb = e[0:1, :] / jnp.sum(e, axis=0, keepdims=True)
    sig = jax.nn.sigmoid(accs[0])
    log_f = jnp.log(lb + (1.0 - lb) * sig)
    k = (1.0 - lb) * (1.0 - sig)
    return log_f, k


HG_SPLIT = 2


def _hgrn_consts(c, seg):
    nlev = int(np.log2(seg))
    mat = np.zeros(((nlev + 2) * c, c), np.float32)
    masks = np.zeros((nlev + 1, c, c), np.float32)
    for t in range(c):
        tl = t % seg
        base = t - tl
        for l in range(nlev):
            h = 1 << l
            pos = tl % (2 * h)
            ref = base + tl - pos + h - 1
            if pos >= h:
                mat[l * c + t, ref + 1:t + 1] = 1
            else:
                mat[l * c + t, t + 1:ref + 1] = 1
            for s in range(base, base + seg):
                sl = s % seg
                if sl // (2 * h) == tl // (2 * h) and pos >= h and sl % (2 * h) < h:
                    masks[l, t, s] = 1
        mat[nlev * c + t, base:t + 1] = 1
        mat[(nlev + 1) * c + t, t + 1:base + seg] = 1
        masks[nlev, t, t] = 1
    return mat, masks, nlev


def _hgrn_chunks(q_ref, k_ref, g_ref, v_ref, mat, masks, nlev, c):
    n_chunks = q_ref.shape[0] // c
    heads = q_ref.shape[1] // LANES
    rows = lambda ci: slice(ci * c, (ci + 1) * c)
    cols = lambda h: slice(h * LANES, (h + 1) * LANES)
    items = [(ci, h) for ci in range(n_chunks) for h in range(heads)]
    e_all = [jnp.exp(_dot3(mat, g_ref[rows(ci), :])) for ci in range(n_chunks)]
    q = {(ci, h): q_ref[rows(ci), cols(h)] for ci, h in items}
    k = {(ci, h): k_ref[rows(ci), cols(h)] for ci, h in items}
    v = {(ci, h): v_ref[rows(ci), cols(h)].astype(bf16) for ci, h in items}
    s = {it: jnp.where(masks[nlev], _dot_nt(q[it].astype(bf16), k[it].astype(bf16)), 0.0) for it in items}
    for l in range(nlev):
        for ci, h in items:
            e = e_all[ci][l * c:(l + 1) * c, cols(h)]
            s[ci, h] = jnp.where(masks[l], _dot_nt((q[ci, h] * e).astype(bf16), (k[ci, h] * e).astype(bf16)), s[ci, h])
    res = {}
    for ci, h in items:
        o = _dot(s[ci, h].astype(bf16), v[ci, h])
        eb = e_all[ci][nlev * c:(nlev + 1) * c, cols(h)]
        er = e_all[ci][(nlev + 1) * c:, cols(h)]
        res[ci, h] = (o, q[ci, h] * eb, k[ci, h] * er, eb, v[ci, h])
    return res


def _hgrn_out(o, sg, nw):
    ms = jnp.mean(o * o, axis=-1, keepdims=True)
    return o * lax.rsqrt(ms + EPS) * nw * sg


def _hgrn_prompt_main(q_ref, g_ref, k_ref, v_ref, sg_ref, nw_ref, mat_ref, mask_ref, o_ref, st_ref, *, c, nlev, n_chunks):
    masks = [mask_ref[l] > 0.5 for l in range(nlev + 1)]
    mat = mat_ref[...]
    nw = nw_ref[...]

    res = _hgrn_chunks(q_ref, k_ref, g_ref, v_ref, mat, masks, nlev, c)
    upd = {it: _dot_tn(r[4], r[2].astype(bf16)) for it, r in res.items()}
    for h in range(st_ref.shape[0]):
        cs = slice(h * LANES, (h + 1) * LANES)
        st = st_ref[h]
        for ci in range(n_chunks):
            rows = slice(ci * c, (ci + 1) * c)
            o, qe, _, eb, _ = res[ci, h]
            o = o + _dot_nt(qe.astype(bf16), st.astype(bf16))
            st = st * eb[c - 1:c, :] + upd[ci, h]
            o_ref[rows, cs] = _hgrn_out(o, sg_ref[rows, cs], nw[:, cs]).astype(o_ref.dtype)
        st_ref[h] = st


def _hgrn_sample_kernel(q_ref, g_ref, k_ref, v_ref, sg_ref, nw_ref, mat_ref, mask_ref, s_in_ref,
                        o_ref, s_out_ref, *, c, nlev, seg):
    masks = [mask_ref[l] > 0.5 for l in range(nlev + 1)]
    mat = mat_ref[...]
    nw = nw_ref[...]
    small = seg < 16
    per = c // seg
    res = _hgrn_chunks(q_ref, k_ref, g_ref, v_ref, mat, masks, nlev, c)
    for (ci, h), (o, qe, kd, eb, v) in res.items():
        rows = slice(ci * c, (ci + 1) * c)
        cs = slice(h * LANES, (h + 1) * LANES)
        v = v.astype(f32)
        eb_t = jnp.concatenate([eb, eb], axis=0).T
        parts = []
        for j in range(per):
            sl = slice(j * seg, (j + 1) * seg)
            s0 = s_in_ref[ci * per + j, h]
            parts.append(_dot(_mxu_operand(qe[sl], small), _mxu_operand(s0, small)))
            last = j * seg + seg - 1
            upd = _dot_tn(_mxu_operand(kd[sl], small), _mxu_operand(v[sl], small))
            s_out_ref[ci * per + j, h] = s0 * eb_t[:, last:last + 1] + upd
        o = o + jnp.concatenate(parts, axis=0)
        o_ref[rows, cs] = _hgrn_out(o, sg_ref[rows, cs], nw[:, cs]).astype(o_ref.dtype)


HG_HP = 4


def _hgrn_sample(q, g, k, v, sg, nw, state, n_prompt, dec_seq):
    m, width = q.shape
    heads = width // LANES
    nw = nw.reshape(1, width)
    c = CHUNK
    hp = HG_HP
    hw = hp * LANES
    n_sample = m - n_prompt
    rows = 2 * c
    per = rows // dec_seq
    mat, masks, nlev = _hgrn_consts(c, dec_seq)
    base = n_prompt // rows
    row_spec = pl.BlockSpec((rows, hw), lambda jb, h: (base + jb, h))
    st_spec = pl.BlockSpec((per, hp, HG_DK, LANES), lambda jb, h: (jb, h, 0, 0))
    const2 = lambda shape: pl.BlockSpec(shape, lambda jb, h: (0,) * len(shape))
    o_s, s_s = pl.pallas_call(
        functools.partial(_hgrn_sample_kernel, c=c, nlev=nlev, seg=dec_seq),
        grid=(n_sample // rows, heads // hp),
        in_specs=[row_spec] * 5 + [pl.BlockSpec((1, hw), lambda jb, h: (0, h)), const2((mat.shape[0], HG_SPLIT * c)),
                                   const2(masks.shape), st_spec],
        out_specs=[pl.BlockSpec((rows, hw), lambda jb, h: (jb, h)), st_spec],
        out_shape=[jax.ShapeDtypeStruct((n_sample, width), bf16), jax.ShapeDtypeStruct(state.shape, f32)],
        compiler_params=_cparams(("arbitrary", "arbitrary")),
        name="hgrn_sample",
    )(q, g, k, v, sg, nw, _tile3(mat, HG_SPLIT), jnp.asarray(masks), state)
    return o_s, s_s


GROUP_W = 512
PAIRS = GROUP_W // LANES
XBC_W = GROUP_W + 2 * M_DSTATE


def _ssd_consts(c, seg):
    t = np.arange(c)
    same = (t[:, None] // seg) == (t[None, :] // seg)
    tril = (same & (t[None, :] <= t[:, None])).astype(np.float32)
    return tril


def _ssd_chunks(chunks, a_row, dsk, nw, lc, tril, get_state, set_state, c, seg, state_t, after_first_stage=None):
    nseg = c // seg
    small = seg < 16
    assert 2 * c == LANES and 2 * M_HEADDIM == LANES and not (state_t and nseg > 1)
    lo_half = lax.broadcasted_iota(jnp.int32, (c, LANES), 1) < M_HEADDIM
    lo_row = lo_half[0:1, :]
    n = len(chunks)
    items = [(ci, p) for ci in range(n) for p in range(PAIRS)]
    pc = lambda p: slice(p * LANES, (p + 1) * LANES)
    acum = [_dot3(lc, dt * a_row) for _, _, _, dt, _ in chunks]
    bmb = [bm.astype(bf16) for _, bm, _, _, _ in chunks]
    cmb = [cm.astype(bf16) for _, _, cm, _, _ in chunks]
    cb2 = [_dot_nt(cmb[ci], jnp.concatenate([bmb[ci], bmb[ci]], axis=0)) for ci in range(n)]
    acum_t = [jnp.concatenate([a, a], axis=0).T for a in acum]
    dt_t = [jnp.concatenate([ch[3], ch[3]], axis=0).T for ch in chunks]
    if state_t:
        bm_t = [ch[1].T.astype(bf16) for ch in chunks]
    else:
        ea_t = [jnp.exp(a[0:8, 0:c]) for a in acum_t]
    if after_first_stage is not None:
        after_first_stage()
    y, xw, ea = {}, {}, {}
    for ci, p in items:
        h0, h1 = 2 * p, 2 * p + 1
        xp, dt = chunks[ci][0][:, pc(p)], chunks[ci][3]
        acp = jnp.where(lo_half, acum[ci][:, h0:h0 + 1], acum[ci][:, h1:h1 + 1])
        dtp = jnp.where(lo_half, dt[:, h0:h0 + 1], dt[:, h1:h1 + 1])
        a_src = jnp.where(lo_row, acum_t[ci][h0:h0 + 1, :], acum_t[ci][h1:h1 + 1, :])
        dt_src = jnp.where(lo_row, dt_t[ci][h0:h0 + 1, :], dt_t[ci][h1:h1 + 1, :])
        lm = jnp.exp(jnp.where(tril, acp - a_src, -1e30))
        sc = (cb2[ci] * lm * dt_src).astype(bf16)
        x_blk = jnp.concatenate([jnp.where(lo_half, xp, 0.0), jnp.where(lo_half, 0.0, xp)], axis=0).astype(bf16)
        y[ci, p] = _dot(sc, x_blk)
        if nseg == 1:
            alast = acp[c - 1:c, :]
        else:
            alast = jnp.concatenate(
                [jnp.broadcast_to(acp[j * seg + seg - 1:j * seg + seg, :], (seg, LANES)) for j in range(nseg)], axis=0)
        xw[ci, p] = xp * (jnp.exp(alast - acp) * dtp)
        ea[ci, p] = jnp.exp(acp)
    cs = {}
    if state_t:
        upd = {(ci, p): _dot(bm_t[ci], xw[ci, p].astype(bf16)) for ci, p in items}
        for p in range(PAIRS):
            st = get_state(0, 0, p)
            for ci in range(n):
                cs[ci, p] = _dot(cmb[ci], st.astype(bf16))
                st = st * ea[ci, p][c - 1:c, :] + upd[ci, p]
            set_state(0, 0, p, st)
    else:
        for ci, p in items:
            h0, h1 = 2 * p, 2 * p + 1
            bm, cm = chunks[ci][1], chunks[ci][2]
            parts = []
            for j in range(nseg):
                sl = slice(j * seg, (j + 1) * seg)
                s0 = get_state(ci, j, p)
                parts.append(_dot_nt(_mxu_operand(cm[sl], small), _mxu_operand(s0, small)))
                upd = _dot_tn(_mxu_operand(xw[ci, p][sl], small), _mxu_operand(bm[sl], small))
                last = j * seg + seg - 1
                decay = jnp.concatenate(
                    [jnp.broadcast_to(ea_t[ci][h0:h0 + 1, last:last + 1], (M_HEADDIM, LANES)),
                     jnp.broadcast_to(ea_t[ci][h1:h1 + 1, last:last + 1], (M_HEADDIM, LANES))], axis=0)
                set_state(ci, j, p, s0 * decay + upd)
            cs[ci, p] = parts[0] if nseg == 1 else jnp.concatenate(parts, axis=0)
    outs = []
    for ci in range(n):
        xs, zs = chunks[ci][0], chunks[ci][4]
        ssq = jnp.zeros((c, 1), f32)
        ys = []
        for p in range(PAIRS):
            yp = (y[ci, p] + cs[ci, p] * ea[ci, p] + dsk[:, pc(p)] * xs[:, pc(p)]) * zs[:, pc(p)]
            ssq = ssq + jnp.sum(yp * yp, axis=-1, keepdims=True)
            ys.append(yp)
        scale = lax.rsqrt(ssq * (1.0 / GROUP_W) + EPS)
        outs.append([ys[p] * scale * nw[:, pc(p)] for p in range(PAIRS)])
    return outs


def _conv_taps(xpad_ref, lead, rows, w, b):
    acc = None
    for j in range(M_CONV):
        term = xpad_ref[lead + (slice(5 + j + rows[0], 5 + j + rows[1]), slice(None))] * w[j:j + 1, :]
        acc = term if acc is None else acc + term
    return jax.nn.silu(b + acc)


def _scan_prompt_kernel(*refs, c, rows, n_batch, nlev, gate_tiles):
    hg_in, refs = refs[:8], refs[8:]
    sd_in, refs = refs[:18], refs[18:]
    o_hg_ref, s_hg_ref, o_m_ref, s_m_ref, gate_ref, st_hg_ref, xpad_ref, xc_ref, st_m_ref, wgb_ref = refs
    h_ref, wg_ref = sd_in[16], sd_in[17]
    b_idx, t_blk = pl.program_id(1), pl.program_id(2)
    is_scan = b_idx < n_batch

    @pl.when(jnp.logical_and(is_scan, t_blk == 0))
    def _():
        st_hg_ref[...] = jnp.zeros_like(st_hg_ref)
        st_m_ref[...] = jnp.zeros_like(st_m_ref)
        xpad_ref[0:8, :] = jnp.zeros((8, XBC_W), f32)

    @pl.when(jnp.logical_and(is_scan, t_blk > 0))
    def _():
        xpad_ref[0:8, :] = xpad_ref[rows:rows + 8, :]

    @pl.when(jnp.logical_and(b_idx == 0, t_blk == 0))
    def _():
        wgb_ref[...] = wg_ref[...].astype(bf16)

    @pl.when(is_scan)
    def _():
        _ssd_prompt_main(*sd_in[:16], h_ref, o_m_ref, gate_ref, xpad_ref, xc_ref, st_m_ref, wgb_ref, c=c, rows=rows)
        _hgrn_prompt_main(*hg_in, o_hg_ref, st_hg_ref, c=c, nlev=nlev, n_chunks=rows // c)

    @pl.when(jnp.logical_and(is_scan, t_blk == pl.num_programs(2) - 1))
    def _():
        for hh in range(st_hg_ref.shape[0]):
            s_hg_ref[0, hh] = st_hg_ref[hh].T
        for p in range(PAIRS):
            s_m_ref[0, p * LANES:(p + 1) * LANES, :] = st_m_ref[p].T

    @pl.when(jnp.logical_and(b_idx == n_batch, t_blk < gate_tiles))
    def _():
        gate_ref[...] = _dot_nt(h_ref[...], wgb_ref[...])


def _ssd_prompt_main(xr_ref, br_ref, cr_ref, dt_ref, zs_ref, wx_ref, wb_ref, wc_ref, bx_ref, bb_ref, bc_ref,
                     alog_ref, dsk_ref, nw_ref, lc_ref, tril_ref, h_ref, o_ref, gate_ref,
                     xpad_ref, xc_ref, st_ref, wgb_ref, *, c, rows):
    half = rows // 2

    def gate_half(i):
        rs = slice(i * half, (i + 1) * half)
        gate_ref[rs, :] = _dot_nt(h_ref[rs, :], wgb_ref[...])

    gate_half(0)
    xpad_ref[8:8 + rows, 0:GROUP_W] = xr_ref[...]
    xpad_ref[8:8 + rows, GROUP_W:GROUP_W + M_DSTATE] = br_ref[...]
    xpad_ref[8:8 + rows, GROUP_W + M_DSTATE:XBC_W] = cr_ref[...]
    w = jnp.concatenate([wx_ref[...], wb_ref[...], wc_ref[...]], axis=1)
    b = jnp.concatenate([bx_ref[...], bb_ref[...], bc_ref[...]], axis=1)
    for i in range(rows // c):
        xc_ref[i * c:(i + 1) * c, :] = _conv_taps(xpad_ref, (), (i * c, (i + 1) * c), w, b)

    a_row = -jnp.exp(alog_ref[...])
    dsk = dsk_ref[...]
    nw = nw_ref[...]
    lc = lc_ref[...]
    tril = tril_ref[...] > 0.5

    def get_state(ci, j, p):
        return st_ref[p]

    def set_state(ci, j, p, val):
        st_ref[p] = val

    rs = lambda ci: slice(ci * c, (ci + 1) * c)
    chunks = [(xc_ref[rs(ci), 0:GROUP_W], xc_ref[rs(ci), GROUP_W:GROUP_W + M_DSTATE], xc_ref[rs(ci), GROUP_W + M_DSTATE:XBC_W],
               dt_ref[rs(ci), :], zs_ref[rs(ci), :]) for ci in range(rows // c)]
    outs = _ssd_chunks(chunks, a_row, dsk, nw, lc, tril, get_state, set_state, c, c, state_t=True,
                       after_first_stage=lambda: gate_half(1))
    for ci, out in enumerate(outs):
        for p in range(PAIRS):
            o_ref[rs(ci), p * LANES:(p + 1) * LANES] = out[p].astype(o_ref.dtype)


def _ssd_sample_kernel(xr_ref, br_ref, cr_ref, dt_ref, zs_ref, wx_ref, wb_ref, wc_ref, bx_ref, bb_ref, bc_ref,
                       alog_ref, dsk_ref, nw_ref, lc_ref, tril_ref, hx_ref, hb_ref, hc_ref, s_in_ref,
                       o_ref, s_out_ref, xpad_ref, *, c, seg):
    nseg = c // seg
    n_chunks = xr_ref.shape[0] // c
    w = jnp.concatenate([wx_ref[...], wb_ref[...], wc_ref[...]], axis=1)
    b = jnp.concatenate([bx_ref[...], bb_ref[...], bc_ref[...]], axis=1)
    conv = []
    for j in range(n_chunks * nseg):
        sl = slice(j * seg, (j + 1) * seg)
        xpad_ref[j, 5:8, 0:GROUP_W] = hx_ref[j]
        xpad_ref[j, 5:8, GROUP_W:GROUP_W + M_DSTATE] = hb_ref[j]
        xpad_ref[j, 5:8, GROUP_W + M_DSTATE:XBC_W] = hc_ref[j]
        xpad_ref[j, 8:8 + seg, 0:GROUP_W] = xr_ref[sl, :]
        xpad_ref[j, 8:8 + seg, GROUP_W:GROUP_W + M_DSTATE] = br_ref[sl, :]
        xpad_ref[j, 8:8 + seg, GROUP_W + M_DSTATE:XBC_W] = cr_ref[sl, :]
        conv.append(_conv_taps(xpad_ref, (j,), (0, seg), w, b))

    def get_state(ci, j, p):
        return s_in_ref[ci * nseg + j, p * LANES:(p + 1) * LANES, :]

    def set_state(ci, j, p, val):
        s_out_ref[ci * nseg + j, p * LANES:(p + 1) * LANES, :] = val

    chunks = []
    for ci in range(n_chunks):
        xc = jnp.concatenate(conv[ci * nseg:(ci + 1) * nseg], axis=0)
        rs = slice(ci * c, (ci + 1) * c)
        chunks.append((xc[:, 0:GROUP_W], xc[:, GROUP_W:GROUP_W + M_DSTATE], xc[:, GROUP_W + M_DSTATE:XBC_W],
                       dt_ref[rs, :], zs_ref[rs, :]))
    outs = _ssd_chunks(chunks, -jnp.exp(alog_ref[...]), dsk_ref[...], nw_ref[...], lc_ref[...],
                       tril_ref[...] > 0.5, get_state, set_state, c, seg, state_t=False)
    for ci, out in enumerate(outs):
        for p in range(PAIRS):
            o_ref[ci * c:(ci + 1) * c, p * LANES:(p + 1) * LANES] = out[p].astype(o_ref.dtype)


def _ssd_specs(inner, row_map, nrow, gpos):
    c = CHUNK
    xb_blk = inner // M_DSTATE
    cb_blk = xb_blk + M_GROUPS

    def rm(fn):
        return lambda *ix: fn(row_map(*ix), ix[gpos])
    zero = lambda fn: (lambda *ix: fn(0, ix[gpos]))
    return [
        pl.BlockSpec((nrow, GROUP_W), rm(lambda r, g: (r, g))),
        pl.BlockSpec((nrow, M_DSTATE), rm(lambda r, g: (r, xb_blk + g))),
        pl.BlockSpec((nrow, M_DSTATE), rm(lambda r, g: (r, cb_blk + g))),
        pl.BlockSpec((nrow, LANES), rm(lambda r, g: (r, g))),
        pl.BlockSpec((nrow, GROUP_W), rm(lambda r, g: (r, g))),
        pl.BlockSpec((M_CONV, GROUP_W), zero(lambda r, g: (0, g))),
        pl.BlockSpec((M_CONV, M_DSTATE), zero(lambda r, g: (0, xb_blk + g))),
        pl.BlockSpec((M_CONV, M_DSTATE), zero(lambda r, g: (0, cb_blk + g))),
        pl.BlockSpec((1, GROUP_W), zero(lambda r, g: (0, g))),
        pl.BlockSpec((1, M_DSTATE), zero(lambda r, g: (0, xb_blk + g))),
        pl.BlockSpec((1, M_DSTATE), zero(lambda r, g: (0, cb_blk + g))),
        pl.BlockSpec((1, LANES), zero(lambda r, g: (0, g))),
        pl.BlockSpec((1, GROUP_W), zero(lambda r, g: (0, g))),
        pl.BlockSpec((1, GROUP_W), zero(lambda r, g: (0, g))),
        pl.BlockSpec((c, 3 * c), zero(lambda r, g: (0, 0))),
        pl.BlockSpec((c, 2 * c), zero(lambda r, g: (0, 0))),
    ]


def _scans_prompt(hg_ops, nw_hg, ssd_ops, h, wt, o_gate, gate_cols, n_prompt, batch, seq, rows=512):
    q, g_log, k, v, sg = hg_ops
    xbc, dt, zs, conv_w, conv_b, alog_p, dsk, nw = ssd_ops
    m, width = q.shape
    inner = zs.shape[1]
    heads = width // LANES
    c = CHUNK
    hp = HG_HP
    hw = hp * LANES
    assert heads // hp == M_GROUPS and (m - n_prompt) % rows == 0
    nt = seq // rows
    gate_tiles = (m - n_prompt) // rows
    last = batch * nt - 1
    scan_row = lambda g, b, t: jnp.minimum(b * nt + t, last)
    gate_row = lambda g, b, t: jnp.minimum(b * nt + t, last + gate_tiles)
    last_b = lambda b: jnp.minimum(b, batch - 1)
    mat, masks, nlev = _hgrn_consts(c, c)
    tril = _ssd_consts(c, c)
    hg_row = pl.BlockSpec((rows, hw), lambda g, b, t: (scan_row(g, b, t), g))
    const = lambda shape: pl.BlockSpec(shape, lambda g, b, t: (0,) * len(shape))
    d_model = h.shape[1]
    gw = gate_cols // M_GROUPS
    in_specs = [hg_row] * 5 + [pl.BlockSpec((1, hw), lambda g, b, t: (0, g)),
                               const((mat.shape[0], HG_SPLIT * c)), const(masks.shape)]
    in_specs += _ssd_specs(inner, scan_row, rows, 0)
    in_specs += [
        pl.BlockSpec((rows, d_model), lambda g, b, t: (gate_row(g, b, t), 0)),
        pl.BlockSpec((pl.Element(gw), pl.Element(d_model)), lambda g, b, t: (pl.multiple_of(o_gate + g * gw, 8), 0),
                     pipeline_mode=pl.Buffered(1)),
    ]
    return pl.pallas_call(
        functools.partial(_scan_prompt_kernel, c=c, rows=rows, n_batch=batch, nlev=nlev, gate_tiles=gate_tiles),
        grid=(M_GROUPS, batch + 1, nt),
        in_specs=in_specs,
        out_specs=[hg_row,
                   pl.BlockSpec((1, hp, HG_DK, LANES), lambda g, b, t: (last_b(b), g, 0, 0)),
                   pl.BlockSpec((rows, GROUP_W), lambda g, b, t: (scan_row(g, b, t), g)),
                   pl.BlockSpec((1, GROUP_W, M_DSTATE), lambda g, b, t: (last_b(b), g, 0)),
                   pl.BlockSpec((rows, gw), lambda g, b, t: (gate_row(g, b, t), g))],
        out_shape=[jax.ShapeDtypeStruct((n_prompt, width), bf16),
                   jax.ShapeDtypeStruct((batch, heads, HG_DK, LANES), f32),
                   jax.ShapeDtypeStruct((n_prompt, inner), bf16),
                   jax.ShapeDtypeStruct((batch, inner, M_DSTATE), f32),
                   jax.ShapeDtypeStruct((m, gate_cols), f32)],
        scratch_shapes=[pltpu.VMEM((hp, LANES, HG_DK), f32),
                        pltpu.VMEM((rows + 8, XBC_W), f32), pltpu.VMEM((rows, XBC_W), f32),
                        pltpu.VMEM((PAIRS, LANES, M_DSTATE), f32), pltpu.VMEM((gw, d_model), bf16)],
        compiler_params=_cparams(("arbitrary", "arbitrary", "arbitrary")),
        name="scans_prompt",
    )(q, g_log, k, v, sg, nw_hg.reshape(1, width), _tile3(mat, HG_SPLIT), jnp.asarray(masks),
      xbc, xbc, xbc, dt, zs, conv_w, conv_w, conv_w, conv_b, conv_b, conv_b, alog_p, dsk, nw,
      _tile3(tril), jnp.asarray(np.tile(tril, (1, 2))), h, wt)


def _ssd_sample(xbc, dt, zs, conv_w, conv_b, alog_p, dsk, nw, state, hist, n_prompt, dec_seq):
    m = xbc.shape[0]
    inner = zs.shape[1]
    c = CHUNK
    xb_blk = inner // M_DSTATE
    cb_blk = xb_blk + M_GROUPS
    common = (xbc, xbc, xbc, dt, zs, conv_w, conv_w, conv_w, conv_b, conv_b, conv_b, alog_p, dsk, nw)
    n_sample = m - n_prompt
    rows = 2 * c
    per = rows // dec_seq
    base = n_prompt // rows
    tril = _ssd_consts(c, dec_seq)
    in_specs = _ssd_specs(inner, lambda jb, g: base + jb, rows, 1)
    in_specs += [
        pl.BlockSpec((per, M_CONV - 1, GROUP_W), lambda jb, g: (jb, 0, g)),
        pl.BlockSpec((per, M_CONV - 1, M_DSTATE), lambda jb, g: (jb, 0, xb_blk + g)),
        pl.BlockSpec((per, M_CONV - 1, M_DSTATE), lambda jb, g: (jb, 0, cb_blk + g)),
        pl.BlockSpec((per, GROUP_W, M_DSTATE), lambda jb, g: (jb, g, 0)),
    ]
    o_s, s_s = pl.pallas_call(
        functools.partial(_ssd_sample_kernel, c=c, seg=dec_seq),
        grid=(n_sample // rows, M_GROUPS),
        in_specs=in_specs,
        out_specs=[pl.BlockSpec((rows, GROUP_W), lambda jb, g: (jb, g)),
                   pl.BlockSpec((per, GROUP_W, M_DSTATE), lambda jb, g: (jb, g, 0))],
        out_shape=[jax.ShapeDtypeStruct((n_sample, inner), bf16), jax.ShapeDtypeStruct(state.shape, f32)],
        scratch_shapes=[pltpu.VMEM((per, 16, XBC_W), f32)],
        compiler_params=_cparams(("arbitrary", "arbitrary")),
        name="ssd_sample",
    )(*common, _tile3(tril), jnp.asarray(np.tile(tril, (1, 2))), hist, hist, hist, state)
    return o_s, s_s


def kernel(x_prompt, x_sample, state_hgrn, state_ssm, state_conv, norm_mix, w_in, hg_lb_logits, hg_norm, conv_w, conv_b,
           dt_bias, a_log, d_skip, ssm_norm, w_branch_hg, w_branch_ssm, w_out, norm_ffn, w_ffn_gate, w_ffn_up,
           w_ffn_down, norm_final):
    batch, seq, d = x_prompt.shape
    dec_batch, dec_seq, _ = x_sample.shape
    n_prompt, n_sample = batch * seq, dec_batch * dec_seq
    hg_heads = state_hgrn.shape[2]
    kdim = hg_heads * HG_DK
    vdim = d
    inner = d
    m_heads = state_ssm.shape[2]
    conv_dim = conv_w.shape[2]
    hpg = m_heads // M_GROUPS

    xp2, xs2 = x_prompt.reshape(n_prompt, d), x_sample.reshape(n_sample, d)

    wt = jnp.swapaxes(w_in, 1, 2)[0]
    o_q, o_f, o_v, o_g, o_z, o_xbc = 0, kdim, 2 * kdim, 2 * kdim + vdim, 2 * kdim + 2 * vdim, 2 * kdim + 2 * vdim + inner
    o_dt = o_xbc + conv_dim
    o_gate = o_dt + m_heads
    pad_heads = lambda p: jnp.pad(p.reshape(M_GROUPS, hpg, -1), ((0, 0), (0, LANES - hpg), (0, 0))).reshape(M_GROUPS * LANES, -1)
    h, dt = _prenorm_dt(xp2, xs2, norm_mix[0], pad_heads(wt[o_dt:o_gate]), pad_heads(dt_bias[0]).T)
    scale = HG_DK ** -0.5
    ident = lambda a, c, t: (a[0],)
    silu = lambda a, c, t: (jax.nn.silu(a[0]),)
    wide = dict(tm=1024, tn=2048, w_single=True)
    (q,) = _mm([h], [(wt, o_q, True)], kdim, lambda a, c, t: (a[0] * scale,), [f32], name="proj_q", **wide)
    log_f, k = _mm([h], [(wt, o_f, True)], kdim, _epi_forget, [f32, f32], cols=[hg_lb_logits], name="proj_f",
                   tm=512, tn=2048, w_single=True)
    (v,) = _mm([h], [(wt, o_v, True)], vdim, ident, [bf16], name="proj_v", **wide)
    (sg,) = _mm([h], [(wt, o_g, True)], vdim, silu, [f32], name="proj_g", **wide)
    (zs,) = _mm([h], [(wt, o_z, True)], inner, silu, [f32], name="proj_z", **wide)
    (xbc,) = _mm([h], [(wt, o_xbc, True)], conv_dim, ident, [f32], tm=1024, tn=conv_dim // 2, w_single=True, name="proj_xbc")
    dsk = jnp.repeat(d_skip[0], M_HEADDIM).reshape(1, inner)
    hg_ops = (q, log_f, k, v, sg)
    ssd_ops = (xbc, dt, zs, conv_w[0], conv_b[0].reshape(1, conv_dim), pad_heads(a_log[0]).T, dsk, ssm_norm[0].reshape(1, inner))
    ohp, shp, omp, smp, gates = _scans_prompt(hg_ops, hg_norm[0], ssd_ops, h, wt, o_gate, 2 * d, n_prompt, batch, seq)
    ohs, shs = _hgrn_sample(*hg_ops, hg_norm[0], state_hgrn[0], n_prompt, dec_seq)
    oms, sms = _ssd_sample(*ssd_ops, state_ssm[0].reshape(dec_batch, inner, M_DSTATE), state_conv[0], n_prompt, dec_seq)
    o_hg, o_m = (ohp, ohs), (omp, oms)

    tm, tn = 512, 1024
    (merged,) = _mm([o_hg, o_m], [(w_branch_hg[0], 0, False), (w_branch_ssm[0], 0, False)], d,
                    lambda a, c, t: (jax.nn.sigmoid(t[0]) * a[0] + jax.nn.sigmoid(t[1]) * a[1],), [bf16],
                    tiles=[(gates, lambda j, i: (i, j)), (gates, lambda j, i: (i, j + d // tn))],
                    tm=tm, tn=tn, w_single=True, name="merge")
    tm = 512
    n_p_tiles = n_prompt // tm

    def add_x_norm(a, c, t):
        x1 = jnp.where(pl.program_id(1) < n_p_tiles, t[0], t[1]) + a[0]
        ms = jnp.mean(x1 * x1, axis=-1, keepdims=True)
        return x1, x1 * lax.rsqrt(ms + EPS) * c[0]

    x1, h2 = _mm([merged], [(w_out[0], 0, False)], d, add_x_norm, [f32, bf16], cols=[norm_ffn],
                 tiles=[(xp2, lambda j, i: (jnp.minimum(i, n_p_tiles - 1), j)),
                        (xs2, lambda j, i: (jnp.maximum(i - n_p_tiles, 0), j), pl.Buffered(1))],
                 tm=tm, tn=d, w_single=True, name="out_proj")
    ffn = _ffn(h2, w_ffn_gate[0], w_ffn_up[0], w_ffn_down[0])
    y_p, y_s = _rmsnorm([x1], norm_final, f32, n_prompt, split_out=True, res=ffn)

    hist = M_CONV - 1
    new_conv_p = jnp.stack([xbc[(b + 1) * seq - hist:(b + 1) * seq] for b in range(batch)])[None]
    new_conv_s = xbc.reshape(-1, dec_seq, conv_dim)[n_prompt // dec_seq:, dec_seq - hist:][None]
    return (y_p.reshape(batch, seq, d), y_s.reshape(dec_batch, dec_seq, d),
            shp[None], smp.reshape(1, batch, m_heads, M_HEADDIM, M_DSTATE), new_conv_p,
            shs[None], sms.reshape(1, dec_batch, m_heads, M_HEADDIM, M_DSTATE), new_conv_s)
```

```python
import functools

import numpy as np
import jax
import jax.numpy as jnp
from jax import lax
from jax.experimental import pallas as pl
from jax.experimental.pallas import tpu as pltpu

f32 = jnp.float32
bf16 = jnp.bfloat16

EPS = 1e-6
LANES = 128
CHUNK = 64
HG_DK = 128
M_HEADDIM = 64
M_DSTATE = 128
M_GROUPS = 4
M_CONV = 4
VMEM_LIMIT = 56 * 1024 * 1024


def _cparams(sem):
    return pltpu.CompilerParams(dimension_semantics=sem, vmem_limit_bytes=VMEM_LIMIT)


def _dot(a, b):
    return jnp.dot(a, b, preferred_element_type=f32)


def _dot_nt(a, b):
    return lax.dot_general(a, b, (((1,), (1,)), ((), ())), preferred_element_type=f32)


def _dot_tn(a, b):
    return lax.dot_general(a, b, (((0,), (0,)), ((), ())), preferred_element_type=f32)


def _split(x, n):
    pieces = []
    for _ in range(n - 1):
        p = x.astype(bf16)
        pieces.append(p)
        x = x - p.astype(f32)
    return pieces + [x.astype(bf16)]


def _tiled(m, n=3):
    return jnp.asarray(np.tile(m, (1, n)), bf16)


def _dot3(mn, x):
    return _dot(mn, jnp.concatenate(_split(x, mn.shape[1] // x.shape[0]), axis=0))


def _mxu_operand(x, small):
    xb = x.astype(bf16)
    return xb.astype(f32) if small else xb


def _rmsnorm_kernel(*refs, n_in, n_out, split, has_res):
    x_refs, w_ref, o_refs = refs[:n_in], refs[n_in], refs[n_in + 1 + has_res:]

    def run(x_ref, o_ref):
        x = x_ref[...]
        if has_res:
            x = x + refs[n_in + 1][...]
        ms = jnp.mean(x * x, axis=-1, keepdims=True)
        o_ref[...] = (x * lax.rsqrt(ms + EPS) * w_ref[...]).astype(o_ref.dtype)

    if n_in == 1 and n_out == 1:
        run(x_refs[0], o_refs[0])
    else:
        i = pl.program_id(0)
        pl.when(i < split)(lambda: run(x_refs[0], o_refs[0]))
        pl.when(i >= split)(lambda: run(x_refs[-1], o_refs[-1]))


def _rmsnorm(xs, w, out_dtype, n_first, split_out=False, res=None, tm=512):
    d = xs[0].shape[1]
    m = sum(x.shape[0] for x in xs)
    split = n_first // tm
    first = lambda i: (jnp.minimum(i, split - 1), 0)
    second = lambda i: (jnp.maximum(i - split, 0), 0)
    whole = lambda i: (i, 0)
    blk = lambda fn: pl.BlockSpec((tm, d), fn)
    in_specs = [blk(whole)] if len(xs) == 1 else [blk(first), blk(second)]
    if split_out:
        out_specs = [blk(first), blk(second)]
        out_shape = [jax.ShapeDtypeStruct((n_first, d), out_dtype), jax.ShapeDtypeStruct((m - n_first, d), out_dtype)]
    else:
        out_specs = [blk(whole)]
        out_shape = [jax.ShapeDtypeStruct((m, d), out_dtype)]
    extra = [] if res is None else [res]
    outs = pl.pallas_call(
        functools.partial(_rmsnorm_kernel, n_in=len(xs), n_out=len(out_shape), split=split, has_res=len(extra)),
        grid=(m // tm,),
        in_specs=in_specs + [pl.BlockSpec((1, d), lambda i: (0, 0))] + [blk(whole) for _ in extra],
        out_specs=out_specs,
        out_shape=out_shape,
        compiler_params=_cparams(("arbitrary",)),
        name="rmsnorm",
    )(*xs, w.reshape(1, d), *extra)
    return outs if split_out else outs[0]


def _prenorm_dt_kernel(xp_ref, xs_ref, w_ref, wdt_ref, bias_ref, h_ref, dt_ref, wdtb_ref, *, split):
    i = pl.program_id(0)

    @pl.when(i == 0)
    def _():
        wdtb_ref[...] = wdt_ref[...].astype(bf16)

    def run(x_ref):
        x = x_ref[...]
        ms = jnp.mean(x * x, axis=-1, keepdims=True)
        hb = (x * lax.rsqrt(ms + EPS) * w_ref[...]).astype(bf16)
        h_ref[...] = hb
        dt_ref[...] = _softplus(_dot_nt(hb, wdtb_ref[...]) + bias_ref[...])

    pl.when(i < split)(lambda: run(xp_ref))
    pl.when(i >= split)(lambda: run(xs_ref))


def _prenorm_dt(xp, xs, w, w_dt_t, dt_bias, tm=512):
    d = xp.shape[1]
    m = xp.shape[0] + xs.shape[0]
    n_dt = w_dt_t.shape[0]
    split = xp.shape[0] // tm
    return pl.pallas_call(
        functools.partial(_prenorm_dt_kernel, split=split),
        grid=(m // tm,),
        in_specs=[pl.BlockSpec((tm, d), lambda i: (jnp.minimum(i, split - 1), 0)),
                  pl.BlockSpec((tm, d), lambda i: (jnp.maximum(i - split, 0), 0)),
                  pl.BlockSpec((1, d), lambda i: (0, 0)),
                  pl.BlockSpec((n_dt, d), lambda i: (0, 0)),
                  pl.BlockSpec((1, n_dt), lambda i: (0, 0))],
        out_specs=[pl.BlockSpec((tm, d), lambda i: (i, 0)), pl.BlockSpec((tm, n_dt), lambda i: (i, 0))],
        out_shape=[jax.ShapeDtypeStruct((m, d), bf16), jax.ShapeDtypeStruct((m, n_dt), f32)],
        scratch_shapes=[pltpu.VMEM((n_dt, d), bf16)],
        compiler_params=_cparams(("arbitrary",)),
        name="prenorm_dt",
    )(xp, xs, w.reshape(1, d), w_dt_t, dt_bias)


def _mm_kernel(*refs, a_parts, split, nd, nc, nt, no, epi, w_t):
    a_refs, refs = refs[:sum(a_parts)], refs[sum(a_parts):]
    w_refs = refs[:nd]
    c_refs = refs[nd:nd + nc]
    t_refs = refs[nd + nc:nd + nc + nt]
    o_refs = refs[nd + nc + nt:nd + nc + nt + no]
    wb_refs = refs[nd + nc + nt + no:]

    @pl.when(pl.program_id(1) == 0)
    def _():
        for w_ref, wb_ref in zip(w_refs, wb_refs):
            wb_ref[...] = w_ref[...].astype(bf16)

    a_vals = []
    for parts in a_parts:
        if parts == 2:
            a_vals.append(jnp.where(pl.program_id(1) < split, a_refs[0][...], a_refs[1][...]))
        else:
            a_vals.append(a_refs[0][...])
        a_refs = a_refs[parts:]
    na = len(a_vals)
    accs = [(_dot_nt if t else _dot)(a_vals[min(i, na - 1)], wb_ref[...])
            for i, (wb_ref, t) in enumerate(zip(wb_refs, w_t))]
    outs = epi(accs, [c[...] for c in c_refs], [t[...] for t in t_refs])
    for o_ref, o in zip(o_refs, outs):
        o_ref[...] = o.astype(o_ref.dtype)


def _col_map(j, i, off):
    return (0, j + off)


def _tile_map(j, i, off):
    return (i, j + off)


def _row_elem_map(j, i, off, tn):
    return (pl.multiple_of(off + j * tn, 8), 0)


def _mm(a_list, w_list, n_cols, epi, out_dtypes, cols=(), tiles=(), tm=1024, tn=512, w_single=False, name="mm"):
    na, nd, nc, nt, no = len(a_list), len(w_list), len(cols), len(tiles), len(out_dtypes)
    assert na in (1, nd)
    a_list = [a if isinstance(a, tuple) else (a,) for a in a_list]
    m = sum(p.shape[0] for p in a_list[0])
    split = a_list[0][0].shape[0] // tm
    in_specs, a_flat = [], []
    for parts in a_list:
        k = parts[0].shape[1]
        if len(parts) == 2:
            in_specs.append(pl.BlockSpec((tm, k), lambda j, i: (jnp.minimum(i, split - 1), 0)))
            in_specs.append(pl.BlockSpec((tm, k), lambda j, i: (jnp.maximum(i - split, 0), 0),
                                         pipeline_mode=pl.Buffered(1)))
        else:
            in_specs.append(pl.BlockSpec((tm, k), lambda j, i: (i, 0)))
        a_flat += list(parts)
    scratch = []
    mode = dict(pipeline_mode=pl.Buffered(1)) if w_single else {}
    for w, off, transposed in w_list:
        if transposed:
            k = w.shape[1]
            assert off % 8 == 0
            in_specs.append(pl.BlockSpec((pl.Element(tn), pl.Element(k)), functools.partial(_row_elem_map, off=off, tn=tn), **mode))
            scratch.append(pltpu.VMEM((tn, k), bf16))
        else:
            k = w.shape[0]
            in_specs.append(pl.BlockSpec((k, tn), functools.partial(_col_map, off=off // tn), **mode))
            scratch.append(pltpu.VMEM((k, tn), bf16))
    in_specs += [pl.BlockSpec((c.shape[0], tn), functools.partial(_col_map, off=0)) for c in cols]
    in_specs += [pl.BlockSpec((tm, tn), t[1], **(dict(pipeline_mode=t[2]) if len(t) > 2 else {})) for t in tiles]
    outs = pl.pallas_call(
        functools.partial(_mm_kernel, a_parts=tuple(len(p) for p in a_list), split=split, nd=nd, nc=nc, nt=nt, no=no,
                          epi=epi, w_t=tuple(t for _, _, t in w_list)),
        grid=(n_cols // tn, m // tm),
        in_specs=in_specs,
        out_specs=[pl.BlockSpec((tm, tn), functools.partial(_tile_map, off=0)) for _ in out_dtypes],
        out_shape=[jax.ShapeDtypeStruct((m, n_cols), dt) for dt in out_dtypes],
        scratch_shapes=scratch,
        compiler_params=_cparams(("arbitrary", "arbitrary")),
        name=name,
    )(*a_flat, *[w for w, _, _ in w_list], *cols, *[t[0] for t in tiles])
    return outs


def _ffn_kernel(h_ref, wg_ref, wu_ref, wd_ref, o_ref):
    @pl.when(pl.program_id(1) == 0)
    def _():
        o_ref[...] = jnp.zeros_like(o_ref)

    h = h_ref[...]
    gate = _dot(h, wg_ref[...].astype(bf16))
    up = _dot(h, wu_ref[...].astype(bf16))
    act = (jax.nn.silu(gate) * up).astype(bf16)
    o_ref[...] += _dot(act, wd_ref[...].astype(bf16))


def _ffn(h, w_gate, w_up, w_down, tm=1536, tf=256):
    m, d = h.shape
    hidden = w_gate.shape[1]
    return pl.pallas_call(
        _ffn_kernel,
        grid=(m // tm, hidden // tf),
        in_specs=[pl.BlockSpec((tm, d), lambda i, f: (i, 0), pipeline_mode=pl.Buffered(1)),
                  pl.BlockSpec((d, tf), lambda i, f: (0, f)),
                  pl.BlockSpec((d, tf), lambda i, f: (0, f)),
                  pl.BlockSpec((tf, d), lambda i, f: (f, 0))],
        out_specs=pl.BlockSpec((tm, d), lambda i, f: (i, 0)),
        out_shape=jax.ShapeDtypeStruct((m, d), f32),
        compiler_params=_cparams(("arbitrary", "arbitrary")),
        name="ffn",
    )(h, w_gate, w_up, w_down)


def _softplus(x):
    return jnp.maximum(x, 0.0) + jnp.log1p(jnp.exp(-jnp.abs(x)))


def _epi_forget(accs, cols, tiles):
    logits = cols[0]
    mx = jnp.max(logits, axis=0, keepdims=True)
    e = jnp.exp(logits - mx)
    lb = e[0:1, :] / jnp.sum(e, axis=0, keepdims=True)
    sig = jax.nn.sigmoid(accs[0])
    log_f = jnp.log(lb + (1.0 - lb) * sig)
    k = (1.0 - lb) * (1.0 - sig)
    return log_f, k


HG_SPLIT = 2


def _hgrn_consts(c, seg):
    nlev = int(np.log2(seg))
    mat = np.zeros(((nlev + 2) * c, c), np.float32)
    masks = np.zeros((nlev + 1, c, c), np.float32)
    for t in range(c):
        tl = t % seg
        base = t - tl
        for l in range(nlev):
            h = 1 << l
            pos = tl % (2 * h)
            ref = base + tl - pos + h - 1
            if pos >= h:
                mat[l * c + t, ref + 1:t + 1] = 1
            else:
                mat[l * c + t, t + 1:ref + 1] = 1
            for s in range(base, base + seg):
                sl = s % seg
                if sl // (2 * h) == tl // (2 * h) and pos >= h and sl % (2 * h) < h:
                    masks[l, t, s] = 1
        mat[nlev * c + t, base:t + 1] = 1
        mat[(nlev + 1) * c + t, t + 1:base + seg] = 1
        masks[nlev, t, t] = 1
    return mat, masks, nlev


def _hgrn_chunks(q_ref, k_ref, g_ref, v_ref, mat, masks, nlev, c):
    n_chunks = q_ref.shape[0] // c
    heads = q_ref.shape[1] // LANES
    rows = lambda ci: slice(ci * c, (ci + 1) * c)
    cols = lambda h: slice(h * LANES, (h + 1) * LANES)
    items = [(ci, h) for ci in range(n_chunks) for h in range(heads)]
    e_all = [jnp.exp(_dot3(mat, g_ref[rows(ci), :])) for ci in range(n_chunks)]
    q = {(ci, h): q_ref[rows(ci), cols(h)] for ci, h in items}
    k = {(ci, h): k_ref[rows(ci), cols(h)] for ci, h in items}
    v = {(ci, h): v_ref[rows(ci), cols(h)].astype(bf16) for ci, h in items}
    s = {it: jnp.where(masks[nlev], _dot_nt(q[it].astype(bf16), k[it].astype(bf16)), 0.0) for it in items}
    for l in range(nlev):
        for ci, h in items:
            e = e_all[ci][l * c:(l + 1) * c, cols(h)]
            s[ci, h] = jnp.where(masks[l], _dot_nt((q[ci, h] * e).astype(bf16), (k[ci, h] * e).astype(bf16)), s[ci, h])
    res = {}
    for ci, h in items:
        o = _dot(s[ci, h].astype(bf16), v[ci, h])
        eb = e_all[ci][nlev * c:(nlev + 1) * c, cols(h)]
        er = e_all[ci][(nlev + 1) * c:, cols(h)]
        res[ci, h] = (o, q[ci, h] * eb, k[ci, h] * er, eb, v[ci, h])
    return res


def _hgrn_out(o, sg, nw):
    ms = jnp.mean(o * o, axis=-1, keepdims=True)
    return o * lax.rsqrt(ms + EPS) * nw * sg


def _hgrn_prompt_kernel(q_ref, g_ref, k_ref, v_ref, sg_ref, nw_ref, mat_ref, mask_ref, o_ref, s_ref, st_ref,
                        *, c, nlev, n_chunks):
    t_blk = pl.program_id(2)

    @pl.when(t_blk == 0)
    def _():
        st_ref[...] = jnp.zeros_like(st_ref)

    masks = [mask_ref[l] > 0.5 for l in range(nlev + 1)]
    mat = mat_ref[...]
    nw = nw_ref[...]

    res = _hgrn_chunks(q_ref, k_ref, g_ref, v_ref, mat, masks, nlev, c)
    upd = {it: _dot_tn(r[4], r[2].astype(bf16)) for it, r in res.items()}
    for h in range(st_ref.shape[0]):
        cs = slice(h * LANES, (h + 1) * LANES)
        st = st_ref[h]
        for ci in range(n_chunks):
            rows = slice(ci * c, (ci + 1) * c)
            o, qe, _, eb, _ = res[ci, h]
            o = o + _dot_nt(qe.astype(bf16), st.astype(bf16))
            st = st * eb[c - 1:c, :] + upd[ci, h]
            o_ref[rows, cs] = _hgrn_out(o, sg_ref[rows, cs], nw[:, cs]).astype(o_ref.dtype)
        st_ref[h] = st

    @pl.when(t_blk == pl.num_programs(2) - 1)
    def _():
        for h in range(st_ref.shape[0]):
            s_ref[0, h] = st_ref[h].T


def _hgrn_sample_kernel(q_ref, g_ref, k_ref, v_ref, sg_ref, nw_ref, mat_ref, mask_ref, s_in_ref,
                        o_ref, s_out_ref, *, c, nlev, seg):
    masks = [mask_ref[l] > 0.5 for l in range(nlev + 1)]
    mat = mat_ref[...]
    nw = nw_ref[...]
    small = seg < 16
    per = c // seg
    res = _hgrn_chunks(q_ref, k_ref, g_ref, v_ref, mat, masks, nlev, c)
    for (ci, h), (o, qe, kd, eb, v) in res.items():
        rows = slice(ci * c, (ci + 1) * c)
        cs = slice(h * LANES, (h + 1) * LANES)
        v = v.astype(f32)
        eb_t = jnp.concatenate([eb, eb], axis=0).T
        parts = []
        for j in range(per):
            sl = slice(j * seg, (j + 1) * seg)
            s0 = s_in_ref[ci * per + j, h]
            parts.append(_dot(_mxu_operand(qe[sl], small), _mxu_operand(s0, small)))
            last = j * seg + seg - 1
            upd = _dot_tn(_mxu_operand(kd[sl], small), _mxu_operand(v[sl], small))
            s_out_ref[ci * per + j, h] = s0 * eb_t[:, last:last + 1] + upd
        o = o + jnp.concatenate(parts, axis=0)
        o_ref[rows, cs] = _hgrn_out(o, sg_ref[rows, cs], nw[:, cs]).astype(o_ref.dtype)


def _hgrn(q, g, k, v, sg, nw, state, n_prompt, batch, seq, dec_seq):
    m, width = q.shape
    heads = width // LANES
    nw = nw.reshape(1, width)
    c = CHUNK
    hp = 4
    hw = hp * LANES
    tc = 1024
    nt = seq // tc
    mat, masks, nlev = _hgrn_consts(c, c)
    row_spec = pl.BlockSpec((tc, hw), lambda b, h, t: (b * nt + t, h))
    const2 = lambda shape: pl.BlockSpec(shape, lambda b, h, t: (0,) * len(shape))
    o_p, s_p = pl.pallas_call(
        functools.partial(_hgrn_prompt_kernel, c=c, nlev=nlev, n_chunks=tc // c),
        grid=(batch, heads // hp, nt),
        in_specs=[row_spec] * 5 + [pl.BlockSpec((1, hw), lambda b, h, t: (0, h)), const2((mat.shape[0], HG_SPLIT * c)), const2(masks.shape)],
        out_specs=[row_spec, pl.BlockSpec((1, hp, HG_DK, LANES), lambda b, h, t: (b, h, 0, 0))],
        out_shape=[jax.ShapeDtypeStruct((n_prompt, width), bf16),
                   jax.ShapeDtypeStruct((batch, heads, HG_DK, LANES), f32)],
        scratch_shapes=[pltpu.VMEM((hp, LANES, HG_DK), f32)],
        compiler_params=_cparams(("arbitrary", "arbitrary", "arbitrary")),
        name="hgrn_prompt",
    )(q, g, k, v, sg, nw, _tiled(mat, HG_SPLIT), jnp.asarray(masks))
    n_sample = m - n_prompt
    rows = 2 * c
    per = rows // dec_seq
    mat, masks, nlev = _hgrn_consts(c, dec_seq)
    base = n_prompt // rows
    row_spec = pl.BlockSpec((rows, hw), lambda jb, h: (base + jb, h))
    st_spec = pl.BlockSpec((per, hp, HG_DK, LANES), lambda jb, h: (jb, h, 0, 0))
    const2 = lambda shape: pl.BlockSpec(shape, lambda jb, h: (0,) * len(shape))
    o_s, s_s = pl.pallas_call(
        functools.partial(_hgrn_sample_kernel, c=c, nlev=nlev, seg=dec_seq),
        grid=(n_sample // rows, heads // hp),
        in_specs=[row_spec] * 5 + [pl.BlockSpec((1, hw), lambda jb, h: (0, h)), const2((mat.shape[0], HG_SPLIT * c)),
                                   const2(masks.shape), st_spec],
        out_specs=[pl.BlockSpec((rows, hw), lambda jb, h: (jb, h)), st_spec],
        out_shape=[jax.ShapeDtypeStruct((n_sample, width), bf16), jax.ShapeDtypeStruct(state.shape, f32)],
        compiler_params=_cparams(("arbitrary", "arbitrary")),
        name="hgrn_sample",
    )(q, g, k, v, sg, nw, _tiled(mat, HG_SPLIT), jnp.asarray(masks), state)
    return (o_p, o_s), s_p, s_s


GROUP_W = 512
PAIRS = GROUP_W // LANES
XBC_W = GROUP_W + 2 * M_DSTATE


def _ssd_consts(c, seg):
    t = np.arange(c)
    same = (t[:, None] // seg) == (t[None, :] // seg)
    tril = (same & (t[None, :] <= t[:, None])).astype(np.float32)
    return tril


def _ssd_chunks(chunks, a_row, dsk, nw, lc, tril, get_state, set_state, c, seg, state_t, after_first_stage=None):
    nseg = c // seg
    small = seg < 16
    assert 2 * c == LANES and 2 * M_HEADDIM == LANES and not (state_t and nseg > 1)
    lo_half = lax.broadcasted_iota(jnp.int32, (c, LANES), 1) < M_HEADDIM
    lo_row = lo_half[0:1, :]
    n = len(chunks)
    items = [(ci, p) for ci in range(n) for p in range(PAIRS)]
    pc = lambda p: slice(p * LANES, (p + 1) * LANES)
    acum = [_dot3(lc, dt * a_row) for _, _, _, dt, _ in chunks]
    bmb = [bm.astype(bf16) for _, bm, _, _, _ in chunks]
    cmb = [cm.astype(bf16) for _, _, cm, _, _ in chunks]
    cb2 = [_dot_nt(cmb[ci], jnp.concatenate([bmb[ci], bmb[ci]], axis=0)) for ci in range(n)]
    acum_t = [jnp.concatenate([a, a], axis=0).T for a in acum]
    dt_t = [jnp.concatenate([ch[3], ch[3]], axis=0).T for ch in chunks]
    if state_t:
        bm_t = [ch[1].T.astype(bf16) for ch in chunks]
    else:
        ea_t = [jnp.exp(a[0:8, 0:c]) for a in acum_t]
    if after_first_stage is not None:
        after_first_stage()
    y, xw, ea = {}, {}, {}
    for ci, p in items:
        h0, h1 = 2 * p, 2 * p + 1
        xp, dt = chunks[ci][0][:, pc(p)], chunks[ci][3]
        acp = jnp.where(lo_half, acum[ci][:, h0:h0 + 1], acum[ci][:, h1:h1 + 1])
        dtp = jnp.where(lo_half, dt[:, h0:h0 + 1], dt[:, h1:h1 + 1])
        a_src = jnp.where(lo_row, acum_t[ci][h0:h0 + 1, :], acum_t[ci][h1:h1 + 1, :])
        dt_src = jnp.where(lo_row, dt_t[ci][h0:h0 + 1, :], dt_t[ci][h1:h1 + 1, :])
        lm = jnp.exp(jnp.where(tril, acp - a_src, -1e30))
        sc = (cb2[ci] * lm * dt_src).astype(bf16)
        x_blk = jnp.concatenate([jnp.where(lo_half, xp, 0.0), jnp.where(lo_half, 0.0, xp)], axis=0).astype(bf16)
        y[ci, p] = _dot(sc, x_blk)
        if nseg == 1:
            alast = acp[c - 1:c, :]
        else:
            alast = jnp.concatenate(
                [jnp.broadcast_to(acp[j * seg + seg - 1:j * seg + seg, :], (seg, LANES)) for j in range(nseg)], axis=0)
        xw[ci, p] = xp * (jnp.exp(alast - acp) * dtp)
        ea[ci, p] = jnp.exp(acp)
    cs = {}
    if state_t:
        upd = {(ci, p): _dot(bm_t[ci], xw[ci, p].astype(bf16)) for ci, p in items}
        for p in range(PAIRS):
            st = get_state(0, 0, p)
            for ci in range(n):
                cs[ci, p] = _dot(cmb[ci], st.astype(bf16))
                st = st * ea[ci, p][c - 1:c, :] + upd[ci, p]
            set_state(0, 0, p, st)
    else:
        for ci, p in items:
            h0, h1 = 2 * p, 2 * p + 1
            bm, cm = chunks[ci][1], chunks[ci][2]
            parts = []
            for j in range(nseg):
                sl = slice(j * seg, (j + 1) * seg)
                s0 = get_state(ci, j, p)
                parts.append(_dot_nt(_mxu_operand(cm[sl], small), _mxu_operand(s0, small)))
                upd = _dot_tn(_mxu_operand(xw[ci, p][sl], small), _mxu_operand(bm[sl], small))
                last = j * seg + seg - 1
                decay = jnp.concatenate(
                    [jnp.broadcast_to(ea_t[ci][h0:h0 + 1, last:last + 1], (M_HEADDIM, LANES)),
                     jnp.broadcast_to(ea_t[ci][h1:h1 + 1, last:last + 1], (M_HEADDIM, LANES))], axis=0)
                set_state(ci, j, p, s0 * decay + upd)
            cs[ci, p] = parts[0] if nseg == 1 else jnp.concatenate(parts, axis=0)
    outs = []
    for ci in range(n):
        xs, zs = chunks[ci][0], chunks[ci][4]
        ssq = jnp.zeros((c, 1), f32)
        ys = []
        for p in range(PAIRS):
            yp = (y[ci, p] + cs[ci, p] * ea[ci, p] + dsk[:, pc(p)] * xs[:, pc(p)]) * zs[:, pc(p)]
            ssq = ssq + jnp.sum(yp * yp, axis=-1, keepdims=True)
            ys.append(yp)
        scale = lax.rsqrt(ssq * (1.0 / GROUP_W) + EPS)
        outs.append([ys[p] * scale * nw[:, pc(p)] for p in range(PAIRS)])
    return outs


def _conv_taps(xpad_ref, lead, rows, w, b):
    acc = None
    for j in range(M_CONV):
        term = xpad_ref[lead + (slice(5 + j + rows[0], 5 + j + rows[1]), slice(None))] * w[j:j + 1, :]
        acc = term if acc is None else acc + term
    return jax.nn.silu(b + acc)


def _ssd_prompt_kernel(*refs, c, rows, n_batch):
    h_ref, gate_ref, wgb_ref = refs[16], refs[20], refs[24]
    pl.when(pl.program_id(1) < n_batch)(lambda: _ssd_prompt_step(*refs, c=c, rows=rows))

    @pl.when(jnp.logical_and(pl.program_id(1) == n_batch, pl.program_id(2) == 0))
    def _():
        gate_ref[...] = _dot_nt(h_ref[...], wgb_ref[...])


def _ssd_prompt_step(xr_ref, br_ref, cr_ref, dt_ref, zs_ref, wx_ref, wb_ref, wc_ref, bx_ref, bb_ref, bc_ref,
                     alog_ref, dsk_ref, nw_ref, lc_ref, tril_ref, h_ref, wg_ref, o_ref, s_ref, gate_ref,
                     xpad_ref, xc_ref, st_ref, wgb_ref, *, c, rows):
    t_blk = pl.program_id(2)

    @pl.when(t_blk == 0)
    def _():
        st_ref[...] = jnp.zeros_like(st_ref)
        xpad_ref[0:8, :] = jnp.zeros((8, XBC_W), f32)

    @pl.when(jnp.logical_and(t_blk == 0, pl.program_id(1) == 0))
    def _():
        wgb_ref[...] = wg_ref[...].astype(bf16)

    @pl.when(t_blk > 0)
    def _():
        xpad_ref[0:8, :] = xpad_ref[rows:rows + 8, :]

    half = rows // 2

    def gate_half(i):
        rs = slice(i * half, (i + 1) * half)
        gate_ref[rs, :] = _dot_nt(h_ref[rs, :], wgb_ref[...])

    gate_half(0)
    xpad_ref[8:8 + rows, 0:GROUP_W] = xr_ref[...]
    xpad_ref[8:8 + rows, GROUP_W:GROUP_W + M_DSTATE] = br_ref[...]
    xpad_ref[8:8 + rows, GROUP_W + M_DSTATE:XBC_W] = cr_ref[...]
    w = jnp.concatenate([wx_ref[...], wb_ref[...], wc_ref[...]], axis=1)
    b = jnp.concatenate([bx_ref[...], bb_ref[...], bc_ref[...]], axis=1)
    for i in range(rows // c):
        xc_ref[i * c:(i + 1) * c, :] = _conv_taps(xpad_ref, (), (i * c, (i + 1) * c), w, b)

    a_row = -jnp.exp(alog_ref[...])
    dsk = dsk_ref[...]
    nw = nw_ref[...]
    lc = lc_ref[...]
    tril = tril_ref[...] > 0.5

    def get_state(ci, j, p):
        return st_ref[p]

    def set_state(ci, j, p, val):
        st_ref[p] = val

    rs = lambda ci: slice(ci * c, (ci + 1) * c)
    chunks = [(xc_ref[rs(ci), 0:GROUP_W], xc_ref[rs(ci), GROUP_W:GROUP_W + M_DSTATE], xc_ref[rs(ci), GROUP_W + M_DSTATE:XBC_W],
               dt_ref[rs(ci), :], zs_ref[rs(ci), :]) for ci in range(rows // c)]
    outs = _ssd_chunks(chunks, a_row, dsk, nw, lc, tril, get_state, set_state, c, c, state_t=True,
                       after_first_stage=lambda: gate_half(1))
    for ci, out in enumerate(outs):
        for p in range(PAIRS):
            o_ref[rs(ci), p * LANES:(p + 1) * LANES] = out[p].astype(o_ref.dtype)

    @pl.when(t_blk == pl.num_programs(2) - 1)
    def _():
        for p in range(PAIRS):
            s_ref[0, p * LANES:(p + 1) * LANES, :] = st_ref[p].T


def _ssd_sample_kernel(xr_ref, br_ref, cr_ref, dt_ref, zs_ref, wx_ref, wb_ref, wc_ref, bx_ref, bb_ref, bc_ref,
                       alog_ref, dsk_ref, nw_ref, lc_ref, tril_ref, hx_ref, hb_ref, hc_ref, s_in_ref,
                       o_ref, s_out_ref, xpad_ref, *, c, seg):
    nseg = c // seg
    n_chunks = xr_ref.shape[0] // c
    w = jnp.concatenate([wx_ref[...], wb_ref[...], wc_ref[...]], axis=1)
    b = jnp.concatenate([bx_ref[...], bb_ref[...], bc_ref[...]], axis=1)
    conv = []
    for j in range(n_chunks * nseg):
        sl = slice(j * seg, (j + 1) * seg)
        xpad_ref[j, 5:8, 0:GROUP_W] = hx_ref[j]
        xpad_ref[j, 5:8, GROUP_W:GROUP_W + M_DSTATE] = hb_ref[j]
        xpad_ref[j, 5:8, GROUP_W + M_DSTATE:XBC_W] = hc_ref[j]
        xpad_ref[j, 8:8 + seg, 0:GROUP_W] = xr_ref[sl, :]
        xpad_ref[j, 8:8 + seg, GROUP_W:GROUP_W + M_DSTATE] = br_ref[sl, :]
        xpad_ref[j, 8:8 + seg, GROUP_W + M_DSTATE:XBC_W] = cr_ref[sl, :]
        conv.append(_conv_taps(xpad_ref, (j,), (0, seg), w, b))

    def get_state(ci, j, p):
        return s_in_ref[ci * nseg + j, p * LANES:(p + 1) * LANES, :]

    def set_state(ci, j, p, val):
        s_out_ref[ci * nseg + j, p * LANES:(p + 1) * LANES, :] = val

    chunks = []
    for ci in range(n_chunks):
        xc = jnp.concatenate(conv[ci * nseg:(ci + 1) * nseg], axis=0)
        rs = slice(ci * c, (ci + 1) * c)
        chunks.append((xc[:, 0:GROUP_W], xc[:, GROUP_W:GROUP_W + M_DSTATE], xc[:, GROUP_W + M_DSTATE:XBC_W],
                       dt_ref[rs, :], zs_ref[rs, :]))
    outs = _ssd_chunks(chunks, -jnp.exp(alog_ref[...]), dsk_ref[...], nw_ref[...], lc_ref[...],
                       tril_ref[...] > 0.5, get_state, set_state, c, seg, state_t=False)
    for ci, out in enumerate(outs):
        for p in range(PAIRS):
            o_ref[ci * c:(ci + 1) * c, p * LANES:(p + 1) * LANES] = out[p].astype(o_ref.dtype)


def _ssd(xbc, dt, zs, conv_w, conv_b, alog_p, dsk, nw, state, hist, n_prompt, batch, seq, dec_seq, h, wt, o_gate, gate_cols):
    m = xbc.shape[0]
    inner = zs.shape[1]
    c = CHUNK
    xb_blk = inner // M_DSTATE
    cb_blk = xb_blk + M_GROUPS

    def specs(row_map, nrow, gpos):
        def rm(fn):
            return lambda *ix: fn(row_map(*ix), ix[gpos])
        zero = lambda fn: (lambda *ix: fn(0, ix[gpos]))
        return [
            pl.BlockSpec((nrow, GROUP_W), rm(lambda r, g: (r, g))),
            pl.BlockSpec((nrow, M_DSTATE), rm(lambda r, g: (r, xb_blk + g))),
            pl.BlockSpec((nrow, M_DSTATE), rm(lambda r, g: (r, cb_blk + g))),
            pl.BlockSpec((nrow, LANES), rm(lambda r, g: (r, g))),
            pl.BlockSpec((nrow, GROUP_W), rm(lambda r, g: (r, g))),
            pl.BlockSpec((M_CONV, GROUP_W), zero(lambda r, g: (0, g))),
            pl.BlockSpec((M_CONV, M_DSTATE), zero(lambda r, g: (0, xb_blk + g))),
            pl.BlockSpec((M_CONV, M_DSTATE), zero(lambda r, g: (0, cb_blk + g))),
            pl.BlockSpec((1, GROUP_W), zero(lambda r, g: (0, g))),
            pl.BlockSpec((1, M_DSTATE), zero(lambda r, g: (0, xb_blk + g))),
            pl.BlockSpec((1, M_DSTATE), zero(lambda r, g: (0, cb_blk + g))),
            pl.BlockSpec((1, LANES), zero(lambda r, g: (0, g))),
            pl.BlockSpec((1, GROUP_W), zero(lambda r, g: (0, g))),
            pl.BlockSpec((1, GROUP_W), zero(lambda r, g: (0, g))),
            pl.BlockSpec((c, 3 * c), zero(lambda r, g: (0, 0))),
            pl.BlockSpec((c, 2 * c), zero(lambda r, g: (0, 0))),
        ]

    common = (xbc, xbc, xbc, dt, zs, conv_w, conv_w, conv_w, conv_b, conv_b, conv_b, alog_p, dsk, nw)
    rows = 1024
    nt = seq // rows
    tril = _ssd_consts(c, c)
    assert m - n_prompt == rows
    last = batch * nt - 1
    scan_row = lambda g, b, t: jnp.minimum(b * nt + t, last)
    gate_row = lambda g, b, t: jnp.minimum(b * nt + t, last + 1)
    in_specs = specs(scan_row, rows, 0)
    d_model = h.shape[1]
    gw = gate_cols // M_GROUPS
    in_specs += [
        pl.BlockSpec((rows, d_model), lambda g, b, t: (gate_row(g, b, t), 0)),
        pl.BlockSpec((pl.Element(gw), pl.Element(d_model)), lambda g, b, t: (pl.multiple_of(o_gate + g * gw, 8), 0),
                     pipeline_mode=pl.Buffered(1)),
    ]
    o_p, s_p, gates = pl.pallas_call(
        functools.partial(_ssd_prompt_kernel, c=c, rows=rows, n_batch=batch),
        grid=(M_GROUPS, batch + 1, nt),
        in_specs=in_specs,
        out_specs=[pl.BlockSpec((rows, GROUP_W), lambda g, b, t: (scan_row(g, b, t), g)),
                   pl.BlockSpec((1, GROUP_W, M_DSTATE), lambda g, b, t: (jnp.minimum(b, batch - 1), g, 0)),
                   pl.BlockSpec((rows, gw), lambda g, b, t: (gate_row(g, b, t), g))],
        out_shape=[jax.ShapeDtypeStruct((n_prompt, inner), bf16),
                   jax.ShapeDtypeStruct((batch, inner, M_DSTATE), f32),
                   jax.ShapeDtypeStruct((m, gate_cols), f32)],
        scratch_shapes=[pltpu.VMEM((rows + 8, XBC_W), f32), pltpu.VMEM((rows, XBC_W), f32),
                        pltpu.VMEM((PAIRS, LANES, M_DSTATE), f32), pltpu.VMEM((gw, d_model), bf16)],
        compiler_params=_cparams(("arbitrary", "arbitrary", "arbitrary")),
        name="ssd_prompt",
    )(*common, _tiled(tril), jnp.asarray(np.tile(tril, (1, 2))), h, wt)
    n_sample = m - n_prompt
    rows = 2 * c
    per = rows // dec_seq
    base = n_prompt // rows
    tril = _ssd_consts(c, dec_seq)
    in_specs = specs(lambda jb, g: base + jb, rows, 1)
    in_specs += [
        pl.BlockSpec((per, M_CONV - 1, GROUP_W), lambda jb, g: (jb, 0, g)),
        pl.BlockSpec((per, M_CONV - 1, M_DSTATE), lambda jb, g: (jb, 0, xb_blk + g)),
        pl.BlockSpec((per, M_CONV - 1, M_DSTATE), lambda jb, g: (jb, 0, cb_blk + g)),
        pl.BlockSpec((per, GROUP_W, M_DSTATE), lambda jb, g: (jb, g, 0)),
    ]
    o_s, s_s = pl.pallas_call(
        functools.partial(_ssd_sample_kernel, c=c, seg=dec_seq),
        grid=(n_sample // rows, M_GROUPS),
        in_specs=in_specs,
        out_specs=[pl.BlockSpec((rows, GROUP_W), lambda jb, g: (jb, g)),
                   pl.BlockSpec((per, GROUP_W, M_DSTATE), lambda jb, g: (jb, g, 0))],
        out_shape=[jax.ShapeDtypeStruct((n_sample, inner), bf16), jax.ShapeDtypeStruct(state.shape, f32)],
        scratch_shapes=[pltpu.VMEM((per, 16, XBC_W), f32)],
        compiler_params=_cparams(("arbitrary", "arbitrary")),
        name="ssd_sample",
    )(*common, _tiled(tril), jnp.asarray(np.tile(tril, (1, 2))), hist, hist, hist, state)
    return (o_p, o_s), s_p, s_s, gates


def kernel(x_prompt, x_sample, state_hgrn, state_ssm, state_conv, norm_mix, w_in, hg_lb_logits, hg_norm, conv_w, conv_b,
           dt_bias, a_log, d_skip, ssm_norm, w_branch_hg, w_branch_ssm, w_out, norm_ffn, w_ffn_gate, w_ffn_up,
           w_ffn_down, norm_final):
    batch, seq, d = x_prompt.shape
    dec_batch, dec_seq, _ = x_sample.shape
    n_prompt, n_sample = batch * seq, dec_batch * dec_seq
    hg_heads = state_hgrn.shape[2]
    kdim = hg_heads * HG_DK
    vdim = d
    inner = d
    m_heads = state_ssm.shape[2]
    conv_dim = conv_w.shape[2]
    hpg = m_heads // M_GROUPS

    xp2, xs2 = x_prompt.reshape(n_prompt, d), x_sample.reshape(n_sample, d)

    wt = jnp.swapaxes(w_in, 1, 2)[0]
    o_q, o_f, o_v, o_g, o_z, o_xbc = 0, kdim, 2 * kdim, 2 * kdim + vdim, 2 * kdim + 2 * vdim, 2 * kdim + 2 * vdim + inner
    o_dt = o_xbc + conv_dim
    o_gate = o_dt + m_heads
    pad_heads = lambda p: jnp.pad(p.reshape(M_GROUPS, hpg, -1), ((0, 0), (0, LANES - hpg), (0, 0))).reshape(M_GROUPS * LANES, -1)
    h, dt = _prenorm_dt(xp2, xs2, norm_mix[0], pad_heads(wt[o_dt:o_gate]), pad_heads(dt_bias[0]).T)
    scale = HG_DK ** -0.5
    ident = lambda a, c, t: (a[0],)
    silu = lambda a, c, t: (jax.nn.silu(a[0]),)
    wide = dict(tm=1024, tn=2048, w_single=True)
    (q,) = _mm([h], [(wt, o_q, True)], kdim, lambda a, c, t: (a[0] * scale,), [f32], name="proj_q", **wide)
    log_f, k = _mm([h], [(wt, o_f, True)], kdim, _epi_forget, [f32, f32], cols=[hg_lb_logits], name="proj_f",
                   tm=512, tn=2048, w_single=True)
    (v,) = _mm([h], [(wt, o_v, True)], vdim, ident, [bf16], name="proj_v", **wide)
    (sg,) = _mm([h], [(wt, o_g, True)], vdim, silu, [f32], name="proj_g", **wide)
    (zs,) = _mm([h], [(wt, o_z, True)], inner, silu, [f32], name="proj_z", **wide)
    (xbc,) = _mm([h], [(wt, o_xbc, True)], conv_dim, ident, [f32], tm=1024, tn=conv_dim // 2, w_single=True, name="proj_xbc")
    o_hg, shp, shs = _hgrn(q, log_f, k, v, sg, hg_norm[0], state_hgrn[0], n_prompt, batch, seq, dec_seq)
    dsk = jnp.repeat(d_skip[0], M_HEADDIM).reshape(1, inner)
    o_m, smp, sms, gates = _ssd(xbc, dt, zs, conv_w[0], conv_b[0].reshape(1, conv_dim), pad_heads(a_log[0]).T, dsk,
                                ssm_norm[0].reshape(1, inner), state_ssm[0].reshape(dec_batch, inner, M_DSTATE),
                                state_conv[0], n_prompt, batch, seq, dec_seq, h, wt, o_gate, 2 * d)

    tm, tn = 512, 1024
    (merged,) = _mm([o_hg, o_m], [(w_branch_hg[0], 0, False), (w_branch_ssm[0], 0, False)], d,
                    lambda a, c, t: (jax.nn.sigmoid(t[0]) * a[0] + jax.nn.sigmoid(t[1]) * a[1],), [bf16],
                    tiles=[(gates, lambda j, i: (i, j)), (gates, lambda j, i: (i, j + d // tn))],
                    tm=tm, tn=tn, w_single=True, name="merge")
    tm = 512
    n_p_tiles = n_prompt // tm

    def add_x_norm(a, c, t):
        x1 = jnp.where(pl.program_id(1) < n_p_tiles, t[0], t[1]) + a[0]
        ms = jnp.mean(x1 * x1, axis=-1, keepdims=True)
        return x1, x1 * lax.rsqrt(ms + EPS) * c[0]

    x1, h2 = _mm([merged], [(w_out[0], 0, False)], d, add_x_norm, [f32, bf16], cols=[norm_ffn],
                 tiles=[(xp2, lambda j, i: (jnp.minimum(i, n_p_tiles - 1), j)),
                        (xs2, lambda j, i: (jnp.maximum(i - n_p_tiles, 0), j), pl.Buffered(1))],
                 tm=tm, tn=d, w_single=True, name="out_proj")
    ffn = _ffn(h2, w_ffn_gate[0], w_ffn_up[0], w_ffn_down[0])
    y_p, y_s = _rmsnorm([x1], norm_final, f32, n_prompt, split_out=True, res=ffn)

    hist = M_CONV - 1
    new_conv_p = jnp.stack([xbc[(b + 1) * seq - hist:(b + 1) * seq] for b in range(batch)])[None]
    new_conv_s = xbc.reshape(-1, dec_seq, conv_dim)[n_prompt // dec_seq:, dec_seq - hist:][None]
    return (y_p.reshape(batch, seq, d), y_s.reshape(dec_batch, dec_seq, d),
            shp[None], smp.reshape(1, batch, m_heads, M_HEADDIM, M_DSTATE), new_conv_p,
            shs[None], sms.reshape(1, dec_batch, m_heads, M_HEADDIM, M_DSTATE), new_conv_s)
```

```python
import functools

import numpy as np
import jax
import jax.numpy as jnp
from jax import lax
from jax.experimental import pallas as pl
from jax.experimental.pallas import tpu as pltpu

f32 = jnp.float32
bf16 = jnp.bfloat16

EPS = 1e-6
LANES = 128
CHUNK = 64
HG_DK = 128
M_HEADDIM = 64
M_DSTATE = 128
M_GROUPS = 4
M_CONV = 4
VMEM_LIMIT = 56 * 1024 * 1024


def _cparams(sem):
    return pltpu.CompilerParams(dimension_semantics=sem, vmem_limit_bytes=VMEM_LIMIT)


def _dot(a, b):
    return jnp.dot(a, b, preferred_element_type=f32)


def _dot_nt(a, b):
    return lax.dot_general(a, b, (((1,), (1,)), ((), ())), preferred_element_type=f32)


def _dot_tn(a, b):
    return lax.dot_general(a, b, (((0,), (0,)), ((), ())), preferred_element_type=f32)


def _split(x, n):
    pieces = []
    for _ in range(n - 1):
        p = x.astype(bf16)
        pieces.append(p)
        x = x - p.astype(f32)
    return pieces + [x.astype(bf16)]


def _tiled(m, n=3):
    return jnp.asarray(np.tile(m, (1, n)), bf16)


def _dot3(mn, x):
    return _dot(mn, jnp.concatenate(_split(x, mn.shape[1] // x.shape[0]), axis=0))


def _mxu_operand(x, small):
    xb = x.astype(bf16)
    return xb.astype(f32) if small else xb


def _rmsnorm_kernel(*refs, n_in, n_out, split, has_res):
    x_refs, w_ref, o_refs = refs[:n_in], refs[n_in], refs[n_in + 1 + has_res:]

    def run(x_ref, o_ref):
        x = x_ref[...]
        if has_res:
            x = x + refs[n_in + 1][...]
        ms = jnp.mean(x * x, axis=-1, keepdims=True)
        o_ref[...] = (x * lax.rsqrt(ms + EPS) * w_ref[...]).astype(o_ref.dtype)

    if n_in == 1 and n_out == 1:
        run(x_refs[0], o_refs[0])
    else:
        i = pl.program_id(0)
        pl.when(i < split)(lambda: run(x_refs[0], o_refs[0]))
        pl.when(i >= split)(lambda: run(x_refs[-1], o_refs[-1]))


def _rmsnorm(xs, w, out_dtype, n_first, split_out=False, res=None, tm=512):
    d = xs[0].shape[1]
    m = sum(x.shape[0] for x in xs)
    split = n_first // tm
    first = lambda i: (jnp.minimum(i, split - 1), 0)
    second = lambda i: (jnp.maximum(i - split, 0), 0)
    whole = lambda i: (i, 0)
    blk = lambda fn: pl.BlockSpec((tm, d), fn)
    in_specs = [blk(whole)] if len(xs) == 1 else [blk(first), blk(second)]
    if split_out:
        out_specs = [blk(first), blk(second)]
        out_shape = [jax.ShapeDtypeStruct((n_first, d), out_dtype), jax.ShapeDtypeStruct((m - n_first, d), out_dtype)]
    else:
        out_specs = [blk(whole)]
        out_shape = [jax.ShapeDtypeStruct((m, d), out_dtype)]
    extra = [] if res is None else [res]
    outs = pl.pallas_call(
        functools.partial(_rmsnorm_kernel, n_in=len(xs), n_out=len(out_shape), split=split, has_res=len(extra)),
        grid=(m // tm,),
        in_specs=in_specs + [pl.BlockSpec((1, d), lambda i: (0, 0))] + [blk(whole) for _ in extra],
        out_specs=out_specs,
        out_shape=out_shape,
        compiler_params=_cparams(("arbitrary",)),
        name="rmsnorm",
    )(*xs, w.reshape(1, d), *extra)
    return outs if split_out else outs[0]


def _prenorm_dt_kernel(xp_ref, xs_ref, w_ref, wdt_ref, bias_ref, h_ref, dt_ref, wdtb_ref, *, split):
    i = pl.program_id(0)

    @pl.when(i == 0)
    def _():
        wdtb_ref[...] = wdt_ref[...].astype(bf16)

    def run(x_ref):
        x = x_ref[...]
        ms = jnp.mean(x * x, axis=-1, keepdims=True)
        hb = (x * lax.rsqrt(ms + EPS) * w_ref[...]).astype(bf16)
        h_ref[...] = hb
        dt_ref[...] = _softplus(_dot_nt(hb, wdtb_ref[...]) + bias_ref[...])

    pl.when(i < split)(lambda: run(xp_ref))
    pl.when(i >= split)(lambda: run(xs_ref))


def _prenorm_dt(xp, xs, w, w_dt_t, dt_bias, tm=512):
    d = xp.shape[1]
    m = xp.shape[0] + xs.shape[0]
    n_dt = w_dt_t.shape[0]
    split = xp.shape[0] // tm
    return pl.pallas_call(
        functools.partial(_prenorm_dt_kernel, split=split),
        grid=(m // tm,),
        in_specs=[pl.BlockSpec((tm, d), lambda i: (jnp.minimum(i, split - 1), 0)),
                  pl.BlockSpec((tm, d), lambda i: (jnp.maximum(i - split, 0), 0)),
                  pl.BlockSpec((1, d), lambda i: (0, 0)),
                  pl.BlockSpec((n_dt, d), lambda i: (0, 0)),
                  pl.BlockSpec((1, n_dt), lambda i: (0, 0))],
        out_specs=[pl.BlockSpec((tm, d), lambda i: (i, 0)), pl.BlockSpec((tm, n_dt), lambda i: (i, 0))],
        out_shape=[jax.ShapeDtypeStruct((m, d), bf16), jax.ShapeDtypeStruct((m, n_dt), f32)],
        scratch_shapes=[pltpu.VMEM((n_dt, d), bf16)],
        compiler_params=_cparams(("arbitrary",)),
        name="prenorm_dt",
    )(xp, xs, w.reshape(1, d), w_dt_t, dt_bias)


def _mm_kernel(*refs, a_parts, split, nd, nc, nt, no, epi, w_t):
    a_refs, refs = refs[:sum(a_parts)], refs[sum(a_parts):]
    w_refs = refs[:nd]
    c_refs = refs[nd:nd + nc]
    t_refs = refs[nd + nc:nd + nc + nt]
    o_refs = refs[nd + nc + nt:nd + nc + nt + no]
    wb_refs = refs[nd + nc + nt + no:]

    @pl.when(pl.program_id(1) == 0)
    def _():
        for w_ref, wb_ref in zip(w_refs, wb_refs):
            wb_ref[...] = w_ref[...].astype(bf16)

    a_vals = []
    for parts in a_parts:
        if parts == 2:
            a_vals.append(jnp.where(pl.program_id(1) < split, a_refs[0][...], a_refs[1][...]))
        else:
            a_vals.append(a_refs[0][...])
        a_refs = a_refs[parts:]
    na = len(a_vals)
    accs = [(_dot_nt if t else _dot)(a_vals[min(i, na - 1)], wb_ref[...])
            for i, (wb_ref, t) in enumerate(zip(wb_refs, w_t))]
    outs = epi(accs, [c[...] for c in c_refs], [t[...] for t in t_refs])
    for o_ref, o in zip(o_refs, outs):
        o_ref[...] = o.astype(o_ref.dtype)


def _col_map(j, i, off):
    return (0, j + off)


def _tile_map(j, i, off):
    return (i, j + off)


def _row_elem_map(j, i, off, tn):
    return (pl.multiple_of(off + j * tn, 8), 0)


def _mm(a_list, w_list, n_cols, epi, out_dtypes, cols=(), tiles=(), tm=1024, tn=512, w_single=False, name="mm"):
    na, nd, nc, nt, no = len(a_list), len(w_list), len(cols), len(tiles), len(out_dtypes)
    assert na in (1, nd)
    a_list = [a if isinstance(a, tuple) else (a,) for a in a_list]
    m = sum(p.shape[0] for p in a_list[0])
    split = a_list[0][0].shape[0] // tm
    in_specs, a_flat = [], []
    for parts in a_list:
        k = parts[0].shape[1]
        if len(parts) == 2:
            in_specs.append(pl.BlockSpec((tm, k), lambda j, i: (jnp.minimum(i, split - 1), 0)))
            in_specs.append(pl.BlockSpec((tm, k), lambda j, i: (jnp.maximum(i - split, 0), 0),
                                         pipeline_mode=pl.Buffered(1)))
        else:
            in_specs.append(pl.BlockSpec((tm, k), lambda j, i: (i, 0)))
        a_flat += list(parts)
    scratch = []
    mode = dict(pipeline_mode=pl.Buffered(1)) if w_single else {}
    for w, off, transposed in w_list:
        if transposed:
            k = w.shape[1]
            assert off % 8 == 0
            in_specs.append(pl.BlockSpec((pl.Element(tn), pl.Element(k)), functools.partial(_row_elem_map, off=off, tn=tn), **mode))
            scratch.append(pltpu.VMEM((tn, k), bf16))
        else:
            k = w.shape[0]
            in_specs.append(pl.BlockSpec((k, tn), functools.partial(_col_map, off=off // tn), **mode))
            scratch.append(pltpu.VMEM((k, tn), bf16))
    in_specs += [pl.BlockSpec((c.shape[0], tn), functools.partial(_col_map, off=0)) for c in cols]
    in_specs += [pl.BlockSpec((tm, tn), t[1], **(dict(pipeline_mode=t[2]) if len(t) > 2 else {})) for t in tiles]
    outs = pl.pallas_call(
        functools.partial(_mm_kernel, a_parts=tuple(len(p) for p in a_list), split=split, nd=nd, nc=nc, nt=nt, no=no,
                          epi=epi, w_t=tuple(t for _, _, t in w_list)),
        grid=(n_cols // tn, m // tm),
        in_specs=in_specs,
        out_specs=[pl.BlockSpec((tm, tn), functools.partial(_tile_map, off=0)) for _ in out_dtypes],
        out_shape=[jax.ShapeDtypeStruct((m, n_cols), dt) for dt in out_dtypes],
        scratch_shapes=scratch,
        compiler_params=_cparams(("arbitrary", "arbitrary")),
        name=name,
    )(*a_flat, *[w for w, _, _ in w_list], *cols, *[t[0] for t in tiles])
    return outs


def _ffn_kernel(h_ref, wg_ref, wu_ref, wd_ref, o_ref):
    @pl.when(pl.program_id(1) == 0)
    def _():
        o_ref[...] = jnp.zeros_like(o_ref)

    h = h_ref[...]
    gate = _dot(h, wg_ref[...].astype(bf16))
    up = _dot(h, wu_ref[...].astype(bf16))
    act = (jax.nn.silu(gate) * up).astype(bf16)
    o_ref[...] += _dot(act, wd_ref[...].astype(bf16))


def _ffn(h, w_gate, w_up, w_down, tm=1536, tf=256):
    m, d = h.shape
    hidden = w_gate.shape[1]
    return pl.pallas_call(
        _ffn_kernel,
        grid=(m // tm, hidden // tf),
        in_specs=[pl.BlockSpec((tm, d), lambda i, f: (i, 0), pipeline_mode=pl.Buffered(1)),
                  pl.BlockSpec((d, tf), lambda i, f: (0, f)),
                  pl.BlockSpec((d, tf), lambda i, f: (0, f)),
                  pl.BlockSpec((tf, d), lambda i, f: (f, 0))],
        out_specs=pl.BlockSpec((tm, d), lambda i, f: (i, 0)),
        out_shape=jax.ShapeDtypeStruct((m, d), f32),
        compiler_params=_cparams(("arbitrary", "arbitrary")),
        name="ffn",
    )(h, w_gate, w_up, w_down)


def _softplus(x):
    return jnp.maximum(x, 0.0) + jnp.log1p(jnp.exp(-jnp.abs(x)))


def _epi_forget(accs, cols, tiles):
    logits = cols[0]
    mx = jnp.max(logits, axis=0, keepdims=True)
    e = jnp.exp(logits - mx)
    lb = e[0:1, :] / jnp.sum(e, axis=0, keepdims=True)
    sig = jax.nn.sigmoid(accs[0])
    log_f = jnp.log(lb + (1.0 - lb) * sig)
    k = (1.0 - lb) * (1.0 - sig)
    return log_f, k


HG_SPLIT = 2


def _hgrn_consts(c, seg):
    nlev = int(np.log2(seg))
    mat = np.zeros(((nlev + 2) * c, c), np.float32)
    masks = np.zeros((nlev + 1, c, c), np.float32)
    for t in range(c):
        tl = t % seg
        base = t - tl
        for l in range(nlev):
            h = 1 << l
            pos = tl % (2 * h)
            ref = base + tl - pos + h - 1
            if pos >= h:
                mat[l * c + t, ref + 1:t + 1] = 1
            else:
                mat[l * c + t, t + 1:ref + 1] = 1
            for s in range(base, base + seg):
                sl = s % seg
                if sl // (2 * h) == tl // (2 * h) and pos >= h and sl % (2 * h) < h:
                    masks[l, t, s] = 1
        mat[nlev * c + t, base:t + 1] = 1
        mat[(nlev + 1) * c + t, t + 1:base + seg] = 1
        masks[nlev, t, t] = 1
    return mat, masks, nlev


def _hgrn_chunks(q_ref, k_ref, g_ref, v_ref, mat, masks, nlev, c):
    n_chunks = q_ref.shape[0] // c
    heads = q_ref.shape[1] // LANES
    rows = lambda ci: slice(ci * c, (ci + 1) * c)
    cols = lambda h: slice(h * LANES, (h + 1) * LANES)
    items = [(ci, h) for ci in range(n_chunks) for h in range(heads)]
    e_all = [jnp.exp(_dot3(mat, g_ref[rows(ci), :])) for ci in range(n_chunks)]
    q = {(ci, h): q_ref[rows(ci), cols(h)] for ci, h in items}
    k = {(ci, h): k_ref[rows(ci), cols(h)] for ci, h in items}
    v = {(ci, h): v_ref[rows(ci), cols(h)].astype(bf16) for ci, h in items}
    s = {it: jnp.where(masks[nlev], _dot_nt(q[it].astype(bf16), k[it].astype(bf16)), 0.0) for it in items}
    for l in range(nlev):
        for ci, h in items:
            e = e_all[ci][l * c:(l + 1) * c, cols(h)]
            s[ci, h] = jnp.where(masks[l], _dot_nt((q[ci, h] * e).astype(bf16), (k[ci, h] * e).astype(bf16)), s[ci, h])
    res = {}
    for ci, h in items:
        o = _dot(s[ci, h].astype(bf16), v[ci, h])
        eb = e_all[ci][nlev * c:(nlev + 1) * c, cols(h)]
        er = e_all[ci][(nlev + 1) * c:, cols(h)]
        res[ci, h] = (o, q[ci, h] * eb, k[ci, h] * er, eb, v[ci, h])
    return res


def _hgrn_out(o, sg, nw):
    ms = jnp.mean(o * o, axis=-1, keepdims=True)
    return o * lax.rsqrt(ms + EPS) * nw * sg


def _hgrn_prompt_kernel(q_ref, g_ref, k_ref, v_ref, sg_ref, nw_ref, mat_ref, mask_ref, o_ref, s_ref, st_ref,
                        *, c, nlev, n_chunks):
    t_blk = pl.program_id(2)

    @pl.when(t_blk == 0)
    def _():
        st_ref[...] = jnp.zeros_like(st_ref)

    masks = [mask_ref[l] > 0.5 for l in range(nlev + 1)]
    mat = mat_ref[...]
    nw = nw_ref[...]

    res = _hgrn_chunks(q_ref, k_ref, g_ref, v_ref, mat, masks, nlev, c)
    upd = {it: _dot_tn(r[4], r[2].astype(bf16)) for it, r in res.items()}
    for h in range(st_ref.shape[0]):
        cs = slice(h * LANES, (h + 1) * LANES)
        st = st_ref[h]
        for ci in range(n_chunks):
            rows = slice(ci * c, (ci + 1) * c)
            o, qe, _, eb, _ = res[ci, h]
            o = o + _dot_nt(qe.astype(bf16), st.astype(bf16))
            st = st * eb[c - 1:c, :] + upd[ci, h]
            o_ref[rows, cs] = _hgrn_out(o, sg_ref[rows, cs], nw[:, cs]).astype(o_ref.dtype)
        st_ref[h] = st

    @pl.when(t_blk == pl.num_programs(2) - 1)
    def _():
        for h in range(st_ref.shape[0]):
            s_ref[0, h] = st_ref[h].T


def _hgrn_sample_kernel(q_ref, g_ref, k_ref, v_ref, sg_ref, nw_ref, mat_ref, mask_ref, s_in_ref,
                        o_ref, s_out_ref, *, c, nlev, seg):
    masks = [mask_ref[l] > 0.5 for l in range(nlev + 1)]
    mat = mat_ref[...]
    nw = nw_ref[...]
    small = seg < 16
    per = c // seg
    res = _hgrn_chunks(q_ref, k_ref, g_ref, v_ref, mat, masks, nlev, c)
    for (ci, h), (o, qe, kd, eb, v) in res.items():
        rows = slice(ci * c, (ci + 1) * c)
        cs = slice(h * LANES, (h + 1) * LANES)
        v = v.astype(f32)
        eb_t = jnp.concatenate([eb, eb], axis=0).T
        parts = []
        for j in range(per):
            sl = slice(j * seg, (j + 1) * seg)
            s0 = s_in_ref[ci * per + j, h]
            parts.append(_dot(_mxu_operand(qe[sl], small), _mxu_operand(s0, small)))
            last = j * seg + seg - 1
            upd = _dot_tn(_mxu_operand(kd[sl], small), _mxu_operand(v[sl], small))
            s_out_ref[ci * per + j, h] = s0 * eb_t[:, last:last + 1] + upd
        o = o + jnp.concatenate(parts, axis=0)
        o_ref[rows, cs] = _hgrn_out(o, sg_ref[rows, cs], nw[:, cs]).astype(o_ref.dtype)


def _hgrn(q, g, k, v, sg, nw, state, n_prompt, batch, seq, dec_seq):
    m, width = q.shape
    heads = width // LANES
    nw = nw.reshape(1, width)
    c = CHUNK
    hp = 4
    hw = hp * LANES
    tc = 1024
    nt = seq // tc
    mat, masks, nlev = _hgrn_consts(c, c)
    row_spec = pl.BlockSpec((tc, hw), lambda b, h, t: (b * nt + t, h))
    const2 = lambda shape: pl.BlockSpec(shape, lambda b, h, t: (0,) * len(shape))
    o_p, s_p = pl.pallas_call(
        functools.partial(_hgrn_prompt_kernel, c=c, nlev=nlev, n_chunks=tc // c),
        grid=(batch, heads // hp, nt),
        in_specs=[row_spec] * 5 + [pl.BlockSpec((1, hw), lambda b, h, t: (0, h)), const2((mat.shape[0], HG_SPLIT * c)), const2(masks.shape)],
        out_specs=[row_spec, pl.BlockSpec((1, hp, HG_DK, LANES), lambda b, h, t: (b, h, 0, 0))],
        out_shape=[jax.ShapeDtypeStruct((n_prompt, width), bf16),
                   jax.ShapeDtypeStruct((batch, heads, HG_DK, LANES), f32)],
        scratch_shapes=[pltpu.VMEM((hp, LANES, HG_DK), f32)],
        compiler_params=_cparams(("arbitrary", "arbitrary", "arbitrary")),
        name="hgrn_prompt",
    )(q, g, k, v, sg, nw, _tiled(mat, HG_SPLIT), jnp.asarray(masks))
    n_sample = m - n_prompt
    rows = 2 * c
    per = rows // dec_seq
    mat, masks, nlev = _hgrn_consts(c, dec_seq)
    base = n_prompt // rows
    row_spec = pl.BlockSpec((rows, hw), lambda jb, h: (base + jb, h))
    st_spec = pl.BlockSpec((per, hp, HG_DK, LANES), lambda jb, h: (jb, h, 0, 0))
    const2 = lambda shape: pl.BlockSpec(shape, lambda jb, h: (0,) * len(shape))
    o_s, s_s = pl.pallas_call(
        functools.partial(_hgrn_sample_kernel, c=c, nlev=nlev, seg=dec_seq),
        grid=(n_sample // rows, heads // hp),
        in_specs=[row_spec] * 5 + [pl.BlockSpec((1, hw), lambda jb, h: (0, h)), const2((mat.shape[0], HG_SPLIT * c)),
                                   const2(masks.shape), st_spec],
        out_specs=[pl.BlockSpec((rows, hw), lambda jb, h: (jb, h)), st_spec],
        out_shape=[jax.ShapeDtypeStruct((n_sample, width), bf16), jax.ShapeDtypeStruct(state.shape, f32)],
        compiler_params=_cparams(("arbitrary", "arbitrary")),
        name="hgrn_sample",
    )(q, g, k, v, sg, nw, _tiled(mat, HG_SPLIT), jnp.asarray(masks), state)
    return (o_p, o_s), s_p, s_s


GROUP_W = 512
PAIRS = GROUP_W // LANES
XBC_W = GROUP_W + 2 * M_DSTATE


def _ssd_consts(c, seg):
    t = np.arange(c)
    same = (t[:, None] // seg) == (t[None, :] // seg)
    tril = (same & (t[None, :] <= t[:, None])).astype(np.float32)
    return tril


def _ssd_chunks(chunks, a_row, dsk, nw, lc, tril, get_state, set_state, c, seg, state_t, after_first_stage=None):
    nseg = c // seg
    small = seg < 16
    assert 2 * c == LANES and 2 * M_HEADDIM == LANES and not (state_t and nseg > 1)
    lo_half = lax.broadcasted_iota(jnp.int32, (c, LANES), 1) < M_HEADDIM
    lo_row = lo_half[0:1, :]
    n = len(chunks)
    items = [(ci, p) for ci in range(n) for p in range(PAIRS)]
    pc = lambda p: slice(p * LANES, (p + 1) * LANES)
    acum = [_dot3(lc, dt * a_row) for _, _, _, dt, _ in chunks]
    bmb = [bm.astype(bf16) for _, bm, _, _, _ in chunks]
    cmb = [cm.astype(bf16) for _, _, cm, _, _ in chunks]
    cb2 = [_dot_nt(cmb[ci], jnp.concatenate([bmb[ci], bmb[ci]], axis=0)) for ci in range(n)]
    acum_t = [jnp.concatenate([a, a], axis=0).T for a in acum]
    dt_t = [jnp.concatenate([ch[3], ch[3]], axis=0).T for ch in chunks]
    if state_t:
        bm_t = [ch[1].T.astype(bf16) for ch in chunks]
    else:
        ea_t = [jnp.exp(a[0:8, 0:c]) for a in acum_t]
    if after_first_stage is not None:
        after_first_stage()
    y, xw, ea = {}, {}, {}
    for ci, p in items:
        h0, h1 = 2 * p, 2 * p + 1
        xp, dt = chunks[ci][0][:, pc(p)], chunks[ci][3]
        acp = jnp.where(lo_half, acum[ci][:, h0:h0 + 1], acum[ci][:, h1:h1 + 1])
        dtp = jnp.where(lo_half, dt[:, h0:h0 + 1], dt[:, h1:h1 + 1])
        a_src = jnp.where(lo_row, acum_t[ci][h0:h0 + 1, :], acum_t[ci][h1:h1 + 1, :])
        dt_src = jnp.where(lo_row, dt_t[ci][h0:h0 + 1, :], dt_t[ci][h1:h1 + 1, :])
        lm = jnp.exp(jnp.where(tril, acp - a_src, -1e30))
        sc = (cb2[ci] * lm * dt_src).astype(bf16)
        x_blk = jnp.concatenate([jnp.where(lo_half, xp, 0.0), jnp.where(lo_half, 0.0, xp)], axis=0).astype(bf16)
        y[ci, p] = _dot(sc, x_blk)
        if nseg == 1:
            alast = acp[c - 1:c, :]
        else:
            alast = jnp.concatenate(
                [jnp.broadcast_to(acp[j * seg + seg - 1:j * seg + seg, :], (seg, LANES)) for j in range(nseg)], axis=0)
        xw[ci, p] = xp * (jnp.exp(alast - acp) * dtp)
        ea[ci, p] = jnp.exp(acp)
    cs = {}
    if state_t:
        upd = {(ci, p): _dot(bm_t[ci], xw[ci, p].astype(bf16)) for ci, p in items}
        for p in range(PAIRS):
            st = get_state(0, 0, p)
            for ci in range(n):
                cs[ci, p] = _dot(cmb[ci], st.astype(bf16))
                st = st * ea[ci, p][c - 1:c, :] + upd[ci, p]
            set_state(0, 0, p, st)
    else:
        for ci, p in items:
            h0, h1 = 2 * p, 2 * p + 1
            bm, cm = chunks[ci][1], chunks[ci][2]
            parts = []
            for j in range(nseg):
                sl = slice(j * seg, (j + 1) * seg)
                s0 = get_state(ci, j, p)
                parts.append(_dot_nt(_mxu_operand(cm[sl], small), _mxu_operand(s0, small)))
                upd = _dot_tn(_mxu_operand(xw[ci, p][sl], small), _mxu_operand(bm[sl], small))
                last = j * seg + seg - 1
                decay = jnp.concatenate(
                    [jnp.broadcast_to(ea_t[ci][h0:h0 + 1, last:last + 1], (M_HEADDIM, LANES)),
                     jnp.broadcast_to(ea_t[ci][h1:h1 + 1, last:last + 1], (M_HEADDIM, LANES))], axis=0)
                set_state(ci, j, p, s0 * decay + upd)
            cs[ci, p] = parts[0] if nseg == 1 else jnp.concatenate(parts, axis=0)
    outs = []
    for ci in range(n):
        xs, zs = chunks[ci][0], chunks[ci][4]
        ssq = jnp.zeros((c, 1), f32)
        ys = []
        for p in range(PAIRS):
            yp = (y[ci, p] + cs[ci, p] * ea[ci, p] + dsk[:, pc(p)] * xs[:, pc(p)]) * zs[:, pc(p)]
            ssq = ssq + jnp.sum(yp * yp, axis=-1, keepdims=True)
            ys.append(yp)
        scale = lax.rsqrt(ssq * (1.0 / GROUP_W) + EPS)
        outs.append([ys[p] * scale * nw[:, pc(p)] for p in range(PAIRS)])
    return outs


def _conv_taps(xpad_ref, lead, rows, w, b):
    acc = None
    for j in range(M_CONV):
        term = xpad_ref[lead + (slice(5 + j + rows[0], 5 + j + rows[1]), slice(None))] * w[j:j + 1, :]
        acc = term if acc is None else acc + term
    return jax.nn.silu(b + acc)


def _ssd_prompt_kernel(*refs, c, rows, n_batch):
    h_ref, gate_ref, wgb_ref = refs[16], refs[20], refs[24]
    pl.when(pl.program_id(1) < n_batch)(lambda: _ssd_prompt_step(*refs, c=c, rows=rows))

    @pl.when(jnp.logical_and(pl.program_id(1) == n_batch, pl.program_id(2) == 0))
    def _():
        gate_ref[...] = _dot_nt(h_ref[...], wgb_ref[...])


def _ssd_prompt_step(xr_ref, br_ref, cr_ref, dt_ref, zs_ref, wx_ref, wb_ref, wc_ref, bx_ref, bb_ref, bc_ref,
                     alog_ref, dsk_ref, nw_ref, lc_ref, tril_ref, h_ref, wg_ref, o_ref, s_ref, gate_ref,
                     xpad_ref, xc_ref, st_ref, wgb_ref, *, c, rows):
    t_blk = pl.program_id(2)

    @pl.when(t_blk == 0)
    def _():
        st_ref[...] = jnp.zeros_like(st_ref)
        xpad_ref[0:8, :] = jnp.zeros((8, XBC_W), f32)

    @pl.when(jnp.logical_and(t_blk == 0, pl.program_id(1) == 0))
    def _():
        wgb_ref[...] = wg_ref[...].astype(bf16)

    @pl.when(t_blk > 0)
    def _():
        xpad_ref[0:8, :] = xpad_ref[rows:rows + 8, :]

    half = rows // 2

    def gate_half(i):
        rs = slice(i * half, (i + 1) * half)
        gate_ref[rs, :] = _dot_nt(h_ref[rs, :], wgb_ref[...])

    gate_half(0)
    xpad_ref[8:8 + rows, 0:GROUP_W] = xr_ref[...]
    xpad_ref[8:8 + rows, GROUP_W:GROUP_W + M_DSTATE] = br_ref[...]
    xpad_ref[8:8 + rows, GROUP_W + M_DSTATE:XBC_W] = cr_ref[...]
    w = jnp.concatenate([wx_ref[...], wb_ref[...], wc_ref[...]], axis=1)
    b = jnp.concatenate([bx_ref[...], bb_ref[...], bc_ref[...]], axis=1)
    for i in range(rows // c):
        xc_ref[i * c:(i + 1) * c, :] = _conv_taps(xpad_ref, (), (i * c, (i + 1) * c), w, b)

    a_row = -jnp.exp(alog_ref[...])
    dsk = dsk_ref[...]
    nw = nw_ref[...]
    lc = lc_ref[...]
    tril = tril_ref[...] > 0.5

    def get_state(ci, j, p):
        return st_ref[p]

    def set_state(ci, j, p, val):
        st_ref[p] = val

    rs = lambda ci: slice(ci * c, (ci + 1) * c)
    chunks = [(xc_ref[rs(ci), 0:GROUP_W], xc_ref[rs(ci), GROUP_W:GROUP_W + M_DSTATE], xc_ref[rs(ci), GROUP_W + M_DSTATE:XBC_W],
               dt_ref[rs(ci), :], zs_ref[rs(ci), :]) for ci in range(rows // c)]
    outs = _ssd_chunks(chunks, a_row, dsk, nw, lc, tril, get_state, set_state, c, c, state_t=True,
                       after_first_stage=lambda: gate_half(1))
    for ci, out in enumerate(outs):
        for p in range(PAIRS):
            o_ref[rs(ci), p * LANES:(p + 1) * LANES] = out[p].astype(o_ref.dtype)

    @pl.when(t_blk == pl.num_programs(2) - 1)
    def _():
        for p in range(PAIRS):
            s_ref[0, p * LANES:(p + 1) * LANES, :] = st_ref[p].T


def _ssd_sample_kernel(xr_ref, br_ref, cr_ref, dt_ref, zs_ref, wx_ref, wb_ref, wc_ref, bx_ref, bb_ref, bc_ref,
                       alog_ref, dsk_ref, nw_ref, lc_ref, tril_ref, hx_ref, hb_ref, hc_ref, s_in_ref,
                       o_ref, s_out_ref, xpad_ref, *, c, seg):
    nseg = c // seg
    n_chunks = xr_ref.shape[0] // c
    w = jnp.concatenate([wx_ref[...], wb_ref[...], wc_ref[...]], axis=1)
    b = jnp.concatenate([bx_ref[...], bb_ref[...], bc_ref[...]], axis=1)
    conv = []
    for j in range(n_chunks * nseg):
        sl = slice(j * seg, (j + 1) * seg)
        xpad_ref[j, 5:8, 0:GROUP_W] = hx_ref[j]
        xpad_ref[j, 5:8, GROUP_W:GROUP_W + M_DSTATE] = hb_ref[j]
        xpad_ref[j, 5:8, GROUP_W + M_DSTATE:XBC_W] = hc_ref[j]
        xpad_ref[j, 8:8 + seg, 0:GROUP_W] = xr_ref[sl, :]
        xpad_ref[j, 8:8 + seg, GROUP_W:GROUP_W + M_DSTATE] = br_ref[sl, :]
        xpad_ref[j, 8:8 + seg, GROUP_W + M_DSTATE:XBC_W] = cr_ref[sl, :]
        conv.append(_conv_taps(xpad_ref, (j,), (0, seg), w, b))

    def get_state(ci, j, p):
        return s_in_ref[ci * nseg + j, p * LANES:(p + 1) * LANES, :]

    def set_state(ci, j, p, val):
        s_out_ref[ci * nseg + j, p * LANES:(p + 1) * LANES, :] = val

    chunks = []
    for ci in range(n_chunks):
        xc = jnp.concatenate(conv[ci * nseg:(ci + 1) * nseg], axis=0)
        rs = slice(ci * c, (ci + 1) * c)
        chunks.append((xc[:, 0:GROUP_W], xc[:, GROUP_W:GROUP_W + M_DSTATE], xc[:, GROUP_W + M_DSTATE:XBC_W],
                       dt_ref[rs, :], zs_ref[rs, :]))
    outs = _ssd_chunks(chunks, -jnp.exp(alog_ref[...]), dsk_ref[...], nw_ref[...], lc_ref[...],
                       tril_ref[...] > 0.5, get_state, set_state, c, seg, state_t=False)
    for ci, out in enumerate(outs):
        for p in range(PAIRS):
            o_ref[ci * c:(ci + 1) * c, p * LANES:(p + 1) * LANES] = out[p].astype(o_ref.dtype)


def _ssd(xbc, dt, zs, conv_w, conv_b, alog_p, dsk, nw, state, hist, n_prompt, batch, seq, dec_seq, h, wt, o_gate, gate_cols):
    m = xbc.shape[0]
    inner = zs.shape[1]
    c = CHUNK
    xb_blk = inner // M_DSTATE
    cb_blk = xb_blk + M_GROUPS

    def specs(row_map, nrow, gpos):
        def rm(fn):
            return lambda *ix: fn(row_map(*ix), ix[gpos])
        zero = lambda fn: (lambda *ix: fn(0, ix[gpos]))
        return [
            pl.BlockSpec((nrow, GROUP_W), rm(lambda r, g: (r, g))),
            pl.BlockSpec((nrow, M_DSTATE), rm(lambda r, g: (r, xb_blk + g))),
            pl.BlockSpec((nrow, M_DSTATE), rm(lambda r, g: (r, cb_blk + g))),
            pl.BlockSpec((nrow, LANES), rm(lambda r, g: (r, g))),
            pl.BlockSpec((nrow, GROUP_W), rm(lambda r, g: (r, g))),
            pl.BlockSpec((M_CONV, GROUP_W), zero(lambda r, g: (0, g))),
            pl.BlockSpec((M_CONV, M_DSTATE), zero(lambda r, g: (0, xb_blk + g))),
            pl.BlockSpec((M_CONV, M_DSTATE), zero(lambda r, g: (0, cb_blk + g))),
            pl.BlockSpec((1, GROUP_W), zero(lambda r, g: (0, g))),
            pl.BlockSpec((1, M_DSTATE), zero(lambda r, g: (0, xb_blk + g))),
            pl.BlockSpec((1, M_DSTATE), zero(lambda r, g: (0, cb_blk + g))),
            pl.BlockSpec((1, LANES), zero(lambda r, g: (0, g))),
            pl.BlockSpec((1, GROUP_W), zero(lambda r, g: (0, g))),
            pl.BlockSpec((1, GROUP_W), zero(lambda r, g: (0, g))),
            pl.BlockSpec((c, 3 * c), zero(lambda r, g: (0, 0))),
            pl.BlockSpec((c, 2 * c), zero(lambda r, g: (0, 0))),
        ]

    common = (xbc, xbc, xbc, dt, zs, conv_w, conv_w, conv_w, conv_b, conv_b, conv_b, alog_p, dsk, nw)
    rows = 1024
    nt = seq // rows
    tril = _ssd_consts(c, c)
    assert m - n_prompt == rows
    last = batch * nt - 1
    scan_row = lambda g, b, t: jnp.minimum(b * nt + t, last)
    gate_row = lambda g, b, t: jnp.minimum(b * nt + t, last + 1)
    in_specs = specs(scan_row, rows, 0)
    d_model = h.shape[1]
    gw = gate_cols // M_GROUPS
    in_specs += [
        pl.BlockSpec((rows, d_model), lambda g, b, t: (gate_row(g, b, t), 0)),
        pl.BlockSpec((pl.Element(gw), pl.Element(d_model)), lambda g, b, t: (pl.multiple_of(o_gate + g * gw, 8), 0),
                     pipeline_mode=pl.Buffered(1)),
    ]
    o_p, s_p, gates = pl.pallas_call(
        functools.partial(_ssd_prompt_kernel, c=c, rows=rows, n_batch=batch),
        grid=(M_GROUPS, batch + 1, nt),
        in_specs=in_specs,
        out_specs=[pl.BlockSpec((rows, GROUP_W), lambda g, b, t: (scan_row(g, b, t), g)),
                   pl.BlockSpec((1, GROUP_W, M_DSTATE), lambda g, b, t: (jnp.minimum(b, batch - 1), g, 0)),
                   pl.BlockSpec((rows, gw), lambda g, b, t: (gate_row(g, b, t), g))],
        out_shape=[jax.ShapeDtypeStruct((n_prompt, inner), bf16),
                   jax.ShapeDtypeStruct((batch, inner, M_DSTATE), f32),
                   jax.ShapeDtypeStruct((m, gate_cols), f32)],
        scratch_shapes=[pltpu.VMEM((rows + 8, XBC_W), f32), pltpu.VMEM((rows, XBC_W), f32),
                        pltpu.VMEM((PAIRS, LANES, M_DSTATE), f32), pltpu.VMEM((gw, d_model), bf16)],
        compiler_params=_cparams(("arbitrary", "arbitrary", "arbitrary")),
        name="ssd_prompt",
    )(*common, _tiled(tril), jnp.asarray(np.tile(tril, (1, 2))), h, wt)
    n_sample = m - n_prompt
    rows = 2 * c
    per = rows // dec_seq
    base = n_prompt // rows
    tril = _ssd_consts(c, dec_seq)
    in_specs = specs(lambda jb, g: base + jb, rows, 1)
    in_specs += [
        pl.BlockSpec((per, M_CONV - 1, GROUP_W), lambda jb, g: (jb, 0, g)),
        pl.BlockSpec((per, M_CONV - 1, M_DSTATE), lambda jb, g: (jb, 0, xb_blk + g)),
        pl.BlockSpec((per, M_CONV - 1, M_DSTATE), lambda jb, g: (jb, 0, cb_blk + g)),
        pl.BlockSpec((per, GROUP_W, M_DSTATE), lambda jb, g: (jb, g, 0)),
    ]
    o_s, s_s = pl.pallas_call(
        functools.partial(_ssd_sample_kernel, c=c, seg=dec_seq),
        grid=(n_sample // rows, M_GROUPS),
        in_specs=in_specs,
        out_specs=[pl.BlockSpec((rows, GROUP_W), lambda jb, g: (jb, g)),
                   pl.BlockSpec((per, GROUP_W, M_DSTATE), lambda jb, g: (jb, g, 0))],
        out_shape=[jax.ShapeDtypeStruct((n_sample, inner), bf16), jax.ShapeDtypeStruct(state.shape, f32)],
        scratch_shapes=[pltpu.VMEM((per, 16, XBC_W), f32)],
        compiler_params=_cparams(("arbitrary", "arbitrary")),
        name="ssd_sample",
    )(*common, _tiled(tril), jnp.asarray(np.tile(tril, (1, 2))), hist, hist, hist, state)
    return (o_p, o_s), s_p, s_s, gates


def kernel(x_prompt, x_sample, state_hgrn, state_ssm, state_conv, norm_mix, w_in, hg_lb_logits, hg_norm, conv_w, conv_b,
           dt_bias, a_log, d_skip, ssm_norm, w_branch_hg, w_branch_ssm, w_out, norm_ffn, w_ffn_gate, w_ffn_up,
           w_ffn_down, norm_final):
    batch, seq, d = x_prompt.shape
    dec_batch, dec_seq, _ = x_sample.shape
    n_prompt, n_sample = batch * seq, dec_batch * dec_seq
    hg_heads = state_hgrn.shape[2]
    kdim = hg_heads * HG_DK
    vdim = d
    inner = d
    m_heads = state_ssm.shape[2]
    conv_dim = conv_w.shape[2]
    hpg = m_heads // M_GROUPS

    xp2, xs2 = x_prompt.reshape(n_prompt, d), x_sample.reshape(n_sample, d)

    wt = jnp.swapaxes(w_in, 1, 2)[0]
    o_q, o_f, o_v, o_g, o_z, o_xbc = 0, kdim, 2 * kdim, 2 * kdim + vdim, 2 * kdim + 2 * vdim, 2 * kdim + 2 * vdim + inner
    o_dt = o_xbc + conv_dim
    o_gate = o_dt + m_heads
    pad_heads = lambda p: jnp.pad(p.reshape(M_GROUPS, hpg, -1), ((0, 0), (0, LANES - hpg), (0, 0))).reshape(M_GROUPS * LANES, -1)
    h, dt = _prenorm_dt(xp2, xs2, norm_mix[0], pad_heads(wt[o_dt:o_gate]), pad_heads(dt_bias[0]).T)
    scale = HG_DK ** -0.5
    ident = lambda a, c, t: (a[0],)
    silu = lambda a, c, t: (jax.nn.silu(a[0]),)
    wide = dict(tm=1024, tn=1024, w_single=False)
    (q,) = _mm([h], [(wt, o_q, True)], kdim, lambda a, c, t: (a[0] * scale,), [f32], name="proj_q", **wide)
    log_f, k = _mm([h], [(wt, o_f, True)], kdim, _epi_forget, [f32, f32], cols=[hg_lb_logits], name="proj_f",
                   tm=512, tn=2048, w_single=True)
    (v,) = _mm([h], [(wt, o_v, True)], vdim, ident, [bf16], name="proj_v", **wide)
    (sg,) = _mm([h], [(wt, o_g, True)], vdim, silu, [f32], name="proj_g", **wide)
    (zs,) = _mm([h], [(wt, o_z, True)], inner, silu, [f32], name="proj_z", **wide)
    (xbc,) = _mm([h], [(wt, o_xbc, True)], conv_dim, ident, [f32], tm=1024, tn=conv_dim // 2, w_single=True, name="proj_xbc")
    o_hg, shp, shs = _hgrn(q, log_f, k, v, sg, hg_norm[0], state_hgrn[0], n_prompt, batch, seq, dec_seq)
    dsk = jnp.repeat(d_skip[0], M_HEADDIM).reshape(1, inner)
    o_m, smp, sms, gates = _ssd(xbc, dt, zs, conv_w[0], conv_b[0].reshape(1, conv_dim), pad_heads(a_log[0]).T, dsk,
                                ssm_norm[0].reshape(1, inner), state_ssm[0].reshape(dec_batch, inner, M_DSTATE),
                                state_conv[0], n_prompt, batch, seq, dec_seq, h, wt, o_gate, 2 * d)

    tm, tn = 512, 1024
    (merged,) = _mm([o_hg, o_m], [(w_branch_hg[0], 0, False), (w_branch_ssm[0], 0, False)], d,
                    lambda a, c, t: (jax.nn.sigmoid(t[0]) * a[0] + jax.nn.sigmoid(t[1]) * a[1],), [bf16],
                    tiles=[(gates, lambda j, i: (i, j)), (gates, lambda j, i: (i, j + d // tn))],
                    tm=tm, tn=tn, w_single=True, name="merge")
    tm = 512
    n_p_tiles = n_prompt // tm

    def add_x_norm(a, c, t):
        x1 = jnp.where(pl.program_id(1) < n_p_tiles, t[0], t[1]) + a[0]
        ms = jnp.mean(x1 * x1, axis=-1, keepdims=True)
        return x1, x1 * lax.rsqrt(ms + EPS) * c[0]

    x1, h2 = _mm([merged], [(w_out[0], 0, False)], d, add_x_norm, [f32, bf16], cols=[norm_ffn],
                 tiles=[(xp2, lambda j, i: (jnp.minimum(i, n_p_tiles - 1), j)),
                        (xs2, lambda j, i: (jnp.maximum(i - n_p_tiles, 0), j), pl.Buffered(1))],
                 tm=tm, tn=d, w_single=True, name="out_proj")
    ffn = _ffn(h2, w_ffn_gate[0], w_ffn_up[0], w_ffn_down[0])
    y_p, y_s = _rmsnorm([x1], norm_final, f32, n_prompt, split_out=True, res=ffn)

    hist = M_CONV - 1
    new_conv_p = jnp.stack([xbc[(b + 1) * seq - hist:(b + 1) * seq] for b in range(batch)])[None]
    new_conv_s = xbc.reshape(-1, dec_seq, conv_dim)[n_prompt // dec_seq:, dec_seq - hist:][None]
    return (y_p.reshape(batch, seq, d), y_s.reshape(dec_batch, dec_seq, d),
            shp[None], smp.reshape(1, batch, m_heads, M_HEADDIM, M_DSTATE), new_conv_p,
            shs[None], sms.reshape(1, dec_batch, m_heads, M_HEADDIM, M_DSTATE), new_conv_s)
```

```python
import functools

import numpy as np
import jax
import jax.numpy as jnp
from jax import lax
from jax.experimental import pallas as pl
from jax.experimental.pallas import tpu as pltpu

f32 = jnp.float32
bf16 = jnp.bfloat16

EPS = 1e-6
LANES = 128
CHUNK = 64
HG_DK = 128
M_HEADDIM = 64
M_DSTATE = 128
M_GROUPS = 4
M_CONV = 4
VMEM_LIMIT = 56 * 1024 * 1024


def _cparams(sem):
    return pltpu.CompilerParams(dimension_semantics=sem, vmem_limit_bytes=VMEM_LIMIT)


def _dot(a, b):
    return jnp.dot(a, b, preferred_element_type=f32)


def _dot_nt(a, b):
    return lax.dot_general(a, b, (((1,), (1,)), ((), ())), preferred_element_type=f32)


def _dot_tn(a, b):
    return lax.dot_general(a, b, (((0,), (0,)), ((), ())), preferred_element_type=f32)


def _split(x, n):
    pieces = []
    for _ in range(n - 1):
        p = x.astype(bf16)
        pieces.append(p)
        x = x - p.astype(f32)
    return pieces + [x.astype(bf16)]


def _tiled(m, n=3):
    return jnp.asarray(np.tile(m, (1, n)), bf16)


def _dot3(mn, x):
    return _dot(mn, jnp.concatenate(_split(x, mn.shape[1] // x.shape[0]), axis=0))


def _mxu_operand(x, small):
    xb = x.astype(bf16)
    return xb.astype(f32) if small else xb


def _rmsnorm_kernel(*refs, n_in, n_out, split, has_res):
    x_refs, w_ref, o_refs = refs[:n_in], refs[n_in], refs[n_in + 1 + has_res:]

    def run(x_ref, o_ref):
        x = x_ref[...]
        if has_res:
            x = x + refs[n_in + 1][...]
        ms = jnp.mean(x * x, axis=-1, keepdims=True)
        o_ref[...] = (x * lax.rsqrt(ms + EPS) * w_ref[...]).astype(o_ref.dtype)

    if n_in == 1 and n_out == 1:
        run(x_refs[0], o_refs[0])
    else:
        i = pl.program_id(0)
        pl.when(i < split)(lambda: run(x_refs[0], o_refs[0]))
        pl.when(i >= split)(lambda: run(x_refs[-1], o_refs[-1]))


def _rmsnorm(xs, w, out_dtype, n_first, split_out=False, res=None, tm=512):
    d = xs[0].shape[1]
    m = sum(x.shape[0] for x in xs)
    split = n_first // tm
    first = lambda i: (jnp.minimum(i, split - 1), 0)
    second = lambda i: (jnp.maximum(i - split, 0), 0)
    whole = lambda i: (i, 0)
    blk = lambda fn: pl.BlockSpec((tm, d), fn)
    in_specs = [blk(whole)] if len(xs) == 1 else [blk(first), blk(second)]
    if split_out:
        out_specs = [blk(first), blk(second)]
        out_shape = [jax.ShapeDtypeStruct((n_first, d), out_dtype), jax.ShapeDtypeStruct((m - n_first, d), out_dtype)]
    else:
        out_specs = [blk(whole)]
        out_shape = [jax.ShapeDtypeStruct((m, d), out_dtype)]
    extra = [] if res is None else [res]
    outs = pl.pallas_call(
        functools.partial(_rmsnorm_kernel, n_in=len(xs), n_out=len(out_shape), split=split, has_res=len(extra)),
        grid=(m // tm,),
        in_specs=in_specs + [pl.BlockSpec((1, d), lambda i: (0, 0))] + [blk(whole) for _ in extra],
        out_specs=out_specs,
        out_shape=out_shape,
        compiler_params=_cparams(("arbitrary",)),
        name="rmsnorm",
    )(*xs, w.reshape(1, d), *extra)
    return outs if split_out else outs[0]


def _prenorm_dt_kernel(xp_ref, xs_ref, w_ref, wdt_ref, bias_ref, h_ref, dt_ref, wdtb_ref, *, split):
    i = pl.program_id(0)

    @pl.when(i == 0)
    def _():
        wdtb_ref[...] = wdt_ref[...].astype(bf16)

    def run(x_ref):
        x = x_ref[...]
        ms = jnp.mean(x * x, axis=-1, keepdims=True)
        hb = (x * lax.rsqrt(ms + EPS) * w_ref[...]).astype(bf16)
        h_ref[...] = hb
        dt_ref[...] = _softplus(_dot_nt(hb, wdtb_ref[...]) + bias_ref[...])

    pl.when(i < split)(lambda: run(xp_ref))
    pl.when(i >= split)(lambda: run(xs_ref))


def _prenorm_dt(xp, xs, w, w_dt_t, dt_bias, tm=512):
    d = xp.shape[1]
    m = xp.shape[0] + xs.shape[0]
    n_dt = w_dt_t.shape[0]
    split = xp.shape[0] // tm
    return pl.pallas_call(
        functools.partial(_prenorm_dt_kernel, split=split),
        grid=(m // tm,),
        in_specs=[pl.BlockSpec((tm, d), lambda i: (jnp.minimum(i, split - 1), 0)),
                  pl.BlockSpec((tm, d), lambda i: (jnp.maximum(i - split, 0), 0)),
                  pl.BlockSpec((1, d), lambda i: (0, 0)),
                  pl.BlockSpec((n_dt, d), lambda i: (0, 0)),
                  pl.BlockSpec((1, n_dt), lambda i: (0, 0))],
        out_specs=[pl.BlockSpec((tm, d), lambda i: (i, 0)), pl.BlockSpec((tm, n_dt), lambda i: (i, 0))],
        out_shape=[jax.ShapeDtypeStruct((m, d), bf16), jax.ShapeDtypeStruct((m, n_dt), f32)],
        scratch_shapes=[pltpu.VMEM((n_dt, d), bf16)],
        compiler_params=_cparams(("arbitrary",)),
        name="prenorm_dt",
    )(xp, xs, w.reshape(1, d), w_dt_t, dt_bias)


def _mm_kernel(*refs, a_parts, split, nd, nc, nt, no, epi, w_t):
    a_refs, refs = refs[:sum(a_parts)], refs[sum(a_parts):]
    w_refs = refs[:nd]
    c_refs = refs[nd:nd + nc]
    t_refs = refs[nd + nc:nd + nc + nt]
    o_refs = refs[nd + nc + nt:nd + nc + nt + no]
    wb_refs = refs[nd + nc + nt + no:]

    @pl.when(pl.program_id(1) == 0)
    def _():
        for w_ref, wb_ref in zip(w_refs, wb_refs):
            wb_ref[...] = w_ref[...].astype(bf16)

    a_vals = []
    for parts in a_parts:
        if parts == 2:
            a_vals.append(jnp.where(pl.program_id(1) < split, a_refs[0][...], a_refs[1][...]))
        else:
            a_vals.append(a_refs[0][...])
        a_refs = a_refs[parts:]
    na = len(a_vals)
    accs = [(_dot_nt if t else _dot)(a_vals[min(i, na - 1)], wb_ref[...])
            for i, (wb_ref, t) in enumerate(zip(wb_refs, w_t))]
    outs = epi(accs, [c[...] for c in c_refs], [t[...] for t in t_refs])
    for o_ref, o in zip(o_refs, outs):
        o_ref[...] = o.astype(o_ref.dtype)


def _col_map(j, i, off):
    return (0, j + off)


def _tile_map(j, i, off):
    return (i, j + off)


def _row_elem_map(j, i, off, tn):
    return (pl.multiple_of(off + j * tn, 8), 0)


def _mm(a_list, w_list, n_cols, epi, out_dtypes, cols=(), tiles=(), tm=1024, tn=512, w_single=False, name="mm"):
    na, nd, nc, nt, no = len(a_list), len(w_list), len(cols), len(tiles), len(out_dtypes)
    assert na in (1, nd)
    a_list = [a if isinstance(a, tuple) else (a,) for a in a_list]
    m = sum(p.shape[0] for p in a_list[0])
    split = a_list[0][0].shape[0] // tm
    in_specs, a_flat = [], []
    for parts in a_list:
        k = parts[0].shape[1]
        if len(parts) == 2:
            in_specs.append(pl.BlockSpec((tm, k), lambda j, i: (jnp.minimum(i, split - 1), 0)))
            in_specs.append(pl.BlockSpec((tm, k), lambda j, i: (jnp.maximum(i - split, 0), 0),
                                         pipeline_mode=pl.Buffered(1)))
        else:
            in_specs.append(pl.BlockSpec((tm, k), lambda j, i: (i, 0)))
        a_flat += list(parts)
    scratch = []
    mode = dict(pipeline_mode=pl.Buffered(1)) if w_single else {}
    for w, off, transposed in w_list:
        if transposed:
            k = w.shape[1]
            assert off % 8 == 0
            in_specs.append(pl.BlockSpec((pl.Element(tn), pl.Element(k)), functools.partial(_row_elem_map, off=off, tn=tn), **mode))
            scratch.append(pltpu.VMEM((tn, k), bf16))
        else:
            k = w.shape[0]
            in_specs.append(pl.BlockSpec((k, tn), functools.partial(_col_map, off=off // tn), **mode))
            scratch.append(pltpu.VMEM((k, tn), bf16))
    in_specs += [pl.BlockSpec((c.shape[0], tn), functools.partial(_col_map, off=0)) for c in cols]
    in_specs += [pl.BlockSpec((tm, tn), t[1], **(dict(pipeline_mode=t[2]) if len(t) > 2 else {})) for t in tiles]
    outs = pl.pallas_call(
        functools.partial(_mm_kernel, a_parts=tuple(len(p) for p in a_list), split=split, nd=nd, nc=nc, nt=nt, no=no,
                          epi=epi, w_t=tuple(t for _, _, t in w_list)),
        grid=(n_cols // tn, m // tm),
        in_specs=in_specs,
        out_specs=[pl.BlockSpec((tm, tn), functools.partial(_tile_map, off=0)) for _ in out_dtypes],
        out_shape=[jax.ShapeDtypeStruct((m, n_cols), dt) for dt in out_dtypes],
        scratch_shapes=scratch,
        compiler_params=_cparams(("arbitrary", "arbitrary")),
        name=name,
    )(*a_flat, *[w for w, _, _ in w_list], *cols, *[t[0] for t in tiles])
    return outs


def _ffn_kernel(h_ref, wg_ref, wu_ref, wd_ref, o_ref):
    @pl.when(pl.program_id(1) == 0)
    def _():
        o_ref[...] = jnp.zeros_like(o_ref)

    h = h_ref[...]
    gate = _dot(h, wg_ref[...].astype(bf16))
    up = _dot(h, wu_ref[...].astype(bf16))
    act = (jax.nn.silu(gate) * up).astype(bf16)
    o_ref[...] += _dot(act, wd_ref[...].astype(bf16))


def _ffn(h, w_gate, w_up, w_down, tm=1536, tf=256):
    m, d = h.shape
    hidden = w_gate.shape[1]
    return pl.pallas_call(
        _ffn_kernel,
        grid=(m // tm, hidden // tf),
        in_specs=[pl.BlockSpec((tm, d), lambda i, f: (i, 0), pipeline_mode=pl.Buffered(1)),
                  pl.BlockSpec((d, tf), lambda i, f: (0, f)),
                  pl.BlockSpec((d, tf), lambda i, f: (0, f)),
                  pl.BlockSpec((tf, d), lambda i, f: (f, 0))],
        out_specs=pl.BlockSpec((tm, d), lambda i, f: (i, 0)),
        out_shape=jax.ShapeDtypeStruct((m, d), f32),
        compiler_params=_cparams(("arbitrary", "arbitrary")),
        name="ffn",
    )(h, w_gate, w_up, w_down)


def _softplus(x):
    return jnp.maximum(x, 0.0) + jnp.log1p(jnp.exp(-jnp.abs(x)))


def _epi_forget(accs, cols, tiles):
    logits = cols[0]
    mx = jnp.max(logits, axis=0, keepdims=True)
    e = jnp.exp(logits - mx)
    lb = e[0:1, :] / jnp.sum(e, axis=0, keepdims=True)
    sig = jax.nn.sigmoid(accs[0])
    log_f = jnp.log(lb + (1.0 - lb) * sig)
    k = (1.0 - lb) * (1.0 - sig)
    return log_f, k


HG_SPLIT = 2


def _hgrn_consts(c, seg):
    nlev = int(np.log2(seg))
    mat = np.zeros(((nlev + 2) * c, c), np.float32)
    masks = np.zeros((nlev + 1, c, c), np.float32)
    for t in range(c):
        tl = t % seg
        base = t - tl
        for l in range(nlev):
            h = 1 << l
            pos = tl % (2 * h)
            ref = base + tl - pos + h - 1
            if pos >= h:
                mat[l * c + t, ref + 1:t + 1] = 1
            else:
                mat[l * c + t, t + 1:ref + 1] = 1
            for s in range(base, base + seg):
                sl = s % seg
                if sl // (2 * h) == tl // (2 * h) and pos >= h and sl % (2 * h) < h:
                    masks[l, t, s] = 1
        mat[nlev * c + t, base:t + 1] = 1
        mat[(nlev + 1) * c + t, t + 1:base + seg] = 1
        masks[nlev, t, t] = 1
    return mat, masks, nlev


def _hgrn_chunks(q_ref, k_ref, g_ref, v_ref, mat, masks, nlev, c):
    n_chunks = q_ref.shape[0] // c
    heads = q_ref.shape[1] // LANES
    rows = lambda ci: slice(ci * c, (ci + 1) * c)
    cols = lambda h: slice(h * LANES, (h + 1) * LANES)
    items = [(ci, h) for ci in range(n_chunks) for h in range(heads)]
    e_all = [jnp.exp(_dot3(mat, g_ref[rows(ci), :])) for ci in range(n_chunks)]
    q = {(ci, h): q_ref[rows(ci), cols(h)] for ci, h in items}
    k = {(ci, h): k_ref[rows(ci), cols(h)] for ci, h in items}
    v = {(ci, h): v_ref[rows(ci), cols(h)].astype(bf16) for ci, h in items}
    s = {it: jnp.where(masks[nlev], _dot_nt(q[it].astype(bf16), k[it].astype(bf16)), 0.0) for it in items}
    for l in range(nlev):
        for ci, h in items:
            e = e_all[ci][l * c:(l + 1) * c, cols(h)]
            s[ci, h] = jnp.where(masks[l], _dot_nt((q[ci, h] * e).astype(bf16), (k[ci, h] * e).astype(bf16)), s[ci, h])
    res = {}
    for ci, h in items:
        o = _dot(s[ci, h].astype(bf16), v[ci, h])
        eb = e_all[ci][nlev * c:(nlev + 1) * c, cols(h)]
        er = e_all[ci][(nlev + 1) * c:, cols(h)]
        res[ci, h] = (o, q[ci, h] * eb, k[ci, h] * er, eb, v[ci, h])
    return res


def _hgrn_out(o, sg, nw):
    ms = jnp.mean(o * o, axis=-1, keepdims=True)
    return o * lax.rsqrt(ms + EPS) * nw * sg


def _hgrn_prompt_kernel(q_ref, g_ref, k_ref, v_ref, sg_ref, nw_ref, mat_ref, mask_ref, o_ref, s_ref, st_ref,
                        *, c, nlev, n_chunks):
    t_blk = pl.program_id(2)

    @pl.when(t_blk == 0)
    def _():
        st_ref[...] = jnp.zeros_like(st_ref)

    masks = [mask_ref[l] > 0.5 for l in range(nlev + 1)]
    mat = mat_ref[...]
    nw = nw_ref[...]

    res = _hgrn_chunks(q_ref, k_ref, g_ref, v_ref, mat, masks, nlev, c)
    upd = {it: _dot_tn(r[4], r[2].astype(bf16)) for it, r in res.items()}
    for h in range(st_ref.shape[0]):
        cs = slice(h * LANES, (h + 1) * LANES)
        st = st_ref[h]
        for ci in range(n_chunks):
            rows = slice(ci * c, (ci + 1) * c)
            o, qe, _, eb, _ = res[ci, h]
            o = o + _dot_nt(qe.astype(bf16), st.astype(bf16))
            st = st * eb[c - 1:c, :] + upd[ci, h]
            o_ref[rows, cs] = _hgrn_out(o, sg_ref[rows, cs], nw[:, cs]).astype(o_ref.dtype)
        st_ref[h] = st

    @pl.when(t_blk == pl.num_programs(2) - 1)
    def _():
        for h in range(st_ref.shape[0]):
            s_ref[0, h] = st_ref[h].T


def _hgrn_sample_kernel(q_ref, g_ref, k_ref, v_ref, sg_ref, nw_ref, mat_ref, mask_ref, s_in_ref,
                        o_ref, s_out_ref, *, c, nlev, seg):
    masks = [mask_ref[l] > 0.5 for l in range(nlev + 1)]
    mat = mat_ref[...]
    nw = nw_ref[...]
    small = seg < 16
    per = c // seg
    res = _hgrn_chunks(q_ref, k_ref, g_ref, v_ref, mat, masks, nlev, c)
    for (ci, h), (o, qe, kd, eb, v) in res.items():
        rows = slice(ci * c, (ci + 1) * c)
        cs = slice(h * LANES, (h + 1) * LANES)
        v = v.astype(f32)
        eb_t = jnp.concatenate([eb, eb], axis=0).T
        parts = []
        for j in range(per):
            sl = slice(j * seg, (j + 1) * seg)
            s0 = s_in_ref[ci * per + j, h]
            parts.append(_dot(_mxu_operand(qe[sl], small), _mxu_operand(s0, small)))
            last = j * seg + seg - 1
            upd = _dot_tn(_mxu_operand(kd[sl], small), _mxu_operand(v[sl], small))
            s_out_ref[ci * per + j, h] = s0 * eb_t[:, last:last + 1] + upd
        o = o + jnp.concatenate(parts, axis=0)
        o_ref[rows, cs] = _hgrn_out(o, sg_ref[rows, cs], nw[:, cs]).astype(o_ref.dtype)


def _hgrn(q, g, k, v, sg, nw, state, n_prompt, batch, seq, dec_seq):
    m, width = q.shape
    heads = width // LANES
    nw = nw.reshape(1, width)
    c = CHUNK
    hp = 4
    hw = hp * LANES
    tc = 1024
    nt = seq // tc
    mat, masks, nlev = _hgrn_consts(c, c)
    row_spec = pl.BlockSpec((tc, hw), lambda b, h, t: (b * nt + t, h))
    const2 = lambda shape: pl.BlockSpec(shape, lambda b, h, t: (0,) * len(shape))
    o_p, s_p = pl.pallas_call(
        functools.partial(_hgrn_prompt_kernel, c=c, nlev=nlev, n_chunks=tc // c),
        grid=(batch, heads // hp, nt),
        in_specs=[row_spec] * 5 + [pl.BlockSpec((1, hw), lambda b, h, t: (0, h)), const2((mat.shape[0], HG_SPLIT * c)), const2(masks.shape)],
        out_specs=[row_spec, pl.BlockSpec((1, hp, HG_DK, LANES), lambda b, h, t: (b, h, 0, 0))],
        out_shape=[jax.ShapeDtypeStruct((n_prompt, width), bf16),
                   jax.ShapeDtypeStruct((batch, heads, HG_DK, LANES), f32)],
        scratch_shapes=[pltpu.VMEM((hp, LANES, HG_DK), f32)],
        compiler_params=_cparams(("arbitrary", "arbitrary", "arbitrary")),
        name="hgrn_prompt",
    )(q, g, k, v, sg, nw, _tiled(mat, HG_SPLIT), jnp.asarray(masks))
    n_sample = m - n_prompt
    rows = 4 * c
    per = rows // dec_seq
    mat, masks, nlev = _hgrn_consts(c, dec_seq)
    base = n_prompt // rows
    row_spec = pl.BlockSpec((rows, hw), lambda jb, h: (base + jb, h))
    st_spec = pl.BlockSpec((per, hp, HG_DK, LANES), lambda jb, h: (jb, h, 0, 0))
    const2 = lambda shape: pl.BlockSpec(shape, lambda jb, h: (0,) * len(shape))
    o_s, s_s = pl.pallas_call(
        functools.partial(_hgrn_sample_kernel, c=c, nlev=nlev, seg=dec_seq),
        grid=(n_sample // rows, heads // hp),
        in_specs=[row_spec] * 5 + [pl.BlockSpec((1, hw), lambda jb, h: (0, h)), const2((mat.shape[0], HG_SPLIT * c)),
                                   const2(masks.shape), st_spec],
        out_specs=[pl.BlockSpec((rows, hw), lambda jb, h: (jb, h)), st_spec],
        out_shape=[jax.ShapeDtypeStruct((n_sample, width), bf16), jax.ShapeDtypeStruct(state.shape, f32)],
        compiler_params=_cparams(("arbitrary", "arbitrary")),
        name="hgrn_sample",
    )(q, g, k, v, sg, nw, _tiled(mat, HG_SPLIT), jnp.asarray(masks), state)
    return (o_p, o_s), s_p, s_s


GROUP_W = 512
PAIRS = GROUP_W // LANES
XBC_W = GROUP_W + 2 * M_DSTATE


def _ssd_consts(c, seg):
    t = np.arange(c)
    same = (t[:, None] // seg) == (t[None, :] // seg)
    tril = (same & (t[None, :] <= t[:, None])).astype(np.float32)
    return tril


def _ssd_chunks(chunks, a_row, dsk, nw, lc, tril, get_state, set_state, c, seg, state_t, after_first_stage=None):
    nseg = c // seg
    small = seg < 16
    assert 2 * c == LANES and 2 * M_HEADDIM == LANES and not (state_t and nseg > 1)
    lo_half = lax.broadcasted_iota(jnp.int32, (c, LANES), 1) < M_HEADDIM
    lo_row = lo_half[0:1, :]
    n = len(chunks)
    items = [(ci, p) for ci in range(n) for p in range(PAIRS)]
    pc = lambda p: slice(p * LANES, (p + 1) * LANES)
    acum = [_dot3(lc, dt * a_row) for _, _, _, dt, _ in chunks]
    bmb = [bm.astype(bf16) for _, bm, _, _, _ in chunks]
    cmb = [cm.astype(bf16) for _, _, cm, _, _ in chunks]
    cb2 = [_dot_nt(cmb[ci], jnp.concatenate([bmb[ci], bmb[ci]], axis=0)) for ci in range(n)]
    acum_t = [jnp.concatenate([a, a], axis=0).T for a in acum]
    dt_t = [jnp.concatenate([ch[3], ch[3]], axis=0).T for ch in chunks]
    if state_t:
        bm_t = [ch[1].T.astype(bf16) for ch in chunks]
    else:
        ea_t = [jnp.exp(a[0:8, 0:c]) for a in acum_t]
    if after_first_stage is not None:
        after_first_stage()
    y, xw, ea = {}, {}, {}
    for ci, p in items:
        h0, h1 = 2 * p, 2 * p + 1
        xp, dt = chunks[ci][0][:, pc(p)], chunks[ci][3]
        acp = jnp.where(lo_half, acum[ci][:, h0:h0 + 1], acum[ci][:, h1:h1 + 1])
        dtp = jnp.where(lo_half, dt[:, h0:h0 + 1], dt[:, h1:h1 + 1])
        a_src = jnp.where(lo_row, acum_t[ci][h0:h0 + 1, :], acum_t[ci][h1:h1 + 1, :])
        dt_src = jnp.where(lo_row, dt_t[ci][h0:h0 + 1, :], dt_t[ci][h1:h1 + 1, :])
        lm = jnp.exp(jnp.where(tril, acp - a_src, -1e30))
        sc = (cb2[ci] * lm * dt_src).astype(bf16)
        x_blk = jnp.concatenate([jnp.where(lo_half, xp, 0.0), jnp.where(lo_half, 0.0, xp)], axis=0).astype(bf16)
        y[ci, p] = _dot(sc, x_blk)
        if nseg == 1:
            alast = acp[c - 1:c, :]
        else:
            alast = jnp.concatenate(
                [jnp.broadcast_to(acp[j * seg + seg - 1:j * seg + seg, :], (seg, LANES)) for j in range(nseg)], axis=0)
        xw[ci, p] = xp * (jnp.exp(alast - acp) * dtp)
        ea[ci, p] = jnp.exp(acp)
    cs = {}
    if state_t:
        upd = {(ci, p): _dot(bm_t[ci], xw[ci, p].astype(bf16)) for ci, p in items}
        for p in range(PAIRS):
            st = get_state(0, 0, p)
            for ci in range(n):
                cs[ci, p] = _dot(cmb[ci], st.astype(bf16))
                st = st * ea[ci, p][c - 1:c, :] + upd[ci, p]
            set_state(0, 0, p, st)
    else:
        for ci, p in items:
            h0, h1 = 2 * p, 2 * p + 1
            bm, cm = chunks[ci][1], chunks[ci][2]
            parts = []
            for j in range(nseg):
                sl = slice(j * seg, (j + 1) * seg)
                s0 = get_state(ci, j, p)
                parts.append(_dot_nt(_mxu_operand(cm[sl], small), _mxu_operand(s0, small)))
                upd = _dot_tn(_mxu_operand(xw[ci, p][sl], small), _mxu_operand(bm[sl], small))
                last = j * seg + seg - 1
                decay = jnp.concatenate(
                    [jnp.broadcast_to(ea_t[ci][h0:h0 + 1, last:last + 1], (M_HEADDIM, LANES)),
                     jnp.broadcast_to(ea_t[ci][h1:h1 + 1, last:last + 1], (M_HEADDIM, LANES))], axis=0)
                set_state(ci, j, p, s0 * decay + upd)
            cs[ci, p] = parts[0] if nseg == 1 else jnp.concatenate(parts, axis=0)
    outs = []
    for ci in range(n):
        xs, zs = chunks[ci][0], chunks[ci][4]
        ssq = jnp.zeros((c, 1), f32)
        ys = []
        for p in range(PAIRS):
            yp = (y[ci, p] + cs[ci, p] * ea[ci, p] + dsk[:, pc(p)] * xs[:, pc(p)]) * zs[:, pc(p)]
            ssq = ssq + jnp.sum(yp * yp, axis=-1, keepdims=True)
            ys.append(yp)
        scale = lax.rsqrt(ssq * (1.0 / GROUP_W) + EPS)
        outs.append([ys[p] * scale * nw[:, pc(p)] for p in range(PAIRS)])
    return outs


def _conv_taps(xpad_ref, lead, rows, w, b):
    acc = None
    for j in range(M_CONV):
        term = xpad_ref[lead + (slice(5 + j + rows[0], 5 + j + rows[1]), slice(None))] * w[j:j + 1, :]
        acc = term if acc is None else acc + term
    return jax.nn.silu(b + acc)


def _ssd_prompt_kernel(*refs, c, rows, n_batch):
    h_ref, gate_ref, wgb_ref = refs[16], refs[20], refs[24]
    pl.when(pl.program_id(1) < n_batch)(lambda: _ssd_prompt_step(*refs, c=c, rows=rows))

    @pl.when(jnp.logical_and(pl.program_id(1) == n_batch, pl.program_id(2) == 0))
    def _():
        gate_ref[...] = _dot_nt(h_ref[...], wgb_ref[...])


def _ssd_prompt_step(xr_ref, br_ref, cr_ref, dt_ref, zs_ref, wx_ref, wb_ref, wc_ref, bx_ref, bb_ref, bc_ref,
                     alog_ref, dsk_ref, nw_ref, lc_ref, tril_ref, h_ref, wg_ref, o_ref, s_ref, gate_ref,
                     xpad_ref, xc_ref, st_ref, wgb_ref, *, c, rows):
    t_blk = pl.program_id(2)

    @pl.when(t_blk == 0)
    def _():
        st_ref[...] = jnp.zeros_like(st_ref)
        xpad_ref[0:8, :] = jnp.zeros((8, XBC_W), f32)

    @pl.when(jnp.logical_and(t_blk == 0, pl.program_id(1) == 0))
    def _():
        wgb_ref[...] = wg_ref[...].astype(bf16)

    @pl.when(t_blk > 0)
    def _():
        xpad_ref[0:8, :] = xpad_ref[rows:rows + 8, :]

    half = rows // 2

    def gate_half(i):
        rs = slice(i * half, (i + 1) * half)
        gate_ref[rs, :] = _dot_nt(h_ref[rs, :], wgb_ref[...])

    gate_half(0)
    xpad_ref[8:8 + rows, 0:GROUP_W] = xr_ref[...]
    xpad_ref[8:8 + rows, GROUP_W:GROUP_W + M_DSTATE] = br_ref[...]
    xpad_ref[8:8 + rows, GROUP_W + M_DSTATE:XBC_W] = cr_ref[...]
    w = jnp.concatenate([wx_ref[...], wb_ref[...], wc_ref[...]], axis=1)
    b = jnp.concatenate([bx_ref[...], bb_ref[...], bc_ref[...]], axis=1)
    for i in range(rows // c):
        xc_ref[i * c:(i + 1) * c, :] = _conv_taps(xpad_ref, (), (i * c, (i + 1) * c), w, b)

    a_row = -jnp.exp(alog_ref[...])
    dsk = dsk_ref[...]
    nw = nw_ref[...]
    lc = lc_ref[...]
    tril = tril_ref[...] > 0.5

    def get_state(ci, j, p):
        return st_ref[p]

    def set_state(ci, j, p, val):
        st_ref[p] = val

    rs = lambda ci: slice(ci * c, (ci + 1) * c)
    chunks = [(xc_ref[rs(ci), 0:GROUP_W], xc_ref[rs(ci), GROUP_W:GROUP_W + M_DSTATE], xc_ref[rs(ci), GROUP_W + M_DSTATE:XBC_W],
               dt_ref[rs(ci), :], zs_ref[rs(ci), :]) for ci in range(rows // c)]
    outs = _ssd_chunks(chunks, a_row, dsk, nw, lc, tril, get_state, set_state, c, c, state_t=True,
                       after_first_stage=lambda: gate_half(1))
    for ci, out in enumerate(outs):
        for p in range(PAIRS):
            o_ref[rs(ci), p * LANES:(p + 1) * LANES] = out[p].astype(o_ref.dtype)

    @pl.when(t_blk == pl.num_programs(2) - 1)
    def _():
        for p in range(PAIRS):
            s_ref[0, p * LANES:(p + 1) * LANES, :] = st_ref[p].T


def _ssd_sample_kernel(xr_ref, br_ref, cr_ref, dt_ref, zs_ref, wx_ref, wb_ref, wc_ref, bx_ref, bb_ref, bc_ref,
                       alog_ref, dsk_ref, nw_ref, lc_ref, tril_ref, hx_ref, hb_ref, hc_ref, s_in_ref,
                       o_ref, s_out_ref, xpad_ref, *, c, seg):
    nseg = c // seg
    n_chunks = xr_ref.shape[0] // c
    w = jnp.concatenate([wx_ref[...], wb_ref[...], wc_ref[...]], axis=1)
    b = jnp.concatenate([bx_ref[...], bb_ref[...], bc_ref[...]], axis=1)
    conv = []
    for j in range(n_chunks * nseg):
        sl = slice(j * seg, (j + 1) * seg)
        xpad_ref[j, 5:8, 0:GROUP_W] = hx_ref[j]
        xpad_ref[j, 5:8, GROUP_W:GROUP_W + M_DSTATE] = hb_ref[j]
        xpad_ref[j, 5:8, GROUP_W + M_DSTATE:XBC_W] = hc_ref[j]
        xpad_ref[j, 8:8 + seg, 0:GROUP_W] = xr_ref[sl, :]
        xpad_ref[j, 8:8 + seg, GROUP_W:GROUP_W + M_DSTATE] = br_ref[sl, :]
        xpad_ref[j, 8:8 + seg, GROUP_W + M_DSTATE:XBC_W] = cr_ref[sl, :]
        conv.append(_conv_taps(xpad_ref, (j,), (0, seg), w, b))

    def get_state(ci, j, p):
        return s_in_ref[ci * nseg + j, p * LANES:(p + 1) * LANES, :]

    def set_state(ci, j, p, val):
        s_out_ref[ci * nseg + j, p * LANES:(p + 1) * LANES, :] = val

    chunks = []
    for ci in range(n_chunks):
        xc = jnp.concatenate(conv[ci * nseg:(ci + 1) * nseg], axis=0)
        rs = slice(ci * c, (ci + 1) * c)
        chunks.append((xc[:, 0:GROUP_W], xc[:, GROUP_W:GROUP_W + M_DSTATE], xc[:, GROUP_W + M_DSTATE:XBC_W],
                       dt_ref[rs, :], zs_ref[rs, :]))
    outs = _ssd_chunks(chunks, -jnp.exp(alog_ref[...]), dsk_ref[...], nw_ref[...], lc_ref[...],
                       tril_ref[...] > 0.5, get_state, set_state, c, seg, state_t=False)
    for ci, out in enumerate(outs):
        for p in range(PAIRS):
            o_ref[ci * c:(ci + 1) * c, p * LANES:(p + 1) * LANES] = out[p].astype(o_ref.dtype)


def _ssd(xbc, dt, zs, conv_w, conv_b, alog_p, dsk, nw, state, hist, n_prompt, batch, seq, dec_seq, h, wt, o_gate, gate_cols):
    m = xbc.shape[0]
    inner = zs.shape[1]
    c = CHUNK
    xb_blk = inner // M_DSTATE
    cb_blk = xb_blk + M_GROUPS

    def specs(row_map, nrow, gpos):
        def rm(fn):
            return lambda *ix: fn(row_map(*ix), ix[gpos])
        zero = lambda fn: (lambda *ix: fn(0, ix[gpos]))
        return [
            pl.BlockSpec((nrow, GROUP_W), rm(lambda r, g: (r, g))),
            pl.BlockSpec((nrow, M_DSTATE), rm(lambda r, g: (r, xb_blk + g))),
            pl.BlockSpec((nrow, M_DSTATE), rm(lambda r, g: (r, cb_blk + g))),
            pl.BlockSpec((nrow, LANES), rm(lambda r, g: (r, g))),
            pl.BlockSpec((nrow, GROUP_W), rm(lambda r, g: (r, g))),
            pl.BlockSpec((M_CONV, GROUP_W), zero(lambda r, g: (0, g))),
            pl.BlockSpec((M_CONV, M_DSTATE), zero(lambda r, g: (0, xb_blk + g))),
            pl.BlockSpec((M_CONV, M_DSTATE), zero(lambda r, g: (0, cb_blk + g))),
            pl.BlockSpec((1, GROUP_W), zero(lambda r, g: (0, g))),
            pl.BlockSpec((1, M_DSTATE), zero(lambda r, g: (0, xb_blk + g))),
            pl.BlockSpec((1, M_DSTATE), zero(lambda r, g: (0, cb_blk + g))),
            pl.BlockSpec((1, LANES), zero(lambda r, g: (0, g))),
            pl.BlockSpec((1, GROUP_W), zero(lambda r, g: (0, g))),
            pl.BlockSpec((1, GROUP_W), zero(lambda r, g: (0, g))),
            pl.BlockSpec((c, 3 * c), zero(lambda r, g: (0, 0))),
            pl.BlockSpec((c, 2 * c), zero(lambda r, g: (0, 0))),
        ]

    common = (xbc, xbc, xbc, dt, zs, conv_w, conv_w, conv_w, conv_b, conv_b, conv_b, alog_p, dsk, nw)
    rows = 1024
    nt = seq // rows
    tril = _ssd_consts(c, c)
    assert m - n_prompt == rows
    last = batch * nt - 1
    scan_row = lambda g, b, t: jnp.minimum(b * nt + t, last)
    gate_row = lambda g, b, t: jnp.minimum(b * nt + t, last + 1)
    in_specs = specs(scan_row, rows, 0)
    d_model = h.shape[1]
    gw = gate_cols // M_GROUPS
    in_specs += [
        pl.BlockSpec((rows, d_model), lambda g, b, t: (gate_row(g, b, t), 0)),
        pl.BlockSpec((pl.Element(gw), pl.Element(d_model)), lambda g, b, t: (pl.multiple_of(o_gate + g * gw, 8), 0),
                     pipeline_mode=pl.Buffered(1)),
    ]
    o_p, s_p, gates = pl.pallas_call(
        functools.partial(_ssd_prompt_kernel, c=c, rows=rows, n_batch=batch),
        grid=(M_GROUPS, batch + 1, nt),
        in_specs=in_specs,
        out_specs=[pl.BlockSpec((rows, GROUP_W), lambda g, b, t: (scan_row(g, b, t), g)),
                   pl.BlockSpec((1, GROUP_W, M_DSTATE), lambda g, b, t: (jnp.minimum(b, batch - 1), g, 0)),
                   pl.BlockSpec((rows, gw), lambda g, b, t: (gate_row(g, b, t), g))],
        out_shape=[jax.ShapeDtypeStruct((n_prompt, inner), bf16),
                   jax.ShapeDtypeStruct((batch, inner, M_DSTATE), f32),
                   jax.ShapeDtypeStruct((m, gate_cols), f32)],
        scratch_shapes=[pltpu.VMEM((rows + 8, XBC_W), f32), pltpu.VMEM((rows, XBC_W), f32),
                        pltpu.VMEM((PAIRS, LANES, M_DSTATE), f32), pltpu.VMEM((gw, d_model), bf16)],
        compiler_params=_cparams(("arbitrary", "arbitrary", "arbitrary")),
        name="ssd_prompt",
    )(*common, _tiled(tril), jnp.asarray(np.tile(tril, (1, 2))), h, wt)
    n_sample = m - n_prompt
    rows = 4 * c
    per = rows // dec_seq
    base = n_prompt // rows
    tril = _ssd_consts(c, dec_seq)
    in_specs = specs(lambda jb, g: base + jb, rows, 1)
    in_specs += [
        pl.BlockSpec((per, M_CONV - 1, GROUP_W), lambda jb, g: (jb, 0, g)),
        pl.BlockSpec((per, M_CONV - 1, M_DSTATE), lambda jb, g: (jb, 0, xb_blk + g)),
        pl.BlockSpec((per, M_CONV - 1, M_DSTATE), lambda jb, g: (jb, 0, cb_blk + g)),
        pl.BlockSpec((per, GROUP_W, M_DSTATE), lambda jb, g: (jb, g, 0)),
    ]
    o_s, s_s = pl.pallas_call(
        functools.partial(_ssd_sample_kernel, c=c, seg=dec_seq),
        grid=(n_sample // rows, M_GROUPS),
        in_specs=in_specs,
        out_specs=[pl.BlockSpec((rows, GROUP_W), lambda jb, g: (jb, g)),
                   pl.BlockSpec((per, GROUP_W, M_DSTATE), lambda jb, g: (jb, g, 0))],
        out_shape=[jax.ShapeDtypeStruct((n_sample, inner), bf16), jax.ShapeDtypeStruct(state.shape, f32)],
        scratch_shapes=[pltpu.VMEM((per, 16, XBC_W), f32)],
        compiler_params=_cparams(("arbitrary", "arbitrary")),
        name="ssd_sample",
    )(*common, _tiled(tril), jnp.asarray(np.tile(tril, (1, 2))), hist, hist, hist, state)
    return (o_p, o_s), s_p, s_s, gates


def kernel(x_prompt, x_sample, state_hgrn, state_ssm, state_conv, norm_mix, w_in, hg_lb_logits, hg_norm, conv_w, conv_b,
           dt_bias, a_log, d_skip, ssm_norm, w_branch_hg, w_branch_ssm, w_out, norm_ffn, w_ffn_gate, w_ffn_up,
           w_ffn_down, norm_final):
    batch, seq, d = x_prompt.shape
    dec_batch, dec_seq, _ = x_sample.shape
    n_prompt, n_sample = batch * seq, dec_batch * dec_seq
    hg_heads = state_hgrn.shape[2]
    kdim = hg_heads * HG_DK
    vdim = d
    inner = d
    m_heads = state_ssm.shape[2]
    conv_dim = conv_w.shape[2]
    hpg = m_heads // M_GROUPS

    xp2, xs2 = x_prompt.reshape(n_prompt, d), x_sample.reshape(n_sample, d)

    wt = jnp.swapaxes(w_in, 1, 2)[0]
    o_q, o_f, o_v, o_g, o_z, o_xbc = 0, kdim, 2 * kdim, 2 * kdim + vdim, 2 * kdim + 2 * vdim, 2 * kdim + 2 * vdim + inner
    o_dt = o_xbc + conv_dim
    o_gate = o_dt + m_heads
    pad_heads = lambda p: jnp.pad(p.reshape(M_GROUPS, hpg, -1), ((0, 0), (0, LANES - hpg), (0, 0))).reshape(M_GROUPS * LANES, -1)
    h, dt = _prenorm_dt(xp2, xs2, norm_mix[0], pad_heads(wt[o_dt:o_gate]), pad_heads(dt_bias[0]).T)
    scale = HG_DK ** -0.5
    ident = lambda a, c, t: (a[0],)
    silu = lambda a, c, t: (jax.nn.silu(a[0]),)
    wide = dict(tm=1024, tn=2048, w_single=True)
    (q,) = _mm([h], [(wt, o_q, True)], kdim, lambda a, c, t: (a[0] * scale,), [f32], name="proj_q", **wide)
    log_f, k = _mm([h], [(wt, o_f, True)], kdim, _epi_forget, [f32, f32], cols=[hg_lb_logits], name="proj_f",
                   tm=512, tn=2048, w_single=True)
    (v,) = _mm([h], [(wt, o_v, True)], vdim, ident, [bf16], name="proj_v", **wide)
    (sg,) = _mm([h], [(wt, o_g, True)], vdim, silu, [f32], name="proj_g", **wide)
    (zs,) = _mm([h], [(wt, o_z, True)], inner, silu, [f32], name="proj_z", **wide)
    (xbc,) = _mm([h], [(wt, o_xbc, True)], conv_dim, ident, [f32], tm=1024, tn=conv_dim // 2, w_single=True, name="proj_xbc")
    o_hg, shp, shs = _hgrn(q, log_f, k, v, sg, hg_norm[0], state_hgrn[0], n_prompt, batch, seq, dec_seq)
    dsk = jnp.repeat(d_skip[0], M_HEADDIM).reshape(1, inner)
    o_m, smp, sms, gates = _ssd(xbc, dt, zs, conv_w[0], conv_b[0].reshape(1, conv_dim), pad_heads(a_log[0]).T, dsk,
                                ssm_norm[0].reshape(1, inner), state_ssm[0].reshape(dec_batch, inner, M_DSTATE),
                                state_conv[0], n_prompt, batch, seq, dec_seq, h, wt, o_gate, 2 * d)

    tm, tn = 512, 1024
    (merged,) = _mm([o_hg, o_m], [(w_branch_hg[0], 0, False), (w_branch_ssm[0], 0, False)], d,
                    lambda a, c, t: (jax.nn.sigmoid(t[0]) * a[0] + jax.nn.sigmoid(t[1]) * a[1],), [bf16],
                    tiles=[(gates, lambda j, i: (i, j)), (gates, lambda j, i: (i, j + d // tn))],
                    tm=tm, tn=tn, w_single=True, name="merge")
    tm = 512
    n_p_tiles = n_prompt // tm

    def add_x_norm(a, c, t):
        x1 = jnp.where(pl.program_id(1) < n_p_tiles, t[0], t[1]) + a[0]
        ms = jnp.mean(x1 * x1, axis=-1, keepdims=True)
        return x1, x1 * lax.rsqrt(ms + EPS) * c[0]

    x1, h2 = _mm([merged], [(w_out[0], 0, False)], d, add_x_norm, [f32, bf16], cols=[norm_ffn],
                 tiles=[(xp2, lambda j, i: (jnp.minimum(i, n_p_tiles - 1), j)),
                        (xs2, lambda j, i: (jnp.maximum(i - n_p_tiles, 0), j), pl.Buffered(1))],
                 tm=tm, tn=d, w_single=True, name="out_proj")
    ffn = _ffn(h2, w_ffn_gate[0], w_ffn_up[0], w_ffn_down[0])
    y_p, y_s = _rmsnorm([x1], norm_final, f32, n_prompt, split_out=True, res=ffn)

    hist = M_CONV - 1
    new_conv_p = jnp.stack([xbc[(b + 1) * seq - hist:(b + 1) * seq] for b in range(batch)])[None]
    new_conv_s = xbc.reshape(-1, dec_seq, conv_dim)[n_prompt // dec_seq:, dec_seq - hist:][None]
    return (y_p.reshape(batch, seq, d), y_s.reshape(dec_batch, dec_seq, d),
            shp[None], smp.reshape(1, batch, m_heads, M_HEADDIM, M_DSTATE), new_conv_p,
            shs[None], sms.reshape(1, dec_batch, m_heads, M_HEADDIM, M_DSTATE), new_conv_s)
```

```python
import functools

import numpy as np
import jax
import jax.numpy as jnp
from jax import lax
from jax.experimental import pallas as pl
from jax.experimental.pallas import tpu as pltpu

f32 = jnp.float32
bf16 = jnp.bfloat16

EPS = 1e-6
LANES = 128
CHUNK = 64
HG_DK = 128
M_HEADDIM = 64
M_DSTATE = 128
M_GROUPS = 4
M_CONV = 4
VMEM_LIMIT = 56 * 1024 * 1024


def _cparams(sem):
    return pltpu.CompilerParams(dimension_semantics=sem, vmem_limit_bytes=VMEM_LIMIT)


def _dot(a, b):
    return jnp.dot(a, b, preferred_element_type=f32)


def _dot_nt(a, b):
    return lax.dot_general(a, b, (((1,), (1,)), ((), ())), preferred_element_type=f32)


def _dot_tn(a, b):
    return lax.dot_general(a, b, (((0,), (0,)), ((), ())), preferred_element_type=f32)


def _split(x, n):
    pieces = []
    for _ in range(n - 1):
        p = x.astype(bf16)
        pieces.append(p)
        x = x - p.astype(f32)
    return pieces + [x.astype(bf16)]


def _tiled(m, n=3):
    return jnp.asarray(np.tile(m, (1, n)), bf16)


def _dot3(mn, x):
    return _dot(mn, jnp.concatenate(_split(x, mn.shape[1] // x.shape[0]), axis=0))


def _mxu_operand(x, small):
    xb = x.astype(bf16)
    return xb.astype(f32) if small else xb


def _rmsnorm_kernel(*refs, n_in, n_out, split, has_res):
    x_refs, w_ref, o_refs = refs[:n_in], refs[n_in], refs[n_in + 1 + has_res:]

    def run(x_ref, o_ref):
        x = x_ref[...]
        if has_res:
            x = x + refs[n_in + 1][...]
        ms = jnp.mean(x * x, axis=-1, keepdims=True)
        o_ref[...] = (x * lax.rsqrt(ms + EPS) * w_ref[...]).astype(o_ref.dtype)

    if n_in == 1 and n_out == 1:
        run(x_refs[0], o_refs[0])
    else:
        i = pl.program_id(0)
        pl.when(i < split)(lambda: run(x_refs[0], o_refs[0]))
        pl.when(i >= split)(lambda: run(x_refs[-1], o_refs[-1]))


def _rmsnorm(xs, w, out_dtype, n_first, split_out=False, res=None, tm=512):
    d = xs[0].shape[1]
    m = sum(x.shape[0] for x in xs)
    split = n_first // tm
    first = lambda i: (jnp.minimum(i, split - 1), 0)
    second = lambda i: (jnp.maximum(i - split, 0), 0)
    whole = lambda i: (i, 0)
    blk = lambda fn: pl.BlockSpec((tm, d), fn)
    in_specs = [blk(whole)] if len(xs) == 1 else [blk(first), blk(second)]
    if split_out:
        out_specs = [blk(first), blk(second)]
        out_shape = [jax.ShapeDtypeStruct((n_first, d), out_dtype), jax.ShapeDtypeStruct((m - n_first, d), out_dtype)]
    else:
        out_specs = [blk(whole)]
        out_shape = [jax.ShapeDtypeStruct((m, d), out_dtype)]
    extra = [] if res is None else [res]
    outs = pl.pallas_call(
        functools.partial(_rmsnorm_kernel, n_in=len(xs), n_out=len(out_shape), split=split, has_res=len(extra)),
        grid=(m // tm,),
        in_specs=in_specs + [pl.BlockSpec((1, d), lambda i: (0, 0))] + [blk(whole) for _ in extra],
        out_specs=out_specs,
        out_shape=out_shape,
        compiler_params=_cparams(("arbitrary",)),
        name="rmsnorm",
    )(*xs, w.reshape(1, d), *extra)
    return outs if split_out else outs[0]


def _prenorm_dt_kernel(xp_ref, xs_ref, w_ref, wdt_ref, bias_ref, h_ref, dt_ref, wdtb_ref, *, split):
    i = pl.program_id(0)

    @pl.when(i == 0)
    def _():
        wdtb_ref[...] = wdt_ref[...].astype(bf16)

    def run(x_ref):
        x = x_ref[...]
        ms = jnp.mean(x * x, axis=-1, keepdims=True)
        hb = (x * lax.rsqrt(ms + EPS) * w_ref[...]).astype(bf16)
        h_ref[...] = hb
        dt_ref[...] = _softplus(_dot_nt(hb, wdtb_ref[...]) + bias_ref[...])

    pl.when(i < split)(lambda: run(xp_ref))
    pl.when(i >= split)(lambda: run(xs_ref))


def _prenorm_dt(xp, xs, w, w_dt_t, dt_bias, tm=512):
    d = xp.shape[1]
    m = xp.shape[0] + xs.shape[0]
    n_dt = w_dt_t.shape[0]
    split = xp.shape[0] // tm
    return pl.pallas_call(
        functools.partial(_prenorm_dt_kernel, split=split),
        grid=(m // tm,),
        in_specs=[pl.BlockSpec((tm, d), lambda i: (jnp.minimum(i, split - 1), 0)),
                  pl.BlockSpec((tm, d), lambda i: (jnp.maximum(i - split, 0), 0)),
                  pl.BlockSpec((1, d), lambda i: (0, 0)),
                  pl.BlockSpec((n_dt, d), lambda i: (0, 0)),
                  pl.BlockSpec((1, n_dt), lambda i: (0, 0))],
        out_specs=[pl.BlockSpec((tm, d), lambda i: (i, 0)), pl.BlockSpec((tm, n_dt), lambda i: (i, 0))],
        out_shape=[jax.ShapeDtypeStruct((m, d), bf16), jax.ShapeDtypeStruct((m, n_dt), f32)],
        scratch_shapes=[pltpu.VMEM((n_dt, d), bf16)],
        compiler_params=_cparams(("arbitrary",)),
        name="prenorm_dt",
    )(xp, xs, w.reshape(1, d), w_dt_t, dt_bias)


def _mm_kernel(*refs, a_parts, split, nd, nc, nt, no, epi, w_t):
    a_refs, refs = refs[:sum(a_parts)], refs[sum(a_parts):]
    w_refs = refs[:nd]
    c_refs = refs[nd:nd + nc]
    t_refs = refs[nd + nc:nd + nc + nt]
    o_refs = refs[nd + nc + nt:nd + nc + nt + no]
    wb_refs = refs[nd + nc + nt + no:]

    @pl.when(pl.program_id(1) == 0)
    def _():
        for w_ref, wb_ref in zip(w_refs, wb_refs):
            wb_ref[...] = w_ref[...].astype(bf16)

    a_vals = []
    for parts in a_parts:
        if parts == 2:
            a_vals.append(jnp.where(pl.program_id(1) < split, a_refs[0][...], a_refs[1][...]))
        else:
            a_vals.append(a_refs[0][...])
        a_refs = a_refs[parts:]
    na = len(a_vals)
    accs = [(_dot_nt if t else _dot)(a_vals[min(i, na - 1)], wb_ref[...])
            for i, (wb_ref, t) in enumerate(zip(wb_refs, w_t))]
    outs = epi(accs, [c[...] for c in c_refs], [t[...] for t in t_refs])
    for o_ref, o in zip(o_refs, outs):
        o_ref[...] = o.astype(o_ref.dtype)


def _col_map(j, i, off):
    return (0, j + off)


def _tile_map(j, i, off):
    return (i, j + off)


def _row_elem_map(j, i, off, tn):
    return (pl.multiple_of(off + j * tn, 8), 0)


def _mm(a_list, w_list, n_cols, epi, out_dtypes, cols=(), tiles=(), tm=1024, tn=512, w_single=False, name="mm"):
    na, nd, nc, nt, no = len(a_list), len(w_list), len(cols), len(tiles), len(out_dtypes)
    assert na in (1, nd)
    a_list = [a if isinstance(a, tuple) else (a,) for a in a_list]
    m = sum(p.shape[0] for p in a_list[0])
    split = a_list[0][0].shape[0] // tm
    in_specs, a_flat = [], []
    for parts in a_list:
        k = parts[0].shape[1]
        if len(parts) == 2:
            in_specs.append(pl.BlockSpec((tm, k), lambda j, i: (jnp.minimum(i, split - 1), 0)))
            in_specs.append(pl.BlockSpec((tm, k), lambda j, i: (jnp.maximum(i - split, 0), 0),
                                         pipeline_mode=pl.Buffered(1)))
        else:
            in_specs.append(pl.BlockSpec((tm, k), lambda j, i: (i, 0)))
        a_flat += list(parts)
    scratch = []
    mode = dict(pipeline_mode=pl.Buffered(1)) if w_single else {}
    for w, off, transposed in w_list:
        if transposed:
            k = w.shape[1]
            assert off % 8 == 0
            in_specs.append(pl.BlockSpec((pl.Element(tn), pl.Element(k)), functools.partial(_row_elem_map, off=off, tn=tn), **mode))
            scratch.append(pltpu.VMEM((tn, k), bf16))
        else:
            k = w.shape[0]
            in_specs.append(pl.BlockSpec((k, tn), functools.partial(_col_map, off=off // tn), **mode))
            scratch.append(pltpu.VMEM((k, tn), bf16))
    in_specs += [pl.BlockSpec((c.shape[0], tn), functools.partial(_col_map, off=0)) for c in cols]
    in_specs += [pl.BlockSpec((tm, tn), t[1], **(dict(pipeline_mode=t[2]) if len(t) > 2 else {})) for t in tiles]
    outs = pl.pallas_call(
        functools.partial(_mm_kernel, a_parts=tuple(len(p) for p in a_list), split=split, nd=nd, nc=nc, nt=nt, no=no,
                          epi=epi, w_t=tuple(t for _, _, t in w_list)),
        grid=(n_cols // tn, m // tm),
        in_specs=in_specs,
        out_specs=[pl.BlockSpec((tm, tn), functools.partial(_tile_map, off=0)) for _ in out_dtypes],
        out_shape=[jax.ShapeDtypeStruct((m, n_cols), dt) for dt in out_dtypes],
        scratch_shapes=scratch,
        compiler_params=_cparams(("arbitrary", "arbitrary")),
        name=name,
    )(*a_flat, *[w for w, _, _ in w_list], *cols, *[t[0] for t in tiles])
    return outs


def _ffn_kernel(h_ref, wg_ref, wu_ref, wd_ref, o_ref):
    @pl.when(pl.program_id(1) == 0)
    def _():
        o_ref[...] = jnp.zeros_like(o_ref)

    h = h_ref[...]
    gate = _dot(h, wg_ref[...].astype(bf16))
    up = _dot(h, wu_ref[...].astype(bf16))
    act = (jax.nn.silu(gate) * up).astype(bf16)
    o_ref[...] += _dot(act, wd_ref[...].astype(bf16))


def _ffn(h, w_gate, w_up, w_down, tm=1536, tf=256):
    m, d = h.shape
    hidden = w_gate.shape[1]
    return pl.pallas_call(
        _ffn_kernel,
        grid=(m // tm, hidden // tf),
        in_specs=[pl.BlockSpec((tm, d), lambda i, f: (i, 0), pipeline_mode=pl.Buffered(1)),
                  pl.BlockSpec((d, tf), lambda i, f: (0, f)),
                  pl.BlockSpec((d, tf), lambda i, f: (0, f)),
                  pl.BlockSpec((tf, d), lambda i, f: (f, 0))],
        out_specs=pl.BlockSpec((tm, d), lambda i, f: (i, 0)),
        out_shape=jax.ShapeDtypeStruct((m, d), f32),
        compiler_params=_cparams(("arbitrary", "arbitrary")),
        name="ffn",
    )(h, w_gate, w_up, w_down)


def _softplus(x):
    return jnp.maximum(x, 0.0) + jnp.log1p(jnp.exp(-jnp.abs(x)))


def _epi_forget(accs, cols, tiles):
    logits = cols[0]
    mx = jnp.max(logits, axis=0, keepdims=True)
    e = jnp.exp(logits - mx)
    lb = e[0:1, :] / jnp.sum(e, axis=0, keepdims=True)
    sig = jax.nn.sigmoid(accs[0])
    log_f = jnp.log(lb + (1.0 - lb) * sig)
    k = (1.0 - lb) * (1.0 - sig)
    return log_f, k


HG_SPLIT = 2


def _hgrn_consts(c, seg):
    nlev = int(np.log2(seg))
    mat = np.zeros(((nlev + 2) * c, c), np.float32)
    masks = np.zeros((nlev + 1, c, c), np.float32)
    for t in range(c):
        tl = t % seg
        base = t - tl
        for l in range(nlev):
            h = 1 << l
            pos = tl % (2 * h)
            ref = base + tl - pos + h - 1
            if pos >= h:
                mat[l * c + t, ref + 1:t + 1] = 1
            else:
                mat[l * c + t, t + 1:ref + 1] = 1
            for s in range(base, base + seg):
                sl = s % seg
                if sl // (2 * h) == tl // (2 * h) and pos >= h and sl % (2 * h) < h:
                    masks[l, t, s] = 1
        mat[nlev * c + t, base:t + 1] = 1
        mat[(nlev + 1) * c + t, t + 1:base + seg] = 1
        masks[nlev, t, t] = 1
    return mat, masks, nlev


def _hgrn_chunks(q_ref, k_ref, g_ref, v_ref, mat, masks, nlev, c):
    n_chunks = q_ref.shape[0] // c
    heads = q_ref.shape[1] // LANES
    rows = lambda ci: slice(ci * c, (ci + 1) * c)
    cols = lambda h: slice(h * LANES, (h + 1) * LANES)
    items = [(ci, h) for ci in range(n_chunks) for h in range(heads)]
    e_all = [jnp.exp(_dot3(mat, g_ref[rows(ci), :])) for ci in range(n_chunks)]
    q = {(ci, h): q_ref[rows(ci), cols(h)] for ci, h in items}
    k = {(ci, h): k_ref[rows(ci), cols(h)] for ci, h in items}
    v = {(ci, h): v_ref[rows(ci), cols(h)].astype(bf16) for ci, h in items}
    s = {it: jnp.where(masks[nlev], _dot_nt(q[it].astype(bf16), k[it].astype(bf16)), 0.0) for it in items}
    for l in range(nlev):
        for ci, h in items:
            e = e_all[ci][l * c:(l + 1) * c, cols(h)]
            s[ci, h] = jnp.where(masks[l], _dot_nt((q[ci, h] * e).astype(bf16), (k[ci, h] * e).astype(bf16)), s[ci, h])
    res = {}
    for ci, h in items:
        o = _dot(s[ci, h].astype(bf16), v[ci, h])
        eb = e_all[ci][nlev * c:(nlev + 1) * c, cols(h)]
        er = e_all[ci][(nlev + 1) * c:, cols(h)]
        res[ci, h] = (o, q[ci, h] * eb, k[ci, h] * er, eb, v[ci, h])
    return res


def _hgrn_out(o, sg, nw):
    ms = jnp.mean(o * o, axis=-1, keepdims=True)
    return o * lax.rsqrt(ms + EPS) * nw * sg


def _hgrn_prompt_kernel(q_ref, g_ref, k_ref, v_ref, sg_ref, nw_ref, mat_ref, mask_ref, o_ref, s_ref, st_ref,
                        *, c, nlev, n_chunks):
    t_blk = pl.program_id(2)

    @pl.when(t_blk == 0)
    def _():
        st_ref[...] = jnp.zeros_like(st_ref)

    masks = [mask_ref[l] > 0.5 for l in range(nlev + 1)]
    mat = mat_ref[...]
    nw = nw_ref[...]

    res = _hgrn_chunks(q_ref, k_ref, g_ref, v_ref, mat, masks, nlev, c)
    upd = {it: _dot_tn(r[4], r[2].astype(bf16)) for it, r in res.items()}
    for h in range(st_ref.shape[0]):
        cs = slice(h * LANES, (h + 1) * LANES)
        st = st_ref[h]
        for ci in range(n_chunks):
            rows = slice(ci * c, (ci + 1) * c)
            o, qe, _, eb, _ = res[ci, h]
            o = o + _dot_nt(qe.astype(bf16), st.astype(bf16))
            st = st * eb[c - 1:c, :] + upd[ci, h]
            o_ref[rows, cs] = _hgrn_out(o, sg_ref[rows, cs], nw[:, cs]).astype(o_ref.dtype)
        st_ref[h] = st

    @pl.when(t_blk == pl.num_programs(2) - 1)
    def _():
        for h in range(st_ref.shape[0]):
            s_ref[0, h] = st_ref[h].T


def _hgrn_sample_kernel(q_ref, g_ref, k_ref, v_ref, sg_ref, nw_ref, mat_ref, mask_ref, s_in_ref,
                        o_ref, s_out_ref, *, c, nlev, seg):
    masks = [mask_ref[l] > 0.5 for l in range(nlev + 1)]
    mat = mat_ref[...]
    nw = nw_ref[...]
    small = seg < 16
    per = c // seg
    res = _hgrn_chunks(q_ref, k_ref, g_ref, v_ref, mat, masks, nlev, c)
    for (ci, h), (o, qe, kd, eb, v) in res.items():
        rows = slice(ci * c, (ci + 1) * c)
        cs = slice(h * LANES, (h + 1) * LANES)
        v = v.astype(f32)
        eb_t = jnp.concatenate([eb, eb], axis=0).T
        parts = []
        for j in range(per):
            sl = slice(j * seg, (j + 1) * seg)
            s0 = s_in_ref[ci * per + j, h]
            parts.append(_dot(_mxu_operand(qe[sl], small), _mxu_operand(s0, small)))
            last = j * seg + seg - 1
            upd = _dot_tn(_mxu_operand(kd[sl], small), _mxu_operand(v[sl], small))
            s_out_ref[ci * per + j, h] = s0 * eb_t[:, last:last + 1] + upd
        o = o + jnp.concatenate(parts, axis=0)
        o_ref[rows, cs] = _hgrn_out(o, sg_ref[rows, cs], nw[:, cs]).astype(o_ref.dtype)


def _hgrn(q, g, k, v, sg, nw, state, n_prompt, batch, seq, dec_seq):
    m, width = q.shape
    heads = width // LANES
    nw = nw.reshape(1, width)
    c = CHUNK
    hp = 4
    hw = hp * LANES
    tc = 1024
    nt = seq // tc
    mat, masks, nlev = _hgrn_consts(c, c)
    row_spec = pl.BlockSpec((tc, hw), lambda b, h, t: (b * nt + t, h))
    const2 = lambda shape: pl.BlockSpec(shape, lambda b, h, t: (0,) * len(shape))
    o_p, s_p = pl.pallas_call(
        functools.partial(_hgrn_prompt_kernel, c=c, nlev=nlev, n_chunks=tc // c),
        grid=(batch, heads // hp, nt),
        in_specs=[row_spec] * 5 + [pl.BlockSpec((1, hw), lambda b, h, t: (0, h)), const2((mat.shape[0], HG_SPLIT * c)), const2(masks.shape)],
        out_specs=[row_spec, pl.BlockSpec((1, hp, HG_DK, LANES), lambda b, h, t: (b, h, 0, 0))],
        out_shape=[jax.ShapeDtypeStruct((n_prompt, width), bf16),
                   jax.ShapeDtypeStruct((batch, heads, HG_DK, LANES), f32)],
        scratch_shapes=[pltpu.VMEM((hp, LANES, HG_DK), f32)],
        compiler_params=_cparams(("arbitrary", "arbitrary", "arbitrary")),
        name="hgrn_prompt",
    )(q, g, k, v, sg, nw, _tiled(mat, HG_SPLIT), jnp.asarray(masks))
    n_sample = m - n_prompt
    rows = 4 * c
    per = rows // dec_seq
    mat, masks, nlev = _hgrn_consts(c, dec_seq)
    base = n_prompt // rows
    row_spec = pl.BlockSpec((rows, hw), lambda jb, h: (base + jb, h))
    st_spec = pl.BlockSpec((per, hp, HG_DK, LANES), lambda jb, h: (jb, h, 0, 0))
    const2 = lambda shape: pl.BlockSpec(shape, lambda jb, h: (0,) * len(shape))
    o_s, s_s = pl.pallas_call(
        functools.partial(_hgrn_sample_kernel, c=c, nlev=nlev, seg=dec_seq),
        grid=(n_sample // rows, heads // hp),
        in_specs=[row_spec] * 5 + [pl.BlockSpec((1, hw), lambda jb, h: (0, h)), const2((mat.shape[0], HG_SPLIT * c)),
                                   const2(masks.shape), st_spec],
        out_specs=[pl.BlockSpec((rows, hw), lambda jb, h: (jb, h)), st_spec],
        out_shape=[jax.ShapeDtypeStruct((n_sample, width), bf16), jax.ShapeDtypeStruct(state.shape, f32)],
        compiler_params=_cparams(("arbitrary", "arbitrary")),
        name="hgrn_sample",
    )(q, g, k, v, sg, nw, _tiled(mat, HG_SPLIT), jnp.asarray(masks), state)
    return (o_p, o_s), s_p, s_s


GROUP_W = 512
PAIRS = GROUP_W // LANES
XBC_W = GROUP_W + 2 * M_DSTATE


def _ssd_consts(c, seg):
    t = np.arange(c)
    same = (t[:, None] // seg) == (t[None, :] // seg)
    tril = (same & (t[None, :] <= t[:, None])).astype(np.float32)
    return tril


def _ssd_chunks(chunks, a_row, dsk, nw, lc, tril, get_state, set_state, c, seg, state_t, after_first_stage=None):
    nseg = c // seg
    small = seg < 16
    assert 2 * c == LANES and 2 * M_HEADDIM == LANES and not (state_t and nseg > 1)
    lo_half = lax.broadcasted_iota(jnp.int32, (c, LANES), 1) < M_HEADDIM
    lo_row = lo_half[0:1, :]
    n = len(chunks)
    items = [(ci, p) for ci in range(n) for p in range(PAIRS)]
    pc = lambda p: slice(p * LANES, (p + 1) * LANES)
    acum = [_dot3(lc, dt * a_row) for _, _, _, dt, _ in chunks]
    bmb = [bm.astype(bf16) for _, bm, _, _, _ in chunks]
    cmb = [cm.astype(bf16) for _, _, cm, _, _ in chunks]
    cb2 = [_dot_nt(cmb[ci], jnp.concatenate([bmb[ci], bmb[ci]], axis=0)) for ci in range(n)]
    acum_t = [jnp.concatenate([a, a], axis=0).T for a in acum]
    dt_t = [jnp.concatenate([ch[3], ch[3]], axis=0).T for ch in chunks]
    if state_t:
        bm_t = [ch[1].T.astype(bf16) for ch in chunks]
    else:
        ea_t = [jnp.exp(a[0:8, 0:c]) for a in acum_t]
    if after_first_stage is not None:
        after_first_stage()
    y, xw, ea = {}, {}, {}
    for ci, p in items:
        h0, h1 = 2 * p, 2 * p + 1
        xp, dt = chunks[ci][0][:, pc(p)], chunks[ci][3]
        acp = jnp.where(lo_half, acum[ci][:, h0:h0 + 1], acum[ci][:, h1:h1 + 1])
        dtp = jnp.where(lo_half, dt[:, h0:h0 + 1], dt[:, h1:h1 + 1])
        a_src = jnp.where(lo_row, acum_t[ci][h0:h0 + 1, :], acum_t[ci][h1:h1 + 1, :])
        dt_src = jnp.where(lo_row, dt_t[ci][h0:h0 + 1, :], dt_t[ci][h1:h1 + 1, :])
        lm = jnp.exp(jnp.where(tril, acp - a_src, -1e30))
        sc = (cb2[ci] * lm * dt_src).astype(bf16)
        x_blk = jnp.concatenate([jnp.where(lo_half, xp, 0.0), jnp.where(lo_half, 0.0, xp)], axis=0).astype(bf16)
        y[ci, p] = _dot(sc, x_blk)
        if nseg == 1:
            alast = acp[c - 1:c, :]
        else:
            alast = jnp.concatenate(
                [jnp.broadcast_to(acp[j * seg + seg - 1:j * seg + seg, :], (seg, LANES)) for j in range(nseg)], axis=0)
        xw[ci, p] = xp * (jnp.exp(alast - acp) * dtp)
        ea[ci, p] = jnp.exp(acp)
    cs = {}
    if state_t:
        upd = {(ci, p): _dot(bm_t[ci], xw[ci, p].astype(bf16)) for ci, p in items}
        for p in range(PAIRS):
            st = get_state(0, 0, p)
            for ci in range(n):
                cs[ci, p] = _dot(cmb[ci], st.astype(bf16))
                st = st * ea[ci, p][c - 1:c, :] + upd[ci, p]
            set_state(0, 0, p, st)
    else:
        for ci, p in items:
            h0, h1 = 2 * p, 2 * p + 1
            bm, cm = chunks[ci][1], chunks[ci][2]
            parts = []
            for j in range(nseg):
                sl = slice(j * seg, (j + 1) * seg)
                s0 = get_state(ci, j, p)
                parts.append(_dot_nt(_mxu_operand(cm[sl], small), _mxu_operand(s0, small)))
                upd = _dot_tn(_mxu_operand(xw[ci, p][sl], small), _mxu_operand(bm[sl], small))
                last = j * seg + seg - 1
                decay = jnp.concatenate(
                    [jnp.broadcast_to(ea_t[ci][h0:h0 + 1, last:last + 1], (M_HEADDIM, LANES)),
                     jnp.broadcast_to(ea_t[ci][h1:h1 + 1, last:last + 1], (M_HEADDIM, LANES))], axis=0)
                set_state(ci, j, p, s0 * decay + upd)
            cs[ci, p] = parts[0] if nseg == 1 else jnp.concatenate(parts, axis=0)
    outs = []
    for ci in range(n):
        xs, zs = chunks[ci][0], chunks[ci][4]
        ssq = jnp.zeros((c, 1), f32)
        ys = []
        for p in range(PAIRS):
            yp = (y[ci, p] + cs[ci, p] * ea[ci, p] + dsk[:, pc(p)] * xs[:, pc(p)]) * zs[:, pc(p)]
            ssq = ssq + jnp.sum(yp * yp, axis=-1, keepdims=True)
            ys.append(yp)
        scale = lax.rsqrt(ssq * (1.0 / GROUP_W) + EPS)
        outs.append([ys[p] * scale * nw[:, pc(p)] for p in range(PAIRS)])
    return outs


def _conv_taps(xpad_ref, lead, rows, w, b):
    acc = None
    for j in range(M_CONV):
        term = xpad_ref[lead + (slice(5 + j + rows[0], 5 + j + rows[1]), slice(None))] * w[j:j + 1, :]
        acc = term if acc is None else acc + term
    return jax.nn.silu(b + acc)


def _ssd_prompt_kernel(*refs, c, rows, n_batch):
    h_ref, gate_ref, wgb_ref = refs[16], refs[20], refs[24]
    pl.when(pl.program_id(1) < n_batch)(lambda: _ssd_prompt_step(*refs, c=c, rows=rows))

    @pl.when(jnp.logical_and(pl.program_id(1) == n_batch, pl.program_id(2) == 0))
    def _():
        gate_ref[...] = _dot_nt(h_ref[...], wgb_ref[...])


def _ssd_prompt_step(xr_ref, br_ref, cr_ref, dt_ref, zs_ref, wx_ref, wb_ref, wc_ref, bx_ref, bb_ref, bc_ref,
                     alog_ref, dsk_ref, nw_ref, lc_ref, tril_ref, h_ref, wg_ref, o_ref, s_ref, gate_ref,
                     xpad_ref, xc_ref, st_ref, wgb_ref, *, c, rows):
    t_blk = pl.program_id(2)

    @pl.when(t_blk == 0)
    def _():
        st_ref[...] = jnp.zeros_like(st_ref)
        xpad_ref[0:8, :] = jnp.zeros((8, XBC_W), f32)

    @pl.when(jnp.logical_and(t_blk == 0, pl.program_id(1) == 0))
    def _():
        wgb_ref[...] = wg_ref[...].astype(bf16)

    @pl.when(t_blk > 0)
    def _():
        xpad_ref[0:8, :] = xpad_ref[rows:rows + 8, :]

    half = rows // 2

    def gate_half(i):
        rs = slice(i * half, (i + 1) * half)
        gate_ref[rs, :] = _dot_nt(h_ref[rs, :], wgb_ref[...])

    gate_half(0)
    xpad_ref[8:8 + rows, 0:GROUP_W] = xr_ref[...]
    xpad_ref[8:8 + rows, GROUP_W:GROUP_W + M_DSTATE] = br_ref[...]
    xpad_ref[8:8 + rows, GROUP_W + M_DSTATE:XBC_W] = cr_ref[...]
    w = jnp.concatenate([wx_ref[...], wb_ref[...], wc_ref[...]], axis=1)
    b = jnp.concatenate([bx_ref[...], bb_ref[...], bc_ref[...]], axis=1)
    for i in range(rows // c):
        xc_ref[i * c:(i + 1) * c, :] = _conv_taps(xpad_ref, (), (i * c, (i + 1) * c), w, b)

    a_row = -jnp.exp(alog_ref[...])
    dsk = dsk_ref[...]
    nw = nw_ref[...]
    lc = lc_ref[...]
    tril = tril_ref[...] > 0.5

    def get_state(ci, j, p):
        return st_ref[p]

    def set_state(ci, j, p, val):
        st_ref[p] = val

    rs = lambda ci: slice(ci * c, (ci + 1) * c)
    chunks = [(xc_ref[rs(ci), 0:GROUP_W], xc_ref[rs(ci), GROUP_W:GROUP_W + M_DSTATE], xc_ref[rs(ci), GROUP_W + M_DSTATE:XBC_W],
               dt_ref[rs(ci), :], zs_ref[rs(ci), :]) for ci in range(rows // c)]
    outs = _ssd_chunks(chunks, a_row, dsk, nw, lc, tril, get_state, set_state, c, c, state_t=True,
                       after_first_stage=lambda: gate_half(1))
    for ci, out in enumerate(outs):
        for p in range(PAIRS):
            o_ref[rs(ci), p * LANES:(p + 1) * LANES] = out[p].astype(o_ref.dtype)

    @pl.when(t_blk == pl.num_programs(2) - 1)
    def _():
        for p in range(PAIRS):
            s_ref[0, p * LANES:(p + 1) * LANES, :] = st_ref[p].T


def _ssd_sample_kernel(xr_ref, br_ref, cr_ref, dt_ref, zs_ref, wx_ref, wb_ref, wc_ref, bx_ref, bb_ref, bc_ref,
                       alog_ref, dsk_ref, nw_ref, lc_ref, tril_ref, hx_ref, hb_ref, hc_ref, s_in_ref,
                       o_ref, s_out_ref, xpad_ref, *, c, seg):
    nseg = c // seg
    n_chunks = xr_ref.shape[0] // c
    w = jnp.concatenate([wx_ref[...], wb_ref[...], wc_ref[...]], axis=1)
    b = jnp.concatenate([bx_ref[...], bb_ref[...], bc_ref[...]], axis=1)
    conv = []
    for j in range(n_chunks * nseg):
        sl = slice(j * seg, (j + 1) * seg)
        xpad_ref[j, 5:8, 0:GROUP_W] = hx_ref[j]
        xpad_ref[j, 5:8, GROUP_W:GROUP_W + M_DSTATE] = hb_ref[j]
        xpad_ref[j, 5:8, GROUP_W + M_DSTATE:XBC_W] = hc_ref[j]
        xpad_ref[j, 8:8 + seg, 0:GROUP_W] = xr_ref[sl, :]
        xpad_ref[j, 8:8 + seg, GROUP_W:GROUP_W + M_DSTATE] = br_ref[sl, :]
        xpad_ref[j, 8:8 + seg, GROUP_W + M_DSTATE:XBC_W] = cr_ref[sl, :]
        conv.append(_conv_taps(xpad_ref, (j,), (0, seg), w, b))

    def get_state(ci, j, p):
        return s_in_ref[ci * nseg + j, p * LANES:(p + 1) * LANES, :]

    def set_state(ci, j, p, val):
        s_out_ref[ci * nseg + j, p * LANES:(p + 1) * LANES, :] = val

    chunks = []
    for ci in range(n_chunks):
        xc = jnp.concatenate(conv[ci * nseg:(ci + 1) * nseg], axis=0)
        rs = slice(ci * c, (ci + 1) * c)
        chunks.append((xc[:, 0:GROUP_W], xc[:, GROUP_W:GROUP_W + M_DSTATE], xc[:, GROUP_W + M_DSTATE:XBC_W],
                       dt_ref[rs, :], zs_ref[rs, :]))
    outs = _ssd_chunks(chunks, -jnp.exp(alog_ref[...]), dsk_ref[...], nw_ref[...], lc_ref[...],
                       tril_ref[...] > 0.5, get_state, set_state, c, seg, state_t=False)
    for ci, out in enumerate(outs):
        for p in range(PAIRS):
            o_ref[ci * c:(ci + 1) * c, p * LANES:(p + 1) * LANES] = out[p].astype(o_ref.dtype)


def _ssd(xbc, dt, zs, conv_w, conv_b, alog_p, dsk, nw, state, hist, n_prompt, batch, seq, dec_seq, h, wt, o_gate, gate_cols):
    m = xbc.shape[0]
    inner = dsk.shape[1]
    zs_blk = (zs.shape[1] - inner) // GROUP_W
    c = CHUNK
    xb_blk = inner // M_DSTATE
    cb_blk = xb_blk + M_GROUPS

    def specs(row_map, nrow, gpos):
        def rm(fn):
            return lambda *ix: fn(row_map(*ix), ix[gpos])
        zero = lambda fn: (lambda *ix: fn(0, ix[gpos]))
        return [
            pl.BlockSpec((nrow, GROUP_W), rm(lambda r, g: (r, g))),
            pl.BlockSpec((nrow, M_DSTATE), rm(lambda r, g: (r, xb_blk + g))),
            pl.BlockSpec((nrow, M_DSTATE), rm(lambda r, g: (r, cb_blk + g))),
            pl.BlockSpec((nrow, LANES), rm(lambda r, g: (r, g))),
            pl.BlockSpec((nrow, GROUP_W), rm(lambda r, g: (r, zs_blk + g))),
            pl.BlockSpec((M_CONV, GROUP_W), zero(lambda r, g: (0, g))),
            pl.BlockSpec((M_CONV, M_DSTATE), zero(lambda r, g: (0, xb_blk + g))),
            pl.BlockSpec((M_CONV, M_DSTATE), zero(lambda r, g: (0, cb_blk + g))),
            pl.BlockSpec((1, GROUP_W), zero(lambda r, g: (0, g))),
            pl.BlockSpec((1, M_DSTATE), zero(lambda r, g: (0, xb_blk + g))),
            pl.BlockSpec((1, M_DSTATE), zero(lambda r, g: (0, cb_blk + g))),
            pl.BlockSpec((1, LANES), zero(lambda r, g: (0, g))),
            pl.BlockSpec((1, GROUP_W), zero(lambda r, g: (0, g))),
            pl.BlockSpec((1, GROUP_W), zero(lambda r, g: (0, g))),
            pl.BlockSpec((c, 3 * c), zero(lambda r, g: (0, 0))),
            pl.BlockSpec((c, 2 * c), zero(lambda r, g: (0, 0))),
        ]

    common = (xbc, xbc, xbc, dt, zs, conv_w, conv_w, conv_w, conv_b, conv_b, conv_b, alog_p, dsk, nw)
    rows = 1024
    nt = seq // rows
    tril = _ssd_consts(c, c)
    assert m - n_prompt == rows
    last = batch * nt - 1
    scan_row = lambda g, b, t: jnp.minimum(b * nt + t, last)
    gate_row = lambda g, b, t: jnp.minimum(b * nt + t, last + 1)
    in_specs = specs(scan_row, rows, 0)
    d_model = h.shape[1]
    gw = gate_cols // M_GROUPS
    in_specs += [
        pl.BlockSpec((rows, d_model), lambda g, b, t: (gate_row(g, b, t), 0)),
        pl.BlockSpec((pl.Element(gw), pl.Element(d_model)), lambda g, b, t: (pl.multiple_of(o_gate + g * gw, 8), 0),
                     pipeline_mode=pl.Buffered(1)),
    ]
    o_p, s_p, gates = pl.pallas_call(
        functools.partial(_ssd_prompt_kernel, c=c, rows=rows, n_batch=batch),
        grid=(M_GROUPS, batch + 1, nt),
        in_specs=in_specs,
        out_specs=[pl.BlockSpec((rows, GROUP_W), lambda g, b, t: (scan_row(g, b, t), g)),
                   pl.BlockSpec((1, GROUP_W, M_DSTATE), lambda g, b, t: (jnp.minimum(b, batch - 1), g, 0)),
                   pl.BlockSpec((rows, gw), lambda g, b, t: (gate_row(g, b, t), g))],
        out_shape=[jax.ShapeDtypeStruct((n_prompt, inner), bf16),
                   jax.ShapeDtypeStruct((batch, inner, M_DSTATE), f32),
                   jax.ShapeDtypeStruct((m, gate_cols), f32)],
        scratch_shapes=[pltpu.VMEM((rows + 8, XBC_W), f32), pltpu.VMEM((rows, XBC_W), f32),
                        pltpu.VMEM((PAIRS, LANES, M_DSTATE), f32), pltpu.VMEM((gw, d_model), bf16)],
        compiler_params=_cparams(("arbitrary", "arbitrary", "arbitrary")),
        name="ssd_prompt",
    )(*common, _tiled(tril), jnp.asarray(np.tile(tril, (1, 2))), h, wt)
    n_sample = m - n_prompt
    rows = 4 * c
    per = rows // dec_seq
    base = n_prompt // rows
    tril = _ssd_consts(c, dec_seq)
    in_specs = specs(lambda jb, g: base + jb, rows, 1)
    in_specs += [
        pl.BlockSpec((per, M_CONV - 1, GROUP_W), lambda jb, g: (jb, 0, g)),
        pl.BlockSpec((per, M_CONV - 1, M_DSTATE), lambda jb, g: (jb, 0, xb_blk + g)),
        pl.BlockSpec((per, M_CONV - 1, M_DSTATE), lambda jb, g: (jb, 0, cb_blk + g)),
        pl.BlockSpec((per, GROUP_W, M_DSTATE), lambda jb, g: (jb, g, 0)),
    ]
    o_s, s_s = pl.pallas_call(
        functools.partial(_ssd_sample_kernel, c=c, seg=dec_seq),
        grid=(n_sample // rows, M_GROUPS),
        in_specs=in_specs,
        out_specs=[pl.BlockSpec((rows, GROUP_W), lambda jb, g: (jb, g)),
                   pl.BlockSpec((per, GROUP_W, M_DSTATE), lambda jb, g: (jb, g, 0))],
        out_shape=[jax.ShapeDtypeStruct((n_sample, inner), bf16), jax.ShapeDtypeStruct(state.shape, f32)],
        scratch_shapes=[pltpu.VMEM((per, 16, XBC_W), f32)],
        compiler_params=_cparams(("arbitrary", "arbitrary")),
        name="ssd_sample",
    )(*common, _tiled(tril), jnp.asarray(np.tile(tril, (1, 2))), hist, hist, hist, state)
    return (o_p, o_s), s_p, s_s, gates


def kernel(x_prompt, x_sample, state_hgrn, state_ssm, state_conv, norm_mix, w_in, hg_lb_logits, hg_norm, conv_w, conv_b,
           dt_bias, a_log, d_skip, ssm_norm, w_branch_hg, w_branch_ssm, w_out, norm_ffn, w_ffn_gate, w_ffn_up,
           w_ffn_down, norm_final):
    batch, seq, d = x_prompt.shape
    dec_batch, dec_seq, _ = x_sample.shape
    n_prompt, n_sample = batch * seq, dec_batch * dec_seq
    hg_heads = state_hgrn.shape[2]
    kdim = hg_heads * HG_DK
    vdim = d
    inner = d
    m_heads = state_ssm.shape[2]
    conv_dim = conv_w.shape[2]
    hpg = m_heads // M_GROUPS

    xp2, xs2 = x_prompt.reshape(n_prompt, d), x_sample.reshape(n_sample, d)

    wt = jnp.swapaxes(w_in, 1, 2)[0]
    o_q, o_f, o_v, o_g, o_z, o_xbc = 0, kdim, 2 * kdim, 2 * kdim + vdim, 2 * kdim + 2 * vdim, 2 * kdim + 2 * vdim + inner
    o_dt = o_xbc + conv_dim
    o_gate = o_dt + m_heads
    pad_heads = lambda p: jnp.pad(p.reshape(M_GROUPS, hpg, -1), ((0, 0), (0, LANES - hpg), (0, 0))).reshape(M_GROUPS * LANES, -1)
    h, dt = _prenorm_dt(xp2, xs2, norm_mix[0], pad_heads(wt[o_dt:o_gate]), pad_heads(dt_bias[0]).T)
    scale = HG_DK ** -0.5
    ident = lambda a, c, t: (a[0],)
    silu = lambda a, c, t: (jax.nn.silu(a[0]),)
    wide = dict(tm=1024, tn=2048, w_single=True)
    (q,) = _mm([h], [(wt, o_q, True)], kdim, lambda a, c, t: (a[0] * scale,), [f32], name="proj_q", **wide)
    log_f, k = _mm([h], [(wt, o_f, True)], kdim, _epi_forget, [f32, f32], cols=[hg_lb_logits], name="proj_f",
                   tm=512, tn=2048, w_single=True)
    (v,) = _mm([h], [(wt, o_v, True)], vdim, ident, [bf16], name="proj_v", **wide)
    assert o_z == o_g + vdim
    (sgz,) = _mm([h], [(wt, o_g, True)], vdim + inner, silu, [f32], tm=512, tn=2048, name="proj_gz")
    sg = zs = sgz
    (xbc,) = _mm([h], [(wt, o_xbc, True)], conv_dim, ident, [f32], tm=1024, tn=conv_dim // 2, w_single=True, name="proj_xbc")
    o_hg, shp, shs = _hgrn(q, log_f, k, v, sg, hg_norm[0], state_hgrn[0], n_prompt, batch, seq, dec_seq)
    dsk = jnp.repeat(d_skip[0], M_HEADDIM).reshape(1, inner)
    o_m, smp, sms, gates = _ssd(xbc, dt, zs, conv_w[0], conv_b[0].reshape(1, conv_dim), pad_heads(a_log[0]).T, dsk,
                                ssm_norm[0].reshape(1, inner), state_ssm[0].reshape(dec_batch, inner, M_DSTATE),
                                state_conv[0], n_prompt, batch, seq, dec_seq, h, wt, o_gate, 2 * d)

    tm, tn = 512, 1024
    (merged,) = _mm([o_hg, o_m], [(w_branch_hg[0], 0, False), (w_branch_ssm[0], 0, False)], d,
                    lambda a, c, t: (jax.nn.sigmoid(t[0]) * a[0] + jax.nn.sigmoid(t[1]) * a[1],), [bf16],
                    tiles=[(gates, lambda j, i: (i, j)), (gates, lambda j, i: (i, j + d // tn))],
                    tm=tm, tn=tn, w_single=True, name="merge")
    tm = 512
    n_p_tiles = n_prompt // tm

    def add_x_norm(a, c, t):
        x1 = jnp.where(pl.program_id(1) < n_p_tiles, t[0], t[1]) + a[0]
        ms = jnp.mean(x1 * x1, axis=-1, keepdims=True)
        return x1, x1 * lax.rsqrt(ms + EPS) * c[0]

    x1, h2 = _mm([merged], [(w_out[0], 0, False)], d, add_x_norm, [f32, bf16], cols=[norm_ffn],
                 tiles=[(xp2, lambda j, i: (jnp.minimum(i, n_p_tiles - 1), j)),
                        (xs2, lambda j, i: (jnp.maximum(i - n_p_tiles, 0), j), pl.Buffered(1))],
                 tm=tm, tn=d, w_single=True, name="out_proj")
    ffn = _ffn(h2, w_ffn_gate[0], w_ffn_up[0], w_ffn_down[0])
    y_p, y_s = _rmsnorm([x1], norm_final, f32, n_prompt, split_out=True, res=ffn)

    hist = M_CONV - 1
    new_conv_p = jnp.stack([xbc[(b + 1) * seq - hist:(b + 1) * seq] for b in range(batch)])[None]
    new_conv_s = xbc.reshape(-1, dec_seq, conv_dim)[n_prompt // dec_seq:, dec_seq - hist:][None]
    return (y_p.reshape(batch, seq, d), y_s.reshape(dec_batch, dec_seq, d),
            shp[None], smp.reshape(1, batch, m_heads, M_HEADDIM, M_DSTATE), new_conv_p,
            shs[None], sms.reshape(1, dec_batch, m_heads, M_HEADDIM, M_DSTATE), new_conv_s)
```
